```python
import math
import jax, jax.numpy as jnp
from jax import lax
import numpy as np

D_MODEL = 1024
BATCH = 4
SEQ = 4096
DEPTH = 2
DEC_BATCH = 16
DEC_SEQ = 2048
PAST_LEN = 128

GRID_W = 64
BRANCH_WIDTH = D_MODEL // 2
ATTN_HEADS = 8
ATTN_KV_HEADS = 2
HEAD_DIM = BRANCH_WIDTH // ATTN_HEADS
ATTN_GROUP = ATTN_HEADS // ATTN_KV_HEADS
KV_WIDTH = ATTN_KV_HEADS * HEAD_DIM
ROPE_AXIS_DIM = HEAD_DIM // 2
ROPE_THETA = 10000.0
Q_BLOCK = 128
HGRN_HEADS = 4
HGRN_KEY_DIM = BRANCH_WIDTH // HGRN_HEADS
HGRN_VAL_DIM = BRANCH_WIDTH // HGRN_HEADS
HGRN_CHUNK = 64
FORGET_FLOOR = 1e-30
HYENA_WIDTH = BRANCH_WIDTH
HYENA_ORDER = 2
SHORT_CONV = 3
FILTER_EMB = 33
FILTER_BANDS = (FILTER_EMB - 1) // 2
FILTER_HIDDEN = 64
FILTER_FAST_DECAY = 0.3
FILTER_SLOW_DECAY = 1.5
FILTER_TARGET = 1e-2
N_FILTER_CH = HYENA_ORDER * 2 * HYENA_WIDTH
N_BRANCH = 3
IN_SPLITS = (BRANCH_WIDTH, KV_WIDTH, KV_WIDTH,
             BRANCH_WIDTH, BRANCH_WIDTH, BRANCH_WIDTH,
             BRANCH_WIDTH, BRANCH_WIDTH,
             3 * HYENA_WIDTH,
             N_BRANCH * D_MODEL)
IN_WIDTH = sum(IN_SPLITS)
D_FF_DENSE = 2816
N_EXPERTS = 8
TOP_K = 2
D_FF_EXPERT = 3584
N_DENSE = (DEPTH + 1) // 2
N_MOE = DEPTH // 2
ALPHA = (2 * DEPTH) ** 0.25
BETA = (8 * DEPTH) ** -0.25
LN_EPS = 1e-5
RMS_EPS = 1e-6

kernel_name = 'hybrid_bidir_encoder'

F32 = jnp.float32


def layer_norm(x, g, b):
    xf = x.astype(F32)
    mu = jnp.mean(xf, axis=-1, keepdims=True)
    var = jnp.mean(jnp.square(xf - mu), axis=-1, keepdims=True)
    return ((xf - mu) * lax.rsqrt(var + LN_EPS) * g + b).astype(x.dtype)


def rms_norm(x, g):
    xf = x.astype(F32)
    return (xf * lax.rsqrt(jnp.mean(jnp.square(xf), axis=-1, keepdims=True) + RMS_EPS) * g).astype(x.dtype)


def axial_rope_tables(L):
    rows = L // GRID_W
    row = jnp.broadcast_to(jnp.arange(rows, dtype=F32)[:, None], (rows, GRID_W)).reshape(-1)
    col = jnp.broadcast_to(jnp.arange(GRID_W, dtype=F32)[None, :], (rows, GRID_W)).reshape(-1)
    half = ROPE_AXIS_DIM // 2
    inv = jnp.power(ROPE_THETA, -2.0 * jnp.arange(half, dtype=F32) / ROPE_AXIS_DIM)
    ang = jnp.stack([row[:, None] * inv, col[:, None] * inv], axis=1)
    return jnp.cos(ang), jnp.sin(ang)


def apply_rope(x, cos, sin):
    B, L, H, _ = x.shape
    half = ROPE_AXIS_DIM // 2
    xr = x.astype(F32).reshape(B, L, H, 2, 2, half)
    x1, x2 = xr[..., 0, :], xr[..., 1, :]
    c, s = cos[None, :, None], sin[None, :, None]
    out = jnp.stack([x1 * c - x2 * s, x1 * s + x2 * c], axis=-2)
    return out.reshape(B, L, H, HEAD_DIM).astype(x.dtype)


def block_attention(q, k, v):
    B, L, _, _ = q.shape
    nb = L // Q_BLOCK
    qb = q.reshape(B, nb, Q_BLOCK, ATTN_KV_HEADS, ATTN_GROUP, HEAD_DIM).transpose(1, 0, 2, 3, 4, 5)
    kf = k.astype(F32)
    scale = HEAD_DIM ** -0.5

    def one_block(qi):
        s = jnp.einsum('bqhgd,bkhd->bhgqk', qi.astype(F32), kf) * scale
        p = jax.nn.softmax(s, axis=-1).astype(v.dtype)
        return jnp.einsum('bhgqk,bkhd->bqhgd', p, v)

    o = lax.map(one_block, qb)
    return o.transpose(1, 0, 2, 3, 4, 5).reshape(B, L, ATTN_HEADS * HEAD_DIM)


def chunked_gated_scan(q, k, v, log_f):
    Z, L, H, dk = q.shape
    dv = v.shape[-1]
    n = L // HGRN_CHUNK

    def to_chunks(a):
        return a.reshape(Z, n, HGRN_CHUNK, H, a.shape[-1]).transpose(1, 0, 3, 2, 4)

    mask = jnp.tril(jnp.ones((HGRN_CHUNK, HGRN_CHUNK), dtype=bool))[:, :, None]

    def step(S, inp):
        qc, kc, vc, gc = inp
        b = jnp.cumsum(gc, axis=2)
        b_last = b[:, :, -1:, :]
        o_inter = jnp.einsum('zhtk,zhkv->zhtv', qc * jnp.exp(b), S)
        rel = b[:, :, :, None, :] - b[:, :, None, :, :]
        decay = jnp.where(mask, jnp.exp(jnp.where(mask, rel, 0.0)), 0.0)
        A = jnp.einsum('zhtk,zhsk,zhtsk->zhts', qc, kc, decay)
        o = o_inter + jnp.einsum('zhts,zhsv->zhtv', A, vc)
        S = jnp.exp(b_last[:, :, 0, :])[..., None] * S + jnp.einsum('zhsk,zhsv->zhkv', kc * jnp.exp(b_last - b), vc)
        return S, o

    S0 = jnp.zeros((Z, H, dk, dv), F32)
    _, o = lax.scan(step, S0, (to_chunks(q), to_chunks(k), to_chunks(v), to_chunks(log_f)))
    return o.transpose(1, 0, 3, 2, 4).reshape(Z, L, H, dv)


def hgrn2_bidirectional(hq, f_fwd, f_bwd, hi, hg, lb, g_norm):
    B, L, _ = hq.shape
    zf = jnp.stack([f_fwd, jnp.flip(f_bwd, axis=1)]).astype(F32)
    lbb = lb[:, None, None, :]
    f = lbb + (1.0 - lbb) * jax.nn.sigmoid(zf)
    log_f = jnp.log(jnp.maximum(f, FORGET_FLOOR))
    k = (1.0 - lbb) * jax.nn.sigmoid(-zf)
    qq = jnp.stack([hq, jnp.flip(hq, axis=1)]).astype(F32)
    vv = jnp.stack([hi, jnp.flip(hi, axis=1)]).astype(F32)

    def heads(a, d):
        return a.reshape(2 * B, L, HGRN_HEADS, d)

    o = chunked_gated_scan(heads(qq, HGRN_KEY_DIM), heads(k, HGRN_KEY_DIM),
                           heads(vv, HGRN_VAL_DIM), heads(log_f, HGRN_KEY_DIM))
    o = o.reshape(2, B, L, HGRN_HEADS, HGRN_VAL_DIM)
    o = o[0] + jnp.flip(o[1], axis=1)
    o = rms_norm(o, g_norm) * jax.nn.silu(hg.reshape(B, L, HGRN_HEADS, HGRN_VAL_DIM).astype(F32))
    return o.reshape(B, L, HGRN_HEADS * HGRN_VAL_DIM).astype(hq.dtype)


def hyena_filters(L, w1, b1, freq, w2, b2, w3):
    t = jnp.linspace(0.0, 1.0, L, dtype=F32)[:, None]
    w = 2.0 * math.pi * jnp.arange(L, dtype=F32) / L
    bands = jnp.linspace(1e-4, FILTER_BANDS - 1, FILTER_BANDS, dtype=F32)
    ang = w[:, None] * bands[None, :]
    feats = jnp.concatenate([t, jnp.cos(ang), -jnp.sin(ang)], axis=-1)
    h = jnp.sin(freq * (feats @ w1 + b1))
    h = jnp.sin(freq * (h @ w2 + b2))
    h = (h @ w3).astype(F32)
    deltas = jnp.abs(jnp.linspace(math.log(FILTER_TARGET) / FILTER_FAST_DECAY,
                                  math.log(FILTER_TARGET) / FILTER_SLOW_DECAY, N_FILTER_CH, dtype=F32))
    h = (h * jnp.exp(-t * deltas)).reshape(L, HYENA_ORDER, 2, HYENA_WIDTH)
    fwd, bwd = h[:, :, 0], h[:, :, 1]
    k = jnp.concatenate([fwd, jnp.zeros((1, HYENA_ORDER, HYENA_WIDTH), F32), jnp.flip(bwd[1:], axis=0)], axis=0)
    return k / jnp.sum(jnp.abs(k), axis=0, keepdims=True)


def long_conv(z, k, bias):
    L = z.shape[1]
    zf = jnp.fft.rfft(z, n=2 * L, axis=1)
    kf = jnp.fft.rfft(k, n=2 * L, axis=0)
    y = jnp.fft.irfft(zf * kf[None], n=2 * L, axis=1)[:, :L]
    return y + z * bias


def hyena_bidirectional(u, filt, conv_w, conv_b, skip):
    B, L, _ = u.shape
    pad = SHORT_CONV // 2
    up = jnp.pad(u, ((0, 0), (pad, pad), (0, 0)))
    uc = sum(up[:, j:j + L] * conv_w[j] for j in range(SHORT_CONV)) + conv_b
    v, x1, x2 = jnp.split(uc.astype(F32), 3, axis=-1)
    z = x1 * long_conv(v, filt[:, 0], skip[0])
    y = x2 * long_conv(z, filt[:, 1], skip[1])
    return y.astype(u.dtype)


def token_mixer(x, cos, sin, lb, filt, w_in, q_g, k_g, hgrn_g, conv_w, conv_b, skip, w_branch, w_out):
    B, L, _ = x.shape
    proj = x @ w_in
    offs = np.cumsum(IN_SPLITS)[:-1].tolist()
    aq, ak, av, hq, hf_fwd, hf_bwd, hi, hg, hy, gl = jnp.split(proj, offs, axis=-1)
    q = apply_rope(rms_norm(aq.reshape(B, L, ATTN_HEADS, HEAD_DIM), q_g), cos, sin)
    k = apply_rope(rms_norm(ak.reshape(B, L, ATTN_KV_HEADS, HEAD_DIM), k_g), cos, sin)
    o_a = block_attention(q, k, av.reshape(B, L, ATTN_KV_HEADS, HEAD_DIM))
    o_h = hgrn2_bidirectional(hq, hf_fwd, hf_bwd, hi, hg, lb, hgrn_g)
    o_c = hyena_bidirectional(hy, filt, conv_w, conv_b, skip)
    gates = jax.nn.sigmoid(gl.astype(F32)).reshape(B, L, N_BRANCH, D_MODEL)
    merged = (gates[:, :, 0] * (o_a @ w_branch[0]) + gates[:, :, 1] * (o_h @ w_branch[1])
              + gates[:, :, 2] * (o_c @ w_branch[2]))
    return merged.astype(x.dtype) @ w_out


def swiglu(x, w_gu, w_down):
    a, b = jnp.split(x @ w_gu, 2, axis=-1)
    return (jax.nn.silu(a) * b) @ w_down


def moe_swiglu(x, router, w_gu, w_down):
    logits = (x @ router).astype(F32)
    top_v, top_i = lax.top_k(logits, TOP_K)
    w = jax.nn.softmax(top_v, axis=-1)
    combine = jnp.sum(jax.nn.one_hot(top_i, N_EXPERTS, dtype=F32) * w[..., None], axis=-2)
    y = jnp.zeros(x.shape, F32)
    for e in range(N_EXPERTS):
        y = y + combine[..., e:e + 1] * swiglu(x, w_gu[e], w_down[e])
    return y.astype(x.dtype)


def trunk(x, w_in, q_norm_g, k_norm_g, hgrn_lb, hgrn_norm_g, conv_w, conv_b, filt_w1, filt_b1, filt_freq,
          filt_w2, filt_b2, filt_w3, hyena_skip, w_branch, w_out, ln1_g, ln1_b, ln2_g, ln2_b,
          ffn_w_gu, ffn_w_down, router_w, expert_w_gu, expert_w_down):
    L = x.shape[1]
    cos, sin = axial_rope_tables(L)
    s = jax.nn.softmax(hgrn_lb.astype(F32), axis=0)
    lower_bounds = jnp.cumsum(s, axis=0) - s[:1]
    for l in range(DEPTH):
        filt = hyena_filters(L, filt_w1[l], filt_b1[l], filt_freq[l], filt_w2[l], filt_b2[l], filt_w3[l])
        mix = token_mixer(x, cos, sin, lower_bounds[l], filt, w_in[l], q_norm_g[l], k_norm_g[l],
                          hgrn_norm_g[l], conv_w[l], conv_b[l], hyena_skip[l], w_branch[l], w_out[l])
        x = layer_norm(ALPHA * x + mix, ln1_g[l], ln1_b[l])
        if l % 2 == 0:
            ff = swiglu(x, ffn_w_gu[l // 2], ffn_w_down[l // 2])
        else:
            ff = moe_swiglu(x, router_w[l // 2], expert_w_gu[l // 2], expert_w_down[l // 2])
        x = layer_norm(ALPHA * x + ff, ln2_g[l], ln2_b[l])
    return x


def setup_inputs(seed: int = 0) -> dict:
    key = jax.random.key(seed)
    ks = jax.random.split(key, 27)

    def nrm(k, shape, scale):
        return scale * jax.random.normal(k, shape, F32)

    return {
        'x_prompt': nrm(ks[0], (BATCH, SEQ, D_MODEL), 1.0),
        'x_sample': nrm(ks[1], (DEC_BATCH, DEC_SEQ, D_MODEL), 1.0),
        'w_in': nrm(ks[2], (DEPTH, D_MODEL, IN_WIDTH), D_MODEL ** -0.5),
        'q_norm_g': 1.0 + nrm(ks[3], (DEPTH, HEAD_DIM), 0.02),
        'k_norm_g': 1.0 + nrm(ks[4], (DEPTH, HEAD_DIM), 0.02),
        'hgrn_lb': nrm(ks[5], (DEPTH, 2, BRANCH_WIDTH), 1.0),
        'hgrn_norm_g': 1.0 + nrm(ks[6], (DEPTH, HGRN_VAL_DIM), 0.02),
        'conv_w': nrm(ks[7], (DEPTH, SHORT_CONV, 3 * HYENA_WIDTH), SHORT_CONV ** -0.5),
        'conv_b': nrm(ks[8], (DEPTH, 3 * HYENA_WIDTH), 0.02),
        'filt_w1': nrm(ks[9], (DEPTH, FILTER_EMB, FILTER_HIDDEN), FILTER_EMB ** -0.5),
        'filt_b1': nrm(ks[10], (DEPTH, FILTER_HIDDEN), 0.02),
        'filt_freq': 1.0 + nrm(ks[11], (DEPTH, FILTER_HIDDEN), 0.1),
        'filt_w2': nrm(ks[12], (DEPTH, FILTER_HIDDEN, FILTER_HIDDEN), FILTER_HIDDEN ** -0.5),
        'filt_b2': nrm(ks[13], (DEPTH, FILTER_HIDDEN), 0.02),
        'filt_w3': nrm(ks[14], (DEPTH, FILTER_HIDDEN, N_FILTER_CH), FILTER_HIDDEN ** -0.5),
        'hyena_skip': nrm(ks[15], (DEPTH, HYENA_ORDER, HYENA_WIDTH), 0.1),
        'w_branch': nrm(ks[16], (DEPTH, N_BRANCH, BRANCH_WIDTH, D_MODEL), BRANCH_WIDTH ** -0.5),
        'w_out': nrm(ks[17], (DEPTH, D_MODEL, D_MODEL), BETA * D_MODEL ** -0.5),
        'ln1_g': 1.0 + nrm(ks[18], (DEPTH, D_MODEL), 0.02),
        'ln1_b': nrm(ks[19], (DEPTH, D_MODEL), 0.02),
        'ln2_g': 1.0 + nrm(ks[20], (DEPTH, D_MODEL), 0.02),
        'ln2_b': nrm(ks[21], (DEPTH, D_MODEL), 0.02),
        'ffn_w_gu': nrm(ks[22], (N_DENSE, D_MODEL, 2 * D_FF_DENSE), D_MODEL ** -0.5),
        'ffn_w_down': nrm(ks[23], (N_DENSE, D_FF_DENSE, D_MODEL), BETA * D_FF_DENSE ** -0.5),
        'router_w': nrm(ks[24], (N_MOE, D_MODEL, N_EXPERTS), D_MODEL ** -0.5),
        'expert_w_gu': nrm(ks[25], (N_MOE, N_EXPERTS, D_MODEL, 2 * D_FF_EXPERT), D_MODEL ** -0.5),
        'expert_w_down': nrm(ks[26], (N_MOE, N_EXPERTS, D_FF_EXPERT, D_MODEL), BETA * D_FF_EXPERT ** -0.5),
    }


def reference(x_prompt, x_sample, w_in, q_norm_g, k_norm_g, hgrn_lb, hgrn_norm_g, conv_w, conv_b,
              filt_w1, filt_b1, filt_freq, filt_w2, filt_b2, filt_w3, hyena_skip, w_branch, w_out,
              ln1_g, ln1_b, ln2_g, ln2_b, ffn_w_gu, ffn_w_down, router_w, expert_w_gu, expert_w_down):
    weights = (w_in, q_norm_g, k_norm_g, hgrn_lb, hgrn_norm_g, conv_w, conv_b, filt_w1, filt_b1, filt_freq,
               filt_w2, filt_b2, filt_w3, hyena_skip, w_branch, w_out, ln1_g, ln1_b, ln2_g, ln2_b,
               ffn_w_gu, ffn_w_down, router_w, expert_w_gu, expert_w_down)
    y_prompt = trunk(x_prompt, *weights)
    y_sample = trunk(x_sample, *weights)
    return (y_prompt, y_sample)
```

```python
import functools
import math

import numpy as np
import jax
import jax.numpy as jnp
from jax import lax
from jax.experimental import pallas as pl
from jax.experimental.pallas import tpu as pltpu

F32 = jnp.float32
BF16 = jnp.bfloat16

D_MODEL = 1024
DEPTH = 2
GRID_W = 64
BRANCH_WIDTH = 512
ATTN_HEADS = 8
ATTN_KV_HEADS = 2
ATTN_GROUP = ATTN_HEADS // ATTN_KV_HEADS
HEAD_DIM = 64
KV_WIDTH = ATTN_KV_HEADS * HEAD_DIM
ROPE_AXIS_DIM = HEAD_DIM // 2
ROPE_HALF = ROPE_AXIS_DIM // 2
ROPE_THETA = 10000.0
HGRN_HEADS = 4
HGRN_DIM = 128
HGRN_CHUNK = 64
FORGET_FLOOR = 1e-30
HYENA_WIDTH = BRANCH_WIDTH
HYENA_ORDER = 2
FILTER_EMB = 33
FILTER_BANDS = (FILTER_EMB - 1) // 2
FILTER_HIDDEN = 64
FILTER_FAST_DECAY = 0.3
FILTER_SLOW_DECAY = 1.5
FILTER_TARGET = 1e-2
N_FILTER_CH = HYENA_ORDER * 2 * HYENA_WIDTH
N_BRANCH = 3
D_FF_DENSE = 2816
N_EXPERTS = 8
D_FF_EXPERT = 3584
ALPHA = (2 * DEPTH) ** 0.25
LN_EPS = 1e-5
RMS_EPS = 1e-6

SEG_WIDTHS = (BRANCH_WIDTH,
              2 * KV_WIDTH,
              BRANCH_WIDTH,
              2 * BRANCH_WIDTH,
              BRANCH_WIDTH,
              BRANCH_WIDTH,
              3 * HYENA_WIDTH,
              N_BRANCH * D_MODEL)
IN_WIDTH = sum(SEG_WIDTHS)

FFT_N2 = 128
VMEM_LIMIT = 56 * 1024 * 1024


def _cparams(sem):
    return pltpu.CompilerParams(dimension_semantics=sem, vmem_limit_bytes=VMEM_LIMIT)


def _dot(a, b):
    return jnp.dot(a, b, preferred_element_type=F32)


def _dot_nt(a, b):
    return lax.dot_general(a, b, (((1,), (1,)), ((), ())), preferred_element_type=F32)


def _dot_tn(a, b):
    return lax.dot_general(a, b, (((0,), (0,)), ((), ())), preferred_element_type=F32)


def _split2(a):
    hi = a.astype(BF16)
    lo = (a - hi.astype(F32)).astype(BF16)
    return hi, lo


def _dot3(a, b):
    ah, al = _split2(a)
    bh, bl = _split2(b)
    return _dot(ah, bh) + _dot(al, bh) + _dot(ah, bl)


def _layer_norm(y, g, b):
    mu = jnp.mean(y, axis=-1, keepdims=True)
    yc = y - mu
    var = jnp.mean(yc * yc, axis=-1, keepdims=True)
    return yc * lax.rsqrt(var + LN_EPS) * g + b


def _sigmoid(x):
    return 1.0 / (1.0 + jnp.exp(-x))


def _silu(x):
    return x * _sigmoid(x)


def _row_tile(n, want):
    t = min(n, want)
    assert n % t == 0
    return t


def _in_proj_kernel(x_ref, w_ref, *out_refs):
    x = x_ref[...].astype(BF16)
    off = 0
    for o_ref, width in zip(out_refs, SEG_WIDTHS):
        o_ref[...] = _dot(x, w_ref[:, off:off + width]).astype(o_ref.dtype)
        off += width


def _in_proj(x2, w_bf16):
    T = x2.shape[0]
    tm = _row_tile(T, 256)
    return pl.pallas_call(
        _in_proj_kernel,
        grid=(T // tm,),
        in_specs=[pl.BlockSpec((tm, D_MODEL), lambda i: (i, 0)),
                  pl.BlockSpec((D_MODEL, IN_WIDTH), lambda i: (0, 0), pipeline_mode=pl.Buffered(1))],
        out_specs=[pl.BlockSpec((tm, w), lambda i: (i, 0)) for w in SEG_WIDTHS],
        out_shape=[jax.ShapeDtypeStruct((T, w), F32) for w in SEG_WIDTHS],
        compiler_params=_cparams(("parallel",)),
        name="in_proj",
    )(x2, w_bf16)


def _rope_tables(L):
    rows = L // GRID_W
    row = jnp.broadcast_to(jnp.arange(rows, dtype=F32)[:, None], (rows, GRID_W)).reshape(-1)
    col = jnp.broadcast_to(jnp.arange(GRID_W, dtype=F32)[None, :], (rows, GRID_W)).reshape(-1)
    inv = jnp.power(ROPE_THETA, -2.0 * jnp.arange(ROPE_HALF, dtype=F32) / ROPE_AXIS_DIM)
    a0, a1 = row[:, None] * inv, col[:, None] * inv
    c0, s0, c1, s1 = jnp.cos(a0), jnp.sin(a0), jnp.cos(a1), jnp.sin(a1)
    cos_h = jnp.concatenate([c0, c0, c1, c1], axis=-1)
    sin_h = jnp.concatenate([-s0, s0, -s1, s1], axis=-1)
    return jnp.tile(cos_h, (1, ATTN_HEADS)), jnp.tile(sin_h, (1, ATTN_HEADS))


def _head_block_ones(width):
    idx = np.arange(width) // HEAD_DIM
    return jnp.asarray(idx[:, None] == idx[None, :], dtype=BF16)


def _norm_rope(x, gain, cos, sin, ones_bd):
    width = x.shape[-1]
    sq_hi, sq_lo = _split2(x * x)
    ms = (_dot(sq_hi, ones_bd) + _dot(sq_lo, ones_bd)) * (1.0 / HEAD_DIM)
    xn = x * lax.rsqrt(ms + RMS_EPS) * gain
    lane = lax.broadcasted_iota(jnp.int32, xn.shape, 1)
    first_half = (lane % ROPE_AXIS_DIM) < ROPE_HALF
    swapped = jnp.where(first_half, pltpu.roll(xn, width - ROPE_HALF, 1), pltpu.roll(xn, ROPE_HALF, 1))
    return xn * cos + swapped * sin


def _attn_prep_kernel(aq_ref, akv_ref, cos_ref, sin_ref, qg_ref, kg_ref, bdq_ref, bdk_ref,
                      q_out, k_out, v_out):
    cos, sin = cos_ref[...], sin_ref[...]
    q = _norm_rope(aq_ref[...], qg_ref[...], cos, sin, bdq_ref[...])
    q_out[...] = (q * (HEAD_DIM ** -0.5)).astype(BF16)
    kv = akv_ref[...]
    k = _norm_rope(kv[:, :KV_WIDTH], kg_ref[...], cos[:, :KV_WIDTH], sin[:, :KV_WIDTH], bdk_ref[...])
    k_out[...] = k.astype(BF16)
    v_out[...] = kv[:, KV_WIDTH:].astype(BF16)


def _attn_prep(aq, akv, cos_t, sin_t, q_gain, k_gain, L):
    T = aq.shape[0]
    tm = _row_tile(L, 512)
    nl = L // tm
    qg = jnp.tile(q_gain, ATTN_HEADS)[None, :]
    kg = jnp.tile(k_gain, ATTN_KV_HEADS)[None, :]
    const = lambda i: (0, 0)
    return pl.pallas_call(
        _attn_prep_kernel,
        grid=(T // tm,),
        in_specs=[pl.BlockSpec((tm, BRANCH_WIDTH), lambda i: (i, 0)),
                  pl.BlockSpec((tm, 2 * KV_WIDTH), lambda i: (i, 0)),
                  pl.BlockSpec((tm, BRANCH_WIDTH), lambda i: (i % nl, 0)),
                  pl.BlockSpec((tm, BRANCH_WIDTH), lambda i: (i % nl, 0)),
                  pl.BlockSpec((1, BRANCH_WIDTH), const),
                  pl.BlockSpec((1, KV_WIDTH), const),
                  pl.BlockSpec((BRANCH_WIDTH, BRANCH_WIDTH), const),
                  pl.BlockSpec((KV_WIDTH, KV_WIDTH), const)],
        out_specs=[pl.BlockSpec((tm, BRANCH_WIDTH), lambda i: (i, 0)),
                   pl.BlockSpec((tm, KV_WIDTH), lambda i: (i, 0)),
                   pl.BlockSpec((tm, KV_WIDTH), lambda i: (i, 0))],
        out_shape=[jax.ShapeDtypeStruct((T, BRANCH_WIDTH), BF16),
                   jax.ShapeDtypeStruct((T, KV_WIDTH), BF16),
                   jax.ShapeDtypeStruct((T, KV_WIDTH), BF16)],
        compiler_params=_cparams(("parallel",)),
        name="attn_prep",
    )(aq, akv, cos_t, sin_t, qg, kg, _head_block_ones(BRANCH_WIDTH), _head_block_ones(KV_WIDTH))


def _flash_kernel(q_ref, k_ref, v_ref, o_ref, qs_ref, m_ref, l_ref, acc_ref, *, tq):
    kv = pl.program_id(2)

    @pl.when(kv == 0)
    def _():
        for h in range(ATTN_HEADS):
            g = h % ATTN_GROUP
            qs_ref[h // ATTN_GROUP, g * tq:(g + 1) * tq, :] = q_ref[0, :, h * HEAD_DIM:(h + 1) * HEAD_DIM]
        m_ref[...] = jnp.full(m_ref.shape, -jnp.inf, F32)
        l_ref[...] = jnp.zeros(l_ref.shape, F32)
        acc_ref[...] = jnp.zeros(acc_ref.shape, F32)

    for kh in range(ATTN_KV_HEADS):
        k = k_ref[0, :, kh * HEAD_DIM:(kh + 1) * HEAD_DIM]
        v = v_ref[0, :, kh * HEAD_DIM:(kh + 1) * HEAD_DIM]
        s = _dot_nt(qs_ref[kh], k)
        m_prev = m_ref[kh]
        m_new = jnp.maximum(m_prev, jnp.max(s, axis=-1, keepdims=True))
        p = jnp.exp(s - m_new)
        corr = jnp.exp(m_prev - m_new)
        l_ref[kh] = corr * l_ref[kh] + jnp.sum(p, axis=-1, keepdims=True)
        acc_ref[kh] = corr * acc_ref[kh] + _dot(p.astype(BF16), v)
        m_ref[kh] = m_new

    @pl.when(kv == pl.num_programs(2) - 1)
    def _():
        for h in range(ATTN_HEADS):
            kh, g = h // ATTN_GROUP, h % ATTN_GROUP
            rows = slice(g * tq, (g + 1) * tq)
            o_ref[0, :, h * HEAD_DIM:(h + 1) * HEAD_DIM] = acc_ref[kh, rows, :] / l_ref[kh, rows, :]


def _flash_attention(qn, kn, vn, B, L):
    tq = _row_tile(L, 256)
    tk = _row_tile(L, 512)
    q3, k3, v3 = (a.reshape(B, L, -1) for a in (qn, kn, vn))
    out = pl.pallas_call(
        functools.partial(_flash_kernel, tq=tq),
        grid=(B, L // tq, L // tk),
        in_specs=[pl.BlockSpec((1, tq, BRANCH_WIDTH), lambda b, i, j: (b, i, 0)),
                  pl.BlockSpec((1, tk, KV_WIDTH), lambda b, i, j: (b, j, 0)),
                  pl.BlockSpec((1, tk, KV_WIDTH), lambda b, i, j: (b, j, 0))],
        out_specs=pl.BlockSpec((1, tq, BRANCH_WIDTH), lambda b, i, j: (b, i, 0)),
        out_shape=jax.ShapeDtypeStruct((B, L, BRANCH_WIDTH), F32),
        scratch_shapes=[pltpu.VMEM((ATTN_KV_HEADS, ATTN_GROUP * tq, HEAD_DIM), BF16),
                        pltpu.VMEM((ATTN_KV_HEADS, ATTN_GROUP * tq, 1), F32),
                        pltpu.VMEM((ATTN_KV_HEADS, ATTN_GROUP * tq, 1), F32),
                        pltpu.VMEM((ATTN_KV_HEADS, ATTN_GROUP * tq, HEAD_DIM), F32)],
        compiler_params=_cparams(("parallel", "parallel", "arbitrary")),
        name="flash_attention",
    )(q3, k3, v3)
    return out.reshape(B * L, BRANCH_WIDTH)


def _ref_rows(b, h, fwd):
    C, W = b.shape
    two_h = 2 * h
    if two_h >= 8:
        n = C // two_h
        b3 = b.reshape(n, two_h, W)
        r = jnp.where(fwd, b3[:, h - 1:h, :], b3[:, h:h + 1, :])
        return jnp.broadcast_to(r, (n, two_h, W)).reshape(C, W)
    b3 = b.reshape(C // 8, 8, W)
    pos8 = lax.broadcasted_iota(jnp.int32, (1, 8, 1), 1)
    out = None
    for blk in range(8 // two_h):
        base = blk * two_h
        r = jnp.where(fwd, b3[:, base + h - 1:base + h, :], b3[:, base + h:base + h + 1, :])
        r = jnp.broadcast_to(r, b3.shape)
        out = r if out is None else jnp.where((pos8 // two_h) == blk, r, out)
    return out.reshape(C, W)


def _hgrn_chunk(q, z, v, lb, tri, st_ref, fwd):
    C, W = q.shape
    f = lb + (1.0 - lb) * _sigmoid(z)
    g = jnp.log(jnp.maximum(f, FORGET_FLOOR))
    kk = (1.0 - lb) * _sigmoid(-z)
    g_hi = g.astype(BF16)
    r1 = g - g_hi.astype(F32)
    g_mid = r1.astype(BF16)
    g_lo = (r1 - g_mid.astype(F32)).astype(BF16)
    b = _dot(tri, g_hi) + _dot(tri, g_mid) + _dot(tri, g_lo)
    b_end = jnp.where(fwd, b[C - 1:C, :], b[0:1, :])
    q_in = (q * jnp.exp(b)).astype(BF16)
    k_out = (kk * jnp.exp(b_end - b)).astype(BF16)
    s_decay = jnp.exp(b_end)
    vb = v.astype(BF16)

    row = lax.broadcasted_iota(jnp.int32, (C, 1), 0)
    rr = lax.broadcasted_iota(jnp.int32, (C, C), 0)
    cc = lax.broadcasted_iota(jnp.int32, (C, C), 1)
    a_mats = [jnp.zeros((C, C), F32) for _ in range(HGRN_HEADS)]
    h = C // 2
    while h >= 1:
        upper = ((row % (2 * h)) >= h).astype(jnp.int32)
        q_rows = upper == fwd.astype(jnp.int32)
        ref = _ref_rows(b, h, fwd)
        ql = jnp.where(q_rows, q * jnp.exp(jnp.where(q_rows, b - ref, 0.0)), 0.0).astype(BF16)
        kl = jnp.where(q_rows, 0.0, kk * jnp.exp(jnp.where(q_rows, 0.0, ref - b))).astype(BF16)
        same_block = (rr // (2 * h)) == (cc // (2 * h))
        for hd in range(HGRN_HEADS):
            sl = slice(hd * HGRN_DIM, (hd + 1) * HGRN_DIM)
            a_mats[hd] = a_mats[hd] + jnp.where(same_block, _dot_nt(ql[:, sl], kl[:, sl]), 0.0)
        h //= 2

    qk = q * kk
    outs = []
    for hd in range(HGRN_HEADS):
        sl = slice(hd * HGRN_DIM, (hd + 1) * HGRN_DIM)
        st = st_ref[hd]
        o = _dot_nt(q_in[:, sl], st.astype(BF16))
        o = o + _dot(a_mats[hd].astype(BF16), vb[:, sl])
        o = o + jnp.sum(qk[:, sl], axis=-1, keepdims=True) * v[:, sl]
        outs.append(o)
        st_ref[hd] = st * s_decay[:, sl] + _dot_tn(vb[:, sl], k_out[:, sl])
    return jnp.concatenate(outs, axis=-1)


def _hgrn_kernel(q_ref, f_ref, v_ref, lb_ref, tri_ref, o_ref, st_ref, *, n_inner):
    d = pl.program_id(0)
    fwd = d == 0

    @pl.when(pl.program_id(2) == 0)
    def _():
        st_ref[...] = jnp.zeros(st_ref.shape, F32)

    lb = lb_ref[0]
    tri = tri_ref[0]

    def body(i, carry):
        ci = jnp.where(fwd, i, n_inner - 1 - i)
        r0 = pl.multiple_of(ci * HGRN_CHUNK, HGRN_CHUNK)
        rows = pl.ds(r0, HGRN_CHUNK)
        o_ref[0, 0, rows, :] = _hgrn_chunk(q_ref[0, rows, :], f_ref[0, rows, :], v_ref[0, rows, :],
                                           lb, tri, st_ref, fwd)
        return carry

    lax.fori_loop(0, n_inner, body, 0)


def _hgrn(hq, hf, hi, lb, B, L):
    W = BRANCH_WIDTH
    rb = _row_tile(L, 512)
    nb = L // rb
    C = HGRN_CHUNK
    tri_f = np.tril(np.ones((C, C), np.float32))
    tri = jnp.asarray(np.stack([tri_f, tri_f.T]), dtype=BF16)
    blk = lambda d, b, c: (b, c + d * (nb - 1 - 2 * c), 0)
    return pl.pallas_call(
        functools.partial(_hgrn_kernel, n_inner=rb // C),
        grid=(2, B, nb),
        in_specs=[pl.BlockSpec((1, rb, W), blk),
                  pl.BlockSpec((1, rb, W), lambda d, b, c: (b, c + d * (nb - 1 - 2 * c), d)),
                  pl.BlockSpec((1, rb, W), blk),
                  pl.BlockSpec((1, 1, W), lambda d, b, c: (d, 0, 0)),
                  pl.BlockSpec((1, C, C), lambda d, b, c: (d, 0, 0))],
        out_specs=pl.BlockSpec((1, 1, rb, W), lambda d, b, c: (d, b, c + d * (nb - 1 - 2 * c), 0)),
        out_shape=jax.ShapeDtypeStruct((2, B, L, W), F32),
        scratch_shapes=[pltpu.VMEM((HGRN_HEADS, HGRN_DIM, HGRN_DIM), F32)],
        compiler_params=_cparams(("parallel", "parallel", "arbitrary")),
        name="hgrn_scan",
    )(hq.reshape(B, L, W), hf.reshape(B, L, 2 * W), hi.reshape(B, L, W), lb[:, None, :], tri)


def _filter_features(L):
    t = jnp.linspace(0.0, 1.0, L, dtype=F32)[:, None]
    w = 2.0 * math.pi * jnp.arange(L, dtype=F32) / L
    bands = jnp.linspace(1e-4, FILTER_BANDS - 1, FILTER_BANDS, dtype=F32)
    ang = w[:, None] * bands[None, :]
    feats = jnp.concatenate([t, jnp.cos(ang), -jnp.sin(ang)], axis=-1)
    feats2 = jnp.concatenate([feats, feats[:1], jnp.flip(feats[1:], axis=0)], axis=0)
    t2 = jnp.concatenate([t, t[:1], jnp.flip(t[1:], axis=0)], axis=0)
    deltas = jnp.abs(jnp.linspace(math.log(FILTER_TARGET) / FILTER_FAST_DECAY,
                                  math.log(FILTER_TARGET) / FILTER_SLOW_DECAY, N_FILTER_CH, dtype=F32))
    return feats2, t2, deltas[None, :]


def _filter_kernel(feat_ref, t_ref, w1_ref, b1_ref, fr_ref, w2_ref, b2_ref, w3f_ref, w3b_ref,
                   df_ref, db_ref, k_ref, *, L):
    freq = fr_ref[...]
    h = jnp.sin(freq * (_dot3(feat_ref[...], w1_ref[...]) + b1_ref[...]))
    h = jnp.sin(freq * (_dot3(h, w2_ref[...]) + b2_ref[...]))
    t = t_ref[...]
    top = _dot3(h[:L], w3f_ref[...]) * jnp.exp(-t[:L] * df_ref[...])
    bot = _dot3(h[L:], w3b_ref[...]) * jnp.exp(-t[L:] * db_ref[...])
    row = lax.broadcasted_iota(jnp.int32, (L, 1), 0)
    bot = jnp.where(row == 0, 0.0, bot)
    norm = jnp.sum(jnp.abs(top), axis=0, keepdims=True) + jnp.sum(jnp.abs(bot), axis=0, keepdims=True)
    k_ref[:L, :] = top / norm
    k_ref[L:, :] = bot / norm


def _hyena_filter(L, consts, w1, b1, freq, w2, b2, w3):
    feats2, t2, deltas = consts
    wb = 256
    nw = HYENA_WIDTH // wb
    full = lambda shape: pl.BlockSpec(shape, lambda o, j: (0,) * len(shape))
    fcol = lambda o, j: (0, o * 2 * nw + j)
    bcol = lambda o, j: (0, o * 2 * nw + nw + j)
    return pl.pallas_call(
        functools.partial(_filter_kernel, L=L),
        grid=(HYENA_ORDER, nw),
        in_specs=[full((2 * L, FILTER_EMB)), full((2 * L, 1)),
                  full((FILTER_EMB, FILTER_HIDDEN)), full((1, FILTER_HIDDEN)), full((1, FILTER_HIDDEN)),
                  full((FILTER_HIDDEN, FILTER_HIDDEN)), full((1, FILTER_HIDDEN)),
                  pl.BlockSpec((FILTER_HIDDEN, wb), fcol), pl.BlockSpec((FILTER_HIDDEN, wb), bcol),
                  pl.BlockSpec((1, wb), fcol), pl.BlockSpec((1, wb), bcol)],
        out_specs=pl.BlockSpec((2 * L, wb), lambda o, j: (0, o * nw + j)),
        out_shape=jax.ShapeDtypeStruct((2 * L, HYENA_ORDER * HYENA_WIDTH), F32),
        compiler_params=_cparams(("parallel", "parallel")),
        name="hyena_filter",
    )(feats2, t2, w1, b1[None, :], freq[None, :], w2, b2[None, :], w3, w3, deltas, deltas)


def _dft_consts(L):
    N = 2 * L
    N2 = FFT_N2
    N1 = N // N2
    k1 = np.arange(N1)
    ang_a = 2.0 * np.pi * np.outer(k1, np.arange(N1)) / N1
    fa_full = np.concatenate([np.cos(ang_a), -np.sin(ang_a)], axis=0)
    fa_half = fa_full[:, :N1 // 2]
    n2 = np.arange(N2)
    kk = k1[:, None, None] + N1 * np.arange(N2)[None, :, None]
    ang_b = 2.0 * np.pi * (kk * n2[None, None, :]) / N
    mr, mi = np.cos(ang_b), -np.sin(ang_b)
    m_fwd = np.concatenate([np.concatenate([mr, -mi], axis=2),
                            np.concatenate([mi, mr], axis=2)], axis=1)
    wr, wi = np.swapaxes(mr, 1, 2), -np.swapaxes(mi, 1, 2)
    m_inv = np.concatenate([np.concatenate([wr, -wi], axis=2),
                            np.concatenate([wi, wr], axis=2)], axis=1)
    ang_c = 2.0 * np.pi * np.outer(np.arange(N1 // 2), k1) / N1
    g_inv = np.concatenate([np.cos(ang_c), -np.sin(ang_c)], axis=1) / N

    def hi_lo(a):
        a = jnp.asarray(a, dtype=F32)
        hi = a.astype(BF16)
        return hi, (a - hi.astype(F32)).astype(BF16)

    return dict(N1=N1, N2=N2, fa_full=hi_lo(fa_full), fa_half=hi_lo(fa_half),
                m_fwd=hi_lo(m_fwd), m_inv=hi_lo(m_inv), g_inv=hi_lo(g_inv))


def _mm3(m_hi, m_lo, x):
    x_hi, x_lo = _split2(x)
    return _dot(m_hi, x_hi) + _dot(m_hi, x_lo) + _dot(m_lo, x_hi)


def _stage_a_kernel(z_ref, fh_ref, fl_ref, a_ref):
    n1 = a_ref.shape[2]
    res = _mm3(fh_ref[...], fl_ref[...], z_ref[0])
    a_ref[0, 0] = res[:n1]
    a_ref[0, 1] = res[n1:]


def _fft_stage_a(z, fa, N1):
    B, nr, cols = z.shape
    cb = min(cols, 4096)
    return pl.pallas_call(
        _stage_a_kernel,
        grid=(B, cols // cb),
        in_specs=[pl.BlockSpec((1, nr, cb), lambda b, j: (b, 0, j)),
                  pl.BlockSpec((2 * N1, nr), lambda b, j: (0, 0)),
                  pl.BlockSpec((2 * N1, nr), lambda b, j: (0, 0))],
        out_specs=pl.BlockSpec((1, 2, N1, cb), lambda b, j: (b, 0, 0, j)),
        out_shape=jax.ShapeDtypeStruct((B, 2, N1, cols), F32),
        compiler_params=_cparams(("parallel", "parallel")),
        name="fft_stage_a",
    )(z, fa[0], fa[1])


def _stage_b_filter_kernel(a_ref, mh_ref, ml_ref, o_ref):
    n2 = a_ref.shape[3]
    a = a_ref[0, :, 0].reshape(2 * n2, a_ref.shape[4])
    res = _mm3(mh_ref[0], ml_ref[0], a)
    o_ref[0, 0] = res[:n2]
    o_ref[1, 0] = res[n2:]


def _filter_spectrum(k2l, dc):
    N1, N2 = dc["N1"], dc["N2"]
    OW = k2l.shape[1]
    a = _fft_stage_a(k2l.reshape(1, N1, N2 * OW), dc["fa_full"], N1)
    a = a.reshape(1, 2, N1, N2, OW)
    wb = 512
    return pl.pallas_call(
        _stage_b_filter_kernel,
        grid=(N1, OW // wb),
        in_specs=[pl.BlockSpec((1, 2, 1, N2, wb), lambda k, j: (0, 0, k, 0, j)),
                  pl.BlockSpec((1, 2 * N2, 2 * N2), lambda k, j: (k, 0, 0)),
                  pl.BlockSpec((1, 2 * N2, 2 * N2), lambda k, j: (k, 0, 0))],
        out_specs=pl.BlockSpec((2, 1, N2, wb), lambda k, j: (0, k, 0, j)),
        out_shape=jax.ShapeDtypeStruct((2, N1, N2, OW), F32),
        compiler_params=_cparams(("parallel", "parallel")),
        name="fft_filter_spectrum",
    )(a, dc["m_fwd"][0], dc["m_fwd"][1])


def _stage_b_kernel(a_ref, h_ref, mfh_ref, mfl_ref, mih_ref, mil_ref, c_ref):
    n2 = a_ref.shape[3]
    a = a_ref[0, :, 0].reshape(2 * n2, a_ref.shape[4])
    z = _mm3(mfh_ref[0], mfl_ref[0], a)
    zr, zi = z[:n2], z[n2:]
    hr, hi = h_ref[0, 0], h_ref[1, 0]
    y = jnp.concatenate([zr * hr - zi * hi, zr * hi + zi * hr], axis=0)
    c = _mm3(mih_ref[0], mil_ref[0], y)
    c_ref[0, 0, 0] = c[:n2]
    c_ref[0, 1, 0] = c[n2:]


def _fft_stage_b(a, hf, order, dc):
    B, _, N1, N2, W = a.shape
    return pl.pallas_call(
        _stage_b_kernel,
        grid=(B, N1),
        in_specs=[pl.BlockSpec((1, 2, 1, N2, W), lambda b, k: (b, 0, k, 0, 0)),
                  pl.BlockSpec((2, 1, N2, W), lambda b, k: (0, k, 0, order)),
                  pl.BlockSpec((1, 2 * N2, 2 * N2), lambda b, k: (k, 0, 0)),
                  pl.BlockSpec((1, 2 * N2, 2 * N2), lambda b, k: (k, 0, 0)),
                  pl.BlockSpec((1, 2 * N2, 2 * N2), lambda b, k: (k, 0, 0)),
                  pl.BlockSpec((1, 2 * N2, 2 * N2), lambda b, k: (k, 0, 0))],
        out_specs=pl.BlockSpec((1, 2, 1, N2, W), lambda b, k: (b, 0, k, 0, 0)),
        out_shape=jax.ShapeDtypeStruct((B, 2, N1, N2, W), F32),
        compiler_params=_cparams(("parallel", "parallel")),
        name="fft_stage_b",
    )(a, hf, dc["m_fwd"][0], dc["m_fwd"][1], dc["m_inv"][0], dc["m_inv"][1])


def _stage_c_kernel(c_ref, z_ref, gate_ref, bias_ref, gh_ref, gl_ref, o_ref):
    n1 = c_ref.shape[2]
    c = c_ref[0].reshape(2 * n1, c_ref.shape[3])
    y = _mm3(gh_ref[...], gl_ref[...], c)
    o_ref[0] = gate_ref[0] * (y + z_ref[0] * bias_ref[...])


def _fft_stage_c(c, z, gate, bias_row, dc):
    B, _, N1, cols = c.shape
    cb = min(cols, 4096)
    nr = N1 // 2
    return pl.pallas_call(
        _stage_c_kernel,
        grid=(B, cols // cb),
        in_specs=[pl.BlockSpec((1, 2, N1, cb), lambda b, j: (b, 0, 0, j)),
                  pl.BlockSpec((1, nr, cb), lambda b, j: (b, 0, j)),
                  pl.BlockSpec((1, nr, cb), lambda b, j: (b, 0, j)),
                  pl.BlockSpec((1, cb), lambda b, j: (0, 0)),
                  pl.BlockSpec((nr, 2 * N1), lambda b, j: (0, 0)),
                  pl.BlockSpec((nr, 2 * N1), lambda b, j: (0, 0))],
        out_specs=pl.BlockSpec((1, nr, cb), lambda b, j: (b, 0, j)),
        out_shape=jax.ShapeDtypeStruct((B, nr, cols), F32),
        compiler_params=_cparams(("parallel", "parallel")),
        name="fft_stage_c",
    )(c, z, gate, bias_row, dc["g_inv"][0], dc["g_inv"][1])


def _long_conv_gated(z, gate, hf, order, skip, dc):
    B, nr, cols = z.shape
    N1, N2 = dc["N1"], dc["N2"]
    W = cols // N2
    a = _fft_stage_a(z, dc["fa_half"], N1).reshape(B, 2, N1, N2, W)
    c = _fft_stage_b(a, hf, order, dc).reshape(B, 2, N1, cols)
    cb = min(cols, 4096)
    bias_row = jnp.tile(skip, cb // W)[None, :]
    return _fft_stage_c(c, z, gate, bias_row, dc)


def _short_conv_kernel(u_ref, prev_ref, next_ref, w_ref, b_ref, v_ref, x1_ref, x2_ref):
    i = pl.program_id(1)
    u = u_ref[0]
    tl = u.shape[0]
    row = lax.broadcasted_iota(jnp.int32, (tl, 1), 0)
    before = jnp.where(i == 0, 0.0, prev_ref[0, 7:8, :])
    after = jnp.where(i == pl.num_programs(1) - 1, 0.0, next_ref[0, 0:1, :])
    up = jnp.where(row == 0, before, pltpu.roll(u, 1, 0))
    un = jnp.where(row == tl - 1, after, pltpu.roll(u, tl - 1, 0))
    w = w_ref[...]
    uc = up * w[0:1] + u * w[1:2] + un * w[2:3] + b_ref[...]
    W = HYENA_WIDTH
    v_ref[0] = uc[:, :W]
    x1_ref[0] = uc[:, W:2 * W]
    x2_ref[0] = uc[:, 2 * W:]


def _short_conv(hy, conv_w, conv_b, B, L):
    W3 = 3 * HYENA_WIDTH
    tl = _row_tile(L, 512)
    nl = L // tl
    u = hy.reshape(B, L, W3)
    r8 = tl // 8
    spec_o = pl.BlockSpec((1, tl, HYENA_WIDTH), lambda b, i: (b, i, 0))
    return pl.pallas_call(
        _short_conv_kernel,
        grid=(B, nl),
        in_specs=[pl.BlockSpec((1, tl, W3), lambda b, i: (b, i, 0)),
                  pl.BlockSpec((1, 8, W3), lambda b, i: (b, jnp.maximum(i * r8 - 1, 0), 0)),
                  pl.BlockSpec((1, 8, W3), lambda b, i: (b, jnp.minimum((i + 1) * r8, L // 8 - 1), 0)),
                  pl.BlockSpec((3, W3), lambda b, i: (0, 0)),
                  pl.BlockSpec((1, W3), lambda b, i: (0, 0))],
        out_specs=[spec_o, spec_o, spec_o],
        out_shape=[jax.ShapeDtypeStruct((B, L, HYENA_WIDTH), F32)] * 3,
        compiler_params=_cparams(("parallel", "parallel")),
        name="hyena_short_conv",
    )(u, u, u, conv_w, conv_b[None, :])


def _hyena(hy, hf, conv_w, conv_b, skip, dc, B, L):
    v, x1, x2 = _short_conv(hy, conv_w, conv_b, B, L)
    shape = (B, dc["N1"] // 2, dc["N2"] * HYENA_WIDTH)
    z = _long_conv_gated(v.reshape(shape), x1.reshape(shape), hf, 0, skip[0], dc)
    y = _long_conv_gated(z, x2.reshape(shape), hf, 1, skip[1], dc)
    return y.reshape(B * L, HYENA_WIDTH)


def _merge_kernel(x_ref, oa_ref, oh_ref, hg_ref, oc_ref, gl_ref, hgn_ref, wb_ref, wo_ref, g_ref, b_ref,
                  out_ref):
    o = oh_ref[0] + oh_ref[1]
    parts = []
    for hd in range(HGRN_HEADS):
        oh = o[:, hd * HGRN_DIM:(hd + 1) * HGRN_DIM]
        ms = jnp.mean(oh * oh, axis=-1, keepdims=True)
        parts.append(oh * lax.rsqrt(ms + RMS_EPS))
    o_h = jnp.concatenate(parts, axis=-1) * hgn_ref[...] * _silu(hg_ref[...])
    gl = gl_ref[...]
    branches = (oa_ref[...], o_h, oc_ref[...])
    merged = None
    for j, ob in enumerate(branches):
        term = _sigmoid(gl[:, j * D_MODEL:(j + 1) * D_MODEL]) * _dot(ob.astype(BF16), wb_ref[j])
        merged = term if merged is None else merged + term
    mix = _dot(merged.astype(BF16), wo_ref[...])
    out_ref[...] = _layer_norm(ALPHA * x_ref[...] + mix, g_ref[...], b_ref[...])


def _merge(x2, o_a, o_dir, hg, o_c, gl, hgrn_gain, wb_bf16, wo_bf16, ln_g, ln_b):
    T = x2.shape[0]
    tm = _row_tile(T, 512)
    W = BRANCH_WIDTH
    row = lambda w: pl.BlockSpec((tm, w), lambda i: (i, 0))
    return pl.pallas_call(
        _merge_kernel,
        grid=(T // tm,),
        in_specs=[row(D_MODEL), row(W), pl.BlockSpec((2, tm, W), lambda i: (0, i, 0)), row(W), row(W),
                  row(N_BRANCH * D_MODEL),
                  pl.BlockSpec((1, W), lambda i: (0, 0)),
                  pl.BlockSpec((N_BRANCH, W, D_MODEL), lambda i: (0, 0, 0)),
                  pl.BlockSpec((D_MODEL, D_MODEL), lambda i: (0, 0)),
                  pl.BlockSpec((1, D_MODEL), lambda i: (0, 0)),
                  pl.BlockSpec((1, D_MODEL), lambda i: (0, 0))],
        out_specs=row(D_MODEL),
        out_shape=jax.ShapeDtypeStruct((T, D_MODEL), F32),
        compiler_params=_cparams(("parallel",)),
        name="merge_out_ln",
    )(x2, o_a, o_dir.reshape(2, T, W), hg, o_c, gl, jnp.tile(hgrn_gain, HGRN_HEADS)[None, :],
      wb_bf16, wo_bf16, ln_g[None, :], ln_b[None, :])


def _ffn_kernel(x_ref, wg_ref, wu_ref, wd_ref, g_ref, b_ref, out_ref, acc_ref):
    j = pl.program_id(1)
    x = x_ref[...]
    xb = x.astype(BF16)
    hmid = _silu(_dot(xb, wg_ref[...])) * _dot(xb, wu_ref[...])
    contrib = _dot(hmid.astype(BF16), wd_ref[...])

    @pl.when(j == 0)
    def _():
        acc_ref[...] = contrib

    @pl.when(j > 0)
    def _():
        acc_ref[...] += contrib

    @pl.when(j == pl.num_programs(1) - 1)
    def _():
        out_ref[...] = _layer_norm(ALPHA * x + acc_ref[...], g_ref[...], b_ref[...])


def _ffn_dense(x2, w_gu, w_down, ln_g, ln_b):
    T = x2.shape[0]
    tm = _row_tile(T, 512)
    ft = 1408
    nf = D_FF_DENSE // ft
    return pl.pallas_call(
        _ffn_kernel,
        grid=(T // tm, nf),
        in_specs=[pl.BlockSpec((tm, D_MODEL), lambda i, j: (i, 0)),
                  pl.BlockSpec((D_MODEL, ft), lambda i, j: (0, j)),
                  pl.BlockSpec((D_MODEL, ft), lambda i, j: (0, j + nf)),
                  pl.BlockSpec((ft, D_MODEL), lambda i, j: (j, 0)),
                  pl.BlockSpec((1, D_MODEL), lambda i, j: (0, 0)),
                  pl.BlockSpec((1, D_MODEL), lambda i, j: (0, 0))],
        out_specs=pl.BlockSpec((tm, D_MODEL), lambda i, j: (i, 0)),
        out_shape=jax.ShapeDtypeStruct((T, D_MODEL), F32),
        scratch_shapes=[pltpu.VMEM((tm, D_MODEL), F32)],
        compiler_params=_cparams(("parallel", "arbitrary")),
        name="ffn_dense_ln",
    )(x2, w_gu, w_gu, w_down, ln_g[None, :], ln_b[None, :])


def _moe_kernel(x_ref, r_ref, wg_ref, wu_ref, wd_ref, g_ref, b_ref, out_ref, acc_ref, comb_ref):
    e = pl.program_id(1)
    j = pl.program_id(2)
    x = x_ref[...]
    first = jnp.logical_and(e == 0, j == 0)

    @pl.when(first)
    def _():
        logits = _dot3(x, r_ref[...])
        lane = lax.broadcasted_iota(jnp.int32, logits.shape, 1)
        m1 = jnp.max(logits, axis=-1, keepdims=True)
        i1 = jnp.min(jnp.where(logits == m1, lane, N_EXPERTS), axis=-1, keepdims=True)
        rest = jnp.where(lane == i1, -jnp.inf, logits)
        m2 = jnp.max(rest, axis=-1, keepdims=True)
        i2 = jnp.min(jnp.where(rest == m2, lane, N_EXPERTS), axis=-1, keepdims=True)
        e2 = jnp.exp(m2 - m1)
        w1 = 1.0 / (1.0 + e2)
        w2 = e2 / (1.0 + e2)
        comb_ref[...] = jnp.where(lane == i1, w1, 0.0) + jnp.where(lane == i2, w2, 0.0)
        acc_ref[...] = jnp.zeros(acc_ref.shape, F32)

    xb = x.astype(BF16)
    hmid = _silu(_dot(xb, wg_ref[0])) * _dot(xb, wu_ref[0])
    comb = comb_ref[...]
    lane = lax.broadcasted_iota(jnp.int32, comb.shape, 1)
    w_e = jnp.sum(jnp.where(lane == e, comb, 0.0), axis=-1, keepdims=True)
    acc_ref[...] += w_e * _dot(hmid.astype(BF16), wd_ref[0])

    last = jnp.logical_and(e == pl.num_programs(1) - 1, j == pl.num_programs(2) - 1)

    @pl.when(last)
    def _():
        out_ref[...] = _layer_norm(ALPHA * x + acc_ref[...], g_ref[...], b_ref[...])


def _ffn_moe(x2, router, w_gu, w_down, ln_g, ln_b):
    T = x2.shape[0]
    tm = _row_tile(T, 512)
    ft = 896
    nf = D_FF_EXPERT // ft
    return pl.pallas_call(
        _moe_kernel,
        grid=(T // tm, N_EXPERTS, nf),
        in_specs=[pl.BlockSpec((tm, D_MODEL), lambda i, e, j: (i, 0)),
                  pl.BlockSpec((D_MODEL, N_EXPERTS), lambda i, e, j: (0, 0)),
                  pl.BlockSpec((1, D_MODEL, ft), lambda i, e, j: (e, 0, j)),
                  pl.BlockSpec((1, D_MODEL, ft), lambda i, e, j: (e, 0, j + nf)),
                  pl.BlockSpec((1, ft, D_MODEL), lambda i, e, j: (e, j, 0)),
                  pl.BlockSpec((1, D_MODEL), lambda i, e, j: (0, 0)),
                  pl.BlockSpec((1, D_MODEL), lambda i, e, j: (0, 0))],
        out_specs=pl.BlockSpec((tm, D_MODEL), lambda i, e, j: (i, 0)),
        out_shape=jax.ShapeDtypeStruct((T, D_MODEL), F32),
        scratch_shapes=[pltpu.VMEM((tm, D_MODEL), F32), pltpu.VMEM((tm, N_EXPERTS), F32)],
        compiler_params=_cparams(("parallel", "arbitrary", "arbitrary")),
        name="ffn_moe_ln",
    )(x2, router, w_gu, w_gu, w_down, ln_g[None, :], ln_b[None, :])


def _trunk(x, p):
    B, L, _ = x.shape
    cos_t, sin_t = _rope_tables(L)
    filt_consts = _filter_features(L)
    dc = _dft_consts(L)
    x2 = x.reshape(B * L, D_MODEL)
    for l in range(DEPTH):
        aq, akv, hq, hf, hi, hg, hy, gl = _in_proj(x2, p["w_in"][l])
        qn, kn, vn = _attn_prep(aq, akv, cos_t, sin_t, p["q_norm_g"][l], p["k_norm_g"][l], L)
        o_a = _flash_attention(qn, kn, vn, B, L)
        o_dir = _hgrn(hq, hf, hi, p["lower_bounds"][l], B, L)
        k2l = _hyena_filter(L, filt_consts, p["filt_w1"][l], p["filt_b1"][l], p["filt_freq"][l],
                            p["filt_w2"][l], p["filt_b2"][l], p["filt_w3"][l])
        spec = _filter_spectrum(k2l, dc)
        o_c = _hyena(hy, spec, p["conv_w"][l], p["conv_b"][l], p["hyena_skip"][l], dc, B, L)
        x2 = _merge(x2, o_a, o_dir, hg, o_c, gl, p["hgrn_norm_g"][l], p["w_branch"][l], p["w_out"][l],
                    p["ln1_g"][l], p["ln1_b"][l])
        if l % 2 == 0:
            x2 = _ffn_dense(x2, p["ffn_w_gu"][l // 2], p["ffn_w_down"][l // 2], p["ln2_g"][l], p["ln2_b"][l])
        else:
            x2 = _ffn_moe(x2, p["router_w"][l // 2], p["expert_w_gu"][l // 2], p["expert_w_down"][l // 2],
                          p["ln2_g"][l], p["ln2_b"][l])
    return x2.reshape(B, L, D_MODEL)


def _in_proj_weight(w_in):
    return w_in.astype(BF16)


def kernel(x_prompt, x_sample, w_in, q_norm_g, k_norm_g, hgrn_lb, hgrn_norm_g, conv_w, conv_b, filt_w1, filt_b1, filt_freq, filt_w2, filt_b2, filt_w3, hyena_skip, w_branch, w_out, ln1_g, ln1_b, ln2_g, ln2_b, ffn_w_gu, ffn_w_down, router_w, expert_w_gu, expert_w_down):
    s = jax.nn.softmax(hgrn_lb.astype(F32), axis=0)
    p = dict(
        w_in=_in_proj_weight(w_in), q_norm_g=q_norm_g, k_norm_g=k_norm_g,
        lower_bounds=jnp.cumsum(s, axis=0) - s[:1], hgrn_norm_g=hgrn_norm_g,
        conv_w=conv_w, conv_b=conv_b, filt_w1=filt_w1, filt_b1=filt_b1, filt_freq=filt_freq,
        filt_w2=filt_w2, filt_b2=filt_b2, filt_w3=filt_w3, hyena_skip=hyena_skip,
        w_branch=w_branch.astype(BF16), w_out=w_out.astype(BF16),
        ln1_g=ln1_g, ln1_b=ln1_b, ln2_g=ln2_g, ln2_b=ln2_b,
        ffn_w_gu=ffn_w_gu.astype(BF16), ffn_w_down=ffn_w_down.astype(BF16), router_w=router_w,
        expert_w_gu=expert_w_gu.astype(BF16), expert_w_down=expert_w_down.astype(BF16))
    return (_trunk(x_prompt, p), _trunk(x_sample, p))
```

```python
import functools
import math

import numpy as np
import jax
import jax.numpy as jnp
from jax import lax
from jax.experimental import pallas as pl
from jax.experimental.pallas import tpu as pltpu

F32 = jnp.float32
BF16 = jnp.bfloat16

D_MODEL = 1024
DEPTH = 2
GRID_W = 64
BRANCH_WIDTH = 512
ATTN_HEADS = 8
ATTN_KV_HEADS = 2
ATTN_GROUP = ATTN_HEADS // ATTN_KV_HEADS
HEAD_DIM = 64
KV_WIDTH = ATTN_KV_HEADS * HEAD_DIM
ROPE_AXIS_DIM = HEAD_DIM // 2
ROPE_HALF = ROPE_AXIS_DIM // 2
ROPE_THETA = 10000.0
HGRN_HEADS = 4
HGRN_DIM = 128
HGRN_CHUNK = 64
FORGET_FLOOR = 1e-30
HYENA_WIDTH = BRANCH_WIDTH
HYENA_ORDER = 2
FILTER_EMB = 33
FILTER_BANDS = (FILTER_EMB - 1) // 2
FILTER_HIDDEN = 64
FILTER_FAST_DECAY = 0.3
FILTER_SLOW_DECAY = 1.5
FILTER_TARGET = 1e-2
N_FILTER_CH = HYENA_ORDER * 2 * HYENA_WIDTH
N_BRANCH = 3
D_FF_DENSE = 2816
N_EXPERTS = 8
D_FF_EXPERT = 3584
ALPHA = (2 * DEPTH) ** 0.25
LN_EPS = 1e-5
RMS_EPS = 1e-6

SEG_WIDTHS = (BRANCH_WIDTH,
              2 * KV_WIDTH,
              BRANCH_WIDTH,
              2 * BRANCH_WIDTH,
              BRANCH_WIDTH,
              BRANCH_WIDTH,
              3 * HYENA_WIDTH,
              N_BRANCH * D_MODEL)
IN_WIDTH = sum(SEG_WIDTHS)

LANES = 128
FFT_N2 = 128
VMEM_LIMIT = 56 * 1024 * 1024


def _cparams(sem):
    return pltpu.CompilerParams(dimension_semantics=sem, vmem_limit_bytes=VMEM_LIMIT)


def _dot(a, b):
    return jnp.dot(a, b, preferred_element_type=F32)


def _dot_nt(a, b):
    return lax.dot_general(a, b, (((1,), (1,)), ((), ())), preferred_element_type=F32)


def _dot_tn(a, b):
    return lax.dot_general(a, b, (((0,), (0,)), ((), ())), preferred_element_type=F32)


def _split2(a):
    hi = a.astype(BF16)
    lo = (a - hi.astype(F32)).astype(BF16)
    return hi, lo


def _dot3(a, b):
    ah, al = _split2(a)
    bh, bl = _split2(b)
    return _dot(ah, bh) + _dot(al, bh) + _dot(ah, bl)


def _layer_norm(y, g, b):
    mu = jnp.mean(y, axis=-1, keepdims=True)
    yc = y - mu
    var = jnp.mean(yc * yc, axis=-1, keepdims=True)
    return yc * lax.rsqrt(var + LN_EPS) * g + b


def _sigmoid(x):
    return 1.0 / (1.0 + jnp.exp(-x))


def _silu(x):
    return x * _sigmoid(x)


def _row_tile(n, want):
    t = min(n, want)
    assert n % t == 0
    return t


def _in_proj_kernel(x_ref, w_ref, *out_refs):
    x = x_ref[...].astype(BF16)
    off = 0
    for o_ref, width in zip(out_refs, SEG_WIDTHS):
        o_ref[...] = _dot(x, w_ref[:, off:off + width]).astype(o_ref.dtype)
        off += width


def _in_proj(x2, w_bf16):
    T = x2.shape[0]
    tm = _row_tile(T, 256)
    return pl.pallas_call(
        _in_proj_kernel,
        grid=(T // tm,),
        in_specs=[pl.BlockSpec((tm, D_MODEL), lambda i: (i, 0)),
                  pl.BlockSpec((D_MODEL, IN_WIDTH), lambda i: (0, 0), pipeline_mode=pl.Buffered(1))],
        out_specs=[pl.BlockSpec((tm, w), lambda i: (i, 0)) for w in SEG_WIDTHS],
        out_shape=[jax.ShapeDtypeStruct((T, w), F32) for w in SEG_WIDTHS],
        compiler_params=_cparams(("parallel",)),
        name="in_proj",
    )(x2, w_bf16)


def _rope_tables(L):
    rows = L // GRID_W
    row = jnp.broadcast_to(jnp.arange(rows, dtype=F32)[:, None], (rows, GRID_W)).reshape(-1)
    col = jnp.broadcast_to(jnp.arange(GRID_W, dtype=F32)[None, :], (rows, GRID_W)).reshape(-1)
    inv = jnp.power(ROPE_THETA, -2.0 * jnp.arange(ROPE_HALF, dtype=F32) / ROPE_AXIS_DIM)
    a0, a1 = row[:, None] * inv, col[:, None] * inv
    c0, s0, c1, s1 = jnp.cos(a0), jnp.sin(a0), jnp.cos(a1), jnp.sin(a1)
    cos_h = jnp.concatenate([c0, c0, c1, c1], axis=-1)
    sin_h = jnp.concatenate([-s0, s0, -s1, s1], axis=-1)
    return jnp.tile(cos_h, (1, ATTN_HEADS)), jnp.tile(sin_h, (1, ATTN_HEADS))


def _head_block_ones(width):
    idx = np.arange(width) // HEAD_DIM
    return jnp.asarray(idx[:, None] == idx[None, :], dtype=BF16)


def _norm_rope(x, gain, cos, sin, ones_bd):
    width = x.shape[-1]
    sq_hi, sq_lo = _split2(x * x)
    ms = (_dot(sq_hi, ones_bd) + _dot(sq_lo, ones_bd)) * (1.0 / HEAD_DIM)
    xn = x * lax.rsqrt(ms + RMS_EPS) * gain
    lane = lax.broadcasted_iota(jnp.int32, xn.shape, 1)
    first_half = (lane % ROPE_AXIS_DIM) < ROPE_HALF
    swapped = jnp.where(first_half, pltpu.roll(xn, width - ROPE_HALF, 1), pltpu.roll(xn, ROPE_HALF, 1))
    return xn * cos + swapped * sin


LOG2E = 1.4426950408889634
V_EXT = 2 * HEAD_DIM
FLASH_SKEW = 4


def _attn_prep_kernel(aq_ref, akv_ref, cos_ref, sin_ref, qg_ref, kg_ref, bdq_ref, bdk_ref,
                      q_out, kt_out, v_out):
    cos, sin = cos_ref[...], sin_ref[...]
    q = _norm_rope(aq_ref[...], qg_ref[...], cos, sin, bdq_ref[...])
    q_out[...] = (q * (LOG2E * HEAD_DIM ** -0.5)).astype(BF16)
    kv = akv_ref[...]
    k = _norm_rope(kv[:, :KV_WIDTH], kg_ref[...], cos[:, :KV_WIDTH], sin[:, :KV_WIDTH], bdk_ref[...])
    kt_out[0] = k.T.astype(BF16)
    ones = jnp.ones((kv.shape[0], HEAD_DIM), F32)
    v_ext = [kv[:, KV_WIDTH + h * HEAD_DIM:KV_WIDTH + (h + 1) * HEAD_DIM] if part == 0 else ones
             for h in range(ATTN_KV_HEADS) for part in range(2)]
    v_out[...] = jnp.concatenate(v_ext, axis=-1).astype(BF16)


def _attn_prep(aq, akv, cos_t, sin_t, q_gain, k_gain, B, L):
    T = aq.shape[0]
    tm = _row_tile(L, 512)
    nl = L // tm
    qg = jnp.tile(q_gain, ATTN_HEADS)[None, :]
    kg = jnp.tile(k_gain, ATTN_KV_HEADS)[None, :]
    const = lambda i: (0, 0)
    return pl.pallas_call(
        _attn_prep_kernel,
        grid=(T // tm,),
        in_specs=[pl.BlockSpec((tm, BRANCH_WIDTH), lambda i: (i, 0)),
                  pl.BlockSpec((tm, 2 * KV_WIDTH), lambda i: (i, 0)),
                  pl.BlockSpec((tm, BRANCH_WIDTH), lambda i: (i % nl, 0)),
                  pl.BlockSpec((tm, BRANCH_WIDTH), lambda i: (i % nl, 0)),
                  pl.BlockSpec((1, BRANCH_WIDTH), const),
                  pl.BlockSpec((1, KV_WIDTH), const),
                  pl.BlockSpec((BRANCH_WIDTH, BRANCH_WIDTH), const),
                  pl.BlockSpec((KV_WIDTH, KV_WIDTH), const)],
        out_specs=[pl.BlockSpec((tm, BRANCH_WIDTH), lambda i: (i, 0)),
                   pl.BlockSpec((1, KV_WIDTH, tm), lambda i: (i // nl, 0, i % nl)),
                   pl.BlockSpec((tm, ATTN_KV_HEADS * V_EXT), lambda i: (i, 0))],
        out_shape=[jax.ShapeDtypeStruct((T, BRANCH_WIDTH), BF16),
                   jax.ShapeDtypeStruct((B, KV_WIDTH, L), BF16),
                   jax.ShapeDtypeStruct((T, ATTN_KV_HEADS * V_EXT), BF16)],
        compiler_params=_cparams(("parallel",)),
        name="attn_prep",
    )(aq, akv, cos_t, sin_t, qg, kg, _head_block_ones(BRANCH_WIDTH), _head_block_ones(KV_WIDTH))


def _flash_kernel(q_ref, kt_ref, v_ref, o_ref, qs_ref, m_ref, acc_ref, *, tq, rs):
    kv = pl.program_id(2)

    @pl.when(kv == 0)
    def _():
        for h in range(ATTN_HEADS):
            g = h % ATTN_GROUP
            qs_ref[h // ATTN_GROUP, g * tq:(g + 1) * tq, :] = q_ref[0, :, h * HEAD_DIM:(h + 1) * HEAD_DIM]
        m_ref[...] = jnp.full(m_ref.shape, -jnp.inf, F32)
        acc_ref[...] = jnp.zeros(acc_ref.shape, F32)

    kts = [kt_ref[0, kh * HEAD_DIM:(kh + 1) * HEAD_DIM, :] for kh in range(ATTN_KV_HEADS)]
    vs = [v_ref[0, :, kh * V_EXT:(kh + 1) * V_EXT] for kh in range(ATTN_KV_HEADS)]
    tasks = [(kh, slice(sb * rs, (sb + 1) * rs))
             for sb in range(ATTN_GROUP * tq // rs) for kh in range(ATTN_KV_HEADS)]

    def scores(t):
        kh, rows = tasks[t]
        return _dot(qs_ref[kh, rows, :], kts[kh])

    pending = {t: scores(t) for t in range(min(FLASH_SKEW, len(tasks)))}
    for t, (kh, rows) in enumerate(tasks):
        s = pending.pop(t)
        if t + FLASH_SKEW < len(tasks):
            pending[t + FLASH_SKEW] = scores(t + FLASH_SKEW)
        m_prev = m_ref[kh, rows, :]
        m_new = jnp.maximum(m_prev, jnp.max(s, axis=-1, keepdims=True))
        p = jnp.exp2(s - m_new[:, 0:1])
        acc_ref[kh, rows, :] = jnp.exp2(m_prev - m_new) * acc_ref[kh, rows, :] + _dot(p.astype(BF16), vs[kh])
        m_ref[kh, rows, :] = m_new

    @pl.when(kv == pl.num_programs(2) - 1)
    def _():
        for h in range(ATTN_HEADS):
            kh, g = h // ATTN_GROUP, h % ATTN_GROUP
            a = acc_ref[kh, g * tq:(g + 1) * tq, :]
            o_ref[0, :, h * HEAD_DIM:(h + 1) * HEAD_DIM] = a[:, :HEAD_DIM] / a[:, HEAD_DIM:]


def _flash_attention(qn, kt, v_ext, B, L):
    tq = _row_tile(L, 256)
    tk = _row_tile(L, 512)
    rs = min(128, ATTN_GROUP * tq)
    out = pl.pallas_call(
        functools.partial(_flash_kernel, tq=tq, rs=rs),
        grid=(B, L // tq, L // tk),
        in_specs=[pl.BlockSpec((1, tq, BRANCH_WIDTH), lambda b, i, j: (b, i, 0)),
                  pl.BlockSpec((1, KV_WIDTH, tk), lambda b, i, j: (b, 0, j)),
                  pl.BlockSpec((1, tk, ATTN_KV_HEADS * V_EXT), lambda b, i, j: (b, j, 0))],
        out_specs=pl.BlockSpec((1, tq, BRANCH_WIDTH), lambda b, i, j: (b, i, 0)),
        out_shape=jax.ShapeDtypeStruct((B, L, BRANCH_WIDTH), F32),
        scratch_shapes=[pltpu.VMEM((ATTN_KV_HEADS, ATTN_GROUP * tq, HEAD_DIM), BF16),
                        pltpu.VMEM((ATTN_KV_HEADS, ATTN_GROUP * tq, LANES), F32),
                        pltpu.VMEM((ATTN_KV_HEADS, ATTN_GROUP * tq, V_EXT), F32)],
        compiler_params=_cparams(("parallel", "parallel", "arbitrary")),
        name="flash_attention",
    )(qn.reshape(B, L, BRANCH_WIDTH), kt, v_ext.reshape(B, L, ATTN_KV_HEADS * V_EXT))
    return out.reshape(B * L, BRANCH_WIDTH)


def _ref_rows(b, h, fwd):
    C, W = b.shape
    two_h = 2 * h
    if two_h >= 8:
        n = C // two_h
        b3 = b.reshape(n, two_h, W)
        r = jnp.where(fwd, b3[:, h - 1:h, :], b3[:, h:h + 1, :])
        return jnp.broadcast_to(r, (n, two_h, W)).reshape(C, W)
    b3 = b.reshape(C // 8, 8, W)
    pos8 = lax.broadcasted_iota(jnp.int32, (1, 8, 1), 1)
    out = None
    for blk in range(8 // two_h):
        base = blk * two_h
        r = jnp.where(fwd, b3[:, base + h - 1:base + h, :], b3[:, base + h:base + h + 1, :])
        r = jnp.broadcast_to(r, b3.shape)
        out = r if out is None else jnp.where((pos8 // two_h) == blk, r, out)
    return out.reshape(C, W)


def _hgrn_chunk(q, z, v, lb, tri, st_ref, fwd):
    C, W = q.shape
    f = lb + (1.0 - lb) * _sigmoid(z)
    g = jnp.log(jnp.maximum(f, FORGET_FLOOR))
    kk = (1.0 - lb) * _sigmoid(-z)
    g_hi = g.astype(BF16)
    r1 = g - g_hi.astype(F32)
    g_mid = r1.astype(BF16)
    g_lo = (r1 - g_mid.astype(F32)).astype(BF16)
    b = _dot(tri, g_hi) + _dot(tri, g_mid) + _dot(tri, g_lo)
    b_end = jnp.where(fwd, b[C - 1:C, :], b[0:1, :])
    q_in = (q * jnp.exp(b)).astype(BF16)
    k_out = (kk * jnp.exp(b_end - b)).astype(BF16)
    s_decay = jnp.exp(b_end)
    vb = v.astype(BF16)

    row = lax.broadcasted_iota(jnp.int32, (C, 1), 0)
    rr = lax.broadcasted_iota(jnp.int32, (C, C), 0)
    cc = lax.broadcasted_iota(jnp.int32, (C, C), 1)
    a_mats = [jnp.zeros((C, C), F32) for _ in range(HGRN_HEADS)]
    h = C // 2
    while h >= 1:
        upper = ((row % (2 * h)) >= h).astype(jnp.int32)
        q_rows = upper == fwd.astype(jnp.int32)
        ref = _ref_rows(b, h, fwd)
        ql = jnp.where(q_rows, q * jnp.exp(jnp.where(q_rows, b - ref, 0.0)), 0.0).astype(BF16)
        kl = jnp.where(q_rows, 0.0, kk * jnp.exp(jnp.where(q_rows, 0.0, ref - b))).astype(BF16)
        same_block = (rr // (2 * h)) == (cc // (2 * h))
        for hd in range(HGRN_HEADS):
            sl = slice(hd * HGRN_DIM, (hd + 1) * HGRN_DIM)
            a_mats[hd] = a_mats[hd] + jnp.where(same_block, _dot_nt(ql[:, sl], kl[:, sl]), 0.0)
        h //= 2

    qk = q * kk
    outs = []
    for hd in range(HGRN_HEADS):
        sl = slice(hd * HGRN_DIM, (hd + 1) * HGRN_DIM)
        st = st_ref[hd]
        o = _dot_nt(q_in[:, sl], st.astype(BF16))
        o = o + _dot(a_mats[hd].astype(BF16), vb[:, sl])
        o = o + jnp.sum(qk[:, sl], axis=-1, keepdims=True) * v[:, sl]
        outs.append(o)
        st_ref[hd] = st * s_decay[:, sl] + _dot_tn(vb[:, sl], k_out[:, sl])
    return jnp.concatenate(outs, axis=-1)


def _hgrn_kernel(q_ref, f_ref, v_ref, lb_ref, tri_ref, o_ref, st_ref, *, n_inner):
    d = pl.program_id(0)
    fwd = d == 0

    @pl.when(pl.program_id(2) == 0)
    def _():
        st_ref[...] = jnp.zeros(st_ref.shape, F32)

    lb = lb_ref[0]
    tri = tri_ref[0]

    def body(i, carry):
        ci = jnp.where(fwd, i, n_inner - 1 - i)
        r0 = pl.multiple_of(ci * HGRN_CHUNK, HGRN_CHUNK)
        rows = pl.ds(r0, HGRN_CHUNK)
        o_ref[0, 0, rows, :] = _hgrn_chunk(q_ref[0, rows, :], f_ref[0, rows, :], v_ref[0, rows, :],
                                           lb, tri, st_ref, fwd)
        return carry

    lax.fori_loop(0, n_inner, body, 0)


def _hgrn(hq, hf, hi, lb, B, L):
    W = BRANCH_WIDTH
    rb = _row_tile(L, 512)
    nb = L // rb
    C = HGRN_CHUNK
    tri_f = np.tril(np.ones((C, C), np.float32))
    tri = jnp.asarray(np.stack([tri_f, tri_f.T]), dtype=BF16)
    blk = lambda d, b, c: (b, c + d * (nb - 1 - 2 * c), 0)
    return pl.pallas_call(
        functools.partial(_hgrn_kernel, n_inner=rb // C),
        grid=(2, B, nb),
        in_specs=[pl.BlockSpec((1, rb, W), blk),
                  pl.BlockSpec((1, rb, W), lambda d, b, c: (b, c + d * (nb - 1 - 2 * c), d)),
                  pl.BlockSpec((1, rb, W), blk),
                  pl.BlockSpec((1, 1, W), lambda d, b, c: (d, 0, 0)),
                  pl.BlockSpec((1, C, C), lambda d, b, c: (d, 0, 0))],
        out_specs=pl.BlockSpec((1, 1, rb, W), lambda d, b, c: (d, b, c + d * (nb - 1 - 2 * c), 0)),
        out_shape=jax.ShapeDtypeStruct((2, B, L, W), F32),
        scratch_shapes=[pltpu.VMEM((HGRN_HEADS, HGRN_DIM, HGRN_DIM), F32)],
        compiler_params=_cparams(("parallel", "parallel", "arbitrary")),
        name="hgrn_scan",
    )(hq.reshape(B, L, W), hf.reshape(B, L, 2 * W), hi.reshape(B, L, W), lb[:, None, :], tri)


def _filter_features(L):
    t = jnp.linspace(0.0, 1.0, L, dtype=F32)[:, None]
    w = 2.0 * math.pi * jnp.arange(L, dtype=F32) / L
    bands = jnp.linspace(1e-4, FILTER_BANDS - 1, FILTER_BANDS, dtype=F32)
    ang = w[:, None] * bands[None, :]
    feats = jnp.concatenate([t, jnp.cos(ang), -jnp.sin(ang)], axis=-1)
    feats2 = jnp.concatenate([feats, feats[:1], jnp.flip(feats[1:], axis=0)], axis=0)
    t2 = jnp.concatenate([t, t[:1], jnp.flip(t[1:], axis=0)], axis=0)
    deltas = jnp.abs(jnp.linspace(math.log(FILTER_TARGET) / FILTER_FAST_DECAY,
                                  math.log(FILTER_TARGET) / FILTER_SLOW_DECAY, N_FILTER_CH, dtype=F32))
    return feats2, t2, deltas[None, :]


def _filter_kernel(feat_ref, t_ref, w1_ref, b1_ref, fr_ref, w2_ref, b2_ref, w3f_ref, w3b_ref,
                   df_ref, db_ref, k_ref, *, L):
    freq = fr_ref[...]
    h = jnp.sin(freq * (_dot3(feat_ref[...], w1_ref[...]) + b1_ref[...]))
    h = jnp.sin(freq * (_dot3(h, w2_ref[...]) + b2_ref[...]))
    t = t_ref[...]
    top = _dot3(h[:L], w3f_ref[...]) * jnp.exp(-t[:L] * df_ref[...])
    bot = _dot3(h[L:], w3b_ref[...]) * jnp.exp(-t[L:] * db_ref[...])
    row = lax.broadcasted_iota(jnp.int32, (L, 1), 0)
    bot = jnp.where(row == 0, 0.0, bot)
    norm = jnp.sum(jnp.abs(top), axis=0, keepdims=True) + jnp.sum(jnp.abs(bot), axis=0, keepdims=True)
    k_ref[:L, :] = top / norm
    k_ref[L:, :] = bot / norm


def _hyena_filter(L, consts, w1, b1, freq, w2, b2, w3):
    feats2, t2, deltas = consts
    wb = 256
    nw = HYENA_WIDTH // wb
    full = lambda shape: pl.BlockSpec(shape, lambda o, j: (0,) * len(shape))
    fcol = lambda o, j: (0, o * 2 * nw + j)
    bcol = lambda o, j: (0, o * 2 * nw + nw + j)
    return pl.pallas_call(
        functools.partial(_filter_kernel, L=L),
        grid=(HYENA_ORDER, nw),
        in_specs=[full((2 * L, FILTER_EMB)), full((2 * L, 1)),
                  full((FILTER_EMB, FILTER_HIDDEN)), full((1, FILTER_HIDDEN)), full((1, FILTER_HIDDEN)),
                  full((FILTER_HIDDEN, FILTER_HIDDEN)), full((1, FILTER_HIDDEN)),
                  pl.BlockSpec((FILTER_HIDDEN, wb), fcol), pl.BlockSpec((FILTER_HIDDEN, wb), bcol),
                  pl.BlockSpec((1, wb), fcol), pl.BlockSpec((1, wb), bcol)],
        out_specs=pl.BlockSpec((2 * L, wb), lambda o, j: (0, o * nw + j)),
        out_shape=jax.ShapeDtypeStruct((2 * L, HYENA_ORDER * HYENA_WIDTH), F32),
        compiler_params=_cparams(("parallel", "parallel")),
        name="hyena_filter",
    )(feats2, t2, w1, b1[None, :], freq[None, :], w2, b2[None, :], w3, w3, deltas, deltas)


def _dft_consts(L):
    N = 2 * L
    N2 = FFT_N2
    N1 = N // N2
    k1 = np.arange(N1)
    ang_a = 2.0 * np.pi * np.outer(k1, np.arange(N1)) / N1
    fa_full = np.concatenate([np.cos(ang_a), -np.sin(ang_a)], axis=0)
    fa_half = fa_full[:, :N1 // 2]
    n2 = np.arange(N2)
    kk = k1[:, None, None] + N1 * np.arange(N2)[None, :, None]
    ang_b = 2.0 * np.pi * (kk * n2[None, None, :]) / N
    mr, mi = np.cos(ang_b), -np.sin(ang_b)
    m_fwd = np.concatenate([np.concatenate([mr, -mi], axis=2),
                            np.concatenate([mi, mr], axis=2)], axis=1)
    wr, wi = np.swapaxes(mr, 1, 2), -np.swapaxes(mi, 1, 2)
    m_inv = np.concatenate([np.concatenate([wr, -wi], axis=2),
                            np.concatenate([wi, wr], axis=2)], axis=1)
    ang_c = 2.0 * np.pi * np.outer(np.arange(N1 // 2), k1) / N1
    g_inv = np.concatenate([np.cos(ang_c), -np.sin(ang_c)], axis=1) / N

    def hi_lo(a):
        a = jnp.asarray(a, dtype=F32)
        hi = a.astype(BF16)
        return hi, (a - hi.astype(F32)).astype(BF16)

    return dict(N1=N1, N2=N2, fa_full=hi_lo(fa_full), fa_half=hi_lo(fa_half),
                m_fwd=hi_lo(m_fwd), m_inv=hi_lo(m_inv), g_inv=hi_lo(g_inv))


def _mm3(m_hi, m_lo, x):
    x_hi, x_lo = _split2(x)
    return _dot(m_hi, x_hi) + _dot(m_hi, x_lo) + _dot(m_lo, x_hi)


def _stage_a_kernel(z_ref, fh_ref, fl_ref, a_ref):
    n1 = a_ref.shape[2]
    res = _mm3(fh_ref[...], fl_ref[...], z_ref[0])
    a_ref[0, 0] = res[:n1]
    a_ref[0, 1] = res[n1:]


def _fft_stage_a(z, fa, N1):
    B, nr, cols = z.shape
    cb = min(cols, 4096)
    return pl.pallas_call(
        _stage_a_kernel,
        grid=(B, cols // cb),
        in_specs=[pl.BlockSpec((1, nr, cb), lambda b, j: (b, 0, j)),
                  pl.BlockSpec((2 * N1, nr), lambda b, j: (0, 0)),
                  pl.BlockSpec((2 * N1, nr), lambda b, j: (0, 0))],
        out_specs=pl.BlockSpec((1, 2, N1, cb), lambda b, j: (b, 0, 0, j)),
        out_shape=jax.ShapeDtypeStruct((B, 2, N1, cols), F32),
        compiler_params=_cparams(("parallel", "parallel")),
        name="fft_stage_a",
    )(z, fa[0], fa[1])


def _stage_b_filter_kernel(a_ref, mh_ref, ml_ref, o_ref):
    n2 = a_ref.shape[3]
    a = a_ref[0, :, 0].reshape(2 * n2, a_ref.shape[4])
    res = _mm3(mh_ref[0], ml_ref[0], a)
    o_ref[0, 0] = res[:n2]
    o_ref[1, 0] = res[n2:]


def _filter_spectrum(k2l, dc):
    N1, N2 = dc["N1"], dc["N2"]
    OW = k2l.shape[1]
    a = _fft_stage_a(k2l.reshape(1, N1, N2 * OW), dc["fa_full"], N1)
    a = a.reshape(1, 2, N1, N2, OW)
    wb = 512
    return pl.pallas_call(
        _stage_b_filter_kernel,
        grid=(N1, OW // wb),
        in_specs=[pl.BlockSpec((1, 2, 1, N2, wb), lambda k, j: (0, 0, k, 0, j)),
                  pl.BlockSpec((1, 2 * N2, 2 * N2), lambda k, j: (k, 0, 0)),
                  pl.BlockSpec((1, 2 * N2, 2 * N2), lambda k, j: (k, 0, 0))],
        out_specs=pl.BlockSpec((2, 1, N2, wb), lambda k, j: (0, k, 0, j)),
        out_shape=jax.ShapeDtypeStruct((2, N1, N2, OW), F32),
        compiler_params=_cparams(("parallel", "parallel")),
        name="fft_filter_spectrum",
    )(a, dc["m_fwd"][0], dc["m_fwd"][1])


def _stage_b_kernel(a_ref, h_ref, mfh_ref, mfl_ref, mih_ref, mil_ref, c_ref):
    n2 = a_ref.shape[3]
    a = a_ref[0, :, 0].reshape(2 * n2, a_ref.shape[4])
    z = _mm3(mfh_ref[0], mfl_ref[0], a)
    zr, zi = z[:n2], z[n2:]
    hr, hi = h_ref[0, 0], h_ref[1, 0]
    y = jnp.concatenate([zr * hr - zi * hi, zr * hi + zi * hr], axis=0)
    c = _mm3(mih_ref[0], mil_ref[0], y)
    c_ref[0, 0, 0] = c[:n2]
    c_ref[0, 1, 0] = c[n2:]


def _fft_stage_b(a, hf, order, dc):
    B, _, N1, N2, W = a.shape
    return pl.pallas_call(
        _stage_b_kernel,
        grid=(B, N1),
        in_specs=[pl.BlockSpec((1, 2, 1, N2, W), lambda b, k: (b, 0, k, 0, 0)),
                  pl.BlockSpec((2, 1, N2, W), lambda b, k: (0, k, 0, order)),
                  pl.BlockSpec((1, 2 * N2, 2 * N2), lambda b, k: (k, 0, 0)),
                  pl.BlockSpec((1, 2 * N2, 2 * N2), lambda b, k: (k, 0, 0)),
                  pl.BlockSpec((1, 2 * N2, 2 * N2), lambda b, k: (k, 0, 0)),
                  pl.BlockSpec((1, 2 * N2, 2 * N2), lambda b, k: (k, 0, 0))],
        out_specs=pl.BlockSpec((1, 2, 1, N2, W), lambda b, k: (b, 0, k, 0, 0)),
        out_shape=jax.ShapeDtypeStruct((B, 2, N1, N2, W), F32),
        compiler_params=_cparams(("parallel", "parallel")),
        name="fft_stage_b",
    )(a, hf, dc["m_fwd"][0], dc["m_fwd"][1], dc["m_inv"][0], dc["m_inv"][1])


def _stage_c_kernel(c_ref, z_ref, gate_ref, bias_ref, gh_ref, gl_ref, o_ref):
    n1 = c_ref.shape[2]
    c = c_ref[0].reshape(2 * n1, c_ref.shape[3])
    y = _mm3(gh_ref[...], gl_ref[...], c)
    o_ref[0] = gate_ref[0] * (y + z_ref[0] * bias_ref[...])


def _fft_stage_c(c, z, gate, bias_row, dc):
    B, _, N1, cols = c.shape
    cb = min(cols, 4096)
    nr = N1 // 2
    return pl.pallas_call(
        _stage_c_kernel,
        grid=(B, cols // cb),
        in_specs=[pl.BlockSpec((1, 2, N1, cb), lambda b, j: (b, 0, 0, j)),
                  pl.BlockSpec((1, nr, cb), lambda b, j: (b, 0, j)),
                  pl.BlockSpec((1, nr, cb), lambda b, j: (b, 0, j)),
                  pl.BlockSpec((1, cb), lambda b, j: (0, 0)),
                  pl.BlockSpec((nr, 2 * N1), lambda b, j: (0, 0)),
                  pl.BlockSpec((nr, 2 * N1), lambda b, j: (0, 0))],
        out_specs=pl.BlockSpec((1, nr, cb), lambda b, j: (b, 0, j)),
        out_shape=jax.ShapeDtypeStruct((B, nr, cols), F32),
        compiler_params=_cparams(("parallel", "parallel")),
        name="fft_stage_c",
    )(c, z, gate, bias_row, dc["g_inv"][0], dc["g_inv"][1])


def _long_conv_gated(z, gate, hf, order, skip, dc):
    B, nr, cols = z.shape
    N1, N2 = dc["N1"], dc["N2"]
    W = cols // N2
    a = _fft_stage_a(z, dc["fa_half"], N1).reshape(B, 2, N1, N2, W)
    c = _fft_stage_b(a, hf, order, dc).reshape(B, 2, N1, cols)
    cb = min(cols, 4096)
    bias_row = jnp.tile(skip, cb // W)[None, :]
    return _fft_stage_c(c, z, gate, bias_row, dc)


def _short_conv_kernel(u_ref, prev_ref, next_ref, w_ref, b_ref, v_ref, x1_ref, x2_ref):
    i = pl.program_id(1)
    u = u_ref[0]
    tl = u.shape[0]
    row = lax.broadcasted_iota(jnp.int32, (tl, 1), 0)
    before = jnp.where(i == 0, 0.0, prev_ref[0, 7:8, :])
    after = jnp.where(i == pl.num_programs(1) - 1, 0.0, next_ref[0, 0:1, :])
    up = jnp.where(row == 0, before, pltpu.roll(u, 1, 0))
    un = jnp.where(row == tl - 1, after, pltpu.roll(u, tl - 1, 0))
    w = w_ref[...]
    uc = up * w[0:1] + u * w[1:2] + un * w[2:3] + b_ref[...]
    W = HYENA_WIDTH
    v_ref[0] = uc[:, :W]
    x1_ref[0] = uc[:, W:2 * W]
    x2_ref[0] = uc[:, 2 * W:]


def _short_conv(hy, conv_w, conv_b, B, L):
    W3 = 3 * HYENA_WIDTH
    tl = _row_tile(L, 512)
    nl = L // tl
    u = hy.reshape(B, L, W3)
    r8 = tl // 8
    spec_o = pl.BlockSpec((1, tl, HYENA_WIDTH), lambda b, i: (b, i, 0))
    return pl.pallas_call(
        _short_conv_kernel,
        grid=(B, nl),
        in_specs=[pl.BlockSpec((1, tl, W3), lambda b, i: (b, i, 0)),
                  pl.BlockSpec((1, 8, W3), lambda b, i: (b, jnp.maximum(i * r8 - 1, 0), 0)),
                  pl.BlockSpec((1, 8, W3), lambda b, i: (b, jnp.minimum((i + 1) * r8, L // 8 - 1), 0)),
                  pl.BlockSpec((3, W3), lambda b, i: (0, 0)),
                  pl.BlockSpec((1, W3), lambda b, i: (0, 0))],
        out_specs=[spec_o, spec_o, spec_o],
        out_shape=[jax.ShapeDtypeStruct((B, L, HYENA_WIDTH), F32)] * 3,
        compiler_params=_cparams(("parallel", "parallel")),
        name="hyena_short_conv",
    )(u, u, u, conv_w, conv_b[None, :])


def _hyena(hy, hf, conv_w, conv_b, skip, dc, B, L):
    v, x1, x2 = _short_conv(hy, conv_w, conv_b, B, L)
    shape = (B, dc["N1"] // 2, dc["N2"] * HYENA_WIDTH)
    z = _long_conv_gated(v.reshape(shape), x1.reshape(shape), hf, 0, skip[0], dc)
    y = _long_conv_gated(z, x2.reshape(shape), hf, 1, skip[1], dc)
    return y.reshape(B * L, HYENA_WIDTH)


def _merge_kernel(x_ref, oa_ref, oh_ref, hg_ref, oc_ref, gl_ref, hgn_ref, wb_ref, wo_ref, g_ref, b_ref,
                  out_ref):
    o = oh_ref[0] + oh_ref[1]
    parts = []
    for hd in range(HGRN_HEADS):
        oh = o[:, hd * HGRN_DIM:(hd + 1) * HGRN_DIM]
        ms = jnp.mean(oh * oh, axis=-1, keepdims=True)
        parts.append(oh * lax.rsqrt(ms + RMS_EPS))
    o_h = jnp.concatenate(parts, axis=-1) * hgn_ref[...] * _silu(hg_ref[...])
    gl = gl_ref[...]
    branches = (oa_ref[...], o_h, oc_ref[...])
    merged = None
    for j, ob in enumerate(branches):
        term = _sigmoid(gl[:, j * D_MODEL:(j + 1) * D_MODEL]) * _dot(ob.astype(BF16), wb_ref[j])
        merged = term if merged is None else merged + term
    mix = _dot(merged.astype(BF16), wo_ref[...])
    out_ref[...] = _layer_norm(ALPHA * x_ref[...] + mix, g_ref[...], b_ref[...])


def _merge(x2, o_a, o_dir, hg, o_c, gl, hgrn_gain, wb_bf16, wo_bf16, ln_g, ln_b):
    T = x2.shape[0]
    tm = _row_tile(T, 512)
    W = BRANCH_WIDTH
    row = lambda w: pl.BlockSpec((tm, w), lambda i: (i, 0))
    return pl.pallas_call(
        _merge_kernel,
        grid=(T // tm,),
        in_specs=[row(D_MODEL), row(W), pl.BlockSpec((2, tm, W), lambda i: (0, i, 0)), row(W), row(W),
                  row(N_BRANCH * D_MODEL),
                  pl.BlockSpec((1, W), lambda i: (0, 0)),
                  pl.BlockSpec((N_BRANCH, W, D_MODEL), lambda i: (0, 0, 0)),
                  pl.BlockSpec((D_MODEL, D_MODEL), lambda i: (0, 0)),
                  pl.BlockSpec((1, D_MODEL), lambda i: (0, 0)),
                  pl.BlockSpec((1, D_MODEL), lambda i: (0, 0))],
        out_specs=row(D_MODEL),
        out_shape=jax.ShapeDtypeStruct((T, D_MODEL), F32),
        compiler_params=_cparams(("parallel",)),
        name="merge_out_ln",
    )(x2, o_a, o_dir.reshape(2, T, W), hg, o_c, gl, jnp.tile(hgrn_gain, HGRN_HEADS)[None, :],
      wb_bf16, wo_bf16, ln_g[None, :], ln_b[None, :])


def _ffn_kernel(x_ref, wg_ref, wu_ref, wd_ref, g_ref, b_ref, out_ref, acc_ref):
    j = pl.program_id(1)
    x = x_ref[...]
    xb = x.astype(BF16)
    hmid = _silu(_dot(xb, wg_ref[...])) * _dot(xb, wu_ref[...])
    contrib = _dot(hmid.astype(BF16), wd_ref[...])

    @pl.when(j == 0)
    def _():
        acc_ref[...] = contrib

    @pl.when(j > 0)
    def _():
        acc_ref[...] += contrib

    @pl.when(j == pl.num_programs(1) - 1)
    def _():
        out_ref[...] = _layer_norm(ALPHA * x + acc_ref[...], g_ref[...], b_ref[...])


def _ffn_dense(x2, w_gu, w_down, ln_g, ln_b):
    T = x2.shape[0]
    tm = _row_tile(T, 512)
    ft = 1408
    nf = D_FF_DENSE // ft
    return pl.pallas_call(
        _ffn_kernel,
        grid=(T // tm, nf),
        in_specs=[pl.BlockSpec((tm, D_MODEL), lambda i, j: (i, 0)),
                  pl.BlockSpec((D_MODEL, ft), lambda i, j: (0, j)),
                  pl.BlockSpec((D_MODEL, ft), lambda i, j: (0, j + nf)),
                  pl.BlockSpec((ft, D_MODEL), lambda i, j: (j, 0)),
                  pl.BlockSpec((1, D_MODEL), lambda i, j: (0, 0)),
                  pl.BlockSpec((1, D_MODEL), lambda i, j: (0, 0))],
        out_specs=pl.BlockSpec((tm, D_MODEL), lambda i, j: (i, 0)),
        out_shape=jax.ShapeDtypeStruct((T, D_MODEL), F32),
        scratch_shapes=[pltpu.VMEM((tm, D_MODEL), F32)],
        compiler_params=_cparams(("parallel", "arbitrary")),
        name="ffn_dense_ln",
    )(x2, w_gu, w_gu, w_down, ln_g[None, :], ln_b[None, :])


MOE_BLOCK = 1024
MOE_SUB = 128
MOE_TILE = 256


def _route_kernel(x_ref, r_ref, col_ref, row_ref, s_ref):
    tb = x_ref.shape[0]
    logits = _dot3(x_ref[...], r_ref[...])
    lane = lax.broadcasted_iota(jnp.int32, logits.shape, 1)
    logits = jnp.where(lane < N_EXPERTS, logits, -jnp.inf)
    m1 = jnp.max(logits, axis=-1, keepdims=True)
    i1 = jnp.min(jnp.where(logits == m1, lane, LANES), axis=-1, keepdims=True)
    rest = jnp.where(lane == i1, -jnp.inf, logits)
    m2 = jnp.max(rest, axis=-1, keepdims=True)
    i2 = jnp.min(jnp.where(rest == m2, lane, LANES), axis=-1, keepdims=True)
    e2 = jnp.exp(m2 - m1)
    w1 = 1.0 / (1.0 + e2)
    w2 = e2 / (1.0 + e2)
    oh1 = lane == i1
    oh2 = lane == i2
    sel = jnp.where(oh1, 1.0, jnp.where(oh2, 1.0, 0.0))
    rr = lax.broadcasted_iota(jnp.int32, (tb, tb), 0)
    cc = lax.broadcasted_iota(jnp.int32, (tb, tb), 1)
    earlier = jnp.where(rr > cc, 1.0, 0.0).astype(BF16)
    rank = _dot(earlier, sel.astype(BF16))
    cnt = jnp.sum(sel, axis=0, keepdims=True)
    nsub = jnp.floor((cnt + (MOE_SUB - 1)) * (1.0 / MOE_SUB))
    r2 = lax.broadcasted_iota(jnp.int32, (LANES, LANES), 0)
    c2 = lax.broadcasted_iota(jnp.int32, (LANES, LANES), 1)
    before = jnp.where(r2 < c2, 1.0, 0.0).astype(BF16)
    nsub8 = jnp.broadcast_to(nsub, (8, LANES))
    off8 = _dot(nsub8.astype(BF16), before) * MOE_SUB
    base = off8[0:1] + rank
    d1 = jnp.sum(jnp.where(oh1, base, 0.0), axis=-1, keepdims=True)
    d2 = jnp.sum(jnp.where(oh2, base, 0.0), axis=-1, keepdims=True)
    col = jnp.where(lane == 0, d1, jnp.where(lane == 1, d2, jnp.where(lane == 2, w1, jnp.where(lane == 3, w2, 0.0))))
    col_ref[...] = col
    row_ref[0] = col.T[0:8, :]
    lane8 = lax.broadcasted_iota(jnp.int32, (8, LANES), 1)
    info = jnp.where(lane8 < N_EXPERTS, off8, pltpu.roll(nsub8, N_EXPERTS, 1))
    s_ref[0] = info.astype(jnp.int32)


def _moe_route(x2, router):
    T = x2.shape[0]
    tb = _row_tile(T, MOE_BLOCK)
    nb = T // tb
    r_pad = jnp.pad(router, ((0, 0), (0, LANES - N_EXPERTS)))
    return pl.pallas_call(
        _route_kernel,
        grid=(nb,),
        in_specs=[pl.BlockSpec((tb, D_MODEL), lambda i: (i, 0)),
                  pl.BlockSpec((D_MODEL, LANES), lambda i: (0, 0))],
        out_specs=[pl.BlockSpec((tb, LANES), lambda i: (i, 0)),
                   pl.BlockSpec((1, 8, tb), lambda i: (i, 0, 0)),
                   pl.BlockSpec((1, 8, LANES), lambda i: (i, 0, 0))],
        out_shape=[jax.ShapeDtypeStruct((T, LANES), F32),
                   jax.ShapeDtypeStruct((nb, 8, tb), F32),
                   jax.ShapeDtypeStruct((nb, 8, LANES), jnp.int32)],
        compiler_params=_cparams(("parallel",)),
        name="moe_route",
    )(x2, r_pad)


def _moe_kernel(s_ref, x_ref, col_ref, row_ref, wg_ref, wu_ref, wd_ref, g_ref, b_ref, out_ref,
                xs_ref, ys_ref):
    blk = pl.program_id(0)
    e = pl.program_id(1)
    j = pl.program_id(2)
    tb = x_ref.shape[0]
    n_tiles = xs_ref.shape[0] // MOE_TILE
    used = s_ref[blk, N_EXPERTS - 1] + s_ref[blk, 2 * N_EXPERTS - 1] * MOE_SUB
    d1r = row_ref[0, 0:1, :]
    d2r = row_ref[0, 1:2, :]

    @pl.when(jnp.logical_and(e == 0, j == 0))
    def _():
        xb = x_ref[...].astype(BF16)
        for rt in range(n_tiles):
            @pl.when(rt * MOE_TILE < used)
            def _():
                r = (lax.broadcasted_iota(jnp.int32, (MOE_TILE, tb), 0) + rt * MOE_TILE).astype(F32)
                p = jnp.where(r == d1r, 1.0, jnp.where(r == d2r, 1.0, 0.0)).astype(BF16)
                xs_ref[rt * MOE_TILE:(rt + 1) * MOE_TILE, :] = _dot(p, xb).astype(BF16)

    start = s_ref[blk, e]
    nsub = s_ref[blk, N_EXPERTS + e]

    def expert_rows(first, n_rows):
        rows = pl.ds(pl.multiple_of(first, MOE_SUB), n_rows)
        xt = xs_ref[rows, :]
        hmid = _silu(_dot(xt, wg_ref[0])) * _dot(xt, wu_ref[0])
        y = _dot(hmid.astype(BF16), wd_ref[0])

        @pl.when(j == 0)
        def _():
            ys_ref[rows, :] = y

        @pl.when(j > 0)
        def _():
            ys_ref[rows, :] += y

    n_big = nsub // 4
    rem = nsub - 4 * n_big

    def big(i, carry):
        expert_rows(start + i * (4 * MOE_SUB), 4 * MOE_SUB)
        return carry

    lax.fori_loop(0, n_big, big, 0)
    tail = start + n_big * (4 * MOE_SUB)

    @pl.when(rem >= 2)
    def _():
        expert_rows(tail, 2 * MOE_SUB)

    @pl.when(rem % 2 == 1)
    def _():
        expert_rows(tail + (rem // 2) * (2 * MOE_SUB), MOE_SUB)

    @pl.when(jnp.logical_and(e == pl.num_programs(1) - 1, j == pl.num_programs(2) - 1))
    def _():
        @pl.when((used // MOE_SUB) % 2 == 1)
        def _():
            ys_ref[pl.ds(pl.multiple_of(used, MOE_SUB), MOE_SUB), :] = jnp.zeros((MOE_SUB, D_MODEL), F32)

        w1r = row_ref[0, 2:3, :]
        w2r = row_ref[0, 3:4, :]
        for rt in range(n_tiles):
            @pl.when(rt * MOE_TILE < used)
            def _():
                r = (lax.broadcasted_iota(jnp.int32, (MOE_TILE, tb), 0) + rt * MOE_TILE).astype(F32)
                w_rows = jnp.sum(jnp.where(r == d1r, w1r, 0.0) + jnp.where(r == d2r, w2r, 0.0),
                                 axis=-1, keepdims=True)
                rows = slice(rt * MOE_TILE, (rt + 1) * MOE_TILE)
                xs_ref[rows, :] = (ys_ref[rows, :] * w_rows).astype(BF16)

        d1c = col_ref[:, 0:1]
        d2c = col_ref[:, 1:2]
        out_ref[...] = jnp.zeros(out_ref.shape, F32)
        for rt in range(n_tiles):
            @pl.when(rt * MOE_TILE < used)
            def _():
                c = (lax.broadcasted_iota(jnp.int32, (tb, MOE_TILE), 1) + rt * MOE_TILE).astype(F32)
                q = jnp.where(c == d1c, 1.0, jnp.where(c == d2c, 1.0, 0.0)).astype(BF16)
                out_ref[...] += _dot(q, xs_ref[rt * MOE_TILE:(rt + 1) * MOE_TILE, :])

        out_ref[...] = _layer_norm(ALPHA * x_ref[...] + out_ref[...], g_ref[...], b_ref[...])


def _ffn_moe(x2, router, w_gu, w_down, ln_g, ln_b):
    T = x2.shape[0]
    tb = _row_tile(T, MOE_BLOCK)
    nb = T // tb
    n_rows = 2 * tb + N_EXPERTS * MOE_SUB
    col, row, sinfo = _moe_route(x2, router)
    ft = 896
    nf = D_FF_EXPERT // ft
    grid_spec = pltpu.PrefetchScalarGridSpec(
        num_scalar_prefetch=1,
        grid=(nb, N_EXPERTS, nf),
        in_specs=[pl.BlockSpec((tb, D_MODEL), lambda i, e, j, s: (i, 0)),
                  pl.BlockSpec((tb, LANES), lambda i, e, j, s: (i, 0)),
                  pl.BlockSpec((1, 8, tb), lambda i, e, j, s: (i, 0, 0)),
                  pl.BlockSpec((1, D_MODEL, ft), lambda i, e, j, s: (e, 0, j)),
                  pl.BlockSpec((1, D_MODEL, ft), lambda i, e, j, s: (e, 0, j + nf)),
                  pl.BlockSpec((1, ft, D_MODEL), lambda i, e, j, s: (e, j, 0)),
                  pl.BlockSpec((1, D_MODEL), lambda i, e, j, s: (0, 0)),
                  pl.BlockSpec((1, D_MODEL), lambda i, e, j, s: (0, 0))],
        out_specs=pl.BlockSpec((tb, D_MODEL), lambda i, e, j, s: (i, 0)),
        scratch_shapes=[pltpu.VMEM((n_rows, D_MODEL), BF16), pltpu.VMEM((n_rows, D_MODEL), F32)])
    return pl.pallas_call(
        _moe_kernel,
        grid_spec=grid_spec,
        out_shape=jax.ShapeDtypeStruct((T, D_MODEL), F32),
        compiler_params=_cparams(("parallel", "arbitrary", "arbitrary")),
        name="ffn_moe_ln",
    )(sinfo[:, 0, :], x2, col, row, w_gu, w_gu, w_down, ln_g[None, :], ln_b[None, :])


def _trunk(x, p):
    B, L, _ = x.shape
    cos_t, sin_t = _rope_tables(L)
    filt_consts = _filter_features(L)
    dc = _dft_consts(L)
    x2 = x.reshape(B * L, D_MODEL)
    for l in range(DEPTH):
        aq, akv, hq, hf, hi, hg, hy, gl = _in_proj(x2, p["w_in"][l])
        qn, kn, vn = _attn_prep(aq, akv, cos_t, sin_t, p["q_norm_g"][l], p["k_norm_g"][l], B, L)
        o_a = _flash_attention(qn, kn, vn, B, L)
        o_dir = _hgrn(hq, hf, hi, p["lower_bounds"][l], B, L)
        k2l = _hyena_filter(L, filt_consts, p["filt_w1"][l], p["filt_b1"][l], p["filt_freq"][l],
                            p["filt_w2"][l], p["filt_b2"][l], p["filt_w3"][l])
        spec = _filter_spectrum(k2l, dc)
        o_c = _hyena(hy, spec, p["conv_w"][l], p["conv_b"][l], p["hyena_skip"][l], dc, B, L)
        x2 = _merge(x2, o_a, o_dir, hg, o_c, gl, p["hgrn_norm_g"][l], p["w_branch"][l], p["w_out"][l],
                    p["ln1_g"][l], p["ln1_b"][l])
        if l % 2 == 0:
            x2 = _ffn_dense(x2, p["ffn_w_gu"][l // 2], p["ffn_w_down"][l // 2], p["ln2_g"][l], p["ln2_b"][l])
        else:
            x2 = _ffn_moe(x2, p["router_w"][l // 2], p["expert_w_gu"][l // 2], p["expert_w_down"][l // 2],
                          p["ln2_g"][l], p["ln2_b"][l])
    return x2.reshape(B, L, D_MODEL)


def _in_proj_weight(w_in):
    return w_in.astype(BF16)


def kernel(x_prompt, x_sample, w_in, q_norm_g, k_norm_g, hgrn_lb, hgrn_norm_g, conv_w, conv_b, filt_w1, filt_b1, filt_freq, filt_w2, filt_b2, filt_w3, hyena_skip, w_branch, w_out, ln1_g, ln1_b, ln2_g, ln2_b, ffn_w_gu, ffn_w_down, router_w, expert_w_gu, expert_w_down):
    s = jax.nn.softmax(hgrn_lb.astype(F32), axis=0)
    p = dict(
        w_in=_in_proj_weight(w_in), q_norm_g=q_norm_g, k_norm_g=k_norm_g,
        lower_bounds=jnp.cumsum(s, axis=0) - s[:1], hgrn_norm_g=hgrn_norm_g,
        conv_w=conv_w, conv_b=conv_b, filt_w1=filt_w1, filt_b1=filt_b1, filt_freq=filt_freq,
        filt_w2=filt_w2, filt_b2=filt_b2, filt_w3=filt_w3, hyena_skip=hyena_skip,
        w_branch=w_branch.astype(BF16), w_out=w_out.astype(BF16),
        ln1_g=ln1_g, ln1_b=ln1_b, ln2_g=ln2_g, ln2_b=ln2_b,
        ffn_w_gu=ffn_w_gu.astype(BF16), ffn_w_down=ffn_w_down.astype(BF16), router_w=router_w,
        expert_w_gu=expert_w_gu.astype(BF16), expert_w_down=expert_w_down.astype(BF16))
    return (_trunk(x_prompt, p), _trunk(x_sample, p))
```

```python
import functools
import math

import numpy as np
import jax
import jax.numpy as jnp
from jax import lax
from jax.experimental import pallas as pl
from jax.experimental.pallas import tpu as pltpu

F32 = jnp.float32
BF16 = jnp.bfloat16

D_MODEL = 1024
DEPTH = 2
GRID_W = 64
BRANCH_WIDTH = 512
ATTN_HEADS = 8
ATTN_KV_HEADS = 2
ATTN_GROUP = ATTN_HEADS // ATTN_KV_HEADS
HEAD_DIM = 64
KV_WIDTH = ATTN_KV_HEADS * HEAD_DIM
ROPE_AXIS_DIM = HEAD_DIM // 2
ROPE_HALF = ROPE_AXIS_DIM // 2
ROPE_THETA = 10000.0
HGRN_HEADS = 4
HGRN_DIM = 128
HGRN_CHUNK = 64
FORGET_FLOOR = 1e-30
HYENA_WIDTH = BRANCH_WIDTH
HYENA_ORDER = 2
FILTER_EMB = 33
FILTER_BANDS = (FILTER_EMB - 1) // 2
FILTER_HIDDEN = 64
FILTER_FAST_DECAY = 0.3
FILTER_SLOW_DECAY = 1.5
FILTER_TARGET = 1e-2
N_FILTER_CH = HYENA_ORDER * 2 * HYENA_WIDTH
N_BRANCH = 3
D_FF_DENSE = 2816
N_EXPERTS = 8
D_FF_EXPERT = 3584
ALPHA = (2 * DEPTH) ** 0.25
LN_EPS = 1e-5
RMS_EPS = 1e-6

SEG_WIDTHS = (BRANCH_WIDTH,
              2 * KV_WIDTH,
              BRANCH_WIDTH,
              2 * BRANCH_WIDTH,
              BRANCH_WIDTH,
              BRANCH_WIDTH,
              3 * HYENA_WIDTH,
              N_BRANCH * D_MODEL)
IN_WIDTH = sum(SEG_WIDTHS)
SEG_DTYPES = (BF16, BF16, BF16, F32, BF16, BF16, BF16, BF16)

LANES = 128
FFT_N2 = 128
VMEM_LIMIT = 56 * 1024 * 1024


def _cparams(sem):
    return pltpu.CompilerParams(dimension_semantics=sem, vmem_limit_bytes=VMEM_LIMIT)


def _dot(a, b):
    return jnp.dot(a, b, preferred_element_type=F32)


def _dot_nt(a, b):
    return lax.dot_general(a, b, (((1,), (1,)), ((), ())), preferred_element_type=F32)


def _dot_tn(a, b):
    return lax.dot_general(a, b, (((0,), (0,)), ((), ())), preferred_element_type=F32)


def _split2(a):
    hi = a.astype(BF16)
    lo = (a - hi.astype(F32)).astype(BF16)
    return hi, lo


def _dot3(a, b):
    ah, al = _split2(a)
    bh, bl = _split2(b)
    return _dot(ah, bh) + _dot(al, bh) + _dot(ah, bl)


def _layer_norm(y, g, b):
    mu = jnp.mean(y, axis=-1, keepdims=True)
    yc = y - mu
    var = jnp.mean(yc * yc, axis=-1, keepdims=True)
    return yc * lax.rsqrt(var + LN_EPS) * g + b


def _sigmoid(x):
    return 1.0 / (1.0 + jnp.exp(-x))


def _silu(x):
    return x * _sigmoid(x)


def _row_tile(n, want):
    t = min(n, want)
    assert n % t == 0
    return t


def _in_proj_kernel(x_ref, w_ref, *out_refs):
    x = x_ref[...].astype(BF16)
    off = 0
    for o_ref, width in zip(out_refs, SEG_WIDTHS):
        o_ref[...] = _dot(x, w_ref[:, off:off + width]).astype(o_ref.dtype)
        off += width


def _in_proj(x2, w_bf16):
    T = x2.shape[0]
    tm = _row_tile(T, 512)
    return pl.pallas_call(
        _in_proj_kernel,
        grid=(T // tm,),
        in_specs=[pl.BlockSpec((tm, D_MODEL), lambda i: (i, 0)),
                  pl.BlockSpec((D_MODEL, IN_WIDTH), lambda i: (0, 0), pipeline_mode=pl.Buffered(1))],
        out_specs=[pl.BlockSpec((tm, w), lambda i: (i, 0)) for w in SEG_WIDTHS],
        out_shape=[jax.ShapeDtypeStruct((T, w), dt) for w, dt in zip(SEG_WIDTHS, SEG_DTYPES)],
        compiler_params=_cparams(("parallel",)),
        name="in_proj",
    )(x2, w_bf16)


def _rope_tables(L):
    rows = L // GRID_W
    row = jnp.broadcast_to(jnp.arange(rows, dtype=F32)[:, None], (rows, GRID_W)).reshape(-1)
    col = jnp.broadcast_to(jnp.arange(GRID_W, dtype=F32)[None, :], (rows, GRID_W)).reshape(-1)
    inv = jnp.power(ROPE_THETA, -2.0 * jnp.arange(ROPE_HALF, dtype=F32) / ROPE_AXIS_DIM)
    a0, a1 = row[:, None] * inv, col[:, None] * inv
    c0, s0, c1, s1 = jnp.cos(a0), jnp.sin(a0), jnp.cos(a1), jnp.sin(a1)
    cos_h = jnp.concatenate([c0, c0, c1, c1], axis=-1)
    sin_h = jnp.concatenate([-s0, s0, -s1, s1], axis=-1)
    return jnp.tile(cos_h, (1, ATTN_HEADS)), jnp.tile(sin_h, (1, ATTN_HEADS))


def _head_block_ones(width):
    idx = np.arange(width) // HEAD_DIM
    return jnp.asarray(idx[:, None] == idx[None, :], dtype=BF16)


def _norm_rope(x, gain, cos, sin, ones_bd):
    width = x.shape[-1]
    sq_hi, sq_lo = _split2(x * x)
    ms = (_dot(sq_hi, ones_bd) + _dot(sq_lo, ones_bd)) * (1.0 / HEAD_DIM)
    xn = x * lax.rsqrt(ms + RMS_EPS) * gain
    lane = lax.broadcasted_iota(jnp.int32, xn.shape, 1)
    first_half = (lane % ROPE_AXIS_DIM) < ROPE_HALF
    swapped = jnp.where(first_half, pltpu.roll(xn, width - ROPE_HALF, 1), pltpu.roll(xn, ROPE_HALF, 1))
    return xn * cos + swapped * sin


LOG2E = 1.4426950408889634
V_EXT = 2 * HEAD_DIM
FLASH_SKEW = 4


def _attn_prep_kernel(aq_ref, akv_ref, cos_ref, sin_ref, qg_ref, kg_ref, bdq_ref, bdk_ref,
                      q_out, kt_out, v_out):
    cos, sin = cos_ref[...], sin_ref[...]
    q = _norm_rope(aq_ref[...].astype(F32), qg_ref[...], cos, sin, bdq_ref[...])
    q_out[...] = (q * (LOG2E * HEAD_DIM ** -0.5)).astype(BF16)
    kv = akv_ref[...].astype(F32)
    k = _norm_rope(kv[:, :KV_WIDTH], kg_ref[...], cos[:, :KV_WIDTH], sin[:, :KV_WIDTH], bdk_ref[...])
    kt_out[0] = k.T.astype(BF16)
    ones = jnp.ones((kv.shape[0], HEAD_DIM), F32)
    v_ext = [kv[:, KV_WIDTH + h * HEAD_DIM:KV_WIDTH + (h + 1) * HEAD_DIM] if part == 0 else ones
             for h in range(ATTN_KV_HEADS) for part in range(2)]
    v_out[...] = jnp.concatenate(v_ext, axis=-1).astype(BF16)


def _attn_prep(aq, akv, cos_t, sin_t, q_gain, k_gain, B, L):
    T = aq.shape[0]
    tm = _row_tile(L, 512)
    nl = L // tm
    qg = jnp.tile(q_gain, ATTN_HEADS)[None, :]
    kg = jnp.tile(k_gain, ATTN_KV_HEADS)[None, :]
    const = lambda i: (0, 0)
    return pl.pallas_call(
        _attn_prep_kernel,
        grid=(T // tm,),
        in_specs=[pl.BlockSpec((tm, BRANCH_WIDTH), lambda i: (i, 0)),
                  pl.BlockSpec((tm, 2 * KV_WIDTH), lambda i: (i, 0)),
                  pl.BlockSpec((tm, BRANCH_WIDTH), lambda i: (i % nl, 0)),
                  pl.BlockSpec((tm, BRANCH_WIDTH), lambda i: (i % nl, 0)),
                  pl.BlockSpec((1, BRANCH_WIDTH), const),
                  pl.BlockSpec((1, KV_WIDTH), const),
                  pl.BlockSpec((BRANCH_WIDTH, BRANCH_WIDTH), const),
                  pl.BlockSpec((KV_WIDTH, KV_WIDTH), const)],
        out_specs=[pl.BlockSpec((tm, BRANCH_WIDTH), lambda i: (i, 0)),
                   pl.BlockSpec((1, KV_WIDTH, tm), lambda i: (i // nl, 0, i % nl)),
                   pl.BlockSpec((tm, ATTN_KV_HEADS * V_EXT), lambda i: (i, 0))],
        out_shape=[jax.ShapeDtypeStruct((T, BRANCH_WIDTH), BF16),
                   jax.ShapeDtypeStruct((B, KV_WIDTH, L), BF16),
                   jax.ShapeDtypeStruct((T, ATTN_KV_HEADS * V_EXT), BF16)],
        compiler_params=_cparams(("parallel",)),
        name="attn_prep",
    )(aq, akv, cos_t, sin_t, qg, kg, _head_block_ones(BRANCH_WIDTH), _head_block_ones(KV_WIDTH))


def _flash_kernel(q_ref, kt_ref, v_ref, o_ref, qs_ref, m_ref, acc_ref, *, tq, rs):
    kv = pl.program_id(2)

    @pl.when(kv == 0)
    def _():
        for h in range(ATTN_HEADS):
            g = h % ATTN_GROUP
            qs_ref[h // ATTN_GROUP, g * tq:(g + 1) * tq, :] = q_ref[0, :, h * HEAD_DIM:(h + 1) * HEAD_DIM]
        m_ref[...] = jnp.full(m_ref.shape, -jnp.inf, F32)
        acc_ref[...] = jnp.zeros(acc_ref.shape, F32)

    kts = [kt_ref[0, kh * HEAD_DIM:(kh + 1) * HEAD_DIM, :] for kh in range(ATTN_KV_HEADS)]
    vs = [v_ref[0, :, kh * V_EXT:(kh + 1) * V_EXT] for kh in range(ATTN_KV_HEADS)]
    tasks = [(kh, slice(sb * rs, (sb + 1) * rs))
             for sb in range(ATTN_GROUP * tq // rs) for kh in range(ATTN_KV_HEADS)]

    def scores(t):
        kh, rows = tasks[t]
        return _dot(qs_ref[kh, rows, :], kts[kh])

    pending = {t: scores(t) for t in range(min(FLASH_SKEW, len(tasks)))}
    for t, (kh, rows) in enumerate(tasks):
        s = pending.pop(t)
        if t + FLASH_SKEW < len(tasks):
            pending[t + FLASH_SKEW] = scores(t + FLASH_SKEW)
        m_prev = m_ref[kh, rows, :]
        m_new = jnp.maximum(m_prev, jnp.max(s, axis=-1, keepdims=True))
        p = jnp.exp2(s - m_new[:, 0:1])
        acc_ref[kh, rows, :] = jnp.exp2(m_prev - m_new) * acc_ref[kh, rows, :] + _dot(p.astype(BF16), vs[kh])
        m_ref[kh, rows, :] = m_new

    @pl.when(kv == pl.num_programs(2) - 1)
    def _():
        for h in range(ATTN_HEADS):
            kh, g = h // ATTN_GROUP, h % ATTN_GROUP
            a = acc_ref[kh, g * tq:(g + 1) * tq, :]
            o_ref[0, :, h * HEAD_DIM:(h + 1) * HEAD_DIM] = a[:, :HEAD_DIM] / a[:, HEAD_DIM:]


def _flash_attention(qn, kt, v_ext, B, L):
    tq = _row_tile(L, 256)
    tk = _row_tile(L, 512)
    rs = min(128, ATTN_GROUP * tq)
    out = pl.pallas_call(
        functools.partial(_flash_kernel, tq=tq, rs=rs),
        grid=(B, L // tq, L // tk),
        in_specs=[pl.BlockSpec((1, tq, BRANCH_WIDTH), lambda b, i, j: (b, i, 0)),
                  pl.BlockSpec((1, KV_WIDTH, tk), lambda b, i, j: (b, 0, j)),
                  pl.BlockSpec((1, tk, ATTN_KV_HEADS * V_EXT), lambda b, i, j: (b, j, 0))],
        out_specs=pl.BlockSpec((1, tq, BRANCH_WIDTH), lambda b, i, j: (b, i, 0)),
        out_shape=jax.ShapeDtypeStruct((B, L, BRANCH_WIDTH), F32),
        scratch_shapes=[pltpu.VMEM((ATTN_KV_HEADS, ATTN_GROUP * tq, HEAD_DIM), BF16),
                        pltpu.VMEM((ATTN_KV_HEADS, ATTN_GROUP * tq, LANES), F32),
                        pltpu.VMEM((ATTN_KV_HEADS, ATTN_GROUP * tq, V_EXT), F32)],
        compiler_params=_cparams(("parallel", "parallel", "arbitrary")),
        name="flash_attention",
    )(qn.reshape(B, L, BRANCH_WIDTH), kt, v_ext.reshape(B, L, ATTN_KV_HEADS * V_EXT))
    return out.reshape(B * L, BRANCH_WIDTH)


def _ref_rows(b, h, fwd):
    C, W = b.shape
    two_h = 2 * h
    if two_h >= 8:
        n = C // two_h
        b3 = b.reshape(n, two_h, W)
        r = jnp.where(fwd, b3[:, h - 1:h, :], b3[:, h:h + 1, :])
        return jnp.broadcast_to(r, (n, two_h, W)).reshape(C, W)
    b3 = b.reshape(C // 8, 8, W)
    pos8 = lax.broadcasted_iota(jnp.int32, (1, 8, 1), 1)
    out = None
    for blk in range(8 // two_h):
        base = blk * two_h
        r = jnp.where(fwd, b3[:, base + h - 1:base + h, :], b3[:, base + h:base + h + 1, :])
        r = jnp.broadcast_to(r, b3.shape)
        out = r if out is None else jnp.where((pos8 // two_h) == blk, r, out)
    return out.reshape(C, W)


def _hgrn_chunk(q, z, v, lb, tri, st_ref, fwd):
    C, W = q.shape
    q = q.astype(F32)
    v = v.astype(F32)
    e = jnp.exp(-jnp.abs(z))
    s_big = 1.0 / (1.0 + e)
    s_small = e * s_big
    pos = z >= 0.0
    f = lb + (1.0 - lb) * jnp.where(pos, s_big, s_small)
    g = jnp.log(jnp.maximum(f, FORGET_FLOOR))
    kk = (1.0 - lb) * jnp.where(pos, s_small, s_big)
    g_hi = g.astype(BF16)
    r1 = g - g_hi.astype(F32)
    g_mid = r1.astype(BF16)
    g_lo = (r1 - g_mid.astype(F32)).astype(BF16)
    b = _dot(tri, g_hi) + _dot(tri, g_mid) + _dot(tri, g_lo)
    b_end = jnp.where(fwd, b[C - 1:C, :], b[0:1, :])
    q_in = (q * jnp.exp(b)).astype(BF16)
    k_out = (kk * jnp.exp(b_end - b)).astype(BF16)
    s_decay = jnp.exp(b_end)
    vb = v.astype(BF16)

    row = lax.broadcasted_iota(jnp.int32, (C, 1), 0)
    rr = lax.broadcasted_iota(jnp.int32, (C, C), 0)
    cc = lax.broadcasted_iota(jnp.int32, (C, C), 1)
    a_mats = [jnp.zeros((C, C), F32) for _ in range(HGRN_HEADS)]
    h = C // 2
    while h >= 1:
        upper = ((row % (2 * h)) >= h).astype(jnp.int32)
        q_rows = upper == fwd.astype(jnp.int32)
        ref = _ref_rows(b, h, fwd)
        ql = jnp.where(q_rows, q * jnp.exp(jnp.where(q_rows, b - ref, 0.0)), 0.0).astype(BF16)
        kl = jnp.where(q_rows, 0.0, kk * jnp.exp(jnp.where(q_rows, 0.0, ref - b))).astype(BF16)
        same_block = (rr // (2 * h)) == (cc // (2 * h))
        for hd in range(HGRN_HEADS):
            sl = slice(hd * HGRN_DIM, (hd + 1) * HGRN_DIM)
            a_mats[hd] = a_mats[hd] + jnp.where(same_block, _dot_nt(ql[:, sl], kl[:, sl]), 0.0)
        h //= 2

    qk = q * kk
    outs = []
    for hd in range(HGRN_HEADS):
        sl = slice(hd * HGRN_DIM, (hd + 1) * HGRN_DIM)
        st = st_ref[hd]
        o = _dot_nt(q_in[:, sl], st.astype(BF16))
        o = o + _dot(a_mats[hd].astype(BF16), vb[:, sl])
        o = o + jnp.sum(qk[:, sl], axis=-1, keepdims=True) * v[:, sl]
        outs.append(o)
        st_ref[hd] = st * s_decay[:, sl] + _dot_tn(vb[:, sl], k_out[:, sl])
    return jnp.concatenate(outs, axis=-1)


def _hgrn_kernel(q_ref, f_ref, v_ref, lb_ref, tri_ref, o_ref, st_ref, *, n_inner):
    d = pl.program_id(0)
    fwd = d == 0

    @pl.when(pl.program_id(2) == 0)
    def _():
        st_ref[...] = jnp.zeros(st_ref.shape, F32)

    lb = lb_ref[0]
    tri = tri_ref[0]

    def body(i, carry):
        ci = jnp.where(fwd, i, n_inner - 1 - i)
        r0 = pl.multiple_of(ci * HGRN_CHUNK, HGRN_CHUNK)
        rows = pl.ds(r0, HGRN_CHUNK)
        o_ref[0, 0, rows, :] = _hgrn_chunk(q_ref[0, rows, :], f_ref[0, rows, :], v_ref[0, rows, :],
                                           lb, tri, st_ref, fwd)
        return carry

    lax.fori_loop(0, n_inner, body, 0)


def _hgrn(hq, hf, hi, lb, B, L):
    W = BRANCH_WIDTH
    rb = _row_tile(L, 512)
    nb = L // rb
    C = HGRN_CHUNK
    tri_f = np.tril(np.ones((C, C), np.float32))
    tri = jnp.asarray(np.stack([tri_f, tri_f.T]), dtype=BF16)
    blk = lambda d, b, c: (b, c + d * (nb - 1 - 2 * c), 0)
    return pl.pallas_call(
        functools.partial(_hgrn_kernel, n_inner=rb // C),
        grid=(2, B, nb),
        in_specs=[pl.BlockSpec((1, rb, W), blk),
                  pl.BlockSpec((1, rb, W), lambda d, b, c: (b, c + d * (nb - 1 - 2 * c), d)),
                  pl.BlockSpec((1, rb, W), blk),
                  pl.BlockSpec((1, 1, W), lambda d, b, c: (d, 0, 0)),
                  pl.BlockSpec((1, C, C), lambda d, b, c: (d, 0, 0))],
        out_specs=pl.BlockSpec((1, 1, rb, W), lambda d, b, c: (d, b, c + d * (nb - 1 - 2 * c), 0)),
        out_shape=jax.ShapeDtypeStruct((2, B, L, W), F32),
        scratch_shapes=[pltpu.VMEM((HGRN_HEADS, HGRN_DIM, HGRN_DIM), F32)],
        compiler_params=_cparams(("parallel", "parallel", "arbitrary")),
        name="hgrn_scan",
    )(hq.reshape(B, L, W), hf.reshape(B, L, 2 * W), hi.reshape(B, L, W), lb[:, None, :], tri)


def _filter_features(L):
    t = jnp.linspace(0.0, 1.0, L, dtype=F32)[:, None]
    w = 2.0 * math.pi * jnp.arange(L, dtype=F32) / L
    bands = jnp.linspace(1e-4, FILTER_BANDS - 1, FILTER_BANDS, dtype=F32)
    ang = w[:, None] * bands[None, :]
    feats = jnp.concatenate([t, jnp.cos(ang), -jnp.sin(ang)], axis=-1)
    feats2 = jnp.concatenate([feats, feats[:1], jnp.flip(feats[1:], axis=0)], axis=0)
    t2 = jnp.concatenate([t, t[:1], jnp.flip(t[1:], axis=0)], axis=0)
    deltas = jnp.abs(jnp.linspace(math.log(FILTER_TARGET) / FILTER_FAST_DECAY,
                                  math.log(FILTER_TARGET) / FILTER_SLOW_DECAY, N_FILTER_CH, dtype=F32))
    return feats2, t2, deltas[None, :]


def _filter_kernel(feat_ref, t_ref, w1_ref, b1_ref, fr_ref, w2_ref, b2_ref, w3f_ref, w3b_ref,
                   df_ref, db_ref, k_ref, *, L):
    freq = fr_ref[...]
    h = jnp.sin(freq * (_dot3(feat_ref[...], w1_ref[...]) + b1_ref[...]))
    h = jnp.sin(freq * (_dot3(h, w2_ref[...]) + b2_ref[...]))
    t = t_ref[...]
    top = _dot3(h[:L], w3f_ref[...]) * jnp.exp(-t[:L] * df_ref[...])
    bot = _dot3(h[L:], w3b_ref[...]) * jnp.exp(-t[L:] * db_ref[...])
    row = lax.broadcasted_iota(jnp.int32, (L, 1), 0)
    bot = jnp.where(row == 0, 0.0, bot)
    norm = jnp.sum(jnp.abs(top), axis=0, keepdims=True) + jnp.sum(jnp.abs(bot), axis=0, keepdims=True)
    k_ref[:L, :] = top / norm
    k_ref[L:, :] = bot / norm


def _hyena_filter(L, consts, w1, b1, freq, w2, b2, w3):
    feats2, t2, deltas = consts
    wb = 256
    nw = HYENA_WIDTH // wb
    full = lambda shape: pl.BlockSpec(shape, lambda o, j: (0,) * len(shape))
    fcol = lambda o, j: (0, o * 2 * nw + j)
    bcol = lambda o, j: (0, o * 2 * nw + nw + j)
    return pl.pallas_call(
        functools.partial(_filter_kernel, L=L),
        grid=(HYENA_ORDER, nw),
        in_specs=[full((2 * L, FILTER_EMB)), full((2 * L, 1)),
                  full((FILTER_EMB, FILTER_HIDDEN)), full((1, FILTER_HIDDEN)), full((1, FILTER_HIDDEN)),
                  full((FILTER_HIDDEN, FILTER_HIDDEN)), full((1, FILTER_HIDDEN)),
                  pl.BlockSpec((FILTER_HIDDEN, wb), fcol), pl.BlockSpec((FILTER_HIDDEN, wb), bcol),
                  pl.BlockSpec((1, wb), fcol), pl.BlockSpec((1, wb), bcol)],
        out_specs=pl.BlockSpec((2 * L, wb), lambda o, j: (0, o * nw + j)),
        out_shape=jax.ShapeDtypeStruct((2 * L, HYENA_ORDER * HYENA_WIDTH), F32),
        compiler_params=_cparams(("parallel", "parallel")),
        name="hyena_filter",
    )(feats2, t2, w1, b1[None, :], freq[None, :], w2, b2[None, :], w3, w3, deltas, deltas)


def _dft_consts(L):
    N = 2 * L
    N2 = FFT_N2
    N1 = N // N2
    k1 = np.arange(N1)
    ang_a = 2.0 * np.pi * np.outer(k1, np.arange(N1)) / N1
    fa_full = np.concatenate([np.cos(ang_a), -np.sin(ang_a)], axis=0)
    fa_half = fa_full[:, :N1 // 2]
    n2 = np.arange(N2)
    kk = k1[:, None, None] + N1 * np.arange(N2)[None, :, None]
    ang_b = 2.0 * np.pi * (kk * n2[None, None, :]) / N
    mr, mi = np.cos(ang_b), -np.sin(ang_b)
    m_fwd = np.concatenate([np.concatenate([mr, -mi], axis=2),
                            np.concatenate([mi, mr], axis=2)], axis=1)
    wr, wi = np.swapaxes(mr, 1, 2), -np.swapaxes(mi, 1, 2)
    m_inv = np.concatenate([np.concatenate([wr, -wi], axis=2),
                            np.concatenate([wi, wr], axis=2)], axis=1)
    ang_c = 2.0 * np.pi * np.outer(np.arange(N1 // 2), k1) / N1
    g_inv = np.concatenate([np.cos(ang_c), -np.sin(ang_c)], axis=1) / N

    def hi_lo(a):
        a = jnp.asarray(a, dtype=F32)
        hi = a.astype(BF16)
        return hi, (a - hi.astype(F32)).astype(BF16)

    return dict(N1=N1, N2=N2, fa_full=hi_lo(fa_full), fa_half=hi_lo(fa_half),
                m_fwd=hi_lo(m_fwd), m_inv=hi_lo(m_inv), g_inv=hi_lo(g_inv))


def _mm3(m_hi, m_lo, x):
    x_hi, x_lo = _split2(x)
    return _dot(m_hi, x_hi) + _dot(m_hi, x_lo) + _dot(m_lo, x_hi)


def _stage_a_kernel(z_ref, fh_ref, fl_ref, a_ref, *, exact):
    n1 = a_ref.shape[2]
    if exact:
        res = _mm3(fh_ref[...], fl_ref[...], z_ref[0])
    else:
        res = _dot(fh_ref[...], z_ref[0].astype(BF16))
    a_ref[0, 0] = res[:n1].astype(a_ref.dtype)
    a_ref[0, 1] = res[n1:].astype(a_ref.dtype)


def _fft_stage_a(z, fa, N1, exact):
    B, nr, cols = z.shape
    cb = min(cols, 4096)
    return pl.pallas_call(
        functools.partial(_stage_a_kernel, exact=exact),
        grid=(B, cols // cb),
        in_specs=[pl.BlockSpec((1, nr, cb), lambda b, j: (b, 0, j)),
                  pl.BlockSpec((2 * N1, nr), lambda b, j: (0, 0)),
                  pl.BlockSpec((2 * N1, nr), lambda b, j: (0, 0))],
        out_specs=pl.BlockSpec((1, 2, N1, cb), lambda b, j: (b, 0, 0, j)),
        out_shape=jax.ShapeDtypeStruct((B, 2, N1, cols), F32 if exact else BF16),
        compiler_params=_cparams(("parallel", "parallel")),
        name="fft_stage_a",
    )(z, fa[0], fa[1])


def _stage_b_filter_kernel(a_ref, mh_ref, ml_ref, o_ref):
    n2 = a_ref.shape[3]
    a = a_ref[0, :, 0].reshape(2 * n2, a_ref.shape[4])
    res = _mm3(mh_ref[0], ml_ref[0], a)
    o_ref[0, 0] = res[:n2]
    o_ref[1, 0] = res[n2:]


def _filter_spectrum(k2l, dc):
    N1, N2 = dc["N1"], dc["N2"]
    OW = k2l.shape[1]
    a = _fft_stage_a(k2l.reshape(1, N1, N2 * OW), dc["fa_full"], N1, True)
    a = a.reshape(1, 2, N1, N2, OW)
    wb = 512
    return pl.pallas_call(
        _stage_b_filter_kernel,
        grid=(N1, OW // wb),
        in_specs=[pl.BlockSpec((1, 2, 1, N2, wb), lambda k, j: (0, 0, k, 0, j)),
                  pl.BlockSpec((1, 2 * N2, 2 * N2), lambda k, j: (k, 0, 0)),
                  pl.BlockSpec((1, 2 * N2, 2 * N2), lambda k, j: (k, 0, 0))],
        out_specs=pl.BlockSpec((2, 1, N2, wb), lambda k, j: (0, k, 0, j)),
        out_shape=jax.ShapeDtypeStruct((2, N1, N2, OW), F32),
        compiler_params=_cparams(("parallel", "parallel")),
        name="fft_filter_spectrum",
    )(a, dc["m_fwd"][0], dc["m_fwd"][1])


def _stage_b_kernel(a_ref, h_ref, mf_ref, mi_ref, c_ref):
    n2 = a_ref.shape[3]
    a = a_ref[0, :, 0].reshape(2 * n2, a_ref.shape[4])
    z = _dot(mf_ref[0], a)
    zr, zi = z[:n2], z[n2:]
    hr, hi = h_ref[0, 0], h_ref[1, 0]
    y = jnp.concatenate([zr * hr - zi * hi, zr * hi + zi * hr], axis=0)
    c = _dot(mi_ref[0], y.astype(BF16))
    c_ref[0, 0, 0] = c[:n2].astype(c_ref.dtype)
    c_ref[0, 1, 0] = c[n2:].astype(c_ref.dtype)


def _fft_stage_b(a, hf, order, dc):
    B, _, N1, N2, W = a.shape
    return pl.pallas_call(
        _stage_b_kernel,
        grid=(B, N1),
        in_specs=[pl.BlockSpec((1, 2, 1, N2, W), lambda b, k: (b, 0, k, 0, 0)),
                  pl.BlockSpec((2, 1, N2, W), lambda b, k: (0, k, 0, order)),
                  pl.BlockSpec((1, 2 * N2, 2 * N2), lambda b, k: (k, 0, 0)),
                  pl.BlockSpec((1, 2 * N2, 2 * N2), lambda b, k: (k, 0, 0))],
        out_specs=pl.BlockSpec((1, 2, 1, N2, W), lambda b, k: (b, 0, k, 0, 0)),
        out_shape=jax.ShapeDtypeStruct((B, 2, N1, N2, W), BF16),
        compiler_params=_cparams(("parallel", "parallel")),
        name="fft_stage_b",
    )(a, hf, dc["m_fwd"][0], dc["m_inv"][0])


def _stage_c_kernel(c_ref, z_ref, gate_ref, bias_ref, g_ref, o_ref):
    n1 = c_ref.shape[2]
    c = c_ref[0].reshape(2 * n1, c_ref.shape[3])
    y = _dot(g_ref[...], c)
    o_ref[0] = gate_ref[0] * (y + z_ref[0] * bias_ref[...])


def _fft_stage_c(c, z, gate, bias_row, dc):
    B, _, N1, cols = c.shape
    cb = min(cols, 4096)
    nr = N1 // 2
    return pl.pallas_call(
        _stage_c_kernel,
        grid=(B, cols // cb),
        in_specs=[pl.BlockSpec((1, 2, N1, cb), lambda b, j: (b, 0, 0, j)),
                  pl.BlockSpec((1, nr, cb), lambda b, j: (b, 0, j)),
                  pl.BlockSpec((1, nr, cb), lambda b, j: (b, 0, j)),
                  pl.BlockSpec((1, cb), lambda b, j: (0, 0)),
                  pl.BlockSpec((nr, 2 * N1), lambda b, j: (0, 0))],
        out_specs=pl.BlockSpec((1, nr, cb), lambda b, j: (b, 0, j)),
        out_shape=jax.ShapeDtypeStruct((B, nr, cols), F32),
        compiler_params=_cparams(("parallel", "parallel")),
        name="fft_stage_c",
    )(c, z, gate, bias_row, dc["g_inv"][0])


def _long_conv_gated(z, gate, hf, order, skip, dc):
    B, nr, cols = z.shape
    N1, N2 = dc["N1"], dc["N2"]
    W = cols // N2
    a = _fft_stage_a(z, dc["fa_half"], N1, False).reshape(B, 2, N1, N2, W)
    c = _fft_stage_b(a, hf, order, dc).reshape(B, 2, N1, cols)
    cb = min(cols, 4096)
    bias_row = jnp.tile(skip, cb // W)[None, :]
    return _fft_stage_c(c, z, gate, bias_row, dc)


HALO_ROWS = 16


def _short_conv_kernel(u_ref, prev_ref, next_ref, w_ref, b_ref, v_ref, x1_ref, x2_ref):
    i = pl.program_id(1)
    u = u_ref[0].astype(F32)
    tl = u.shape[0]
    row = lax.broadcasted_iota(jnp.int32, (tl, 1), 0)
    prev_last = prev_ref[0].astype(F32)[HALO_ROWS - 1:HALO_ROWS, :]
    next_first = next_ref[0].astype(F32)[0:1, :]
    before = jnp.where(i == 0, 0.0, prev_last)
    after = jnp.where(i == pl.num_programs(1) - 1, 0.0, next_first)
    up = jnp.where(row == 0, before, pltpu.roll(u, 1, 0))
    un = jnp.where(row == tl - 1, after, pltpu.roll(u, tl - 1, 0))
    w = w_ref[...]
    uc = up * w[0:1] + u * w[1:2] + un * w[2:3] + b_ref[...]
    W = HYENA_WIDTH
    v_ref[0] = uc[:, :W]
    x1_ref[0] = uc[:, W:2 * W]
    x2_ref[0] = uc[:, 2 * W:]


def _short_conv(hy, conv_w, conv_b, B, L):
    W3 = 3 * HYENA_WIDTH
    tl = _row_tile(L, 512)
    nl = L // tl
    u = hy.reshape(B, L, W3)
    rh = tl // HALO_ROWS
    spec_o = pl.BlockSpec((1, tl, HYENA_WIDTH), lambda b, i: (b, i, 0))
    return pl.pallas_call(
        _short_conv_kernel,
        grid=(B, nl),
        in_specs=[pl.BlockSpec((1, tl, W3), lambda b, i: (b, i, 0)),
                  pl.BlockSpec((1, HALO_ROWS, W3), lambda b, i: (b, jnp.maximum(i * rh - 1, 0), 0)),
                  pl.BlockSpec((1, HALO_ROWS, W3),
                               lambda b, i: (b, jnp.minimum((i + 1) * rh, L // HALO_ROWS - 1), 0)),
                  pl.BlockSpec((3, W3), lambda b, i: (0, 0)),
                  pl.BlockSpec((1, W3), lambda b, i: (0, 0))],
        out_specs=[spec_o, spec_o, spec_o],
        out_shape=[jax.ShapeDtypeStruct((B, L, HYENA_WIDTH), F32)] * 3,
        compiler_params=_cparams(("parallel", "parallel")),
        name="hyena_short_conv",
    )(u, u, u, conv_w, conv_b[None, :])


def _hyena(hy, hf, conv_w, conv_b, skip, dc, B, L):
    v, x1, x2 = _short_conv(hy, conv_w, conv_b, B, L)
    shape = (B, dc["N1"] // 2, dc["N2"] * HYENA_WIDTH)
    z = _long_conv_gated(v.reshape(shape), x1.reshape(shape), hf, 0, skip[0], dc)
    y = _long_conv_gated(z, x2.reshape(shape), hf, 1, skip[1], dc)
    return y.reshape(B * L, HYENA_WIDTH)


def _merge_kernel(x_ref, oa_ref, oh_ref, hg_ref, oc_ref, gl_ref, hgn_ref, wb_ref, wo_ref, g_ref, b_ref,
                  out_ref):
    o = oh_ref[0] + oh_ref[1]
    parts = []
    for hd in range(HGRN_HEADS):
        oh = o[:, hd * HGRN_DIM:(hd + 1) * HGRN_DIM]
        ms = jnp.mean(oh * oh, axis=-1, keepdims=True)
        parts.append(oh * lax.rsqrt(ms + RMS_EPS))
    o_h = jnp.concatenate(parts, axis=-1) * hgn_ref[...] * _silu(hg_ref[...].astype(F32))
    gl = gl_ref[...].astype(F32)
    branches = (oa_ref[...], o_h, oc_ref[...])
    merged = None
    for j, ob in enumerate(branches):
        term = _sigmoid(gl[:, j * D_MODEL:(j + 1) * D_MODEL]) * _dot(ob.astype(BF16), wb_ref[j])
        merged = term if merged is None else merged + term
    mix = _dot(merged.astype(BF16), wo_ref[...])
    out_ref[...] = _layer_norm(ALPHA * x_ref[...] + mix, g_ref[...], b_ref[...])


def _merge(x2, o_a, o_dir, hg, o_c, gl, hgrn_gain, wb_bf16, wo_bf16, ln_g, ln_b):
    T = x2.shape[0]
    tm = _row_tile(T, 512)
    W = BRANCH_WIDTH
    row = lambda w: pl.BlockSpec((tm, w), lambda i: (i, 0))
    return pl.pallas_call(
        _merge_kernel,
        grid=(T // tm,),
        in_specs=[row(D_MODEL), row(W), pl.BlockSpec((2, tm, W), lambda i: (0, i, 0)), row(W), row(W),
                  row(N_BRANCH * D_MODEL),
                  pl.BlockSpec((1, W), lambda i: (0, 0)),
                  pl.BlockSpec((N_BRANCH, W, D_MODEL), lambda i: (0, 0, 0)),
                  pl.BlockSpec((D_MODEL, D_MODEL), lambda i: (0, 0)),
                  pl.BlockSpec((1, D_MODEL), lambda i: (0, 0)),
                  pl.BlockSpec((1, D_MODEL), lambda i: (0, 0))],
        out_specs=row(D_MODEL),
        out_shape=jax.ShapeDtypeStruct((T, D_MODEL), F32),
        compiler_params=_cparams(("parallel",)),
        name="merge_out_ln",
    )(x2, o_a, o_dir.reshape(2, T, W), hg, o_c, gl, jnp.tile(hgrn_gain, HGRN_HEADS)[None, :],
      wb_bf16, wo_bf16, ln_g[None, :], ln_b[None, :])


def _ffn_kernel(x_ref, wg_ref, wu_ref, wd_ref, g_ref, b_ref, out_ref, acc_ref):
    j = pl.program_id(1)
    x = x_ref[...]
    xb = x.astype(BF16)
    hmid = _silu(_dot(xb, wg_ref[...])) * _dot(xb, wu_ref[...])
    contrib = _dot(hmid.astype(BF16), wd_ref[...])

    @pl.when(j == 0)
    def _():
        acc_ref[...] = contrib

    @pl.when(j > 0)
    def _():
        acc_ref[...] += contrib

    @pl.when(j == pl.num_programs(1) - 1)
    def _():
        out_ref[...] = _layer_norm(ALPHA * x + acc_ref[...], g_ref[...], b_ref[...])


def _ffn_dense(x2, w_gu, w_down, ln_g, ln_b):
    T = x2.shape[0]
    tm = _row_tile(T, 512)
    ft = 1408
    nf = D_FF_DENSE // ft
    return pl.pallas_call(
        _ffn_kernel,
        grid=(T // tm, nf),
        in_specs=[pl.BlockSpec((tm, D_MODEL), lambda i, j: (i, 0)),
                  pl.BlockSpec((D_MODEL, ft), lambda i, j: (0, j)),
                  pl.BlockSpec((D_MODEL, ft), lambda i, j: (0, j + nf)),
                  pl.BlockSpec((ft, D_MODEL), lambda i, j: (j, 0)),
                  pl.BlockSpec((1, D_MODEL), lambda i, j: (0, 0)),
                  pl.BlockSpec((1, D_MODEL), lambda i, j: (0, 0))],
        out_specs=pl.BlockSpec((tm, D_MODEL), lambda i, j: (i, 0)),
        out_shape=jax.ShapeDtypeStruct((T, D_MODEL), F32),
        scratch_shapes=[pltpu.VMEM((tm, D_MODEL), F32)],
        compiler_params=_cparams(("parallel", "arbitrary")),
        name="ffn_dense_ln",
    )(x2, w_gu, w_gu, w_down, ln_g[None, :], ln_b[None, :])


MOE_BLOCK = 1024
MOE_SUB = 128
MOE_TILE = 256


def _route_kernel(x_ref, r_ref, col_ref, row_ref, s_ref):
    tb = x_ref.shape[0]
    logits = _dot3(x_ref[...], r_ref[...])
    lane = lax.broadcasted_iota(jnp.int32, logits.shape, 1)
    logits = jnp.where(lane < N_EXPERTS, logits, -jnp.inf)
    m1 = jnp.max(logits, axis=-1, keepdims=True)
    i1 = jnp.min(jnp.where(logits == m1, lane, LANES), axis=-1, keepdims=True)
    rest = jnp.where(lane == i1, -jnp.inf, logits)
    m2 = jnp.max(rest, axis=-1, keepdims=True)
    i2 = jnp.min(jnp.where(rest == m2, lane, LANES), axis=-1, keepdims=True)
    e2 = jnp.exp(m2 - m1)
    w1 = 1.0 / (1.0 + e2)
    w2 = e2 / (1.0 + e2)
    oh1 = lane == i1
    oh2 = lane == i2
    sel = jnp.where(oh1, 1.0, jnp.where(oh2, 1.0, 0.0))
    rr = lax.broadcasted_iota(jnp.int32, (tb, tb), 0)
    cc = lax.broadcasted_iota(jnp.int32, (tb, tb), 1)
    earlier = jnp.where(rr > cc, 1.0, 0.0).astype(BF16)
    rank = _dot(earlier, sel.astype(BF16))
    cnt = jnp.sum(sel, axis=0, keepdims=True)
    nsub = jnp.floor((cnt + (MOE_SUB - 1)) * (1.0 / MOE_SUB))
    r2 = lax.broadcasted_iota(jnp.int32, (LANES, LANES), 0)
    c2 = lax.broadcasted_iota(jnp.int32, (LANES, LANES), 1)
    before = jnp.where(r2 < c2, 1.0, 0.0).astype(BF16)
    nsub8 = jnp.broadcast_to(nsub, (8, LANES))
    off8 = _dot(nsub8.astype(BF16), before) * MOE_SUB
    base = off8[0:1] + rank
    d1 = jnp.sum(jnp.where(oh1, base, 0.0), axis=-1, keepdims=True)
    d2 = jnp.sum(jnp.where(oh2, base, 0.0), axis=-1, keepdims=True)
    col = jnp.where(lane == 0, d1, jnp.where(lane == 1, d2, jnp.where(lane == 2, w1, jnp.where(lane == 3, w2, 0.0))))
    col_ref[...] = col
    row_ref[0] = col.T[0:8, :]
    lane8 = lax.broadcasted_iota(jnp.int32, (8, LANES), 1)
    info = jnp.where(lane8 < N_EXPERTS, off8, pltpu.roll(nsub8, N_EXPERTS, 1))
    s_ref[0] = info.astype(jnp.int32)


def _moe_route(x2, router):
    T = x2.shape[0]
    tb = _row_tile(T, MOE_BLOCK)
    nb = T // tb
    r_pad = jnp.pad(router, ((0, 0), (0, LANES - N_EXPERTS)))
    return pl.pallas_call(
        _route_kernel,
        grid=(nb,),
        in_specs=[pl.BlockSpec((tb, D_MODEL), lambda i: (i, 0)),
                  pl.BlockSpec((D_MODEL, LANES), lambda i: (0, 0))],
        out_specs=[pl.BlockSpec((tb, LANES), lambda i: (i, 0)),
                   pl.BlockSpec((1, 8, tb), lambda i: (i, 0, 0)),
                   pl.BlockSpec((1, 8, LANES), lambda i: (i, 0, 0))],
        out_shape=[jax.ShapeDtypeStruct((T, LANES), F32),
                   jax.ShapeDtypeStruct((nb, 8, tb), F32),
                   jax.ShapeDtypeStruct((nb, 8, LANES), jnp.int32)],
        compiler_params=_cparams(("parallel",)),
        name="moe_route",
    )(x2, r_pad)


def _moe_kernel(s_ref, x_ref, col_ref, row_ref, wg_ref, wu_ref, wd_ref, g_ref, b_ref, out_ref,
                xs_ref, ys_ref):
    blk = pl.program_id(0)
    e = pl.program_id(1)
    j = pl.program_id(2)
    tb = x_ref.shape[0]
    n_tiles = xs_ref.shape[0] // MOE_TILE
    used = s_ref[blk, N_EXPERTS - 1] + s_ref[blk, 2 * N_EXPERTS - 1] * MOE_SUB
    d1r = row_ref[0, 0:1, :]
    d2r = row_ref[0, 1:2, :]

    @pl.when(jnp.logical_and(e == 0, j == 0))
    def _():
        xb = x_ref[...].astype(BF16)
        for rt in range(n_tiles):
            @pl.when(rt * MOE_TILE < used)
            def _():
                r = (lax.broadcasted_iota(jnp.int32, (MOE_TILE, tb), 0) + rt * MOE_TILE).astype(F32)
                p = jnp.where(r == d1r, 1.0, jnp.where(r == d2r, 1.0, 0.0)).astype(BF16)
                xs_ref[rt * MOE_TILE:(rt + 1) * MOE_TILE, :] = _dot(p, xb).astype(BF16)

    start = s_ref[blk, e]
    nsub = s_ref[blk, N_EXPERTS + e]

    def expert_rows(first, n_rows):
        rows = pl.ds(pl.multiple_of(first, MOE_SUB), n_rows)
        xt = xs_ref[rows, :]
        hmid = _silu(_dot(xt, wg_ref[0])) * _dot(xt, wu_ref[0])
        y = _dot(hmid.astype(BF16), wd_ref[0])

        @pl.when(j == 0)
        def _():
            ys_ref[rows, :] = y

        @pl.when(j > 0)
        def _():
            ys_ref[rows, :] += y

    n_big = nsub // 4
    rem = nsub - 4 * n_big

    def big(i, carry):
        expert_rows(start + i * (4 * MOE_SUB), 4 * MOE_SUB)
        return carry

    lax.fori_loop(0, n_big, big, 0)
    tail = start + n_big * (4 * MOE_SUB)

    @pl.when(rem >= 2)
    def _():
        expert_rows(tail, 2 * MOE_SUB)

    @pl.when(rem % 2 == 1)
    def _():
        expert_rows(tail + (rem // 2) * (2 * MOE_SUB), MOE_SUB)

    @pl.when(jnp.logical_and(e == pl.num_programs(1) - 1, j == pl.num_programs(2) - 1))
    def _():
        @pl.when((used // MOE_SUB) % 2 == 1)
        def _():
            ys_ref[pl.ds(pl.multiple_of(used, MOE_SUB), MOE_SUB), :] = jnp.zeros((MOE_SUB, D_MODEL), F32)

        w1r = row_ref[0, 2:3, :]
        w2r = row_ref[0, 3:4, :]
        for rt in range(n_tiles):
            @pl.when(rt * MOE_TILE < used)
            def _():
                r = (lax.broadcasted_iota(jnp.int32, (MOE_TILE, tb), 0) + rt * MOE_TILE).astype(F32)
                w_rows = jnp.sum(jnp.where(r == d1r, w1r, 0.0) + jnp.where(r == d2r, w2r, 0.0),
                                 axis=-1, keepdims=True)
                rows = slice(rt * MOE_TILE, (rt + 1) * MOE_TILE)
                xs_ref[rows, :] = (ys_ref[rows, :] * w_rows).astype(BF16)

        d1c = col_ref[:, 0:1]
        d2c = col_ref[:, 1:2]
        out_ref[...] = jnp.zeros(out_ref.shape, F32)
        for rt in range(n_tiles):
            @pl.when(rt * MOE_TILE < used)
            def _():
                c = (lax.broadcasted_iota(jnp.int32, (tb, MOE_TILE), 1) + rt * MOE_TILE).astype(F32)
                q = jnp.where(c == d1c, 1.0, jnp.where(c == d2c, 1.0, 0.0)).astype(BF16)
                out_ref[...] += _dot(q, xs_ref[rt * MOE_TILE:(rt + 1) * MOE_TILE, :])

        out_ref[...] = _layer_norm(ALPHA * x_ref[...] + out_ref[...], g_ref[...], b_ref[...])


def _ffn_moe(x2, router, w_gu, w_down, ln_g, ln_b):
    T = x2.shape[0]
    tb = _row_tile(T, MOE_BLOCK)
    nb = T // tb
    n_rows = 2 * tb + N_EXPERTS * MOE_SUB
    col, row, sinfo = _moe_route(x2, router)
    ft = 896
    nf = D_FF_EXPERT // ft
    grid_spec = pltpu.PrefetchScalarGridSpec(
        num_scalar_prefetch=1,
        grid=(nb, N_EXPERTS, nf),
        in_specs=[pl.BlockSpec((tb, D_MODEL), lambda i, e, j, s: (i, 0)),
                  pl.BlockSpec((tb, LANES), lambda i, e, j, s: (i, 0)),
                  pl.BlockSpec((1, 8, tb), lambda i, e, j, s: (i, 0, 0)),
                  pl.BlockSpec((1, D_MODEL, ft), lambda i, e, j, s: (e, 0, j)),
                  pl.BlockSpec((1, D_MODEL, ft), lambda i, e, j, s: (e, 0, j + nf)),
                  pl.BlockSpec((1, ft, D_MODEL), lambda i, e, j, s: (e, j, 0)),
                  pl.BlockSpec((1, D_MODEL), lambda i, e, j, s: (0, 0)),
                  pl.BlockSpec((1, D_MODEL), lambda i, e, j, s: (0, 0))],
        out_specs=pl.BlockSpec((tb, D_MODEL), lambda i, e, j, s: (i, 0)),
        scratch_shapes=[pltpu.VMEM((n_rows, D_MODEL), BF16), pltpu.VMEM((n_rows, D_MODEL), F32)])
    return pl.pallas_call(
        _moe_kernel,
        grid_spec=grid_spec,
        out_shape=jax.ShapeDtypeStruct((T, D_MODEL), F32),
        compiler_params=_cparams(("parallel", "arbitrary", "arbitrary")),
        name="ffn_moe_ln",
    )(sinfo[:, 0, :], x2, col, row, w_gu, w_gu, w_down, ln_g[None, :], ln_b[None, :])


def _trunk(x, p):
    B, L, _ = x.shape
    cos_t, sin_t = _rope_tables(L)
    filt_consts = _filter_features(L)
    dc = _dft_consts(L)
    x2 = x.reshape(B * L, D_MODEL)
    for l in range(DEPTH):
        aq, akv, hq, hf, hi, hg, hy, gl = _in_proj(x2, p["w_in"][l])
        qn, kn, vn = _attn_prep(aq, akv, cos_t, sin_t, p["q_norm_g"][l], p["k_norm_g"][l], B, L)
        o_a = _flash_attention(qn, kn, vn, B, L)
        o_dir = _hgrn(hq, hf, hi, p["lower_bounds"][l], B, L)
        k2l = _hyena_filter(L, filt_consts, p["filt_w1"][l], p["filt_b1"][l], p["filt_freq"][l],
                            p["filt_w2"][l], p["filt_b2"][l], p["filt_w3"][l])
        spec = _filter_spectrum(k2l, dc)
        o_c = _hyena(hy, spec, p["conv_w"][l], p["conv_b"][l], p["hyena_skip"][l], dc, B, L)
        x2 = _merge(x2, o_a, o_dir, hg, o_c, gl, p["hgrn_norm_g"][l], p["w_branch"][l], p["w_out"][l],
                    p["ln1_g"][l], p["ln1_b"][l])
        if l % 2 == 0:
            x2 = _ffn_dense(x2, p["ffn_w_gu"][l // 2], p["ffn_w_down"][l // 2], p["ln2_g"][l], p["ln2_b"][l])
        else:
            x2 = _ffn_moe(x2, p["router_w"][l // 2], p["expert_w_gu"][l // 2], p["expert_w_down"][l // 2],
                          p["ln2_g"][l], p["ln2_b"][l])
    return x2.reshape(B, L, D_MODEL)


def _in_proj_weight(w_in):
    return w_in.astype(BF16)


def kernel(x_prompt, x_sample, w_in, q_norm_g, k_norm_g, hgrn_lb, hgrn_norm_g, conv_w, conv_b, filt_w1, filt_b1, filt_freq, filt_w2, filt_b2, filt_w3, hyena_skip, w_branch, w_out, ln1_g, ln1_b, ln2_g, ln2_b, ffn_w_gu, ffn_w_down, router_w, expert_w_gu, expert_w_down):
    s = jax.nn.softmax(hgrn_lb.astype(F32), axis=0)
    p = dict(
        w_in=_in_proj_weight(w_in), q_norm_g=q_norm_g, k_norm_g=k_norm_g,
        lower_bounds=jnp.cumsum(s, axis=0) - s[:1], hgrn_norm_g=hgrn_norm_g,
        conv_w=conv_w, conv_b=conv_b, filt_w1=filt_w1, filt_b1=filt_b1, filt_freq=filt_freq,
        filt_w2=filt_w2, filt_b2=filt_b2, filt_w3=filt_w3, hyena_skip=hyena_skip,
        w_branch=w_branch.astype(BF16), w_out=w_out.astype(BF16),
        ln1_g=ln1_g, ln1_b=ln1_b, ln2_g=ln2_g, ln2_b=ln2_b,
        ffn_w_gu=ffn_w_gu.astype(BF16), ffn_w_down=ffn_w_down.astype(BF16), router_w=router_w,
        expert_w_gu=expert_w_gu.astype(BF16), expert_w_down=expert_w_down.astype(BF16))
    return (_trunk(x_prompt, p), _trunk(x_sample, p))
```

```python
import functools
import math

import numpy as np
import jax
import jax.numpy as jnp
from jax import lax
from jax.experimental import pallas as pl
from jax.experimental.pallas import tpu as pltpu

F32 = jnp.float32
BF16 = jnp.bfloat16

D_MODEL = 1024
DEPTH = 2
GRID_W = 64
BRANCH_WIDTH = 512
ATTN_HEADS = 8
ATTN_KV_HEADS = 2
ATTN_GROUP = ATTN_HEADS // ATTN_KV_HEADS
HEAD_DIM = 64
KV_WIDTH = ATTN_KV_HEADS * HEAD_DIM
ROPE_AXIS_DIM = HEAD_DIM // 2
ROPE_HALF = ROPE_AXIS_DIM // 2
ROPE_THETA = 10000.0
HGRN_HEADS = 4
HGRN_DIM = 128
HGRN_CHUNK = 64
FORGET_FLOOR = 1e-30
HYENA_WIDTH = BRANCH_WIDTH
HYENA_ORDER = 2
FILTER_EMB = 33
FILTER_BANDS = (FILTER_EMB - 1) // 2
FILTER_HIDDEN = 64
FILTER_FAST_DECAY = 0.3
FILTER_SLOW_DECAY = 1.5
FILTER_TARGET = 1e-2
N_FILTER_CH = HYENA_ORDER * 2 * HYENA_WIDTH
N_BRANCH = 3
D_FF_DENSE = 2816
N_EXPERTS = 8
D_FF_EXPERT = 3584
ALPHA = (2 * DEPTH) ** 0.25
LN_EPS = 1e-5
RMS_EPS = 1e-6

SEG_WIDTHS = (BRANCH_WIDTH,
              2 * KV_WIDTH,
              BRANCH_WIDTH,
              2 * BRANCH_WIDTH,
              BRANCH_WIDTH,
              BRANCH_WIDTH,
              3 * HYENA_WIDTH,
              N_BRANCH * D_MODEL)
IN_WIDTH = sum(SEG_WIDTHS)
SEG_DTYPES = (BF16, BF16, BF16, F32, BF16, BF16, BF16, BF16)

LANES = 128
FFT_N2 = 128
VMEM_LIMIT = 56 * 1024 * 1024


def _cparams(sem):
    return pltpu.CompilerParams(dimension_semantics=sem, vmem_limit_bytes=VMEM_LIMIT)


def _dot(a, b):
    return jnp.dot(a, b, preferred_element_type=F32)


def _dot_nt(a, b):
    return lax.dot_general(a, b, (((1,), (1,)), ((), ())), preferred_element_type=F32)


def _dot_tn(a, b):
    return lax.dot_general(a, b, (((0,), (0,)), ((), ())), preferred_element_type=F32)


def _split2(a):
    hi = a.astype(BF16)
    lo = (a - hi.astype(F32)).astype(BF16)
    return hi, lo


def _dot3(a, b):
    ah, al = _split2(a)
    bh, bl = _split2(b)
    return _dot(ah, bh) + _dot(al, bh) + _dot(ah, bl)


def _layer_norm(y, g, b):
    mu = jnp.mean(y, axis=-1, keepdims=True)
    yc = y - mu
    var = jnp.mean(yc * yc, axis=-1, keepdims=True)
    return yc * lax.rsqrt(var + LN_EPS) * g + b


def _sigmoid(x):
    return 1.0 / (1.0 + jnp.exp(-x))


def _silu(x):
    return x * _sigmoid(x)


def _row_tile(n, want):
    t = min(n, want)
    assert n % t == 0
    return t


def _in_proj_kernel(x_ref, w_ref, *out_refs):
    x = x_ref[...].astype(BF16)
    off = 0
    for o_ref, width in zip(out_refs, SEG_WIDTHS):
        o_ref[...] = _dot(x, w_ref[:, off:off + width]).astype(o_ref.dtype)
        off += width


def _in_proj(x2, w_bf16):
    T = x2.shape[0]
    tm = _row_tile(T, 512)
    return pl.pallas_call(
        _in_proj_kernel,
        grid=(T // tm,),
        in_specs=[pl.BlockSpec((tm, D_MODEL), lambda i: (i, 0)),
                  pl.BlockSpec((D_MODEL, IN_WIDTH), lambda i: (0, 0), pipeline_mode=pl.Buffered(1))],
        out_specs=[pl.BlockSpec((tm, w), lambda i: (i, 0)) for w in SEG_WIDTHS],
        out_shape=[jax.ShapeDtypeStruct((T, w), dt) for w, dt in zip(SEG_WIDTHS, SEG_DTYPES)],
        compiler_params=_cparams(("parallel",)),
        name="in_proj",
    )(x2, w_bf16)


def _rope_tables(L):
    rows = L // GRID_W
    row = jnp.broadcast_to(jnp.arange(rows, dtype=F32)[:, None], (rows, GRID_W)).reshape(-1)
    col = jnp.broadcast_to(jnp.arange(GRID_W, dtype=F32)[None, :], (rows, GRID_W)).reshape(-1)
    inv = jnp.power(ROPE_THETA, -2.0 * jnp.arange(ROPE_HALF, dtype=F32) / ROPE_AXIS_DIM)
    a0, a1 = row[:, None] * inv, col[:, None] * inv
    c0, s0, c1, s1 = jnp.cos(a0), jnp.sin(a0), jnp.cos(a1), jnp.sin(a1)
    cos_h = jnp.concatenate([c0, c0, c1, c1], axis=-1)
    sin_h = jnp.concatenate([-s0, s0, -s1, s1], axis=-1)
    return jnp.tile(cos_h, (1, ATTN_HEADS)), jnp.tile(sin_h, (1, ATTN_HEADS))


def _head_block_ones(width):
    idx = np.arange(width) // HEAD_DIM
    return jnp.asarray(idx[:, None] == idx[None, :], dtype=BF16)


def _norm_rope(x, gain, cos, sin, ones_bd):
    width = x.shape[-1]
    sq_hi, sq_lo = _split2(x * x)
    ms = (_dot(sq_hi, ones_bd) + _dot(sq_lo, ones_bd)) * (1.0 / HEAD_DIM)
    xn = x * lax.rsqrt(ms + RMS_EPS) * gain
    lane = lax.broadcasted_iota(jnp.int32, xn.shape, 1)
    first_half = (lane % ROPE_AXIS_DIM) < ROPE_HALF
    swapped = jnp.where(first_half, pltpu.roll(xn, width - ROPE_HALF, 1), pltpu.roll(xn, ROPE_HALF, 1))
    return xn * cos + swapped * sin


LOG2E = 1.4426950408889634
V_EXT = 2 * HEAD_DIM
FLASH_SKEW = 4


def _attn_prep_kernel(aq_ref, akv_ref, cos_ref, sin_ref, qg_ref, kg_ref, bdq_ref, bdk_ref,
                      q_out, kt_out, v_out):
    cos, sin = cos_ref[...], sin_ref[...]
    q = _norm_rope(aq_ref[...].astype(F32), qg_ref[...], cos, sin, bdq_ref[...])
    q_out[...] = (q * (LOG2E * HEAD_DIM ** -0.5)).astype(BF16)
    kv = akv_ref[...].astype(F32)
    k = _norm_rope(kv[:, :KV_WIDTH], kg_ref[...], cos[:, :KV_WIDTH], sin[:, :KV_WIDTH], bdk_ref[...])
    kt_out[0] = k.T.astype(BF16)
    ones = jnp.ones((kv.shape[0], HEAD_DIM), F32)
    v_ext = [kv[:, KV_WIDTH + h * HEAD_DIM:KV_WIDTH + (h + 1) * HEAD_DIM] if part == 0 else ones
             for h in range(ATTN_KV_HEADS) for part in range(2)]
    v_out[...] = jnp.concatenate(v_ext, axis=-1).astype(BF16)


def _attn_prep(aq, akv, cos_t, sin_t, q_gain, k_gain, B, L):
    T = aq.shape[0]
    tm = _row_tile(L, 512)
    nl = L // tm
    qg = jnp.tile(q_gain, ATTN_HEADS)[None, :]
    kg = jnp.tile(k_gain, ATTN_KV_HEADS)[None, :]
    const = lambda i: (0, 0)
    return pl.pallas_call(
        _attn_prep_kernel,
        grid=(T // tm,),
        in_specs=[pl.BlockSpec((tm, BRANCH_WIDTH), lambda i: (i, 0)),
                  pl.BlockSpec((tm, 2 * KV_WIDTH), lambda i: (i, 0)),
                  pl.BlockSpec((tm, BRANCH_WIDTH), lambda i: (i % nl, 0)),
                  pl.BlockSpec((tm, BRANCH_WIDTH), lambda i: (i % nl, 0)),
                  pl.BlockSpec((1, BRANCH_WIDTH), const),
                  pl.BlockSpec((1, KV_WIDTH), const),
                  pl.BlockSpec((BRANCH_WIDTH, BRANCH_WIDTH), const),
                  pl.BlockSpec((KV_WIDTH, KV_WIDTH), const)],
        out_specs=[pl.BlockSpec((tm, BRANCH_WIDTH), lambda i: (i, 0)),
                   pl.BlockSpec((1, KV_WIDTH, tm), lambda i: (i // nl, 0, i % nl)),
                   pl.BlockSpec((tm, ATTN_KV_HEADS * V_EXT), lambda i: (i, 0))],
        out_shape=[jax.ShapeDtypeStruct((T, BRANCH_WIDTH), BF16),
                   jax.ShapeDtypeStruct((B, KV_WIDTH, L), BF16),
                   jax.ShapeDtypeStruct((T, ATTN_KV_HEADS * V_EXT), BF16)],
        compiler_params=_cparams(("parallel",)),
        name="attn_prep",
    )(aq, akv, cos_t, sin_t, qg, kg, _head_block_ones(BRANCH_WIDTH), _head_block_ones(KV_WIDTH))


def _flash_kernel(q_ref, kt_ref, v_ref, o_ref, qs_ref, m_ref, acc_ref, *, tq, rs):
    kv = pl.program_id(2)

    @pl.when(kv == 0)
    def _():
        for h in range(ATTN_HEADS):
            g = h % ATTN_GROUP
            qs_ref[h // ATTN_GROUP, g * tq:(g + 1) * tq, :] = q_ref[0, :, h * HEAD_DIM:(h + 1) * HEAD_DIM]
        m_ref[...] = jnp.full(m_ref.shape, -jnp.inf, F32)
        acc_ref[...] = jnp.zeros(acc_ref.shape, F32)

    kts = [kt_ref[0, kh * HEAD_DIM:(kh + 1) * HEAD_DIM, :] for kh in range(ATTN_KV_HEADS)]
    vs = [v_ref[0, :, kh * V_EXT:(kh + 1) * V_EXT] for kh in range(ATTN_KV_HEADS)]
    tasks = [(kh, slice(sb * rs, (sb + 1) * rs))
             for sb in range(ATTN_GROUP * tq // rs) for kh in range(ATTN_KV_HEADS)]

    def scores(t):
        kh, rows = tasks[t]
        return _dot(qs_ref[kh, rows, :], kts[kh])

    pending = {t: scores(t) for t in range(min(FLASH_SKEW, len(tasks)))}
    for t, (kh, rows) in enumerate(tasks):
        s = pending.pop(t)
        if t + FLASH_SKEW < len(tasks):
            pending[t + FLASH_SKEW] = scores(t + FLASH_SKEW)
        m_prev = m_ref[kh, rows, :]
        m_new = jnp.maximum(m_prev, jnp.max(s, axis=-1, keepdims=True))
        p = jnp.exp2(s - m_new[:, 0:1])
        acc_ref[kh, rows, :] = jnp.exp2(m_prev - m_new) * acc_ref[kh, rows, :] + _dot(p.astype(BF16), vs[kh])
        m_ref[kh, rows, :] = m_new

    @pl.when(kv == pl.num_programs(2) - 1)
    def _():
        for h in range(ATTN_HEADS):
            kh, g = h // ATTN_GROUP, h % ATTN_GROUP
            a = acc_ref[kh, g * tq:(g + 1) * tq, :]
            o_ref[0, :, h * HEAD_DIM:(h + 1) * HEAD_DIM] = a[:, :HEAD_DIM] / a[:, HEAD_DIM:]


def _flash_attention(qn, kt, v_ext, B, L):
    tq = _row_tile(L, 256)
    tk = _row_tile(L, 1024)
    rs = min(128, ATTN_GROUP * tq)
    out = pl.pallas_call(
        functools.partial(_flash_kernel, tq=tq, rs=rs),
        grid=(B, L // tq, L // tk),
        in_specs=[pl.BlockSpec((1, tq, BRANCH_WIDTH), lambda b, i, j: (b, i, 0)),
                  pl.BlockSpec((1, KV_WIDTH, tk), lambda b, i, j: (b, 0, j)),
                  pl.BlockSpec((1, tk, ATTN_KV_HEADS * V_EXT), lambda b, i, j: (b, j, 0))],
        out_specs=pl.BlockSpec((1, tq, BRANCH_WIDTH), lambda b, i, j: (b, i, 0)),
        out_shape=jax.ShapeDtypeStruct((B, L, BRANCH_WIDTH), F32),
        scratch_shapes=[pltpu.VMEM((ATTN_KV_HEADS, ATTN_GROUP * tq, HEAD_DIM), BF16),
                        pltpu.VMEM((ATTN_KV_HEADS, ATTN_GROUP * tq, LANES), F32),
                        pltpu.VMEM((ATTN_KV_HEADS, ATTN_GROUP * tq, V_EXT), F32)],
        compiler_params=_cparams(("parallel", "parallel", "arbitrary")),
        name="flash_attention",
    )(qn.reshape(B, L, BRANCH_WIDTH), kt, v_ext.reshape(B, L, ATTN_KV_HEADS * V_EXT))
    return out.reshape(B * L, BRANCH_WIDTH)


def _ref_rows(b, h, fwd):
    C, W = b.shape
    two_h = 2 * h
    if two_h >= 8:
        n = C // two_h
        b3 = b.reshape(n, two_h, W)
        r = jnp.where(fwd, b3[:, h - 1:h, :], b3[:, h:h + 1, :])
        return jnp.broadcast_to(r, (n, two_h, W)).reshape(C, W)
    b3 = b.reshape(C // 8, 8, W)
    pos8 = lax.broadcasted_iota(jnp.int32, (1, 8, 1), 1)
    out = None
    for blk in range(8 // two_h):
        base = blk * two_h
        r = jnp.where(fwd, b3[:, base + h - 1:base + h, :], b3[:, base + h:base + h + 1, :])
        r = jnp.broadcast_to(r, b3.shape)
        out = r if out is None else jnp.where((pos8 // two_h) == blk, r, out)
    return out.reshape(C, W)


def _hgrn_chunk(q, z, v, lb, tri, st_ref, fwd):
    C, W = q.shape
    q = q.astype(F32)
    v = v.astype(F32)
    e = jnp.exp(-jnp.abs(z))
    s_big = 1.0 / (1.0 + e)
    s_small = e * s_big
    pos = z >= 0.0
    f = lb + (1.0 - lb) * jnp.where(pos, s_big, s_small)
    g = jnp.log(jnp.maximum(f, FORGET_FLOOR))
    kk = (1.0 - lb) * jnp.where(pos, s_small, s_big)
    g_hi = g.astype(BF16)
    r1 = g - g_hi.astype(F32)
    g_mid = r1.astype(BF16)
    g_lo = (r1 - g_mid.astype(F32)).astype(BF16)
    b = _dot(tri, g_hi) + _dot(tri, g_mid) + _dot(tri, g_lo)
    b_end = jnp.where(fwd, b[C - 1:C, :], b[0:1, :])
    q_in = (q * jnp.exp(b)).astype(BF16)
    k_out = (kk * jnp.exp(b_end - b)).astype(BF16)
    s_decay = jnp.exp(b_end)
    vb = v.astype(BF16)

    row = lax.broadcasted_iota(jnp.int32, (C, 1), 0)
    rr = lax.broadcasted_iota(jnp.int32, (C, C), 0)
    cc = lax.broadcasted_iota(jnp.int32, (C, C), 1)
    a_mats = [jnp.zeros((C, C), F32) for _ in range(HGRN_HEADS)]
    h = C // 2
    while h >= 1:
        upper = ((row % (2 * h)) >= h).astype(jnp.int32)
        q_rows = upper == fwd.astype(jnp.int32)
        ref = _ref_rows(b, h, fwd)
        ql = jnp.where(q_rows, q * jnp.exp(jnp.where(q_rows, b - ref, 0.0)), 0.0).astype(BF16)
        kl = jnp.where(q_rows, 0.0, kk * jnp.exp(jnp.where(q_rows, 0.0, ref - b))).astype(BF16)
        same_block = (rr // (2 * h)) == (cc // (2 * h))
        for hd in range(HGRN_HEADS):
            sl = slice(hd * HGRN_DIM, (hd + 1) * HGRN_DIM)
            a_mats[hd] = a_mats[hd] + jnp.where(same_block, _dot_nt(ql[:, sl], kl[:, sl]), 0.0)
        h //= 2

    qk = q * kk
    outs = []
    for hd in range(HGRN_HEADS):
        sl = slice(hd * HGRN_DIM, (hd + 1) * HGRN_DIM)
        st = st_ref[hd]
        o = _dot_nt(q_in[:, sl], st.astype(BF16))
        o = o + _dot(a_mats[hd].astype(BF16), vb[:, sl])
        o = o + jnp.sum(qk[:, sl], axis=-1, keepdims=True) * v[:, sl]
        outs.append(o)
        st_ref[hd] = st * s_decay[:, sl] + _dot_tn(vb[:, sl], k_out[:, sl])
    return jnp.concatenate(outs, axis=-1)


def _hgrn_kernel(q_ref, f_ref, v_ref, lb_ref, tri_ref, o_ref, st_ref, *, n_inner):
    d = pl.program_id(0)
    fwd = d == 0

    @pl.when(pl.program_id(2) == 0)
    def _():
        st_ref[...] = jnp.zeros(st_ref.shape, F32)

    lb = lb_ref[0]
    tri = tri_ref[0]

    def body(i, carry):
        ci = jnp.where(fwd, i, n_inner - 1 - i)
        r0 = pl.multiple_of(ci * HGRN_CHUNK, HGRN_CHUNK)
        rows = pl.ds(r0, HGRN_CHUNK)
        o_ref[0, 0, rows, :] = _hgrn_chunk(q_ref[0, rows, :], f_ref[0, rows, :], v_ref[0, rows, :],
                                           lb, tri, st_ref, fwd)
        return carry

    lax.fori_loop(0, n_inner, body, 0)


def _hgrn(hq, hf, hi, lb, B, L):
    W = BRANCH_WIDTH
    rb = _row_tile(L, 512)
    nb = L // rb
    C = HGRN_CHUNK
    tri_f = np.tril(np.ones((C, C), np.float32))
    tri = jnp.asarray(np.stack([tri_f, tri_f.T]), dtype=BF16)
    blk = lambda d, b, c: (b, c + d * (nb - 1 - 2 * c), 0)
    return pl.pallas_call(
        functools.partial(_hgrn_kernel, n_inner=rb // C),
        grid=(2, B, nb),
        in_specs=[pl.BlockSpec((1, rb, W), blk),
                  pl.BlockSpec((1, rb, W), lambda d, b, c: (b, c + d * (nb - 1 - 2 * c), d)),
                  pl.BlockSpec((1, rb, W), blk),
                  pl.BlockSpec((1, 1, W), lambda d, b, c: (d, 0, 0)),
                  pl.BlockSpec((1, C, C), lambda d, b, c: (d, 0, 0))],
        out_specs=pl.BlockSpec((1, 1, rb, W), lambda d, b, c: (d, b, c + d * (nb - 1 - 2 * c), 0)),
        out_shape=jax.ShapeDtypeStruct((2, B, L, W), F32),
        scratch_shapes=[pltpu.VMEM((HGRN_HEADS, HGRN_DIM, HGRN_DIM), F32)],
        compiler_params=_cparams(("parallel", "parallel", "arbitrary")),
        name="hgrn_scan",
    )(hq.reshape(B, L, W), hf.reshape(B, L, 2 * W), hi.reshape(B, L, W), lb[:, None, :], tri)


def _filter_features(L):
    t = jnp.linspace(0.0, 1.0, L, dtype=F32)[:, None]
    w = 2.0 * math.pi * jnp.arange(L, dtype=F32) / L
    bands = jnp.linspace(1e-4, FILTER_BANDS - 1, FILTER_BANDS, dtype=F32)
    ang = w[:, None] * bands[None, :]
    feats = jnp.concatenate([t, jnp.cos(ang), -jnp.sin(ang)], axis=-1)
    feats2 = jnp.concatenate([feats, feats[:1], jnp.flip(feats[1:], axis=0)], axis=0)
    t2 = jnp.concatenate([t, t[:1], jnp.flip(t[1:], axis=0)], axis=0)
    deltas = jnp.abs(jnp.linspace(math.log(FILTER_TARGET) / FILTER_FAST_DECAY,
                                  math.log(FILTER_TARGET) / FILTER_SLOW_DECAY, N_FILTER_CH, dtype=F32))
    return feats2, t2, deltas[None, :]


def _filter_kernel(feat_ref, t_ref, w1_ref, b1_ref, fr_ref, w2_ref, b2_ref, w3f_ref, w3b_ref,
                   df_ref, db_ref, k_ref, *, L):
    freq = fr_ref[...]
    h = jnp.sin(freq * (_dot3(feat_ref[...], w1_ref[...]) + b1_ref[...]))
    h = jnp.sin(freq * (_dot3(h, w2_ref[...]) + b2_ref[...]))
    t = t_ref[...]
    top = _dot3(h[:L], w3f_ref[...]) * jnp.exp(-t[:L] * df_ref[...])
    bot = _dot3(h[L:], w3b_ref[...]) * jnp.exp(-t[L:] * db_ref[...])
    row = lax.broadcasted_iota(jnp.int32, (L, 1), 0)
    bot = jnp.where(row == 0, 0.0, bot)
    norm = jnp.sum(jnp.abs(top), axis=0, keepdims=True) + jnp.sum(jnp.abs(bot), axis=0, keepdims=True)
    k_ref[:L, :] = top / norm
    k_ref[L:, :] = bot / norm


def _hyena_filter(L, consts, w1, b1, freq, w2, b2, w3):
    feats2, t2, deltas = consts
    wb = 256
    nw = HYENA_WIDTH // wb
    full = lambda shape: pl.BlockSpec(shape, lambda o, j: (0,) * len(shape))
    fcol = lambda o, j: (0, o * 2 * nw + j)
    bcol = lambda o, j: (0, o * 2 * nw + nw + j)
    return pl.pallas_call(
        functools.partial(_filter_kernel, L=L),
        grid=(HYENA_ORDER, nw),
        in_specs=[full((2 * L, FILTER_EMB)), full((2 * L, 1)),
                  full((FILTER_EMB, FILTER_HIDDEN)), full((1, FILTER_HIDDEN)), full((1, FILTER_HIDDEN)),
                  full((FILTER_HIDDEN, FILTER_HIDDEN)), full((1, FILTER_HIDDEN)),
                  pl.BlockSpec((FILTER_HIDDEN, wb), fcol), pl.BlockSpec((FILTER_HIDDEN, wb), bcol),
                  pl.BlockSpec((1, wb), fcol), pl.BlockSpec((1, wb), bcol)],
        out_specs=pl.BlockSpec((2 * L, wb), lambda o, j: (0, o * nw + j)),
        out_shape=jax.ShapeDtypeStruct((2 * L, HYENA_ORDER * HYENA_WIDTH), F32),
        compiler_params=_cparams(("parallel", "parallel")),
        name="hyena_filter",
    )(feats2, t2, w1, b1[None, :], freq[None, :], w2, b2[None, :], w3, w3, deltas, deltas)


def _dft_consts(L):
    N = 2 * L
    N2 = FFT_N2
    N1 = N // N2
    k1 = np.arange(N1)
    ang_a = 2.0 * np.pi * np.outer(k1, np.arange(N1)) / N1
    fa_full = np.concatenate([np.cos(ang_a), -np.sin(ang_a)], axis=0)
    fa_half = fa_full[:, :N1 // 2]
    n2 = np.arange(N2)
    kk = k1[:, None, None] + N1 * np.arange(N2)[None, :, None]
    ang_b = 2.0 * np.pi * (kk * n2[None, None, :]) / N
    mr, mi = np.cos(ang_b), -np.sin(ang_b)
    m_fwd = np.concatenate([np.concatenate([mr, -mi], axis=2),
                            np.concatenate([mi, mr], axis=2)], axis=1)
    wr, wi = np.swapaxes(mr, 1, 2), -np.swapaxes(mi, 1, 2)
    m_inv = np.concatenate([np.concatenate([wr, -wi], axis=2),
                            np.concatenate([wi, wr], axis=2)], axis=1)
    ang_c = 2.0 * np.pi * np.outer(np.arange(N1 // 2), k1) / N1
    g_inv = np.concatenate([np.cos(ang_c), -np.sin(ang_c)], axis=1) / N

    def hi_lo(a):
        a = jnp.asarray(a, dtype=F32)
        hi = a.astype(BF16)
        return hi, (a - hi.astype(F32)).astype(BF16)

    return dict(N1=N1, N2=N2, fa_full=hi_lo(fa_full), fa_half=hi_lo(fa_half),
                m_fwd=hi_lo(m_fwd), m_inv=hi_lo(m_inv), g_inv=hi_lo(g_inv))


def _mm3(m_hi, m_lo, x):
    x_hi, x_lo = _split2(x)
    return _dot(m_hi, x_hi) + _dot(m_hi, x_lo) + _dot(m_lo, x_hi)


def _stage_a_kernel(z_ref, fh_ref, fl_ref, a_ref, *, exact):
    n1 = a_ref.shape[2]
    if exact:
        res = _mm3(fh_ref[...], fl_ref[...], z_ref[0])
    else:
        res = _dot(fh_ref[...], z_ref[0].astype(BF16))
    a_ref[0, 0] = res[:n1].astype(a_ref.dtype)
    a_ref[0, 1] = res[n1:].astype(a_ref.dtype)


def _fft_stage_a(z, fa, N1, exact):
    B, nr, cols = z.shape
    cb = min(cols, 4096)
    return pl.pallas_call(
        functools.partial(_stage_a_kernel, exact=exact),
        grid=(B, cols // cb),
        in_specs=[pl.BlockSpec((1, nr, cb), lambda b, j: (b, 0, j)),
                  pl.BlockSpec((2 * N1, nr), lambda b, j: (0, 0)),
                  pl.BlockSpec((2 * N1, nr), lambda b, j: (0, 0))],
        out_specs=pl.BlockSpec((1, 2, N1, cb), lambda b, j: (b, 0, 0, j)),
        out_shape=jax.ShapeDtypeStruct((B, 2, N1, cols), F32 if exact else BF16),
        compiler_params=_cparams(("parallel", "parallel")),
        name="fft_stage_a",
    )(z, fa[0], fa[1])


def _stage_b_filter_kernel(a_ref, mh_ref, ml_ref, o_ref):
    n2 = a_ref.shape[3]
    a = a_ref[0, :, 0].reshape(2 * n2, a_ref.shape[4])
    res = _mm3(mh_ref[0], ml_ref[0], a)
    o_ref[0, 0] = res[:n2]
    o_ref[1, 0] = res[n2:]


def _filter_spectrum(k2l, dc):
    N1, N2 = dc["N1"], dc["N2"]
    OW = k2l.shape[1]
    a = _fft_stage_a(k2l.reshape(1, N1, N2 * OW), dc["fa_full"], N1, True)
    a = a.reshape(1, 2, N1, N2, OW)
    wb = 512
    return pl.pallas_call(
        _stage_b_filter_kernel,
        grid=(N1, OW // wb),
        in_specs=[pl.BlockSpec((1, 2, 1, N2, wb), lambda k, j: (0, 0, k, 0, j)),
                  pl.BlockSpec((1, 2 * N2, 2 * N2), lambda k, j: (k, 0, 0)),
                  pl.BlockSpec((1, 2 * N2, 2 * N2), lambda k, j: (k, 0, 0))],
        out_specs=pl.BlockSpec((2, 1, N2, wb), lambda k, j: (0, k, 0, j)),
        out_shape=jax.ShapeDtypeStruct((2, N1, N2, OW), F32),
        compiler_params=_cparams(("parallel", "parallel")),
        name="fft_filter_spectrum",
    )(a, dc["m_fwd"][0], dc["m_fwd"][1])


def _stage_b_kernel(a_ref, h_ref, mf_ref, mi_ref, c_ref):
    n2 = a_ref.shape[3]
    for kk in range(a_ref.shape[2]):
        a = a_ref[0, :, kk].reshape(2 * n2, a_ref.shape[4]).astype(BF16)
        z = _dot(mf_ref[kk], a)
        zr, zi = z[:n2], z[n2:]
        hr, hi = h_ref[0, kk], h_ref[1, kk]
        y = jnp.concatenate([zr * hr - zi * hi, zr * hi + zi * hr], axis=0)
        c = _dot(mi_ref[kk], y.astype(BF16))
        c_ref[0, 0, kk] = c[:n2].astype(c_ref.dtype)
        c_ref[0, 1, kk] = c[n2:].astype(c_ref.dtype)


FFT_K1_PER_STEP = 4


def _fft_stage_b(a, hf, order, dc):
    B, _, N1, N2, W = a.shape
    kb = FFT_K1_PER_STEP
    return pl.pallas_call(
        _stage_b_kernel,
        grid=(N1 // kb, B),
        in_specs=[pl.BlockSpec((1, 2, kb, N2, W), lambda k, b: (b, 0, k, 0, 0)),
                  pl.BlockSpec((2, kb, N2, W), lambda k, b: (0, k, 0, order)),
                  pl.BlockSpec((kb, 2 * N2, 2 * N2), lambda k, b: (k, 0, 0)),
                  pl.BlockSpec((kb, 2 * N2, 2 * N2), lambda k, b: (k, 0, 0))],
        out_specs=pl.BlockSpec((1, 2, kb, N2, W), lambda k, b: (b, 0, k, 0, 0)),
        out_shape=jax.ShapeDtypeStruct((B, 2, N1, N2, W), F32),
        compiler_params=_cparams(("parallel", "parallel")),
        name="fft_stage_b",
    )(a, hf, dc["m_fwd"][0], dc["m_inv"][0])


FFT_N2_PER_STEP = 8


def _stage_a_rows_kernel(z_ref, f_ref, a_ref):
    n1 = a_ref.shape[2]
    f = f_ref[...]
    for j in range(z_ref.shape[2]):
        res = _dot(f, z_ref[0, :, j, :].astype(BF16))
        a_ref[0, 0, :, j, :] = res[:n1]
        a_ref[0, 1, :, j, :] = res[n1:]


def _stage_c_rows_kernel(c_ref, z_ref, gate_ref, bias_ref, g_ref, o_ref):
    g = g_ref[...]
    bias = bias_ref[...]
    for j in range(z_ref.shape[2]):
        c = jnp.concatenate([c_ref[0, 0, :, j, :], c_ref[0, 1, :, j, :]], axis=0).astype(BF16)
        y = _dot(g, c)
        o_ref[0, :, j, :] = gate_ref[0, :, j, :] * (y + z_ref[0, :, j, :] * bias)


def _long_conv_gated(z, gate, hf, order, skip, dc):
    B, nr, N2, W = z.shape
    N1 = dc["N1"]
    nj = FFT_N2_PER_STEP
    rows = pl.BlockSpec((1, nr, nj, W), lambda b, j: (b, 0, j, 0))
    spec = pl.BlockSpec((1, 2, N1, nj, W), lambda b, j: (b, 0, 0, j, 0))
    a = pl.pallas_call(
        _stage_a_rows_kernel,
        grid=(B, N2 // nj),
        in_specs=[rows, pl.BlockSpec((2 * N1, nr), lambda b, j: (0, 0))],
        out_specs=spec,
        out_shape=jax.ShapeDtypeStruct((B, 2, N1, N2, W), F32),
        compiler_params=_cparams(("parallel", "parallel")),
        name="fft_stage_a_rows",
    )(z, dc["fa_half"][0])
    c = _fft_stage_b(a, hf, order, dc)
    return pl.pallas_call(
        _stage_c_rows_kernel,
        grid=(B, N2 // nj),
        in_specs=[spec, rows, rows,
                  pl.BlockSpec((1, W), lambda b, j: (0, 0)),
                  pl.BlockSpec((nr, 2 * N1), lambda b, j: (0, 0))],
        out_specs=rows,
        out_shape=jax.ShapeDtypeStruct((B, nr, N2, W), F32),
        compiler_params=_cparams(("parallel", "parallel")),
        name="fft_stage_c_rows",
    )(c, z, gate, skip[None, :], dc["g_inv"][0])


HALO_ROWS = 16


def _short_conv_kernel(u_ref, prev_ref, next_ref, w_ref, b_ref, v_ref, x1_ref, x2_ref):
    i = pl.program_id(1)
    u = u_ref[0].astype(F32)
    tl = u.shape[0]
    row = lax.broadcasted_iota(jnp.int32, (tl, 1), 0)
    prev_last = prev_ref[0].astype(F32)[HALO_ROWS - 1:HALO_ROWS, :]
    next_first = next_ref[0].astype(F32)[0:1, :]
    before = jnp.where(i == 0, 0.0, prev_last)
    after = jnp.where(i == pl.num_programs(1) - 1, 0.0, next_first)
    up = jnp.where(row == 0, before, pltpu.roll(u, 1, 0))
    un = jnp.where(row == tl - 1, after, pltpu.roll(u, tl - 1, 0))
    w = w_ref[...]
    uc = up * w[0:1] + u * w[1:2] + un * w[2:3] + b_ref[...]
    W = HYENA_WIDTH
    v_ref[0] = uc[:, :W]
    x1_ref[0] = uc[:, W:2 * W]
    x2_ref[0] = uc[:, 2 * W:]


def _short_conv(hy, conv_w, conv_b, B, L):
    W3 = 3 * HYENA_WIDTH
    tl = _row_tile(L, 512)
    nl = L // tl
    u = hy.reshape(B, L, W3)
    rh = tl // HALO_ROWS
    spec_o = pl.BlockSpec((1, tl, HYENA_WIDTH), lambda b, i: (b, i, 0))
    return pl.pallas_call(
        _short_conv_kernel,
        grid=(B, nl),
        in_specs=[pl.BlockSpec((1, tl, W3), lambda b, i: (b, i, 0)),
                  pl.BlockSpec((1, HALO_ROWS, W3), lambda b, i: (b, jnp.maximum(i * rh - 1, 0), 0)),
                  pl.BlockSpec((1, HALO_ROWS, W3),
                               lambda b, i: (b, jnp.minimum((i + 1) * rh, L // HALO_ROWS - 1), 0)),
                  pl.BlockSpec((3, W3), lambda b, i: (0, 0)),
                  pl.BlockSpec((1, W3), lambda b, i: (0, 0))],
        out_specs=[spec_o, spec_o, spec_o],
        out_shape=[jax.ShapeDtypeStruct((B, L, HYENA_WIDTH), F32)] * 3,
        compiler_params=_cparams(("parallel", "parallel")),
        name="hyena_short_conv",
    )(u, u, u, conv_w, conv_b[None, :])


def _hyena(hy, hf, conv_w, conv_b, skip, dc, B, L):
    v, x1, x2 = _short_conv(hy, conv_w, conv_b, B, L)
    shape = (B, dc["N1"] // 2, dc["N2"], HYENA_WIDTH)
    z = _long_conv_gated(v.reshape(shape), x1.reshape(shape), hf, 0, skip[0], dc)
    y = _long_conv_gated(z, x2.reshape(shape), hf, 1, skip[1], dc)
    return y.reshape(B * L, HYENA_WIDTH)


def _merge_kernel(x_ref, oa_ref, oh_ref, hg_ref, oc_ref, gl_ref, hgn_ref, wb_ref, wo_ref, g_ref, b_ref,
                  out_ref):
    o = oh_ref[0] + oh_ref[1]
    parts = []
    for hd in range(HGRN_HEADS):
        oh = o[:, hd * HGRN_DIM:(hd + 1) * HGRN_DIM]
        ms = jnp.mean(oh * oh, axis=-1, keepdims=True)
        parts.append(oh * lax.rsqrt(ms + RMS_EPS))
    o_h = jnp.concatenate(parts, axis=-1) * hgn_ref[...] * _silu(hg_ref[...].astype(F32))
    gl = gl_ref[...].astype(F32)
    branches = (oa_ref[...], o_h, oc_ref[...])
    merged = None
    for j, ob in enumerate(branches):
        term = _sigmoid(gl[:, j * D_MODEL:(j + 1) * D_MODEL]) * _dot(ob.astype(BF16), wb_ref[j])
        merged = term if merged is None else merged + term
    mix = _dot(merged.astype(BF16), wo_ref[...])
    out_ref[...] = _layer_norm(ALPHA * x_ref[...] + mix, g_ref[...], b_ref[...])


def _merge(x2, o_a, o_dir, hg, o_c, gl, hgrn_gain, wb_bf16, wo_bf16, ln_g, ln_b):
    T = x2.shape[0]
    tm = _row_tile(T, 512)
    W = BRANCH_WIDTH
    row = lambda w: pl.BlockSpec((tm, w), lambda i: (i, 0))
    return pl.pallas_call(
        _merge_kernel,
        grid=(T // tm,),
        in_specs=[row(D_MODEL), row(W), pl.BlockSpec((2, tm, W), lambda i: (0, i, 0)), row(W), row(W),
                  row(N_BRANCH * D_MODEL),
                  pl.BlockSpec((1, W), lambda i: (0, 0)),
                  pl.BlockSpec((N_BRANCH, W, D_MODEL), lambda i: (0, 0, 0)),
                  pl.BlockSpec((D_MODEL, D_MODEL), lambda i: (0, 0)),
                  pl.BlockSpec((1, D_MODEL), lambda i: (0, 0)),
                  pl.BlockSpec((1, D_MODEL), lambda i: (0, 0))],
        out_specs=row(D_MODEL),
        out_shape=jax.ShapeDtypeStruct((T, D_MODEL), F32),
        compiler_params=_cparams(("parallel",)),
        name="merge_out_ln",
    )(x2, o_a, o_dir.reshape(2, T, W), hg, o_c, gl, jnp.tile(hgrn_gain, HGRN_HEADS)[None, :],
      wb_bf16, wo_bf16, ln_g[None, :], ln_b[None, :])


def _ffn_kernel(x_ref, wg_ref, wu_ref, wd_ref, g_ref, b_ref, out_ref, acc_ref):
    j = pl.program_id(1)
    x = x_ref[...]
    xb = x.astype(BF16)
    hmid = _silu(_dot(xb, wg_ref[...])) * _dot(xb, wu_ref[...])
    contrib = _dot(hmid.astype(BF16), wd_ref[...])

    @pl.when(j == 0)
    def _():
        acc_ref[...] = contrib

    @pl.when(j > 0)
    def _():
        acc_ref[...] += contrib

    @pl.when(j == pl.num_programs(1) - 1)
    def _():
        out_ref[...] = _layer_norm(ALPHA * x + acc_ref[...], g_ref[...], b_ref[...])


def _ffn_dense(x2, w_gu, w_down, ln_g, ln_b):
    T = x2.shape[0]
    tm = _row_tile(T, 512)
    ft = 1408
    nf = D_FF_DENSE // ft
    return pl.pallas_call(
        _ffn_kernel,
        grid=(T // tm, nf),
        in_specs=[pl.BlockSpec((tm, D_MODEL), lambda i, j: (i, 0)),
                  pl.BlockSpec((D_MODEL, ft), lambda i, j: (0, j)),
                  pl.BlockSpec((D_MODEL, ft), lambda i, j: (0, j + nf)),
                  pl.BlockSpec((ft, D_MODEL), lambda i, j: (j, 0)),
                  pl.BlockSpec((1, D_MODEL), lambda i, j: (0, 0)),
                  pl.BlockSpec((1, D_MODEL), lambda i, j: (0, 0))],
        out_specs=pl.BlockSpec((tm, D_MODEL), lambda i, j: (i, 0)),
        out_shape=jax.ShapeDtypeStruct((T, D_MODEL), F32),
        scratch_shapes=[pltpu.VMEM((tm, D_MODEL), F32)],
        compiler_params=_cparams(("parallel", "arbitrary")),
        name="ffn_dense_ln",
    )(x2, w_gu, w_gu, w_down, ln_g[None, :], ln_b[None, :])


MOE_BLOCK = 1024
MOE_SUB = 128
MOE_TILE = 256


def _route_kernel(x_ref, r_ref, col_ref, row_ref, s_ref):
    tb = x_ref.shape[0]
    logits = _dot3(x_ref[...], r_ref[...])
    lane = lax.broadcasted_iota(jnp.int32, logits.shape, 1)
    logits = jnp.where(lane < N_EXPERTS, logits, -jnp.inf)
    m1 = jnp.max(logits, axis=-1, keepdims=True)
    i1 = jnp.min(jnp.where(logits == m1, lane, LANES), axis=-1, keepdims=True)
    rest = jnp.where(lane == i1, -jnp.inf, logits)
    m2 = jnp.max(rest, axis=-1, keepdims=True)
    i2 = jnp.min(jnp.where(rest == m2, lane, LANES), axis=-1, keepdims=True)
    e2 = jnp.exp(m2 - m1)
    w1 = 1.0 / (1.0 + e2)
    w2 = e2 / (1.0 + e2)
    oh1 = lane == i1
    oh2 = lane == i2
    sel = jnp.where(oh1, 1.0, jnp.where(oh2, 1.0, 0.0))
    rr = lax.broadcasted_iota(jnp.int32, (tb, tb), 0)
    cc = lax.broadcasted_iota(jnp.int32, (tb, tb), 1)
    earlier = jnp.where(rr > cc, 1.0, 0.0).astype(BF16)
    rank = _dot(earlier, sel.astype(BF16))
    cnt = jnp.sum(sel, axis=0, keepdims=True)
    nsub = jnp.floor((cnt + (MOE_SUB - 1)) * (1.0 / MOE_SUB))
    r2 = lax.broadcasted_iota(jnp.int32, (LANES, LANES), 0)
    c2 = lax.broadcasted_iota(jnp.int32, (LANES, LANES), 1)
    before = jnp.where(r2 < c2, 1.0, 0.0).astype(BF16)
    nsub8 = jnp.broadcast_to(nsub, (8, LANES))
    off8 = _dot(nsub8.astype(BF16), before) * MOE_SUB
    base = off8[0:1] + rank
    d1 = jnp.sum(jnp.where(oh1, base, 0.0), axis=-1, keepdims=True)
    d2 = jnp.sum(jnp.where(oh2, base, 0.0), axis=-1, keepdims=True)
    col = jnp.where(lane == 0, d1, jnp.where(lane == 1, d2, jnp.where(lane == 2, w1, jnp.where(lane == 3, w2, 0.0))))
    col_ref[...] = col
    row_ref[0] = col.T[0:8, :]
    lane8 = lax.broadcasted_iota(jnp.int32, (8, LANES), 1)
    info = jnp.where(lane8 < N_EXPERTS, off8, pltpu.roll(nsub8, N_EXPERTS, 1))
    s_ref[0] = info.astype(jnp.int32)


def _moe_route(x2, router):
    T = x2.shape[0]
    tb = _row_tile(T, MOE_BLOCK)
    nb = T // tb
    r_pad = jnp.pad(router, ((0, 0), (0, LANES - N_EXPERTS)))
    return pl.pallas_call(
        _route_kernel,
        grid=(nb,),
        in_specs=[pl.BlockSpec((tb, D_MODEL), lambda i: (i, 0)),
                  pl.BlockSpec((D_MODEL, LANES), lambda i: (0, 0))],
        out_specs=[pl.BlockSpec((tb, LANES), lambda i: (i, 0)),
                   pl.BlockSpec((1, 8, tb), lambda i: (i, 0, 0)),
                   pl.BlockSpec((1, 8, LANES), lambda i: (i, 0, 0))],
        out_shape=[jax.ShapeDtypeStruct((T, LANES), F32),
                   jax.ShapeDtypeStruct((nb, 8, tb), F32),
                   jax.ShapeDtypeStruct((nb, 8, LANES), jnp.int32)],
        compiler_params=_cparams(("parallel",)),
        name="moe_route",
    )(x2, r_pad)


def _moe_kernel(s_ref, x_ref, col_ref, row_ref, wg_ref, wu_ref, wd_ref, g_ref, b_ref, out_ref,
                xs_ref, ys_ref):
    blk = pl.program_id(0)
    e = pl.program_id(1)
    j = pl.program_id(2)
    tb = x_ref.shape[0]
    n_tiles = xs_ref.shape[0] // MOE_TILE
    used = s_ref[blk, N_EXPERTS - 1] + s_ref[blk, 2 * N_EXPERTS - 1] * MOE_SUB
    d1r = row_ref[0, 0:1, :]
    d2r = row_ref[0, 1:2, :]

    @pl.when(jnp.logical_and(e == 0, j == 0))
    def _():
        xb = x_ref[...].astype(BF16)
        for rt in range(n_tiles):
            @pl.when(rt * MOE_TILE < used)
            def _():
                r = (lax.broadcasted_iota(jnp.int32, (MOE_TILE, tb), 0) + rt * MOE_TILE).astype(F32)
                p = jnp.where(r == d1r, 1.0, jnp.where(r == d2r, 1.0, 0.0)).astype(BF16)
                xs_ref[rt * MOE_TILE:(rt + 1) * MOE_TILE, :] = _dot(p, xb).astype(BF16)

    start = s_ref[blk, e]
    nsub = s_ref[blk, N_EXPERTS + e]

    def expert_rows(first, n_rows):
        rows = pl.ds(pl.multiple_of(first, MOE_SUB), n_rows)
        xt = xs_ref[rows, :]
        hmid = _silu(_dot(xt, wg_ref[0])) * _dot(xt, wu_ref[0])
        y = _dot(hmid.astype(BF16), wd_ref[0])

        @pl.when(j == 0)
        def _():
            ys_ref[rows, :] = y

        @pl.when(j > 0)
        def _():
            ys_ref[rows, :] += y

    n_big = nsub // 4
    rem = nsub - 4 * n_big

    def big(i, carry):
        expert_rows(start + i * (4 * MOE_SUB), 4 * MOE_SUB)
        return carry

    lax.fori_loop(0, n_big, big, 0)
    tail = start + n_big * (4 * MOE_SUB)

    @pl.when(rem >= 2)
    def _():
        expert_rows(tail, 2 * MOE_SUB)

    @pl.when(rem % 2 == 1)
    def _():
        expert_rows(tail + (rem // 2) * (2 * MOE_SUB), MOE_SUB)

    @pl.when(jnp.logical_and(e == pl.num_programs(1) - 1, j == pl.num_programs(2) - 1))
    def _():
        @pl.when((used // MOE_SUB) % 2 == 1)
        def _():
            ys_ref[pl.ds(pl.multiple_of(used, MOE_SUB), MOE_SUB), :] = jnp.zeros((MOE_SUB, D_MODEL), F32)

        w1r = row_ref[0, 2:3, :]
        w2r = row_ref[0, 3:4, :]
        for rt in range(n_tiles):
            @pl.when(rt * MOE_TILE < used)
            def _():
                r = (lax.broadcasted_iota(jnp.int32, (MOE_TILE, tb), 0) + rt * MOE_TILE).astype(F32)
                w_rows = jnp.sum(jnp.where(r == d1r, w1r, 0.0) + jnp.where(r == d2r, w2r, 0.0),
                                 axis=-1, keepdims=True)
                rows = slice(rt * MOE_TILE, (rt + 1) * MOE_TILE)
                xs_ref[rows, :] = (ys_ref[rows, :] * w_rows).astype(BF16)

        d1c = col_ref[:, 0:1]
        d2c = col_ref[:, 1:2]
        out_ref[...] = jnp.zeros(out_ref.shape, F32)
        for rt in range(n_tiles):
            @pl.when(rt * MOE_TILE < used)
            def _():
                c = (lax.broadcasted_iota(jnp.int32, (tb, MOE_TILE), 1) + rt * MOE_TILE).astype(F32)
                q = jnp.where(c == d1c, 1.0, jnp.where(c == d2c, 1.0, 0.0)).astype(BF16)
                out_ref[...] += _dot(q, xs_ref[rt * MOE_TILE:(rt + 1) * MOE_TILE, :])

        out_ref[...] = _layer_norm(ALPHA * x_ref[...] + out_ref[...], g_ref[...], b_ref[...])


def _ffn_moe(x2, router, w_gu, w_down, ln_g, ln_b):
    T = x2.shape[0]
    tb = _row_tile(T, MOE_BLOCK)
    nb = T // tb
    n_rows = 2 * tb + N_EXPERTS * MOE_SUB
    col, row, sinfo = _moe_route(x2, router)
    ft = 896
    nf = D_FF_EXPERT // ft
    grid_spec = pltpu.PrefetchScalarGridSpec(
        num_scalar_prefetch=1,
        grid=(nb, N_EXPERTS, nf),
        in_specs=[pl.BlockSpec((tb, D_MODEL), lambda i, e, j, s: (i, 0)),
                  pl.BlockSpec((tb, LANES), lambda i, e, j, s: (i, 0)),
                  pl.BlockSpec((1, 8, tb), lambda i, e, j, s: (i, 0, 0)),
                  pl.BlockSpec((1, D_MODEL, ft), lambda i, e, j, s: (e, 0, j)),
                  pl.BlockSpec((1, D_MODEL, ft), lambda i, e, j, s: (e, 0, j + nf)),
                  pl.BlockSpec((1, ft, D_MODEL), lambda i, e, j, s: (e, j, 0)),
                  pl.BlockSpec((1, D_MODEL), lambda i, e, j, s: (0, 0)),
                  pl.BlockSpec((1, D_MODEL), lambda i, e, j, s: (0, 0))],
        out_specs=pl.BlockSpec((tb, D_MODEL), lambda i, e, j, s: (i, 0)),
        scratch_shapes=[pltpu.VMEM((n_rows, D_MODEL), BF16), pltpu.VMEM((n_rows, D_MODEL), F32)])
    return pl.pallas_call(
        _moe_kernel,
        grid_spec=grid_spec,
        out_shape=jax.ShapeDtypeStruct((T, D_MODEL), F32),
        compiler_params=_cparams(("parallel", "arbitrary", "arbitrary")),
        name="ffn_moe_ln",
    )(sinfo[:, 0, :], x2, col, row, w_gu, w_gu, w_down, ln_g[None, :], ln_b[None, :])


def _trunk(x, p):
    B, L, _ = x.shape
    cos_t, sin_t = _rope_tables(L)
    filt_consts = _filter_features(L)
    dc = _dft_consts(L)
    x2 = x.reshape(B * L, D_MODEL)
    for l in range(DEPTH):
        aq, akv, hq, hf, hi, hg, hy, gl = _in_proj(x2, p["w_in"][l])
        qn, kn, vn = _attn_prep(aq, akv, cos_t, sin_t, p["q_norm_g"][l], p["k_norm_g"][l], B, L)
        o_a = _flash_attention(qn, kn, vn, B, L)
        o_dir = _hgrn(hq, hf, hi, p["lower_bounds"][l], B, L)
        k2l = _hyena_filter(L, filt_consts, p["filt_w1"][l], p["filt_b1"][l], p["filt_freq"][l],
                            p["filt_w2"][l], p["filt_b2"][l], p["filt_w3"][l])
        spec = _filter_spectrum(k2l, dc)
        o_c = _hyena(hy, spec, p["conv_w"][l], p["conv_b"][l], p["hyena_skip"][l], dc, B, L)
        x2 = _merge(x2, o_a, o_dir, hg, o_c, gl, p["hgrn_norm_g"][l], p["w_branch"][l], p["w_out"][l],
                    p["ln1_g"][l], p["ln1_b"][l])
        if l % 2 == 0:
            x2 = _ffn_dense(x2, p["ffn_w_gu"][l // 2], p["ffn_w_down"][l // 2], p["ln2_g"][l], p["ln2_b"][l])
        else:
            x2 = _ffn_moe(x2, p["router_w"][l // 2], p["expert_w_gu"][l // 2], p["expert_w_down"][l // 2],
                          p["ln2_g"][l], p["ln2_b"][l])
    return x2.reshape(B, L, D_MODEL)


def _in_proj_weight(w_in):
    return w_in.astype(BF16)


def kernel(x_prompt, x_sample, w_in, q_norm_g, k_norm_g, hgrn_lb, hgrn_norm_g, conv_w, conv_b, filt_w1, filt_b1, filt_freq, filt_w2, filt_b2, filt_w3, hyena_skip, w_branch, w_out, ln1_g, ln1_b, ln2_g, ln2_b, ffn_w_gu, ffn_w_down, router_w, expert_w_gu, expert_w_down):
    s = jax.nn.softmax(hgrn_lb.astype(F32), axis=0)
    p = dict(
        w_in=_in_proj_weight(w_in), q_norm_g=q_norm_g, k_norm_g=k_norm_g,
        lower_bounds=jnp.cumsum(s, axis=0) - s[:1], hgrn_norm_g=hgrn_norm_g,
        conv_w=conv_w, conv_b=conv_b, filt_w1=filt_w1, filt_b1=filt_b1, filt_freq=filt_freq,
        filt_w2=filt_w2, filt_b2=filt_b2, filt_w3=filt_w3, hyena_skip=hyena_skip,
        w_branch=w_branch.astype(BF16), w_out=w_out.astype(BF16),
        ln1_g=ln1_g, ln1_b=ln1_b, ln2_g=ln2_g, ln2_b=ln2_b,
        ffn_w_gu=ffn_w_gu.astype(BF16), ffn_w_down=ffn_w_down.astype(BF16), router_w=router_w,
        expert_w_gu=expert_w_gu.astype(BF16), expert_w_down=expert_w_down.astype(BF16))
    return (_trunk(x_prompt, p), _trunk(x_sample, p))
```

```python
import functools
import math

import numpy as np
import jax
import jax.numpy as jnp
from jax import lax
from jax.experimental import pallas as pl
from jax.experimental.pallas import tpu as pltpu

F32 = jnp.float32
BF16 = jnp.bfloat16

D_MODEL = 1024
DEPTH = 2
GRID_W = 64
BRANCH_WIDTH = 512
ATTN_HEADS = 8
ATTN_KV_HEADS = 2
ATTN_GROUP = ATTN_HEADS // ATTN_KV_HEADS
HEAD_DIM = 64
KV_WIDTH = ATTN_KV_HEADS * HEAD_DIM
ROPE_AXIS_DIM = HEAD_DIM // 2
ROPE_HALF = ROPE_AXIS_DIM // 2
ROPE_THETA = 10000.0
HGRN_HEADS = 4
HGRN_DIM = 128
HGRN_CHUNK = 64
FORGET_FLOOR = 1e-30
HYENA_WIDTH = BRANCH_WIDTH
HYENA_ORDER = 2
FILTER_EMB = 33
FILTER_BANDS = (FILTER_EMB - 1) // 2
FILTER_HIDDEN = 64
FILTER_FAST_DECAY = 0.3
FILTER_SLOW_DECAY = 1.5
FILTER_TARGET = 1e-2
N_FILTER_CH = HYENA_ORDER * 2 * HYENA_WIDTH
N_BRANCH = 3
D_FF_DENSE = 2816
N_EXPERTS = 8
D_FF_EXPERT = 3584
ALPHA = (2 * DEPTH) ** 0.25
LN_EPS = 1e-5
RMS_EPS = 1e-6

SEG_WIDTHS = (BRANCH_WIDTH,
              2 * KV_WIDTH,
              BRANCH_WIDTH,
              2 * BRANCH_WIDTH,
              BRANCH_WIDTH,
              BRANCH_WIDTH,
              3 * HYENA_WIDTH,
              N_BRANCH * D_MODEL)
IN_WIDTH = sum(SEG_WIDTHS)
SEG_DTYPES = (BF16, BF16, BF16, F32, BF16, BF16, BF16, BF16)

LANES = 128
SUBLANES = 8
FFT_N2 = 128
VMEM_LIMIT = 56 * 1024 * 1024


def _cparams(sem):
    return pltpu.CompilerParams(dimension_semantics=sem, vmem_limit_bytes=VMEM_LIMIT)


def _dot(a, b):
    return jnp.dot(a, b, preferred_element_type=F32)


def _dot_nt(a, b):
    return lax.dot_general(a, b, (((1,), (1,)), ((), ())), preferred_element_type=F32)


def _dot_tn(a, b):
    return lax.dot_general(a, b, (((0,), (0,)), ((), ())), preferred_element_type=F32)


def _split2(a):
    hi = a.astype(BF16)
    lo = (a - hi.astype(F32)).astype(BF16)
    return hi, lo


def _dot3(a, b):
    ah, al = _split2(a)
    bh, bl = _split2(b)
    return _dot(ah, bh) + _dot(al, bh) + _dot(ah, bl)


def _layer_norm(y, g, b):
    mu = jnp.mean(y, axis=-1, keepdims=True)
    yc = y - mu
    var = jnp.mean(yc * yc, axis=-1, keepdims=True)
    return yc * lax.rsqrt(var + LN_EPS) * g + b


def _sigmoid(x):
    return 1.0 / (1.0 + jnp.exp(-x))


def _silu(x):
    return x * _sigmoid(x)


def _row_tile(n, want):
    t = min(n, want)
    assert n % t == 0
    return t


def _in_proj_kernel(x_ref, w_ref, *out_refs):
    x = x_ref[...].astype(BF16)
    off = 0
    for o_ref, width in zip(out_refs, SEG_WIDTHS):
        o_ref[...] = _dot(x, w_ref[:, off:off + width]).astype(o_ref.dtype)
        off += width


def _in_proj(x2, w_bf16):
    T = x2.shape[0]
    tm = _row_tile(T, 512)
    return pl.pallas_call(
        _in_proj_kernel,
        grid=(T // tm,),
        in_specs=[pl.BlockSpec((tm, D_MODEL), lambda i: (i, 0)),
                  pl.BlockSpec((D_MODEL, IN_WIDTH), lambda i: (0, 0), pipeline_mode=pl.Buffered(1))],
        out_specs=[pl.BlockSpec((tm, w), lambda i: (i, 0)) for w in SEG_WIDTHS],
        out_shape=[jax.ShapeDtypeStruct((T, w), dt) for w, dt in zip(SEG_WIDTHS, SEG_DTYPES)],
        compiler_params=_cparams(("parallel",)),
        name="in_proj",
    )(x2, w_bf16)


def _rope_tables(L):
    rows = L // GRID_W
    row = jnp.broadcast_to(jnp.arange(rows, dtype=F32)[:, None], (rows, GRID_W)).reshape(-1)
    col = jnp.broadcast_to(jnp.arange(GRID_W, dtype=F32)[None, :], (rows, GRID_W)).reshape(-1)
    inv = jnp.power(ROPE_THETA, -2.0 * jnp.arange(ROPE_HALF, dtype=F32) / ROPE_AXIS_DIM)
    a0, a1 = row[:, None] * inv, col[:, None] * inv
    c0, s0, c1, s1 = jnp.cos(a0), jnp.sin(a0), jnp.cos(a1), jnp.sin(a1)
    cos_h = jnp.concatenate([c0, c0, c1, c1], axis=-1)
    sin_h = jnp.concatenate([-s0, s0, -s1, s1], axis=-1)
    return jnp.tile(cos_h, (1, ATTN_HEADS)), jnp.tile(sin_h, (1, ATTN_HEADS))


def _head_block_ones(width):
    idx = np.arange(width) // HEAD_DIM
    return jnp.asarray(idx[:, None] == idx[None, :], dtype=BF16)


def _norm_rope(x, gain, cos, sin, ones_bd):
    width = x.shape[-1]
    sq_hi, sq_lo = _split2(x * x)
    ms = (_dot(sq_hi, ones_bd) + _dot(sq_lo, ones_bd)) * (1.0 / HEAD_DIM)
    xn = x * lax.rsqrt(ms + RMS_EPS) * gain
    lane = lax.broadcasted_iota(jnp.int32, xn.shape, 1)
    first_half = (lane % ROPE_AXIS_DIM) < ROPE_HALF
    swapped = jnp.where(first_half, pltpu.roll(xn, width - ROPE_HALF, 1), pltpu.roll(xn, ROPE_HALF, 1))
    return xn * cos + swapped * sin


LOG2E = 1.4426950408889634
V_EXT = 2 * HEAD_DIM
FLASH_SKEW = 4


def _attn_prep_kernel(aq_ref, akv_ref, cos_ref, sin_ref, qg_ref, kg_ref, bdq_ref, bdk_ref,
                      q_out, kt_out, v_out):
    cos, sin = cos_ref[...], sin_ref[...]
    q = _norm_rope(aq_ref[...].astype(F32), qg_ref[...], cos, sin, bdq_ref[...])
    q_out[...] = (q * (LOG2E * HEAD_DIM ** -0.5)).astype(BF16)
    kv = akv_ref[...].astype(F32)
    k = _norm_rope(kv[:, :KV_WIDTH], kg_ref[...], cos[:, :KV_WIDTH], sin[:, :KV_WIDTH], bdk_ref[...])
    kt_out[0] = k.T.astype(BF16)
    ones = jnp.ones((kv.shape[0], HEAD_DIM), F32)
    v_ext = [kv[:, KV_WIDTH + h * HEAD_DIM:KV_WIDTH + (h + 1) * HEAD_DIM] if part == 0 else ones
             for h in range(ATTN_KV_HEADS) for part in range(2)]
    v_out[...] = jnp.concatenate(v_ext, axis=-1).astype(BF16)


def _attn_prep(aq, akv, cos_t, sin_t, q_gain, k_gain, B, L):
    T = aq.shape[0]
    tm = _row_tile(L, 512)
    nl = L // tm
    qg = jnp.tile(q_gain, ATTN_HEADS)[None, :]
    kg = jnp.tile(k_gain, ATTN_KV_HEADS)[None, :]
    const = lambda i: (0, 0)
    return pl.pallas_call(
        _attn_prep_kernel,
        grid=(T // tm,),
        in_specs=[pl.BlockSpec((tm, BRANCH_WIDTH), lambda i: (i, 0)),
                  pl.BlockSpec((tm, 2 * KV_WIDTH), lambda i: (i, 0)),
                  pl.BlockSpec((tm, BRANCH_WIDTH), lambda i: (i % nl, 0)),
                  pl.BlockSpec((tm, BRANCH_WIDTH), lambda i: (i % nl, 0)),
                  pl.BlockSpec((1, BRANCH_WIDTH), const),
                  pl.BlockSpec((1, KV_WIDTH), const),
                  pl.BlockSpec((BRANCH_WIDTH, BRANCH_WIDTH), const),
                  pl.BlockSpec((KV_WIDTH, KV_WIDTH), const)],
        out_specs=[pl.BlockSpec((tm, BRANCH_WIDTH), lambda i: (i, 0)),
                   pl.BlockSpec((1, KV_WIDTH, tm), lambda i: (i // nl, 0, i % nl)),
                   pl.BlockSpec((tm, ATTN_KV_HEADS * V_EXT), lambda i: (i, 0))],
        out_shape=[jax.ShapeDtypeStruct((T, BRANCH_WIDTH), BF16),
                   jax.ShapeDtypeStruct((B, KV_WIDTH, L), BF16),
                   jax.ShapeDtypeStruct((T, ATTN_KV_HEADS * V_EXT), BF16)],
        compiler_params=_cparams(("parallel",)),
        name="attn_prep",
    )(aq, akv, cos_t, sin_t, qg, kg, _head_block_ones(BRANCH_WIDTH), _head_block_ones(KV_WIDTH))


def _flash_kernel(q_ref, kt_ref, v_ref, o_ref, qs_ref, m_ref, acc_ref, *, tq, rs):
    kv = pl.program_id(2)

    @pl.when(kv == 0)
    def _():
        for h in range(ATTN_HEADS):
            g = h % ATTN_GROUP
            qs_ref[h // ATTN_GROUP, g * tq:(g + 1) * tq, :] = q_ref[0, :, h * HEAD_DIM:(h + 1) * HEAD_DIM]
        m_ref[...] = jnp.full(m_ref.shape, -jnp.inf, F32)
        acc_ref[...] = jnp.zeros(acc_ref.shape, F32)

    kts = [kt_ref[0, kh * HEAD_DIM:(kh + 1) * HEAD_DIM, :] for kh in range(ATTN_KV_HEADS)]
    vs = [v_ref[0, :, kh * V_EXT:(kh + 1) * V_EXT] for kh in range(ATTN_KV_HEADS)]
    tasks = [(kh, slice(sb * rs, (sb + 1) * rs))
             for sb in range(ATTN_GROUP * tq // rs) for kh in range(ATTN_KV_HEADS)]

    def scores(t):
        kh, rows = tasks[t]
        return _dot(qs_ref[kh, rows, :], kts[kh])

    pending = {t: scores(t) for t in range(min(FLASH_SKEW, len(tasks)))}
    for t, (kh, rows) in enumerate(tasks):
        s = pending.pop(t)
        if t + FLASH_SKEW < len(tasks):
            pending[t + FLASH_SKEW] = scores(t + FLASH_SKEW)
        m_prev = m_ref[kh, rows, :]
        m_new = jnp.maximum(m_prev, jnp.max(s, axis=-1, keepdims=True))
        p = jnp.exp2(s - m_new[:, 0:1])
        acc_ref[kh, rows, :] = jnp.exp2(m_prev - m_new) * acc_ref[kh, rows, :] + _dot(p.astype(BF16), vs[kh])
        m_ref[kh, rows, :] = m_new

    @pl.when(kv == pl.num_programs(2) - 1)
    def _():
        for h in range(ATTN_HEADS):
            kh, g = h // ATTN_GROUP, h % ATTN_GROUP
            a = acc_ref[kh, g * tq:(g + 1) * tq, :]
            o_ref[0, :, h * HEAD_DIM:(h + 1) * HEAD_DIM] = a[:, :HEAD_DIM] / a[:, HEAD_DIM:]


def _flash_attention(qn, kt, v_ext, B, L):
    tq = _row_tile(L, 256)
    tk = _row_tile(L, 1024)
    rs = min(128, ATTN_GROUP * tq)
    out = pl.pallas_call(
        functools.partial(_flash_kernel, tq=tq, rs=rs),
        grid=(B, L // tq, L // tk),
        in_specs=[pl.BlockSpec((1, tq, BRANCH_WIDTH), lambda b, i, j: (b, i, 0)),
                  pl.BlockSpec((1, KV_WIDTH, tk), lambda b, i, j: (b, 0, j)),
                  pl.BlockSpec((1, tk, ATTN_KV_HEADS * V_EXT), lambda b, i, j: (b, j, 0))],
        out_specs=pl.BlockSpec((1, tq, BRANCH_WIDTH), lambda b, i, j: (b, i, 0)),
        out_shape=jax.ShapeDtypeStruct((B, L, BRANCH_WIDTH), F32),
        scratch_shapes=[pltpu.VMEM((ATTN_KV_HEADS, ATTN_GROUP * tq, HEAD_DIM), BF16),
                        pltpu.VMEM((ATTN_KV_HEADS, ATTN_GROUP * tq, LANES), F32),
                        pltpu.VMEM((ATTN_KV_HEADS, ATTN_GROUP * tq, V_EXT), F32)],
        compiler_params=_cparams(("parallel", "parallel", "arbitrary")),
        name="flash_attention",
    )(qn.reshape(B, L, BRANCH_WIDTH), kt, v_ext.reshape(B, L, ATTN_KV_HEADS * V_EXT))
    return out.reshape(B * L, BRANCH_WIDTH)


def _ref_rows(b, h, fwd):
    C, W = b.shape
    two_h = 2 * h
    if two_h >= 8:
        n = C // two_h
        b3 = b.reshape(n, two_h, W)
        r = jnp.where(fwd, b3[:, h - 1:h, :], b3[:, h:h + 1, :])
        return jnp.broadcast_to(r, (n, two_h, W)).reshape(C, W)
    b3 = b.reshape(C // 8, 8, W)
    pos8 = lax.broadcasted_iota(jnp.int32, (1, 8, 1), 1)
    out = None
    for blk in range(8 // two_h):
        base = blk * two_h
        r = jnp.where(fwd, b3[:, base + h - 1:base + h, :], b3[:, base + h:base + h + 1, :])
        r = jnp.broadcast_to(r, b3.shape)
        out = r if out is None else jnp.where((pos8 // two_h) == blk, r, out)
    return out.reshape(C, W)


def _hgrn_chunk(q, z, v, lb, tri, st_ref, fwd):
    C, W = q.shape
    q = q.astype(F32)
    v = v.astype(F32)
    e = jnp.exp(-jnp.abs(z))
    s_big = 1.0 / (1.0 + e)
    s_small = e * s_big
    pos = z >= 0.0
    f = lb + (1.0 - lb) * jnp.where(pos, s_big, s_small)
    g = jnp.log(jnp.maximum(f, FORGET_FLOOR))
    kk = (1.0 - lb) * jnp.where(pos, s_small, s_big)
    g_hi = g.astype(BF16)
    r1 = g - g_hi.astype(F32)
    g_mid = r1.astype(BF16)
    g_lo = (r1 - g_mid.astype(F32)).astype(BF16)
    b = _dot(tri, g_hi) + _dot(tri, g_mid) + _dot(tri, g_lo)
    b_end = jnp.where(fwd, b[C - 1:C, :], b[0:1, :])
    q_in = (q * jnp.exp(b)).astype(BF16)
    k_out = (kk * jnp.exp(b_end - b)).astype(BF16)
    s_decay = jnp.exp(b_end)
    vb = v.astype(BF16)

    row = lax.broadcasted_iota(jnp.int32, (C, 1), 0)
    rr = lax.broadcasted_iota(jnp.int32, (C, C), 0)
    cc = lax.broadcasted_iota(jnp.int32, (C, C), 1)
    a_mats = [jnp.zeros((C, C), F32) for _ in range(HGRN_HEADS)]
    h = C // 2
    while h >= 1:
        upper = ((row % (2 * h)) >= h).astype(jnp.int32)
        q_rows = upper == fwd.astype(jnp.int32)
        ref = _ref_rows(b, h, fwd)
        ql = jnp.where(q_rows, q * jnp.exp(jnp.where(q_rows, b - ref, 0.0)), 0.0).astype(BF16)
        kl = jnp.where(q_rows, 0.0, kk * jnp.exp(jnp.where(q_rows, 0.0, ref - b))).astype(BF16)
        same_block = (rr // (2 * h)) == (cc // (2 * h))
        for hd in range(HGRN_HEADS):
            sl = slice(hd * HGRN_DIM, (hd + 1) * HGRN_DIM)
            a_mats[hd] = a_mats[hd] + jnp.where(same_block, _dot_nt(ql[:, sl], kl[:, sl]), 0.0)
        h //= 2

    qk = q * kk
    outs = []
    for hd in range(HGRN_HEADS):
        sl = slice(hd * HGRN_DIM, (hd + 1) * HGRN_DIM)
        st = st_ref[hd]
        o = _dot_nt(q_in[:, sl], st.astype(BF16))
        o = o + _dot(a_mats[hd].astype(BF16), vb[:, sl])
        o = o + jnp.sum(qk[:, sl], axis=-1, keepdims=True) * v[:, sl]
        outs.append(o)
        st_ref[hd] = st * s_decay[:, sl] + _dot_tn(vb[:, sl], k_out[:, sl])
    return jnp.concatenate(outs, axis=-1)


def _hgrn_kernel(q_ref, f_ref, v_ref, lb_ref, tri_ref, o_ref, st_ref, *, n_inner):
    d = pl.program_id(0)
    fwd = d == 0

    @pl.when(pl.program_id(2) == 0)
    def _():
        st_ref[...] = jnp.zeros(st_ref.shape, F32)

    lb = lb_ref[0]
    tri = tri_ref[0]

    def body(i, carry):
        ci = jnp.where(fwd, i, n_inner - 1 - i)
        r0 = pl.multiple_of(ci * HGRN_CHUNK, HGRN_CHUNK)
        rows = pl.ds(r0, HGRN_CHUNK)
        o_ref[0, 0, rows, :] = _hgrn_chunk(q_ref[0, rows, :], f_ref[0, rows, :], v_ref[0, rows, :],
                                           lb, tri, st_ref, fwd)
        return carry

    lax.fori_loop(0, n_inner, body, 0)


def _hgrn(hq, hf, hi, lb, B, L):
    W = BRANCH_WIDTH
    rb = _row_tile(L, 512)
    nb = L // rb
    C = HGRN_CHUNK
    tri_f = np.tril(np.ones((C, C), np.float32))
    tri = jnp.asarray(np.stack([tri_f, tri_f.T]), dtype=BF16)
    blk = lambda d, b, c: (b, c + d * (nb - 1 - 2 * c), 0)
    return pl.pallas_call(
        functools.partial(_hgrn_kernel, n_inner=rb // C),
        grid=(2, B, nb),
        in_specs=[pl.BlockSpec((1, rb, W), blk),
                  pl.BlockSpec((1, rb, W), lambda d, b, c: (b, c + d * (nb - 1 - 2 * c), d)),
                  pl.BlockSpec((1, rb, W), blk),
                  pl.BlockSpec((1, 1, W), lambda d, b, c: (d, 0, 0)),
                  pl.BlockSpec((1, C, C), lambda d, b, c: (d, 0, 0))],
        out_specs=pl.BlockSpec((1, 1, rb, W), lambda d, b, c: (d, b, c + d * (nb - 1 - 2 * c), 0)),
        out_shape=jax.ShapeDtypeStruct((2, B, L, W), F32),
        scratch_shapes=[pltpu.VMEM((HGRN_HEADS, HGRN_DIM, HGRN_DIM), F32)],
        compiler_params=_cparams(("parallel", "parallel", "arbitrary")),
        name="hgrn_scan",
    )(hq.reshape(B, L, W), hf.reshape(B, L, 2 * W), hi.reshape(B, L, W), lb[:, None, :], tri)


def _filter_features(L):
    t = jnp.linspace(0.0, 1.0, L, dtype=F32)[:, None]
    w = 2.0 * math.pi * jnp.arange(L, dtype=F32) / L
    bands = jnp.linspace(1e-4, FILTER_BANDS - 1, FILTER_BANDS, dtype=F32)
    ang = w[:, None] * bands[None, :]
    feats = jnp.concatenate([t, jnp.cos(ang), -jnp.sin(ang)], axis=-1)
    feats2 = jnp.concatenate([feats, feats[:1], jnp.flip(feats[1:], axis=0)], axis=0)
    t2 = jnp.concatenate([t, t[:1], jnp.flip(t[1:], axis=0)], axis=0)
    deltas = jnp.abs(jnp.linspace(math.log(FILTER_TARGET) / FILTER_FAST_DECAY,
                                  math.log(FILTER_TARGET) / FILTER_SLOW_DECAY, N_FILTER_CH, dtype=F32))
    return feats2, t2, deltas[None, :]


def _filter_kernel(feat_ref, t_ref, w1_ref, b1_ref, fr_ref, w2_ref, b2_ref, w3f_ref, w3b_ref,
                   df_ref, db_ref, k_ref, *, L):
    freq = fr_ref[...]
    h = jnp.sin(freq * (_dot3(feat_ref[...], w1_ref[...]) + b1_ref[...]))
    h = jnp.sin(freq * (_dot3(h, w2_ref[...]) + b2_ref[...]))
    t = t_ref[...]
    top = _dot3(h[:L], w3f_ref[...]) * jnp.exp(-t[:L] * df_ref[...])
    bot = _dot3(h[L:], w3b_ref[...]) * jnp.exp(-t[L:] * db_ref[...])
    row = lax.broadcasted_iota(jnp.int32, (L, 1), 0)
    bot = jnp.where(row == 0, 0.0, bot)
    norm = jnp.sum(jnp.abs(top), axis=0, keepdims=True) + jnp.sum(jnp.abs(bot), axis=0, keepdims=True)
    k_ref[:L, :] = top / norm
    k_ref[L:, :] = bot / norm


def _hyena_filter(L, consts, w1, b1, freq, w2, b2, w3):
    feats2, t2, deltas = consts
    wb = 256
    nw = HYENA_WIDTH // wb
    full = lambda shape: pl.BlockSpec(shape, lambda o, j: (0,) * len(shape))
    fcol = lambda o, j: (0, o * 2 * nw + j)
    bcol = lambda o, j: (0, o * 2 * nw + nw + j)
    return pl.pallas_call(
        functools.partial(_filter_kernel, L=L),
        grid=(HYENA_ORDER, nw),
        in_specs=[full((2 * L, FILTER_EMB)), full((2 * L, 1)),
                  full((FILTER_EMB, FILTER_HIDDEN)), full((1, FILTER_HIDDEN)), full((1, FILTER_HIDDEN)),
                  full((FILTER_HIDDEN, FILTER_HIDDEN)), full((1, FILTER_HIDDEN)),
                  pl.BlockSpec((FILTER_HIDDEN, wb), fcol), pl.BlockSpec((FILTER_HIDDEN, wb), bcol),
                  pl.BlockSpec((1, wb), fcol), pl.BlockSpec((1, wb), bcol)],
        out_specs=pl.BlockSpec((2 * L, wb), lambda o, j: (0, o * nw + j)),
        out_shape=jax.ShapeDtypeStruct((2 * L, HYENA_ORDER * HYENA_WIDTH), F32),
        compiler_params=_cparams(("parallel", "parallel")),
        name="hyena_filter",
    )(feats2, t2, w1, b1[None, :], freq[None, :], w2, b2[None, :], w3, w3, deltas, deltas)


def _dft_consts(L):
    N = 2 * L
    N2 = FFT_N2
    N1 = N // N2
    k1 = np.arange(N1)
    ang_a = 2.0 * np.pi * np.outer(k1, np.arange(N1)) / N1
    fa_full = np.concatenate([np.cos(ang_a), -np.sin(ang_a)], axis=0)
    fa_half = fa_full[:, :N1 // 2]
    n2 = np.arange(N2)
    kk = k1[:, None, None] + N1 * np.arange(N2)[None, :, None]
    ang_b = 2.0 * np.pi * (kk * n2[None, None, :]) / N
    mr, mi = np.cos(ang_b), -np.sin(ang_b)
    m_fwd = np.concatenate([np.concatenate([mr, -mi], axis=2),
                            np.concatenate([mi, mr], axis=2)], axis=1)
    wr, wi = np.swapaxes(mr, 1, 2), -np.swapaxes(mi, 1, 2)
    m_inv = np.concatenate([np.concatenate([wr, -wi], axis=2),
                            np.concatenate([wi, wr], axis=2)], axis=1)
    ang_c = 2.0 * np.pi * np.outer(np.arange(N1 // 2), k1) / N1
    g_inv = np.concatenate([np.cos(ang_c), -np.sin(ang_c)], axis=1) / N

    def hi_lo(a):
        a = jnp.asarray(a, dtype=F32)
        hi = a.astype(BF16)
        return hi, (a - hi.astype(F32)).astype(BF16)

    eye = np.eye(SUBLANES)
    fa_tiles = jnp.asarray(np.kron(fa_half, eye), dtype=BF16)
    g_tiles = jnp.asarray(np.kron(g_inv, eye), dtype=BF16)
    return dict(N1=N1, N2=N2, fa_full=hi_lo(fa_full), fa_tiles=fa_tiles, g_tiles=g_tiles,
                m_fwd=hi_lo(m_fwd), m_inv=hi_lo(m_inv))


def _mm3(m_hi, m_lo, x):
    x_hi, x_lo = _split2(x)
    return _dot(m_hi, x_hi) + _dot(m_hi, x_lo) + _dot(m_lo, x_hi)


def _stage_a_kernel(z_ref, fh_ref, fl_ref, a_ref, *, exact):
    n1 = a_ref.shape[2]
    if exact:
        res = _mm3(fh_ref[...], fl_ref[...], z_ref[0])
    else:
        res = _dot(fh_ref[...], z_ref[0].astype(BF16))
    a_ref[0, 0] = res[:n1].astype(a_ref.dtype)
    a_ref[0, 1] = res[n1:].astype(a_ref.dtype)


def _fft_stage_a(z, fa, N1, exact):
    B, nr, cols = z.shape
    cb = min(cols, 4096)
    return pl.pallas_call(
        functools.partial(_stage_a_kernel, exact=exact),
        grid=(B, cols // cb),
        in_specs=[pl.BlockSpec((1, nr, cb), lambda b, j: (b, 0, j)),
                  pl.BlockSpec((2 * N1, nr), lambda b, j: (0, 0)),
                  pl.BlockSpec((2 * N1, nr), lambda b, j: (0, 0))],
        out_specs=pl.BlockSpec((1, 2, N1, cb), lambda b, j: (b, 0, 0, j)),
        out_shape=jax.ShapeDtypeStruct((B, 2, N1, cols), F32 if exact else BF16),
        compiler_params=_cparams(("parallel", "parallel")),
        name="fft_stage_a",
    )(z, fa[0], fa[1])


def _stage_b_filter_kernel(a_ref, mh_ref, ml_ref, o_ref):
    n2 = a_ref.shape[3]
    a = a_ref[0, :, 0].reshape(2 * n2, a_ref.shape[4])
    res = _mm3(mh_ref[0], ml_ref[0], a)
    o_ref[0, 0] = res[:n2]
    o_ref[1, 0] = res[n2:]


def _filter_spectrum(k2l, dc):
    N1, N2 = dc["N1"], dc["N2"]
    OW = k2l.shape[1]
    a = _fft_stage_a(k2l.reshape(1, N1, N2 * OW), dc["fa_full"], N1, True)
    a = a.reshape(1, 2, N1, N2, OW)
    wb = 512
    return pl.pallas_call(
        _stage_b_filter_kernel,
        grid=(N1, OW // wb),
        in_specs=[pl.BlockSpec((1, 2, 1, N2, wb), lambda k, j: (0, 0, k, 0, j)),
                  pl.BlockSpec((1, 2 * N2, 2 * N2), lambda k, j: (k, 0, 0)),
                  pl.BlockSpec((1, 2 * N2, 2 * N2), lambda k, j: (k, 0, 0))],
        out_specs=pl.BlockSpec((2, 1, N2, wb), lambda k, j: (0, k, 0, j)),
        out_shape=jax.ShapeDtypeStruct((2, N1, N2, OW), F32),
        compiler_params=_cparams(("parallel", "parallel")),
        name="fft_filter_spectrum",
    )(a, dc["m_fwd"][0], dc["m_fwd"][1])


FFT_SPECTRUM_BYTES = 8 * 1024 * 1024


def _long_conv_kernel(z_ref, gate_ref, h_ref, fa_ref, mf_ref, mi_ref, g_ref, bias_ref, o_ref, a_ref):
    n1, n2, wq = a_ref.shape[1], a_ref.shape[2], a_ref.shape[3]
    nr = z_ref.shape[1]
    s8 = SUBLANES
    fa = fa_ref[...]
    for jt in range(n2 // s8):
        tile = slice(jt * s8, (jt + 1) * s8)
        zt = z_ref[0, :, tile, :].reshape(nr * s8, wq).astype(BF16)
        res = _dot(fa, zt)
        a_ref[0, :, tile, :] = res[:n1 * s8].reshape(n1, s8, wq)
        a_ref[1, :, tile, :] = res[n1 * s8:].reshape(n1, s8, wq)

    def per_k1(k, carry):
        a = jnp.concatenate([a_ref[0, k], a_ref[1, k]], axis=0).astype(BF16)
        zz = _dot(mf_ref[k], a)
        zr, zi = zz[:n2], zz[n2:]
        hr, hi = h_ref[0, k], h_ref[1, k]
        y = jnp.concatenate([zr * hr - zi * hi, zr * hi + zi * hr], axis=0).astype(BF16)
        c = _dot(mi_ref[k], y)
        a_ref[0, k] = c[:n2]
        a_ref[1, k] = c[n2:]
        return carry

    lax.fori_loop(0, n1, per_k1, 0, unroll=8)
    g = g_ref[...]
    bias = bias_ref[...]
    for jt in range(n2 // s8):
        tile = slice(jt * s8, (jt + 1) * s8)
        c = jnp.concatenate([a_ref[0, :, tile, :].reshape(n1 * s8, wq),
                             a_ref[1, :, tile, :].reshape(n1 * s8, wq)], axis=0).astype(BF16)
        y = _dot(g, c).reshape(nr, s8, wq)
        o_ref[0, :, tile, :] = gate_ref[0, :, tile, :] * (y + z_ref[0, :, tile, :] * bias)


def _long_conv_gated(z, gate, hf, order, skip, dc):
    B, nr, N2, W = z.shape
    N1 = dc["N1"]
    wq = min(W, max(LANES, FFT_SPECTRUM_BYTES // (2 * N1 * N2 * 4) // LANES * LANES))
    nw = W // wq
    rows = pl.BlockSpec((1, nr, N2, wq), lambda w, b: (b, 0, 0, w))
    once = pl.Buffered(1)
    mats = pl.BlockSpec((N1, 2 * N2, 2 * N2), lambda w, b: (0, 0, 0), pipeline_mode=once)
    return pl.pallas_call(
        _long_conv_kernel,
        grid=(nw, B),
        in_specs=[rows, rows,
                  pl.BlockSpec((2, N1, N2, wq), lambda w, b: (0, 0, 0, order * nw + w), pipeline_mode=once),
                  pl.BlockSpec((2 * N1 * SUBLANES, nr * SUBLANES), lambda w, b: (0, 0)),
                  mats, mats,
                  pl.BlockSpec((nr * SUBLANES, 2 * N1 * SUBLANES), lambda w, b: (0, 0)),
                  pl.BlockSpec((1, wq), lambda w, b: (0, w))],
        out_specs=rows,
        out_shape=jax.ShapeDtypeStruct((B, nr, N2, W), F32),
        scratch_shapes=[pltpu.VMEM((2, N1, N2, wq), F32)],
        compiler_params=_cparams(("parallel", "parallel")),
        name="hyena_long_conv",
    )(z, gate, hf, dc["fa_tiles"], dc["m_fwd"][0], dc["m_inv"][0], dc["g_tiles"], skip[None, :])


HALO_ROWS = 16


def _short_conv_kernel(u_ref, prev_ref, next_ref, w_ref, b_ref, v_ref, x1_ref, x2_ref):
    i = pl.program_id(1)
    u = u_ref[0].astype(F32)
    tl = u.shape[0]
    row = lax.broadcasted_iota(jnp.int32, (tl, 1), 0)
    prev_last = prev_ref[0].astype(F32)[HALO_ROWS - 1:HALO_ROWS, :]
    next_first = next_ref[0].astype(F32)[0:1, :]
    before = jnp.where(i == 0, 0.0, prev_last)
    after = jnp.where(i == pl.num_programs(1) - 1, 0.0, next_first)
    up = jnp.where(row == 0, before, pltpu.roll(u, 1, 0))
    un = jnp.where(row == tl - 1, after, pltpu.roll(u, tl - 1, 0))
    w = w_ref[...]
    uc = up * w[0:1] + u * w[1:2] + un * w[2:3] + b_ref[...]
    W = HYENA_WIDTH
    v_ref[0] = uc[:, :W]
    x1_ref[0] = uc[:, W:2 * W]
    x2_ref[0] = uc[:, 2 * W:]


def _short_conv(hy, conv_w, conv_b, B, L):
    W3 = 3 * HYENA_WIDTH
    tl = _row_tile(L, 512)
    nl = L // tl
    u = hy.reshape(B, L, W3)
    rh = tl // HALO_ROWS
    spec_o = pl.BlockSpec((1, tl, HYENA_WIDTH), lambda b, i: (b, i, 0))
    return pl.pallas_call(
        _short_conv_kernel,
        grid=(B, nl),
        in_specs=[pl.BlockSpec((1, tl, W3), lambda b, i: (b, i, 0)),
                  pl.BlockSpec((1, HALO_ROWS, W3), lambda b, i: (b, jnp.maximum(i * rh - 1, 0), 0)),
                  pl.BlockSpec((1, HALO_ROWS, W3),
                               lambda b, i: (b, jnp.minimum((i + 1) * rh, L // HALO_ROWS - 1), 0)),
                  pl.BlockSpec((3, W3), lambda b, i: (0, 0)),
                  pl.BlockSpec((1, W3), lambda b, i: (0, 0))],
        out_specs=[spec_o, spec_o, spec_o],
        out_shape=[jax.ShapeDtypeStruct((B, L, HYENA_WIDTH), F32)] * 3,
        compiler_params=_cparams(("parallel", "parallel")),
        name="hyena_short_conv",
    )(u, u, u, conv_w, conv_b[None, :])


def _hyena(hy, hf, conv_w, conv_b, skip, dc, B, L):
    v, x1, x2 = _short_conv(hy, conv_w, conv_b, B, L)
    shape = (B, dc["N1"] // 2, dc["N2"], HYENA_WIDTH)
    z = _long_conv_gated(v.reshape(shape), x1.reshape(shape), hf, 0, skip[0], dc)
    y = _long_conv_gated(z, x2.reshape(shape), hf, 1, skip[1], dc)
    return y.reshape(B * L, HYENA_WIDTH)


def _merge_kernel(x_ref, oa_ref, oh_ref, hg_ref, oc_ref, gl_ref, hgn_ref, wb_ref, wo_ref, g_ref, b_ref,
                  out_ref):
    o = oh_ref[0] + oh_ref[1]
    parts = []
    for hd in range(HGRN_HEADS):
        oh = o[:, hd * HGRN_DIM:(hd + 1) * HGRN_DIM]
        ms = jnp.mean(oh * oh, axis=-1, keepdims=True)
        parts.append(oh * lax.rsqrt(ms + RMS_EPS))
    o_h = jnp.concatenate(parts, axis=-1) * hgn_ref[...] * _silu(hg_ref[...].astype(F32))
    gl = gl_ref[...].astype(F32)
    branches = (oa_ref[...], o_h, oc_ref[...])
    merged = None
    for j, ob in enumerate(branches):
        term = _sigmoid(gl[:, j * D_MODEL:(j + 1) * D_MODEL]) * _dot(ob.astype(BF16), wb_ref[j])
        merged = term if merged is None else merged + term
    mix = _dot(merged.astype(BF16), wo_ref[...])
    out_ref[...] = _layer_norm(ALPHA * x_ref[...] + mix, g_ref[...], b_ref[...])


def _merge(x2, o_a, o_dir, hg, o_c, gl, hgrn_gain, wb_bf16, wo_bf16, ln_g, ln_b):
    T = x2.shape[0]
    tm = _row_tile(T, 512)
    W = BRANCH_WIDTH
    row = lambda w: pl.BlockSpec((tm, w), lambda i: (i, 0))
    return pl.pallas_call(
        _merge_kernel,
        grid=(T // tm,),
        in_specs=[row(D_MODEL), row(W), pl.BlockSpec((2, tm, W), lambda i: (0, i, 0)), row(W), row(W),
                  row(N_BRANCH * D_MODEL),
                  pl.BlockSpec((1, W), lambda i: (0, 0)),
                  pl.BlockSpec((N_BRANCH, W, D_MODEL), lambda i: (0, 0, 0)),
                  pl.BlockSpec((D_MODEL, D_MODEL), lambda i: (0, 0)),
                  pl.BlockSpec((1, D_MODEL), lambda i: (0, 0)),
                  pl.BlockSpec((1, D_MODEL), lambda i: (0, 0))],
        out_specs=row(D_MODEL),
        out_shape=jax.ShapeDtypeStruct((T, D_MODEL), F32),
        compiler_params=_cparams(("parallel",)),
        name="merge_out_ln",
    )(x2, o_a, o_dir.reshape(2, T, W), hg, o_c, gl, jnp.tile(hgrn_gain, HGRN_HEADS)[None, :],
      wb_bf16, wo_bf16, ln_g[None, :], ln_b[None, :])


def _ffn_kernel(x_ref, wg_ref, wu_ref, wd_ref, g_ref, b_ref, out_ref, acc_ref):
    j = pl.program_id(1)
    x = x_ref[...]
    xb = x.astype(BF16)
    hmid = _silu(_dot(xb, wg_ref[...])) * _dot(xb, wu_ref[...])
    contrib = _dot(hmid.astype(BF16), wd_ref[...])

    @pl.when(j == 0)
    def _():
        acc_ref[...] = contrib

    @pl.when(j > 0)
    def _():
        acc_ref[...] += contrib

    @pl.when(j == pl.num_programs(1) - 1)
    def _():
        out_ref[...] = _layer_norm(ALPHA * x + acc_ref[...], g_ref[...], b_ref[...])


def _ffn_dense(x2, w_gu, w_down, ln_g, ln_b):
    T = x2.shape[0]
    tm = _row_tile(T, 512)
    ft = 1408
    nf = D_FF_DENSE // ft
    return pl.pallas_call(
        _ffn_kernel,
        grid=(T // tm, nf),
        in_specs=[pl.BlockSpec((tm, D_MODEL), lambda i, j: (i, 0)),
                  pl.BlockSpec((D_MODEL, ft), lambda i, j: (0, j)),
                  pl.BlockSpec((D_MODEL, ft), lambda i, j: (0, j + nf)),
                  pl.BlockSpec((ft, D_MODEL), lambda i, j: (j, 0)),
                  pl.BlockSpec((1, D_MODEL), lambda i, j: (0, 0)),
                  pl.BlockSpec((1, D_MODEL), lambda i, j: (0, 0))],
        out_specs=pl.BlockSpec((tm, D_MODEL), lambda i, j: (i, 0)),
        out_shape=jax.ShapeDtypeStruct((T, D_MODEL), F32),
        scratch_shapes=[pltpu.VMEM((tm, D_MODEL), F32)],
        compiler_params=_cparams(("parallel", "arbitrary")),
        name="ffn_dense_ln",
    )(x2, w_gu, w_gu, w_down, ln_g[None, :], ln_b[None, :])


MOE_BLOCK = 1024
MOE_SUB = 128
MOE_TILE = 256


def _route_kernel(x_ref, r_ref, col_ref, row_ref, s_ref):
    tb = x_ref.shape[0]
    logits = _dot3(x_ref[...], r_ref[...])
    lane = lax.broadcasted_iota(jnp.int32, logits.shape, 1)
    logits = jnp.where(lane < N_EXPERTS, logits, -jnp.inf)
    m1 = jnp.max(logits, axis=-1, keepdims=True)
    i1 = jnp.min(jnp.where(logits == m1, lane, LANES), axis=-1, keepdims=True)
    rest = jnp.where(lane == i1, -jnp.inf, logits)
    m2 = jnp.max(rest, axis=-1, keepdims=True)
    i2 = jnp.min(jnp.where(rest == m2, lane, LANES), axis=-1, keepdims=True)
    e2 = jnp.exp(m2 - m1)
    w1 = 1.0 / (1.0 + e2)
    w2 = e2 / (1.0 + e2)
    oh1 = lane == i1
    oh2 = lane == i2
    sel = jnp.where(oh1, 1.0, jnp.where(oh2, 1.0, 0.0))
    rr = lax.broadcasted_iota(jnp.int32, (tb, tb), 0)
    cc = lax.broadcasted_iota(jnp.int32, (tb, tb), 1)
    earlier = jnp.where(rr > cc, 1.0, 0.0).astype(BF16)
    rank = _dot(earlier, sel.astype(BF16))
    cnt = jnp.sum(sel, axis=0, keepdims=True)
    nsub = jnp.floor((cnt + (MOE_SUB - 1)) * (1.0 / MOE_SUB))
    r2 = lax.broadcasted_iota(jnp.int32, (LANES, LANES), 0)
    c2 = lax.broadcasted_iota(jnp.int32, (LANES, LANES), 1)
    before = jnp.where(r2 < c2, 1.0, 0.0).astype(BF16)
    nsub8 = jnp.broadcast_to(nsub, (8, LANES))
    off8 = _dot(nsub8.astype(BF16), before) * MOE_SUB
    base = off8[0:1] + rank
    d1 = jnp.sum(jnp.where(oh1, base, 0.0), axis=-1, keepdims=True)
    d2 = jnp.sum(jnp.where(oh2, base, 0.0), axis=-1, keepdims=True)
    col = jnp.where(lane == 0, d1, jnp.where(lane == 1, d2, jnp.where(lane == 2, w1, jnp.where(lane == 3, w2, 0.0))))
    col_ref[...] = col
    row_ref[0] = col.T[0:8, :]
    lane8 = lax.broadcasted_iota(jnp.int32, (8, LANES), 1)
    info = jnp.where(lane8 < N_EXPERTS, off8, pltpu.roll(nsub8, N_EXPERTS, 1))
    s_ref[0] = info.astype(jnp.int32)


def _moe_route(x2, router):
    T = x2.shape[0]
    tb = _row_tile(T, MOE_BLOCK)
    nb = T // tb
    r_pad = jnp.pad(router, ((0, 0), (0, LANES - N_EXPERTS)))
    return pl.pallas_call(
        _route_kernel,
        grid=(nb,),
        in_specs=[pl.BlockSpec((tb, D_MODEL), lambda i: (i, 0)),
                  pl.BlockSpec((D_MODEL, LANES), lambda i: (0, 0))],
        out_specs=[pl.BlockSpec((tb, LANES), lambda i: (i, 0)),
                   pl.BlockSpec((1, 8, tb), lambda i: (i, 0, 0)),
                   pl.BlockSpec((1, 8, LANES), lambda i: (i, 0, 0))],
        out_shape=[jax.ShapeDtypeStruct((T, LANES), F32),
                   jax.ShapeDtypeStruct((nb, 8, tb), F32),
                   jax.ShapeDtypeStruct((nb, 8, LANES), jnp.int32)],
        compiler_params=_cparams(("parallel",)),
        name="moe_route",
    )(x2, r_pad)


def _moe_kernel(s_ref, x_ref, col_ref, row_ref, wg_ref, wu_ref, wd_ref, g_ref, b_ref, out_ref,
                xs_ref, ys_ref):
    blk = pl.program_id(0)
    e = pl.program_id(1)
    j = pl.program_id(2)
    tb = x_ref.shape[0]
    n_tiles = xs_ref.shape[0] // MOE_TILE
    used = s_ref[blk, N_EXPERTS - 1] + s_ref[blk, 2 * N_EXPERTS - 1] * MOE_SUB
    d1r = row_ref[0, 0:1, :]
    d2r = row_ref[0, 1:2, :]

    @pl.when(jnp.logical_and(e == 0, j == 0))
    def _():
        xb = x_ref[...].astype(BF16)
        for rt in range(n_tiles):
            @pl.when(rt * MOE_TILE < used)
            def _():
                r = (lax.broadcasted_iota(jnp.int32, (MOE_TILE, tb), 0) + rt * MOE_TILE).astype(F32)
                p = jnp.where(r == d1r, 1.0, jnp.where(r == d2r, 1.0, 0.0)).astype(BF16)
                xs_ref[rt * MOE_TILE:(rt + 1) * MOE_TILE, :] = _dot(p, xb).astype(BF16)

    start = s_ref[blk, e]
    nsub = s_ref[blk, N_EXPERTS + e]

    def expert_rows(first, n_rows):
        rows = pl.ds(pl.multiple_of(first, MOE_SUB), n_rows)
        xt = xs_ref[rows, :]
        hmid = _silu(_dot(xt, wg_ref[0])) * _dot(xt, wu_ref[0])
        y = _dot(hmid.astype(BF16), wd_ref[0])

        @pl.when(j == 0)
        def _():
            ys_ref[rows, :] = y

        @pl.when(j > 0)
        def _():
            ys_ref[rows, :] += y

    n_big = nsub // 4
    rem = nsub - 4 * n_big

    def big(i, carry):
        expert_rows(start + i * (4 * MOE_SUB), 4 * MOE_SUB)
        return carry

    lax.fori_loop(0, n_big, big, 0)
    tail = start + n_big * (4 * MOE_SUB)

    @pl.when(rem >= 2)
    def _():
        expert_rows(tail, 2 * MOE_SUB)

    @pl.when(rem % 2 == 1)
    def _():
        expert_rows(tail + (rem // 2) * (2 * MOE_SUB), MOE_SUB)

    @pl.when(jnp.logical_and(e == pl.num_programs(1) - 1, j == pl.num_programs(2) - 1))
    def _():
        @pl.when((used // MOE_SUB) % 2 == 1)
        def _():
            ys_ref[pl.ds(pl.multiple_of(used, MOE_SUB), MOE_SUB), :] = jnp.zeros((MOE_SUB, D_MODEL), F32)

        w1r = row_ref[0, 2:3, :]
        w2r = row_ref[0, 3:4, :]
        for rt in range(n_tiles):
            @pl.when(rt * MOE_TILE < used)
            def _():
                r = (lax.broadcasted_iota(jnp.int32, (MOE_TILE, tb), 0) + rt * MOE_TILE).astype(F32)
                w_rows = jnp.sum(jnp.where(r == d1r, w1r, 0.0) + jnp.where(r == d2r, w2r, 0.0),
                                 axis=-1, keepdims=True)
                rows = slice(rt * MOE_TILE, (rt + 1) * MOE_TILE)
                xs_ref[rows, :] = (ys_ref[rows, :] * w_rows).astype(BF16)

        d1c = col_ref[:, 0:1]
        d2c = col_ref[:, 1:2]
        out_ref[...] = jnp.zeros(out_ref.shape, F32)
        for rt in range(n_tiles):
            @pl.when(rt * MOE_TILE < used)
            def _():
                c = (lax.broadcasted_iota(jnp.int32, (tb, MOE_TILE), 1) + rt * MOE_TILE).astype(F32)
                q = jnp.where(c == d1c, 1.0, jnp.where(c == d2c, 1.0, 0.0)).astype(BF16)
                out_ref[...] += _dot(q, xs_ref[rt * MOE_TILE:(rt + 1) * MOE_TILE, :])

        out_ref[...] = _layer_norm(ALPHA * x_ref[...] + out_ref[...], g_ref[...], b_ref[...])


def _ffn_moe(x2, router, w_gu, w_down, ln_g, ln_b):
    T = x2.shape[0]
    tb = _row_tile(T, MOE_BLOCK)
    nb = T // tb
    n_rows = 2 * tb + N_EXPERTS * MOE_SUB
    col, row, sinfo = _moe_route(x2, router)
    ft = 896
    nf = D_FF_EXPERT // ft
    grid_spec = pltpu.PrefetchScalarGridSpec(
        num_scalar_prefetch=1,
        grid=(nb, N_EXPERTS, nf),
        in_specs=[pl.BlockSpec((tb, D_MODEL), lambda i, e, j, s: (i, 0)),
                  pl.BlockSpec((tb, LANES), lambda i, e, j, s: (i, 0)),
                  pl.BlockSpec((1, 8, tb), lambda i, e, j, s: (i, 0, 0)),
                  pl.BlockSpec((1, D_MODEL, ft), lambda i, e, j, s: (e, 0, j)),
                  pl.BlockSpec((1, D_MODEL, ft), lambda i, e, j, s: (e, 0, j + nf)),
                  pl.BlockSpec((1, ft, D_MODEL), lambda i, e, j, s: (e, j, 0)),
                  pl.BlockSpec((1, D_MODEL), lambda i, e, j, s: (0, 0)),
                  pl.BlockSpec((1, D_MODEL), lambda i, e, j, s: (0, 0))],
        out_specs=pl.BlockSpec((tb, D_MODEL), lambda i, e, j, s: (i, 0)),
        scratch_shapes=[pltpu.VMEM((n_rows, D_MODEL), BF16), pltpu.VMEM((n_rows, D_MODEL), F32)])
    return pl.pallas_call(
        _moe_kernel,
        grid_spec=grid_spec,
        out_shape=jax.ShapeDtypeStruct((T, D_MODEL), F32),
        compiler_params=_cparams(("parallel", "arbitrary", "arbitrary")),
        name="ffn_moe_ln",
    )(sinfo[:, 0, :], x2, col, row, w_gu, w_gu, w_down, ln_g[None, :], ln_b[None, :])


def _trunk(x, p):
    B, L, _ = x.shape
    cos_t, sin_t = _rope_tables(L)
    filt_consts = _filter_features(L)
    dc = _dft_consts(L)
    x2 = x.reshape(B * L, D_MODEL)
    for l in range(DEPTH):
        aq, akv, hq, hf, hi, hg, hy, gl = _in_proj(x2, p["w_in"][l])
        qn, kn, vn = _attn_prep(aq, akv, cos_t, sin_t, p["q_norm_g"][l], p["k_norm_g"][l], B, L)
        o_a = _flash_attention(qn, kn, vn, B, L)
        o_dir = _hgrn(hq, hf, hi, p["lower_bounds"][l], B, L)
        k2l = _hyena_filter(L, filt_consts, p["filt_w1"][l], p["filt_b1"][l], p["filt_freq"][l],
                            p["filt_w2"][l], p["filt_b2"][l], p["filt_w3"][l])
        spec = _filter_spectrum(k2l, dc)
        o_c = _hyena(hy, spec, p["conv_w"][l], p["conv_b"][l], p["hyena_skip"][l], dc, B, L)
        x2 = _merge(x2, o_a, o_dir, hg, o_c, gl, p["hgrn_norm_g"][l], p["w_branch"][l], p["w_out"][l],
                    p["ln1_g"][l], p["ln1_b"][l])
        if l % 2 == 0:
            x2 = _ffn_dense(x2, p["ffn_w_gu"][l // 2], p["ffn_w_down"][l // 2], p["ln2_g"][l], p["ln2_b"][l])
        else:
            x2 = _ffn_moe(x2, p["router_w"][l // 2], p["expert_w_gu"][l // 2], p["expert_w_down"][l // 2],
                          p["ln2_g"][l], p["ln2_b"][l])
    return x2.reshape(B, L, D_MODEL)


def _in_proj_weight(w_in):
    return w_in.astype(BF16)


def kernel(x_prompt, x_sample, w_in, q_norm_g, k_norm_g, hgrn_lb, hgrn_norm_g, conv_w, conv_b, filt_w1, filt_b1, filt_freq, filt_w2, filt_b2, filt_w3, hyena_skip, w_branch, w_out, ln1_g, ln1_b, ln2_g, ln2_b, ffn_w_gu, ffn_w_down, router_w, expert_w_gu, expert_w_down):
    s = jax.nn.softmax(hgrn_lb.astype(F32), axis=0)
    p = dict(
        w_in=_in_proj_weight(w_in), q_norm_g=q_norm_g, k_norm_g=k_norm_g,
        lower_bounds=jnp.cumsum(s, axis=0) - s[:1], hgrn_norm_g=hgrn_norm_g,
        conv_w=conv_w, conv_b=conv_b, filt_w1=filt_w1, filt_b1=filt_b1, filt_freq=filt_freq,
        filt_w2=filt_w2, filt_b2=filt_b2, filt_w3=filt_w3, hyena_skip=hyena_skip,
        w_branch=w_branch.astype(BF16), w_out=w_out.astype(BF16),
        ln1_g=ln1_g, ln1_b=ln1_b, ln2_g=ln2_g, ln2_b=ln2_b,
        ffn_w_gu=ffn_w_gu.astype(BF16), ffn_w_down=ffn_w_down.astype(BF16), router_w=router_w,
        expert_w_gu=expert_w_gu.astype(BF16), expert_w_down=expert_w_down.astype(BF16))
    return (_trunk(x_prompt, p), _trunk(x_sample, p))
```

```python
import functools
import math

import numpy as np
import jax
import jax.numpy as jnp
from jax import lax
from jax.experimental import pallas as pl
from jax.experimental.pallas import tpu as pltpu

F32 = jnp.float32
BF16 = jnp.bfloat16

D_MODEL = 1024
DEPTH = 2
GRID_W = 64
BRANCH_WIDTH = 512
ATTN_HEADS = 8
ATTN_KV_HEADS = 2
ATTN_GROUP = ATTN_HEADS // ATTN_KV_HEADS
HEAD_DIM = 64
KV_WIDTH = ATTN_KV_HEADS * HEAD_DIM
ROPE_AXIS_DIM = HEAD_DIM // 2
ROPE_HALF = ROPE_AXIS_DIM // 2
ROPE_THETA = 10000.0
HGRN_HEADS = 4
HGRN_DIM = 128
HGRN_CHUNK = 64
FORGET_FLOOR = 1e-30
HYENA_WIDTH = BRANCH_WIDTH
HYENA_ORDER = 2
FILTER_EMB = 33
FILTER_BANDS = (FILTER_EMB - 1) // 2
FILTER_HIDDEN = 64
FILTER_FAST_DECAY = 0.3
FILTER_SLOW_DECAY = 1.5
FILTER_TARGET = 1e-2
N_FILTER_CH = HYENA_ORDER * 2 * HYENA_WIDTH
N_BRANCH = 3
D_FF_DENSE = 2816
N_EXPERTS = 8
D_FF_EXPERT = 3584
ALPHA = (2 * DEPTH) ** 0.25
LN_EPS = 1e-5
RMS_EPS = 1e-6

SEG_WIDTHS = (BRANCH_WIDTH,
              2 * KV_WIDTH,
              BRANCH_WIDTH,
              2 * BRANCH_WIDTH,
              BRANCH_WIDTH,
              BRANCH_WIDTH,
              3 * HYENA_WIDTH,
              N_BRANCH * D_MODEL)
IN_WIDTH = sum(SEG_WIDTHS)
SEG_DTYPES = (BF16, BF16, BF16, F32, BF16, BF16, BF16, BF16)

LANES = 128
SUBLANES = 8
FFT_N2 = 128
VMEM_LIMIT = 56 * 1024 * 1024


def _cparams(sem):
    return pltpu.CompilerParams(dimension_semantics=sem, vmem_limit_bytes=VMEM_LIMIT)


def _dot(a, b):
    return jnp.dot(a, b, preferred_element_type=F32)


def _dot_nt(a, b):
    return lax.dot_general(a, b, (((1,), (1,)), ((), ())), preferred_element_type=F32)


def _dot_tn(a, b):
    return lax.dot_general(a, b, (((0,), (0,)), ((), ())), preferred_element_type=F32)


def _split2(a):
    hi = a.astype(BF16)
    lo = (a - hi.astype(F32)).astype(BF16)
    return hi, lo


def _dot3(a, b):
    ah, al = _split2(a)
    bh, bl = _split2(b)
    return _dot(ah, bh) + _dot(al, bh) + _dot(ah, bl)


def _layer_norm(y, g, b):
    mu = jnp.mean(y, axis=-1, keepdims=True)
    yc = y - mu
    var = jnp.mean(yc * yc, axis=-1, keepdims=True)
    return yc * lax.rsqrt(var + LN_EPS) * g + b


def _sigmoid(x):
    return 1.0 / (1.0 + jnp.exp(-x))


def _silu(x):
    return x * _sigmoid(x)


def _row_tile(n, want):
    t = min(n, want)
    assert n % t == 0
    return t


def _in_proj_kernel(x_ref, w_ref, *out_refs):
    x = x_ref[...].astype(BF16)
    off = 0
    for o_ref, width in zip(out_refs, SEG_WIDTHS):
        o_ref[...] = _dot(x, w_ref[:, off:off + width]).astype(o_ref.dtype)
        off += width


def _in_proj(x2, w_bf16):
    T = x2.shape[0]
    tm = _row_tile(T, 512)
    return pl.pallas_call(
        _in_proj_kernel,
        grid=(T // tm,),
        in_specs=[pl.BlockSpec((tm, D_MODEL), lambda i: (i, 0)),
                  pl.BlockSpec((D_MODEL, IN_WIDTH), lambda i: (0, 0), pipeline_mode=pl.Buffered(1))],
        out_specs=[pl.BlockSpec((tm, w), lambda i: (i, 0)) for w in SEG_WIDTHS],
        out_shape=[jax.ShapeDtypeStruct((T, w), dt) for w, dt in zip(SEG_WIDTHS, SEG_DTYPES)],
        compiler_params=_cparams(("parallel",)),
        name="in_proj",
    )(x2, w_bf16)


def _rope_tables(L):
    rows = L // GRID_W
    row = jnp.broadcast_to(jnp.arange(rows, dtype=F32)[:, None], (rows, GRID_W)).reshape(-1)
    col = jnp.broadcast_to(jnp.arange(GRID_W, dtype=F32)[None, :], (rows, GRID_W)).reshape(-1)
    inv = jnp.power(ROPE_THETA, -2.0 * jnp.arange(ROPE_HALF, dtype=F32) / ROPE_AXIS_DIM)
    a0, a1 = row[:, None] * inv, col[:, None] * inv
    c0, s0, c1, s1 = jnp.cos(a0), jnp.sin(a0), jnp.cos(a1), jnp.sin(a1)
    cos_h = jnp.concatenate([c0, c0, c1, c1], axis=-1)
    sin_h = jnp.concatenate([-s0, s0, -s1, s1], axis=-1)
    return jnp.tile(cos_h, (1, ATTN_HEADS)), jnp.tile(sin_h, (1, ATTN_HEADS))


def _head_block_ones(width):
    idx = np.arange(width) // HEAD_DIM
    return jnp.asarray(idx[:, None] == idx[None, :], dtype=BF16)


def _norm_rope(x, gain, cos, sin, ones_bd):
    width = x.shape[-1]
    sq_hi, sq_lo = _split2(x * x)
    ms = (_dot(sq_hi, ones_bd) + _dot(sq_lo, ones_bd)) * (1.0 / HEAD_DIM)
    xn = x * lax.rsqrt(ms + RMS_EPS) * gain
    lane = lax.broadcasted_iota(jnp.int32, xn.shape, 1)
    first_half = (lane % ROPE_AXIS_DIM) < ROPE_HALF
    swapped = jnp.where(first_half, pltpu.roll(xn, width - ROPE_HALF, 1), pltpu.roll(xn, ROPE_HALF, 1))
    return xn * cos + swapped * sin


LOG2E = 1.4426950408889634
V_EXT = 2 * HEAD_DIM
FLASH_SKEW = 4


def _attn_prep_kernel(aq_ref, akv_ref, cos_ref, sin_ref, qg_ref, kg_ref, bdq_ref, bdk_ref,
                      q_out, kt_out, v_out):
    cos, sin = cos_ref[...], sin_ref[...]
    q = _norm_rope(aq_ref[...].astype(F32), qg_ref[...], cos, sin, bdq_ref[...])
    q_out[...] = (q * (LOG2E * HEAD_DIM ** -0.5)).astype(BF16)
    kv = akv_ref[...].astype(F32)
    k = _norm_rope(kv[:, :KV_WIDTH], kg_ref[...], cos[:, :KV_WIDTH], sin[:, :KV_WIDTH], bdk_ref[...])
    kt_out[0] = k.T.astype(BF16)
    ones = jnp.ones((kv.shape[0], HEAD_DIM), F32)
    v_ext = [kv[:, KV_WIDTH + h * HEAD_DIM:KV_WIDTH + (h + 1) * HEAD_DIM] if part == 0 else ones
             for h in range(ATTN_KV_HEADS) for part in range(2)]
    v_out[...] = jnp.concatenate(v_ext, axis=-1).astype(BF16)


def _attn_prep(aq, akv, cos_t, sin_t, q_gain, k_gain, B, L):
    T = aq.shape[0]
    tm = _row_tile(L, 512)
    nl = L // tm
    qg = jnp.tile(q_gain, ATTN_HEADS)[None, :]
    kg = jnp.tile(k_gain, ATTN_KV_HEADS)[None, :]
    const = lambda i: (0, 0)
    return pl.pallas_call(
        _attn_prep_kernel,
        grid=(T // tm,),
        in_specs=[pl.BlockSpec((tm, BRANCH_WIDTH), lambda i: (i, 0)),
                  pl.BlockSpec((tm, 2 * KV_WIDTH), lambda i: (i, 0)),
                  pl.BlockSpec((tm, BRANCH_WIDTH), lambda i: (i % nl, 0)),
                  pl.BlockSpec((tm, BRANCH_WIDTH), lambda i: (i % nl, 0)),
                  pl.BlockSpec((1, BRANCH_WIDTH), const),
                  pl.BlockSpec((1, KV_WIDTH), const),
                  pl.BlockSpec((BRANCH_WIDTH, BRANCH_WIDTH), const),
                  pl.BlockSpec((KV_WIDTH, KV_WIDTH), const)],
        out_specs=[pl.BlockSpec((tm, BRANCH_WIDTH), lambda i: (i, 0)),
                   pl.BlockSpec((1, KV_WIDTH, tm), lambda i: (i // nl, 0, i % nl)),
                   pl.BlockSpec((tm, ATTN_KV_HEADS * V_EXT), lambda i: (i, 0))],
        out_shape=[jax.ShapeDtypeStruct((T, BRANCH_WIDTH), BF16),
                   jax.ShapeDtypeStruct((B, KV_WIDTH, L), BF16),
                   jax.ShapeDtypeStruct((T, ATTN_KV_HEADS * V_EXT), BF16)],
        compiler_params=_cparams(("parallel",)),
        name="attn_prep",
    )(aq, akv, cos_t, sin_t, qg, kg, _head_block_ones(BRANCH_WIDTH), _head_block_ones(KV_WIDTH))


def _flash_kernel(q_ref, kt_ref, v_ref, o_ref, qs_ref, m_ref, acc_ref, *, tq, rs):
    kv = pl.program_id(2)

    @pl.when(kv == 0)
    def _():
        for h in range(ATTN_HEADS):
            g = h % ATTN_GROUP
            qs_ref[h // ATTN_GROUP, g * tq:(g + 1) * tq, :] = q_ref[0, :, h * HEAD_DIM:(h + 1) * HEAD_DIM]
        m_ref[...] = jnp.full(m_ref.shape, -jnp.inf, F32)
        acc_ref[...] = jnp.zeros(acc_ref.shape, F32)

    kts = [kt_ref[0, kh * HEAD_DIM:(kh + 1) * HEAD_DIM, :] for kh in range(ATTN_KV_HEADS)]
    vs = [v_ref[0, :, kh * V_EXT:(kh + 1) * V_EXT] for kh in range(ATTN_KV_HEADS)]
    tasks = [(kh, slice(sb * rs, (sb + 1) * rs))
             for sb in range(ATTN_GROUP * tq // rs) for kh in range(ATTN_KV_HEADS)]

    def scores(t):
        kh, rows = tasks[t]
        return _dot(qs_ref[kh, rows, :], kts[kh])

    pending = {t: scores(t) for t in range(min(FLASH_SKEW, len(tasks)))}
    for t, (kh, rows) in enumerate(tasks):
        s = pending.pop(t)
        if t + FLASH_SKEW < len(tasks):
            pending[t + FLASH_SKEW] = scores(t + FLASH_SKEW)
        m_prev = m_ref[kh, rows, :]
        m_new = jnp.maximum(m_prev, jnp.max(s, axis=-1, keepdims=True))
        p = jnp.exp2(s - m_new[:, 0:1])
        acc_ref[kh, rows, :] = jnp.exp2(m_prev - m_new) * acc_ref[kh, rows, :] + _dot(p.astype(BF16), vs[kh])
        m_ref[kh, rows, :] = m_new

    @pl.when(kv == pl.num_programs(2) - 1)
    def _():
        for h in range(ATTN_HEADS):
            kh, g = h // ATTN_GROUP, h % ATTN_GROUP
            a = acc_ref[kh, g * tq:(g + 1) * tq, :]
            o_ref[0, :, h * HEAD_DIM:(h + 1) * HEAD_DIM] = a[:, :HEAD_DIM] / a[:, HEAD_DIM:]


def _flash_attention(qn, kt, v_ext, B, L):
    tq = _row_tile(L, 256)
    tk = _row_tile(L, 1024)
    rs = min(128, ATTN_GROUP * tq)
    out = pl.pallas_call(
        functools.partial(_flash_kernel, tq=tq, rs=rs),
        grid=(B, L // tq, L // tk),
        in_specs=[pl.BlockSpec((1, tq, BRANCH_WIDTH), lambda b, i, j: (b, i, 0)),
                  pl.BlockSpec((1, KV_WIDTH, tk), lambda b, i, j: (b, 0, j)),
                  pl.BlockSpec((1, tk, ATTN_KV_HEADS * V_EXT), lambda b, i, j: (b, j, 0))],
        out_specs=pl.BlockSpec((1, tq, BRANCH_WIDTH), lambda b, i, j: (b, i, 0)),
        out_shape=jax.ShapeDtypeStruct((B, L, BRANCH_WIDTH), F32),
        scratch_shapes=[pltpu.VMEM((ATTN_KV_HEADS, ATTN_GROUP * tq, HEAD_DIM), BF16),
                        pltpu.VMEM((ATTN_KV_HEADS, ATTN_GROUP * tq, LANES), F32),
                        pltpu.VMEM((ATTN_KV_HEADS, ATTN_GROUP * tq, V_EXT), F32)],
        compiler_params=_cparams(("parallel", "parallel", "arbitrary")),
        name="flash_attention",
    )(qn.reshape(B, L, BRANCH_WIDTH), kt, v_ext.reshape(B, L, ATTN_KV_HEADS * V_EXT))
    return out.reshape(B * L, BRANCH_WIDTH)


def _ref_rows(b, h, fwd):
    C, W = b.shape
    two_h = 2 * h
    if two_h >= 8:
        n = C // two_h
        b3 = b.reshape(n, two_h, W)
        r = jnp.where(fwd, b3[:, h - 1:h, :], b3[:, h:h + 1, :])
        return jnp.broadcast_to(r, (n, two_h, W)).reshape(C, W)
    b3 = b.reshape(C // 8, 8, W)
    pos8 = lax.broadcasted_iota(jnp.int32, (1, 8, 1), 1)
    out = None
    for blk in range(8 // two_h):
        base = blk * two_h
        r = jnp.where(fwd, b3[:, base + h - 1:base + h, :], b3[:, base + h:base + h + 1, :])
        r = jnp.broadcast_to(r, b3.shape)
        out = r if out is None else jnp.where((pos8 // two_h) == blk, r, out)
    return out.reshape(C, W)


def _hgrn_chunk(q, z, v, lb, tri, st_ref, fwd):
    C, W = q.shape
    q = q.astype(F32)
    v = v.astype(F32)
    e = jnp.exp(-jnp.abs(z))
    s_big = 1.0 / (1.0 + e)
    s_small = e * s_big
    pos = z >= 0.0
    f = lb + (1.0 - lb) * jnp.where(pos, s_big, s_small)
    g = jnp.log2(jnp.maximum(f, FORGET_FLOOR))
    kk = (1.0 - lb) * jnp.where(pos, s_small, s_big)
    g_hi, g_lo = _split2(g)
    b = _dot(tri, g_hi) + _dot(tri, g_lo)
    b_end = jnp.where(fwd, b[C - 1:C, :], b[0:1, :])
    q_in = (q * jnp.exp2(b)).astype(BF16)
    k_out = (kk * jnp.exp2(b_end - b)).astype(BF16)
    s_decay = jnp.exp2(b_end)
    vb = v.astype(BF16)

    row = lax.broadcasted_iota(jnp.int32, (C, 1), 0)
    rr = lax.broadcasted_iota(jnp.int32, (C, C), 0)
    cc = lax.broadcasted_iota(jnp.int32, (C, C), 1)
    a_mats = [jnp.zeros((C, C), F32) for _ in range(HGRN_HEADS)]
    h = C // 2
    while h >= 1:
        upper = ((row % (2 * h)) >= h).astype(jnp.int32)
        q_rows = upper == fwd.astype(jnp.int32)
        ref = _ref_rows(b, h, fwd)
        ql = jnp.where(q_rows, q * jnp.exp2(b - ref), 0.0).astype(BF16)
        kl = jnp.where(q_rows, 0.0, kk * jnp.exp2(ref - b)).astype(BF16)
        level = (jnp.bitwise_xor(rr, cc) // h) == 1
        for hd in range(HGRN_HEADS):
            sl = slice(hd * HGRN_DIM, (hd + 1) * HGRN_DIM)
            a_mats[hd] = jnp.where(level, _dot_nt(ql[:, sl], kl[:, sl]), a_mats[hd])
        h //= 2

    qk = q * kk
    outs = []
    for hd in range(HGRN_HEADS):
        sl = slice(hd * HGRN_DIM, (hd + 1) * HGRN_DIM)
        st = st_ref[hd]
        o = _dot_nt(q_in[:, sl], st.astype(BF16))
        o = o + _dot(a_mats[hd].astype(BF16), vb[:, sl])
        o = o + jnp.sum(qk[:, sl], axis=-1, keepdims=True) * v[:, sl]
        outs.append(o)
        st_ref[hd] = st * s_decay[:, sl] + _dot_tn(vb[:, sl], k_out[:, sl])
    return jnp.concatenate(outs, axis=-1)


def _hgrn_kernel(q_ref, f_ref, v_ref, lb_ref, tri_ref, o_ref, st_ref, *, n_inner):
    d = pl.program_id(0)
    fwd = d == 0

    @pl.when(pl.program_id(2) == 0)
    def _():
        st_ref[...] = jnp.zeros(st_ref.shape, F32)

    lb = lb_ref[0]
    tri = tri_ref[0]

    def body(i, carry):
        ci = jnp.where(fwd, i, n_inner - 1 - i)
        r0 = pl.multiple_of(ci * HGRN_CHUNK, HGRN_CHUNK)
        rows = pl.ds(r0, HGRN_CHUNK)
        o_ref[0, 0, rows, :] = _hgrn_chunk(q_ref[0, rows, :], f_ref[0, rows, :], v_ref[0, rows, :],
                                           lb, tri, st_ref, fwd)
        return carry

    lax.fori_loop(0, n_inner, body, 0)


def _hgrn(hq, hf, hi, lb, B, L):
    W = BRANCH_WIDTH
    rb = _row_tile(L, 512)
    nb = L // rb
    C = HGRN_CHUNK
    tri_f = np.tril(np.ones((C, C), np.float32))
    tri = jnp.asarray(np.stack([tri_f, tri_f.T]), dtype=BF16)
    blk = lambda d, b, c: (b, c + d * (nb - 1 - 2 * c), 0)
    return pl.pallas_call(
        functools.partial(_hgrn_kernel, n_inner=rb // C),
        grid=(2, B, nb),
        in_specs=[pl.BlockSpec((1, rb, W), blk),
                  pl.BlockSpec((1, rb, W), lambda d, b, c: (b, c + d * (nb - 1 - 2 * c), d)),
                  pl.BlockSpec((1, rb, W), blk),
                  pl.BlockSpec((1, 1, W), lambda d, b, c: (d, 0, 0)),
                  pl.BlockSpec((1, C, C), lambda d, b, c: (d, 0, 0))],
        out_specs=pl.BlockSpec((1, 1, rb, W), lambda d, b, c: (d, b, c + d * (nb - 1 - 2 * c), 0)),
        out_shape=jax.ShapeDtypeStruct((2, B, L, W), F32),
        scratch_shapes=[pltpu.VMEM((HGRN_HEADS, HGRN_DIM, HGRN_DIM), F32)],
        compiler_params=_cparams(("parallel", "parallel", "arbitrary")),
        name="hgrn_scan",
    )(hq.reshape(B, L, W), hf.reshape(B, L, 2 * W), hi.reshape(B, L, W), lb[:, None, :], tri)


def _filter_features(L):
    t = jnp.linspace(0.0, 1.0, L, dtype=F32)[:, None]
    w = 2.0 * math.pi * jnp.arange(L, dtype=F32) / L
    bands = jnp.linspace(1e-4, FILTER_BANDS - 1, FILTER_BANDS, dtype=F32)
    ang = w[:, None] * bands[None, :]
    feats = jnp.concatenate([t, jnp.cos(ang), -jnp.sin(ang)], axis=-1)
    feats2 = jnp.concatenate([feats, feats[:1], jnp.flip(feats[1:], axis=0)], axis=0)
    t2 = jnp.concatenate([t, t[:1], jnp.flip(t[1:], axis=0)], axis=0)
    deltas = jnp.abs(jnp.linspace(math.log(FILTER_TARGET) / FILTER_FAST_DECAY,
                                  math.log(FILTER_TARGET) / FILTER_SLOW_DECAY, N_FILTER_CH, dtype=F32))
    return feats2, t2, deltas[None, :]


def _filter_kernel(feat_ref, t_ref, w1_ref, b1_ref, fr_ref, w2_ref, b2_ref, w3f_ref, w3b_ref,
                   df_ref, db_ref, k_ref, h_ref, *, L):
    @pl.when(pl.program_id(0) == 0)
    def _():
        freq = fr_ref[...]
        h1 = jnp.sin(freq * (_dot3(feat_ref[...], w1_ref[...]) + b1_ref[...]))
        h_ref[...] = jnp.sin(freq * (_dot3(h1, w2_ref[...]) + b2_ref[...]))

    h = h_ref[...]
    t = t_ref[...]
    top = _dot3(h[:L], w3f_ref[...]) * jnp.exp(-t[:L] * df_ref[...])
    bot = _dot3(h[L:], w3b_ref[...]) * jnp.exp(-t[L:] * db_ref[...])
    row = lax.broadcasted_iota(jnp.int32, (L, 1), 0)
    bot = jnp.where(row == 0, 0.0, bot)
    norm = jnp.sum(jnp.abs(top), axis=0, keepdims=True) + jnp.sum(jnp.abs(bot), axis=0, keepdims=True)
    k_ref[:L, :] = top / norm
    k_ref[L:, :] = bot / norm


def _hyena_filter(L, consts, w1, b1, freq, w2, b2, w3):
    feats2, t2, deltas = consts
    wb = 256
    nw = HYENA_WIDTH // wb
    full = lambda shape: pl.BlockSpec(shape, lambda c: (0,) * len(shape))
    fcol = lambda c: (0, (c // nw) * 2 * nw + c % nw)
    bcol = lambda c: (0, (c // nw) * 2 * nw + nw + c % nw)
    return pl.pallas_call(
        functools.partial(_filter_kernel, L=L),
        grid=(HYENA_ORDER * nw,),
        in_specs=[full((2 * L, FILTER_EMB)), full((2 * L, 1)),
                  full((FILTER_EMB, FILTER_HIDDEN)), full((1, FILTER_HIDDEN)), full((1, FILTER_HIDDEN)),
                  full((FILTER_HIDDEN, FILTER_HIDDEN)), full((1, FILTER_HIDDEN)),
                  pl.BlockSpec((FILTER_HIDDEN, wb), fcol), pl.BlockSpec((FILTER_HIDDEN, wb), bcol),
                  pl.BlockSpec((1, wb), fcol), pl.BlockSpec((1, wb), bcol)],
        out_specs=pl.BlockSpec((2 * L, wb), lambda c: (0, c)),
        out_shape=jax.ShapeDtypeStruct((2 * L, HYENA_ORDER * HYENA_WIDTH), F32),
        scratch_shapes=[pltpu.VMEM((2 * L, FILTER_HIDDEN), F32)],
        compiler_params=_cparams(("arbitrary",)),
        name="hyena_filter",
    )(feats2, t2, w1, b1[None, :], freq[None, :], w2, b2[None, :], w3, w3, deltas, deltas)


def _dft_consts(L):
    N = 2 * L
    N2 = FFT_N2
    N1 = N // N2
    k1 = np.arange(N1)
    ang_a = 2.0 * np.pi * np.outer(k1, np.arange(N1)) / N1
    fa_full = np.concatenate([np.cos(ang_a), -np.sin(ang_a)], axis=0)
    fa_half = fa_full[:, :N1 // 2]
    n2 = np.arange(N2)
    kk = k1[:, None, None] + N1 * np.arange(N2)[None, :, None]
    ang_b = 2.0 * np.pi * (kk * n2[None, None, :]) / N
    mr, mi = np.cos(ang_b), -np.sin(ang_b)
    m_fwd = np.concatenate([np.concatenate([mr, -mi], axis=2),
                            np.concatenate([mi, mr], axis=2)], axis=1)
    wr, wi = np.swapaxes(mr, 1, 2), -np.swapaxes(mi, 1, 2)
    m_inv = np.concatenate([np.concatenate([wr, -wi], axis=2),
                            np.concatenate([wi, wr], axis=2)], axis=1)
    ang_c = 2.0 * np.pi * np.outer(np.arange(N1 // 2), k1) / N1
    g_inv = np.concatenate([np.cos(ang_c), -np.sin(ang_c)], axis=1) / N

    def hi_lo(a):
        a = jnp.asarray(a, dtype=F32)
        hi = a.astype(BF16)
        return hi, (a - hi.astype(F32)).astype(BF16)

    eye = np.eye(SUBLANES)
    fa_tiles = jnp.asarray(np.kron(fa_half, eye), dtype=BF16)
    g_tiles = jnp.asarray(np.kron(g_inv, eye), dtype=BF16)
    return dict(N1=N1, N2=N2, fa_full=hi_lo(fa_full), fa_tiles=fa_tiles, g_tiles=g_tiles,
                m_fwd=hi_lo(m_fwd), m_inv=hi_lo(m_inv))


def _mm3(m_hi, m_lo, x):
    x_hi, x_lo = _split2(x)
    return _dot(m_hi, x_hi) + _dot(m_hi, x_lo) + _dot(m_lo, x_hi)


def _stage_a_kernel(z_ref, fh_ref, fl_ref, a_ref, *, exact):
    n1 = a_ref.shape[2]
    if exact:
        res = _mm3(fh_ref[...], fl_ref[...], z_ref[0])
    else:
        res = _dot(fh_ref[...], z_ref[0].astype(BF16))
    a_ref[0, 0] = res[:n1].astype(a_ref.dtype)
    a_ref[0, 1] = res[n1:].astype(a_ref.dtype)


def _fft_stage_a(z, fa, N1, exact):
    B, nr, cols = z.shape
    cb = min(cols, 4096)
    return pl.pallas_call(
        functools.partial(_stage_a_kernel, exact=exact),
        grid=(B, cols // cb),
        in_specs=[pl.BlockSpec((1, nr, cb), lambda b, j: (b, 0, j)),
                  pl.BlockSpec((2 * N1, nr), lambda b, j: (0, 0)),
                  pl.BlockSpec((2 * N1, nr), lambda b, j: (0, 0))],
        out_specs=pl.BlockSpec((1, 2, N1, cb), lambda b, j: (b, 0, 0, j)),
        out_shape=jax.ShapeDtypeStruct((B, 2, N1, cols), F32 if exact else BF16),
        compiler_params=_cparams(("parallel", "parallel")),
        name="fft_stage_a",
    )(z, fa[0], fa[1])


def _stage_b_filter_kernel(a_ref, mh_ref, ml_ref, o_ref):
    n2 = a_ref.shape[3]
    a = a_ref[0, :, 0].reshape(2 * n2, a_ref.shape[4])
    res = _mm3(mh_ref[0], ml_ref[0], a)
    o_ref[0, 0] = res[:n2]
    o_ref[1, 0] = res[n2:]


def _filter_spectrum(k2l, dc):
    N1, N2 = dc["N1"], dc["N2"]
    OW = k2l.shape[1]
    a = _fft_stage_a(k2l.reshape(1, N1, N2 * OW), dc["fa_full"], N1, True)
    a = a.reshape(1, 2, N1, N2, OW)
    wb = 512
    return pl.pallas_call(
        _stage_b_filter_kernel,
        grid=(N1, OW // wb),
        in_specs=[pl.BlockSpec((1, 2, 1, N2, wb), lambda k, j: (0, 0, k, 0, j)),
                  pl.BlockSpec((1, 2 * N2, 2 * N2), lambda k, j: (k, 0, 0)),
                  pl.BlockSpec((1, 2 * N2, 2 * N2), lambda k, j: (k, 0, 0))],
        out_specs=pl.BlockSpec((2, 1, N2, wb), lambda k, j: (0, k, 0, j)),
        out_shape=jax.ShapeDtypeStruct((2, N1, N2, OW), F32),
        compiler_params=_cparams(("parallel", "parallel")),
        name="fft_filter_spectrum",
    )(a, dc["m_fwd"][0], dc["m_fwd"][1])


FFT_SPECTRUM_BYTES = 8 * 1024 * 1024


def _long_conv_kernel(z_ref, gate_ref, h_ref, fa_ref, mf_ref, mi_ref, g_ref, bias_ref, o_ref, a_ref):
    n1, n2, wq = a_ref.shape[1], a_ref.shape[2], a_ref.shape[3]
    nr = z_ref.shape[1]
    s8 = SUBLANES
    fa = fa_ref[...]
    for jt in range(n2 // s8):
        tile = slice(jt * s8, (jt + 1) * s8)
        zt = z_ref[0, :, tile, :].reshape(nr * s8, wq).astype(BF16)
        res = _dot(fa, zt)
        a_ref[0, :, tile, :] = res[:n1 * s8].reshape(n1, s8, wq)
        a_ref[1, :, tile, :] = res[n1 * s8:].reshape(n1, s8, wq)

    def per_k1(k, carry):
        a = jnp.concatenate([a_ref[0, k], a_ref[1, k]], axis=0).astype(BF16)
        zz = _dot(mf_ref[k], a)
        zr, zi = zz[:n2], zz[n2:]
        hr, hi = h_ref[0, k], h_ref[1, k]
        y = jnp.concatenate([zr * hr - zi * hi, zr * hi + zi * hr], axis=0).astype(BF16)
        c = _dot(mi_ref[k], y)
        a_ref[0, k] = c[:n2]
        a_ref[1, k] = c[n2:]
        return carry

    lax.fori_loop(0, n1, per_k1, 0, unroll=8)
    g = g_ref[...]
    bias = bias_ref[...]
    for jt in range(n2 // s8):
        tile = slice(jt * s8, (jt + 1) * s8)
        c = jnp.concatenate([a_ref[0, :, tile, :].reshape(n1 * s8, wq),
                             a_ref[1, :, tile, :].reshape(n1 * s8, wq)], axis=0).astype(BF16)
        y = _dot(g, c).reshape(nr, s8, wq)
        o_ref[0, :, tile, :] = gate_ref[0, :, tile, :] * (y + z_ref[0, :, tile, :] * bias)


def _long_conv_gated(z, gate, hf, order, skip, dc):
    B, nr, N2, W = z.shape
    N1 = dc["N1"]
    wq = min(W, max(LANES, FFT_SPECTRUM_BYTES // (2 * N1 * N2 * 4) // LANES * LANES))
    nw = W // wq
    rows = pl.BlockSpec((1, nr, N2, wq), lambda w, b: (b, 0, 0, w))
    once = pl.Buffered(1)
    mats = pl.BlockSpec((N1, 2 * N2, 2 * N2), lambda w, b: (0, 0, 0), pipeline_mode=once)
    return pl.pallas_call(
        _long_conv_kernel,
        grid=(nw, B),
        in_specs=[rows, rows,
                  pl.BlockSpec((2, N1, N2, wq), lambda w, b: (0, 0, 0, order * nw + w), pipeline_mode=once),
                  pl.BlockSpec((2 * N1 * SUBLANES, nr * SUBLANES), lambda w, b: (0, 0)),
                  mats, mats,
                  pl.BlockSpec((nr * SUBLANES, 2 * N1 * SUBLANES), lambda w, b: (0, 0)),
                  pl.BlockSpec((1, wq), lambda w, b: (0, w))],
        out_specs=rows,
        out_shape=jax.ShapeDtypeStruct((B, nr, N2, W), F32),
        scratch_shapes=[pltpu.VMEM((2, N1, N2, wq), F32)],
        compiler_params=_cparams(("parallel", "parallel")),
        name="hyena_long_conv",
    )(z, gate, hf, dc["fa_tiles"], dc["m_fwd"][0], dc["m_inv"][0], dc["g_tiles"], skip[None, :])


HALO_ROWS = 16


def _short_conv_kernel(u_ref, prev_ref, next_ref, w_ref, b_ref, v_ref, x1_ref, x2_ref):
    i = pl.program_id(1)
    u = u_ref[0].astype(F32)
    tl = u.shape[0]
    row = lax.broadcasted_iota(jnp.int32, (tl, 1), 0)
    prev_last = prev_ref[0].astype(F32)[HALO_ROWS - 1:HALO_ROWS, :]
    next_first = next_ref[0].astype(F32)[0:1, :]
    before = jnp.where(i == 0, 0.0, prev_last)
    after = jnp.where(i == pl.num_programs(1) - 1, 0.0, next_first)
    up = jnp.where(row == 0, before, pltpu.roll(u, 1, 0))
    un = jnp.where(row == tl - 1, after, pltpu.roll(u, tl - 1, 0))
    w = w_ref[...]
    uc = up * w[0:1] + u * w[1:2] + un * w[2:3] + b_ref[...]
    W = HYENA_WIDTH
    v_ref[0] = uc[:, :W]
    x1_ref[0] = uc[:, W:2 * W]
    x2_ref[0] = uc[:, 2 * W:]


def _short_conv(hy, conv_w, conv_b, B, L):
    W3 = 3 * HYENA_WIDTH
    tl = _row_tile(L, 512)
    nl = L // tl
    u = hy.reshape(B, L, W3)
    rh = tl // HALO_ROWS
    spec_o = pl.BlockSpec((1, tl, HYENA_WIDTH), lambda b, i: (b, i, 0))
    return pl.pallas_call(
        _short_conv_kernel,
        grid=(B, nl),
        in_specs=[pl.BlockSpec((1, tl, W3), lambda b, i: (b, i, 0)),
                  pl.BlockSpec((1, HALO_ROWS, W3), lambda b, i: (b, jnp.maximum(i * rh - 1, 0), 0)),
                  pl.BlockSpec((1, HALO_ROWS, W3),
                               lambda b, i: (b, jnp.minimum((i + 1) * rh, L // HALO_ROWS - 1), 0)),
                  pl.BlockSpec((3, W3), lambda b, i: (0, 0)),
                  pl.BlockSpec((1, W3), lambda b, i: (0, 0))],
        out_specs=[spec_o, spec_o, spec_o],
        out_shape=[jax.ShapeDtypeStruct((B, L, HYENA_WIDTH), F32)] * 3,
        compiler_params=_cparams(("parallel", "parallel")),
        name="hyena_short_conv",
    )(u, u, u, conv_w, conv_b[None, :])


def _hyena(hy, hf, conv_w, conv_b, skip, dc, B, L):
    v, x1, x2 = _short_conv(hy, conv_w, conv_b, B, L)
    shape = (B, dc["N1"] // 2, dc["N2"], HYENA_WIDTH)
    z = _long_conv_gated(v.reshape(shape), x1.reshape(shape), hf, 0, skip[0], dc)
    y = _long_conv_gated(z, x2.reshape(shape), hf, 1, skip[1], dc)
    return y.reshape(B * L, HYENA_WIDTH)


def _merge_kernel(x_ref, oa_ref, oh_ref, hg_ref, oc_ref, gl_ref, hgn_ref, wb_ref, wo_ref, g_ref, b_ref,
                  out_ref):
    o = oh_ref[0] + oh_ref[1]
    parts = []
    for hd in range(HGRN_HEADS):
        oh = o[:, hd * HGRN_DIM:(hd + 1) * HGRN_DIM]
        ms = jnp.mean(oh * oh, axis=-1, keepdims=True)
        parts.append(oh * lax.rsqrt(ms + RMS_EPS))
    o_h = jnp.concatenate(parts, axis=-1) * hgn_ref[...] * _silu(hg_ref[...].astype(F32))
    gl = gl_ref[...].astype(F32)
    branches = (oa_ref[...], o_h, oc_ref[...])
    merged = None
    for j, ob in enumerate(branches):
        term = _sigmoid(gl[:, j * D_MODEL:(j + 1) * D_MODEL]) * _dot(ob.astype(BF16), wb_ref[j])
        merged = term if merged is None else merged + term
    mix = _dot(merged.astype(BF16), wo_ref[...])
    out_ref[...] = _layer_norm(ALPHA * x_ref[...] + mix, g_ref[...], b_ref[...])


def _merge(x2, o_a, o_dir, hg, o_c, gl, hgrn_gain, wb_bf16, wo_bf16, ln_g, ln_b):
    T = x2.shape[0]
    tm = _row_tile(T, 512)
    W = BRANCH_WIDTH
    row = lambda w: pl.BlockSpec((tm, w), lambda i: (i, 0))
    return pl.pallas_call(
        _merge_kernel,
        grid=(T // tm,),
        in_specs=[row(D_MODEL), row(W), pl.BlockSpec((2, tm, W), lambda i: (0, i, 0)), row(W), row(W),
                  row(N_BRANCH * D_MODEL),
                  pl.BlockSpec((1, W), lambda i: (0, 0)),
                  pl.BlockSpec((N_BRANCH, W, D_MODEL), lambda i: (0, 0, 0)),
                  pl.BlockSpec((D_MODEL, D_MODEL), lambda i: (0, 0)),
                  pl.BlockSpec((1, D_MODEL), lambda i: (0, 0)),
                  pl.BlockSpec((1, D_MODEL), lambda i: (0, 0))],
        out_specs=row(D_MODEL),
        out_shape=jax.ShapeDtypeStruct((T, D_MODEL), F32),
        compiler_params=_cparams(("parallel",)),
        name="merge_out_ln",
    )(x2, o_a, o_dir.reshape(2, T, W), hg, o_c, gl, jnp.tile(hgrn_gain, HGRN_HEADS)[None, :],
      wb_bf16, wo_bf16, ln_g[None, :], ln_b[None, :])


def _ffn_kernel(x_ref, wg_ref, wu_ref, wd_ref, g_ref, b_ref, out_ref, acc_ref):
    j = pl.program_id(1)
    x = x_ref[...]
    xb = x.astype(BF16)
    hmid = _silu(_dot(xb, wg_ref[...])) * _dot(xb, wu_ref[...])
    contrib = _dot(hmid.astype(BF16), wd_ref[...])

    @pl.when(j == 0)
    def _():
        acc_ref[...] = contrib

    @pl.when(j > 0)
    def _():
        acc_ref[...] += contrib

    @pl.when(j == pl.num_programs(1) - 1)
    def _():
        out_ref[...] = _layer_norm(ALPHA * x + acc_ref[...], g_ref[...], b_ref[...])


def _ffn_dense(x2, w_gu, w_down, ln_g, ln_b):
    T = x2.shape[0]
    tm = _row_tile(T, 512)
    ft = 1408
    nf = D_FF_DENSE // ft
    return pl.pallas_call(
        _ffn_kernel,
        grid=(T // tm, nf),
        in_specs=[pl.BlockSpec((tm, D_MODEL), lambda i, j: (i, 0)),
                  pl.BlockSpec((D_MODEL, ft), lambda i, j: (0, j)),
                  pl.BlockSpec((D_MODEL, ft), lambda i, j: (0, j + nf)),
                  pl.BlockSpec((ft, D_MODEL), lambda i, j: (j, 0)),
                  pl.BlockSpec((1, D_MODEL), lambda i, j: (0, 0)),
                  pl.BlockSpec((1, D_MODEL), lambda i, j: (0, 0))],
        out_specs=pl.BlockSpec((tm, D_MODEL), lambda i, j: (i, 0)),
        out_shape=jax.ShapeDtypeStruct((T, D_MODEL), F32),
        scratch_shapes=[pltpu.VMEM((tm, D_MODEL), F32)],
        compiler_params=_cparams(("parallel", "arbitrary")),
        name="ffn_dense_ln",
    )(x2, w_gu, w_gu, w_down, ln_g[None, :], ln_b[None, :])


MOE_BLOCK = 1024
MOE_SUB = 64
MOE_BIG = 8
MOE_TILE = 256


def _route_kernel(x_ref, r_ref, col_ref, row_ref, s_ref):
    tb = x_ref.shape[0]
    logits = _dot3(x_ref[...], r_ref[...])
    lane = lax.broadcasted_iota(jnp.int32, logits.shape, 1)
    logits = jnp.where(lane < N_EXPERTS, logits, -jnp.inf)
    m1 = jnp.max(logits, axis=-1, keepdims=True)
    i1 = jnp.min(jnp.where(logits == m1, lane, LANES), axis=-1, keepdims=True)
    rest = jnp.where(lane == i1, -jnp.inf, logits)
    m2 = jnp.max(rest, axis=-1, keepdims=True)
    i2 = jnp.min(jnp.where(rest == m2, lane, LANES), axis=-1, keepdims=True)
    e2 = jnp.exp(m2 - m1)
    w1 = 1.0 / (1.0 + e2)
    w2 = e2 / (1.0 + e2)
    oh1 = lane == i1
    oh2 = lane == i2
    sel = jnp.where(oh1, 1.0, jnp.where(oh2, 1.0, 0.0))
    rr = lax.broadcasted_iota(jnp.int32, (tb, tb), 0)
    cc = lax.broadcasted_iota(jnp.int32, (tb, tb), 1)
    earlier = jnp.where(rr > cc, 1.0, 0.0).astype(BF16)
    rank = _dot(earlier, sel.astype(BF16))
    cnt = jnp.sum(sel, axis=0, keepdims=True)
    nsub = jnp.floor((cnt + (MOE_SUB - 1)) * (1.0 / MOE_SUB))
    r2 = lax.broadcasted_iota(jnp.int32, (LANES, LANES), 0)
    c2 = lax.broadcasted_iota(jnp.int32, (LANES, LANES), 1)
    before = jnp.where(r2 < c2, 1.0, 0.0).astype(BF16)
    nsub8 = jnp.broadcast_to(nsub, (8, LANES))
    off8 = _dot(nsub8.astype(BF16), before) * MOE_SUB
    base = off8[0:1] + rank
    d1 = jnp.sum(jnp.where(oh1, base, 0.0), axis=-1, keepdims=True)
    d2 = jnp.sum(jnp.where(oh2, base, 0.0), axis=-1, keepdims=True)
    col = jnp.where(lane == 0, d1, jnp.where(lane == 1, d2, jnp.where(lane == 2, w1, jnp.where(lane == 3, w2, 0.0))))
    col_ref[...] = col
    row_ref[0] = col.T[0:8, :]
    lane8 = lax.broadcasted_iota(jnp.int32, (8, LANES), 1)
    info = jnp.where(lane8 < N_EXPERTS, off8, pltpu.roll(nsub8, N_EXPERTS, 1))
    s_ref[0] = info.astype(jnp.int32)


def _moe_route(x2, router):
    T = x2.shape[0]
    tb = _row_tile(T, MOE_BLOCK)
    nb = T // tb
    r_pad = jnp.pad(router, ((0, 0), (0, LANES - N_EXPERTS)))
    return pl.pallas_call(
        _route_kernel,
        grid=(nb,),
        in_specs=[pl.BlockSpec((tb, D_MODEL), lambda i: (i, 0)),
                  pl.BlockSpec((D_MODEL, LANES), lambda i: (0, 0))],
        out_specs=[pl.BlockSpec((tb, LANES), lambda i: (i, 0)),
                   pl.BlockSpec((1, 8, tb), lambda i: (i, 0, 0)),
                   pl.BlockSpec((1, 8, LANES), lambda i: (i, 0, 0))],
        out_shape=[jax.ShapeDtypeStruct((T, LANES), F32),
                   jax.ShapeDtypeStruct((nb, 8, tb), F32),
                   jax.ShapeDtypeStruct((nb, 8, LANES), jnp.int32)],
        compiler_params=_cparams(("parallel",)),
        name="moe_route",
    )(x2, r_pad)


def _moe_kernel(s_ref, x_ref, col_ref, row_ref, wg_ref, wu_ref, wd_ref, g_ref, b_ref, out_ref,
                xs_ref, ys_ref):
    blk = pl.program_id(0)
    e = pl.program_id(1)
    j = pl.program_id(2)
    tb = x_ref.shape[0]
    n_tiles = xs_ref.shape[0] // MOE_TILE
    used = s_ref[blk, N_EXPERTS - 1] + s_ref[blk, 2 * N_EXPERTS - 1] * MOE_SUB
    d1r = row_ref[0, 0:1, :]
    d2r = row_ref[0, 1:2, :]

    @pl.when(jnp.logical_and(e == 0, j == 0))
    def _():
        xb = x_ref[...].astype(BF16)
        for rt in range(n_tiles):
            @pl.when(rt * MOE_TILE < used)
            def _():
                r = (lax.broadcasted_iota(jnp.int32, (MOE_TILE, tb), 0) + rt * MOE_TILE).astype(F32)
                p = jnp.where(r == d1r, 1.0, jnp.where(r == d2r, 1.0, 0.0)).astype(BF16)
                xs_ref[rt * MOE_TILE:(rt + 1) * MOE_TILE, :] = _dot(p, xb).astype(BF16)

    start = s_ref[blk, e]
    nsub = s_ref[blk, N_EXPERTS + e]

    def expert_rows(first, n_rows):
        rows = pl.ds(pl.multiple_of(first, MOE_SUB), n_rows)
        xt = xs_ref[rows, :]
        hmid = _silu(_dot(xt, wg_ref[0])) * _dot(xt, wu_ref[0])
        y = _dot(hmid.astype(BF16), wd_ref[0])

        @pl.when(j == 0)
        def _():
            ys_ref[rows, :] = y

        @pl.when(j > 0)
        def _():
            ys_ref[rows, :] += y

    n_big = nsub // MOE_BIG
    rem = nsub - MOE_BIG * n_big

    def big(i, carry):
        expert_rows(start + i * (MOE_BIG * MOE_SUB), MOE_BIG * MOE_SUB)
        return carry

    lax.fori_loop(0, n_big, big, 0)
    done = n_big * MOE_BIG
    part = MOE_BIG // 2
    while part >= 1:
        @pl.when((rem // part) % 2 == 1)
        def _(done=done, part=part):
            expert_rows(start + (done + (rem // (2 * part)) * (2 * part)) * MOE_SUB, part * MOE_SUB)
        part //= 2

    @pl.when(jnp.logical_and(e == pl.num_programs(1) - 1, j == pl.num_programs(2) - 1))
    def _():
        sub_per_tile = MOE_TILE // MOE_SUB
        filled = (used // MOE_SUB) % sub_per_tile
        for k in range(1, sub_per_tile):
            @pl.when(jnp.logical_and(filled > 0, filled <= k))
            def _(k=k):
                first = (used // MOE_TILE) * MOE_TILE + k * MOE_SUB
                ys_ref[pl.ds(pl.multiple_of(first, MOE_SUB), MOE_SUB), :] = jnp.zeros((MOE_SUB, D_MODEL), F32)

        w1r = row_ref[0, 2:3, :]
        w2r = row_ref[0, 3:4, :]
        for rt in range(n_tiles):
            @pl.when(rt * MOE_TILE < used)
            def _():
                r = (lax.broadcasted_iota(jnp.int32, (MOE_TILE, tb), 0) + rt * MOE_TILE).astype(F32)
                w_rows = jnp.sum(jnp.where(r == d1r, w1r, 0.0) + jnp.where(r == d2r, w2r, 0.0),
                                 axis=-1, keepdims=True)
                rows = slice(rt * MOE_TILE, (rt + 1) * MOE_TILE)
                xs_ref[rows, :] = (ys_ref[rows, :] * w_rows).astype(BF16)

        d1c = col_ref[:, 0:1]
        d2c = col_ref[:, 1:2]
        out_ref[...] = jnp.zeros(out_ref.shape, F32)
        for rt in range(n_tiles):
            @pl.when(rt * MOE_TILE < used)
            def _():
                c = (lax.broadcasted_iota(jnp.int32, (tb, MOE_TILE), 1) + rt * MOE_TILE).astype(F32)
                q = jnp.where(c == d1c, 1.0, jnp.where(c == d2c, 1.0, 0.0)).astype(BF16)
                out_ref[...] += _dot(q, xs_ref[rt * MOE_TILE:(rt + 1) * MOE_TILE, :])

        out_ref[...] = _layer_norm(ALPHA * x_ref[...] + out_ref[...], g_ref[...], b_ref[...])


def _ffn_moe(x2, router, w_gu, w_down, ln_g, ln_b):
    T = x2.shape[0]
    tb = _row_tile(T, MOE_BLOCK)
    nb = T // tb
    n_rows = 2 * tb + N_EXPERTS * MOE_SUB
    col, row, sinfo = _moe_route(x2, router)
    ft = 896
    nf = D_FF_EXPERT // ft
    grid_spec = pltpu.PrefetchScalarGridSpec(
        num_scalar_prefetch=1,
        grid=(nb, N_EXPERTS, nf),
        in_specs=[pl.BlockSpec((tb, D_MODEL), lambda i, e, j, s: (i, 0)),
                  pl.BlockSpec((tb, LANES), lambda i, e, j, s: (i, 0)),
                  pl.BlockSpec((1, 8, tb), lambda i, e, j, s: (i, 0, 0)),
                  pl.BlockSpec((1, D_MODEL, ft), lambda i, e, j, s: (e, 0, j)),
                  pl.BlockSpec((1, D_MODEL, ft), lambda i, e, j, s: (e, 0, j + nf)),
                  pl.BlockSpec((1, ft, D_MODEL), lambda i, e, j, s: (e, j, 0)),
                  pl.BlockSpec((1, D_MODEL), lambda i, e, j, s: (0, 0)),
                  pl.BlockSpec((1, D_MODEL), lambda i, e, j, s: (0, 0))],
        out_specs=pl.BlockSpec((tb, D_MODEL), lambda i, e, j, s: (i, 0)),
        scratch_shapes=[pltpu.VMEM((n_rows, D_MODEL), BF16), pltpu.VMEM((n_rows, D_MODEL), F32)])
    return pl.pallas_call(
        _moe_kernel,
        grid_spec=grid_spec,
        out_shape=jax.ShapeDtypeStruct((T, D_MODEL), F32),
        compiler_params=_cparams(("parallel", "arbitrary", "arbitrary")),
        name="ffn_moe_ln",
    )(sinfo[:, 0, :], x2, col, row, w_gu, w_gu, w_down, ln_g[None, :], ln_b[None, :])


def _trunk(x, p):
    B, L, _ = x.shape
    cos_t, sin_t = _rope_tables(L)
    filt_consts = _filter_features(L)
    dc = _dft_consts(L)
    x2 = x.reshape(B * L, D_MODEL)
    for l in range(DEPTH):
        aq, akv, hq, hf, hi, hg, hy, gl = _in_proj(x2, p["w_in"][l])
        qn, kn, vn = _attn_prep(aq, akv, cos_t, sin_t, p["q_norm_g"][l], p["k_norm_g"][l], B, L)
        o_a = _flash_attention(qn, kn, vn, B, L)
        o_dir = _hgrn(hq, hf, hi, p["lower_bounds"][l], B, L)
        k2l = _hyena_filter(L, filt_consts, p["filt_w1"][l], p["filt_b1"][l], p["filt_freq"][l],
                            p["filt_w2"][l], p["filt_b2"][l], p["filt_w3"][l])
        spec = _filter_spectrum(k2l, dc)
        o_c = _hyena(hy, spec, p["conv_w"][l], p["conv_b"][l], p["hyena_skip"][l], dc, B, L)
        x2 = _merge(x2, o_a, o_dir, hg, o_c, gl, p["hgrn_norm_g"][l], p["w_branch"][l], p["w_out"][l],
                    p["ln1_g"][l], p["ln1_b"][l])
        if l % 2 == 0:
            x2 = _ffn_dense(x2, p["ffn_w_gu"][l // 2], p["ffn_w_down"][l // 2], p["ln2_g"][l], p["ln2_b"][l])
        else:
            x2 = _ffn_moe(x2, p["router_w"][l // 2], p["expert_w_gu"][l // 2], p["expert_w_down"][l // 2],
                          p["ln2_g"][l], p["ln2_b"][l])
    return x2.reshape(B, L, D_MODEL)


def _in_proj_weight(w_in):
    return w_in.astype(BF16)


def kernel(x_prompt, x_sample, w_in, q_norm_g, k_norm_g, hgrn_lb, hgrn_norm_g, conv_w, conv_b, filt_w1, filt_b1, filt_freq, filt_w2, filt_b2, filt_w3, hyena_skip, w_branch, w_out, ln1_g, ln1_b, ln2_g, ln2_b, ffn_w_gu, ffn_w_down, router_w, expert_w_gu, expert_w_down):
    s = jax.nn.softmax(hgrn_lb.astype(F32), axis=0)
    p = dict(
        w_in=_in_proj_weight(w_in), q_norm_g=q_norm_g, k_norm_g=k_norm_g,
        lower_bounds=jnp.cumsum(s, axis=0) - s[:1], hgrn_norm_g=hgrn_norm_g,
        conv_w=conv_w, conv_b=conv_b, filt_w1=filt_w1, filt_b1=filt_b1, filt_freq=filt_freq,
        filt_w2=filt_w2, filt_b2=filt_b2, filt_w3=filt_w3, hyena_skip=hyena_skip,
        w_branch=w_branch.astype(BF16), w_out=w_out.astype(BF16),
        ln1_g=ln1_g, ln1_b=ln1_b, ln2_g=ln2_g, ln2_b=ln2_b,
        ffn_w_gu=ffn_w_gu.astype(BF16), ffn_w_down=ffn_w_down.astype(BF16), router_w=router_w,
        expert_w_gu=expert_w_gu.astype(BF16), expert_w_down=expert_w_down.astype(BF16))
    return (_trunk(x_prompt, p), _trunk(x_sample, p))
```

```python
import functools
import math

import numpy as np
import jax
import jax.numpy as jnp
from jax import lax
from jax.experimental import pallas as pl
from jax.experimental.pallas import tpu as pltpu

F32 = jnp.float32
BF16 = jnp.bfloat16

D_MODEL = 1024
DEPTH = 2
GRID_W = 64
BRANCH_WIDTH = 512
ATTN_HEADS = 8
ATTN_KV_HEADS = 2
ATTN_GROUP = ATTN_HEADS // ATTN_KV_HEADS
HEAD_DIM = 64
KV_WIDTH = ATTN_KV_HEADS * HEAD_DIM
ROPE_AXIS_DIM = HEAD_DIM // 2
ROPE_HALF = ROPE_AXIS_DIM // 2
ROPE_THETA = 10000.0
HGRN_HEADS = 4
HGRN_DIM = 128
HGRN_CHUNK = 64
FORGET_FLOOR = 1e-30
HYENA_WIDTH = BRANCH_WIDTH
HYENA_ORDER = 2
FILTER_EMB = 33
FILTER_BANDS = (FILTER_EMB - 1) // 2
FILTER_HIDDEN = 64
FILTER_FAST_DECAY = 0.3
FILTER_SLOW_DECAY = 1.5
FILTER_TARGET = 1e-2
N_FILTER_CH = HYENA_ORDER * 2 * HYENA_WIDTH
N_BRANCH = 3
D_FF_DENSE = 2816
N_EXPERTS = 8
D_FF_EXPERT = 3584
ALPHA = (2 * DEPTH) ** 0.25
LN_EPS = 1e-5
RMS_EPS = 1e-6

SEG_WIDTHS = (BRANCH_WIDTH,
              2 * KV_WIDTH,
              BRANCH_WIDTH,
              2 * BRANCH_WIDTH,
              BRANCH_WIDTH,
              BRANCH_WIDTH,
              3 * HYENA_WIDTH,
              N_BRANCH * D_MODEL)
IN_WIDTH = sum(SEG_WIDTHS)
SEG_DTYPES = (BF16, BF16, BF16, F32, BF16, BF16, BF16, BF16)

LANES = 128
SUBLANES = 8
FFT_N2 = 128
VMEM_LIMIT = 56 * 1024 * 1024


def _cparams(sem):
    return pltpu.CompilerParams(dimension_semantics=sem, vmem_limit_bytes=VMEM_LIMIT)


def _dot(a, b):
    return jnp.dot(a, b, preferred_element_type=F32)


def _dot_nt(a, b):
    return lax.dot_general(a, b, (((1,), (1,)), ((), ())), preferred_element_type=F32)


def _dot_tn(a, b):
    return lax.dot_general(a, b, (((0,), (0,)), ((), ())), preferred_element_type=F32)


def _split2(a):
    hi = a.astype(BF16)
    lo = (a - hi.astype(F32)).astype(BF16)
    return hi, lo


def _dot3(a, b):
    ah, al = _split2(a)
    bh, bl = _split2(b)
    return _dot(ah, bh) + _dot(al, bh) + _dot(ah, bl)


def _layer_norm(y, g, b):
    mu = jnp.mean(y, axis=-1, keepdims=True)
    yc = y - mu
    var = jnp.mean(yc * yc, axis=-1, keepdims=True)
    return yc * lax.rsqrt(var + LN_EPS) * g + b


def _sigmoid(x):
    return 1.0 / (1.0 + jnp.exp(-x))


def _silu(x):
    return x * _sigmoid(x)


def _row_tile(n, want):
    t = min(n, want)
    assert n % t == 0
    return t


def _in_proj_kernel(x_ref, w_ref, *out_refs):
    x = x_ref[...].astype(BF16)
    off = 0
    for o_ref, width in zip(out_refs, SEG_WIDTHS):
        o_ref[...] = _dot(x, w_ref[:, off:off + width]).astype(o_ref.dtype)
        off += width


def _in_proj(x2, w_bf16):
    T = x2.shape[0]
    tm = _row_tile(T, 512)
    return pl.pallas_call(
        _in_proj_kernel,
        grid=(T // tm,),
        in_specs=[pl.BlockSpec((tm, D_MODEL), lambda i: (i, 0)),
                  pl.BlockSpec((D_MODEL, IN_WIDTH), lambda i: (0, 0), pipeline_mode=pl.Buffered(1))],
        out_specs=[pl.BlockSpec((tm, w), lambda i: (i, 0)) for w in SEG_WIDTHS],
        out_shape=[jax.ShapeDtypeStruct((T, w), dt) for w, dt in zip(SEG_WIDTHS, SEG_DTYPES)],
        compiler_params=_cparams(("parallel",)),
        name="in_proj",
    )(x2, w_bf16)


def _rope_tables(L):
    rows = L // GRID_W
    row = jnp.broadcast_to(jnp.arange(rows, dtype=F32)[:, None], (rows, GRID_W)).reshape(-1)
    col = jnp.broadcast_to(jnp.arange(GRID_W, dtype=F32)[None, :], (rows, GRID_W)).reshape(-1)
    inv = jnp.power(ROPE_THETA, -2.0 * jnp.arange(ROPE_HALF, dtype=F32) / ROPE_AXIS_DIM)
    a0, a1 = row[:, None] * inv, col[:, None] * inv
    c0, s0, c1, s1 = jnp.cos(a0), jnp.sin(a0), jnp.cos(a1), jnp.sin(a1)
    cos_h = jnp.concatenate([c0, c0, c1, c1], axis=-1)
    sin_h = jnp.concatenate([-s0, s0, -s1, s1], axis=-1)
    return jnp.tile(cos_h, (1, ATTN_HEADS)), jnp.tile(sin_h, (1, ATTN_HEADS))


def _head_block_ones(width):
    idx = np.arange(width) // HEAD_DIM
    return jnp.asarray(idx[:, None] == idx[None, :], dtype=BF16)


def _norm_rope(x, gain, cos, sin, ones_bd):
    width = x.shape[-1]
    sq_hi, sq_lo = _split2(x * x)
    ms = (_dot(sq_hi, ones_bd) + _dot(sq_lo, ones_bd)) * (1.0 / HEAD_DIM)
    xn = x * lax.rsqrt(ms + RMS_EPS) * gain
    lane = lax.broadcasted_iota(jnp.int32, xn.shape, 1)
    first_half = (lane % ROPE_AXIS_DIM) < ROPE_HALF
    swapped = jnp.where(first_half, pltpu.roll(xn, width - ROPE_HALF, 1), pltpu.roll(xn, ROPE_HALF, 1))
    return xn * cos + swapped * sin


LOG2E = 1.4426950408889634
V_EXT = 2 * HEAD_DIM
FLASH_SKEW = 4


def _attn_prep_kernel(aq_ref, akv_ref, cos_ref, sin_ref, qg_ref, kg_ref, bdq_ref, bdk_ref,
                      q_out, kt_out, v_out):
    cos, sin = cos_ref[...], sin_ref[...]
    q = _norm_rope(aq_ref[...].astype(F32), qg_ref[...], cos, sin, bdq_ref[...])
    q_out[...] = (q * (LOG2E * HEAD_DIM ** -0.5)).astype(BF16)
    kv = akv_ref[...].astype(F32)
    k = _norm_rope(kv[:, :KV_WIDTH], kg_ref[...], cos[:, :KV_WIDTH], sin[:, :KV_WIDTH], bdk_ref[...])
    kt_out[0] = k.T.astype(BF16)
    ones = jnp.ones((kv.shape[0], HEAD_DIM), F32)
    v_ext = [kv[:, KV_WIDTH + h * HEAD_DIM:KV_WIDTH + (h + 1) * HEAD_DIM] if part == 0 else ones
             for h in range(ATTN_KV_HEADS) for part in range(2)]
    v_out[...] = jnp.concatenate(v_ext, axis=-1).astype(BF16)


def _attn_prep(aq, akv, cos_t, sin_t, q_gain, k_gain, B, L):
    T = aq.shape[0]
    tm = _row_tile(L, 512)
    nl = L // tm
    qg = jnp.tile(q_gain, ATTN_HEADS)[None, :]
    kg = jnp.tile(k_gain, ATTN_KV_HEADS)[None, :]
    const = lambda i: (0, 0)
    return pl.pallas_call(
        _attn_prep_kernel,
        grid=(T // tm,),
        in_specs=[pl.BlockSpec((tm, BRANCH_WIDTH), lambda i: (i, 0)),
                  pl.BlockSpec((tm, 2 * KV_WIDTH), lambda i: (i, 0)),
                  pl.BlockSpec((tm, BRANCH_WIDTH), lambda i: (i % nl, 0)),
                  pl.BlockSpec((tm, BRANCH_WIDTH), lambda i: (i % nl, 0)),
                  pl.BlockSpec((1, BRANCH_WIDTH), const),
                  pl.BlockSpec((1, KV_WIDTH), const),
                  pl.BlockSpec((BRANCH_WIDTH, BRANCH_WIDTH), const),
                  pl.BlockSpec((KV_WIDTH, KV_WIDTH), const)],
        out_specs=[pl.BlockSpec((tm, BRANCH_WIDTH), lambda i: (i, 0)),
                   pl.BlockSpec((1, KV_WIDTH, tm), lambda i: (i // nl, 0, i % nl)),
                   pl.BlockSpec((tm, ATTN_KV_HEADS * V_EXT), lambda i: (i, 0))],
        out_shape=[jax.ShapeDtypeStruct((T, BRANCH_WIDTH), BF16),
                   jax.ShapeDtypeStruct((B, KV_WIDTH, L), BF16),
                   jax.ShapeDtypeStruct((T, ATTN_KV_HEADS * V_EXT), BF16)],
        compiler_params=_cparams(("parallel",)),
        name="attn_prep",
    )(aq, akv, cos_t, sin_t, qg, kg, _head_block_ones(BRANCH_WIDTH), _head_block_ones(KV_WIDTH))


def _flash_kernel(q_ref, kt_ref, v_ref, o_ref, qs_ref, m_ref, acc_ref, *, tq, rs):
    kv = pl.program_id(2)

    @pl.when(kv == 0)
    def _():
        for h in range(ATTN_HEADS):
            g = h % ATTN_GROUP
            qs_ref[h // ATTN_GROUP, g * tq:(g + 1) * tq, :] = q_ref[0, :, h * HEAD_DIM:(h + 1) * HEAD_DIM]
        m_ref[...] = jnp.full(m_ref.shape, -jnp.inf, F32)
        acc_ref[...] = jnp.zeros(acc_ref.shape, F32)

    kts = [kt_ref[0, kh * HEAD_DIM:(kh + 1) * HEAD_DIM, :] for kh in range(ATTN_KV_HEADS)]
    vs = [v_ref[0, :, kh * V_EXT:(kh + 1) * V_EXT] for kh in range(ATTN_KV_HEADS)]
    tasks = [(kh, slice(sb * rs, (sb + 1) * rs))
             for sb in range(ATTN_GROUP * tq // rs) for kh in range(ATTN_KV_HEADS)]

    def scores(t):
        kh, rows = tasks[t]
        return _dot(qs_ref[kh, rows, :], kts[kh])

    pending = {t: scores(t) for t in range(min(FLASH_SKEW, len(tasks)))}
    for t, (kh, rows) in enumerate(tasks):
        s = pending.pop(t)
        if t + FLASH_SKEW < len(tasks):
            pending[t + FLASH_SKEW] = scores(t + FLASH_SKEW)
        m_prev = m_ref[kh, rows, :]
        m_new = jnp.maximum(m_prev, jnp.max(s, axis=-1, keepdims=True))
        p = jnp.exp2(s - m_new[:, 0:1])
        acc_ref[kh, rows, :] = jnp.exp2(m_prev - m_new) * acc_ref[kh, rows, :] + _dot(p.astype(BF16), vs[kh])
        m_ref[kh, rows, :] = m_new

    @pl.when(kv == pl.num_programs(2) - 1)
    def _():
        for h in range(ATTN_HEADS):
            kh, g = h // ATTN_GROUP, h % ATTN_GROUP
            a = acc_ref[kh, g * tq:(g + 1) * tq, :]
            o_ref[0, :, h * HEAD_DIM:(h + 1) * HEAD_DIM] = a[:, :HEAD_DIM] / a[:, HEAD_DIM:]


def _flash_attention(qn, kt, v_ext, B, L):
    tq = _row_tile(L, 256)
    tk = _row_tile(L, 1024)
    rs = min(128, ATTN_GROUP * tq)
    out = pl.pallas_call(
        functools.partial(_flash_kernel, tq=tq, rs=rs),
        grid=(B, L // tq, L // tk),
        in_specs=[pl.BlockSpec((1, tq, BRANCH_WIDTH), lambda b, i, j: (b, i, 0)),
                  pl.BlockSpec((1, KV_WIDTH, tk), lambda b, i, j: (b, 0, j)),
                  pl.BlockSpec((1, tk, ATTN_KV_HEADS * V_EXT), lambda b, i, j: (b, j, 0))],
        out_specs=pl.BlockSpec((1, tq, BRANCH_WIDTH), lambda b, i, j: (b, i, 0)),
        out_shape=jax.ShapeDtypeStruct((B, L, BRANCH_WIDTH), F32),
        scratch_shapes=[pltpu.VMEM((ATTN_KV_HEADS, ATTN_GROUP * tq, HEAD_DIM), BF16),
                        pltpu.VMEM((ATTN_KV_HEADS, ATTN_GROUP * tq, LANES), F32),
                        pltpu.VMEM((ATTN_KV_HEADS, ATTN_GROUP * tq, V_EXT), F32)],
        compiler_params=_cparams(("parallel", "parallel", "arbitrary")),
        name="flash_attention",
    )(qn.reshape(B, L, BRANCH_WIDTH), kt, v_ext.reshape(B, L, ATTN_KV_HEADS * V_EXT))
    return out.reshape(B * L, BRANCH_WIDTH)


def _ref_rows(b, h, fwd):
    C, W = b.shape
    two_h = 2 * h
    if two_h >= 8:
        n = C // two_h
        b3 = b.reshape(n, two_h, W)
        r = jnp.where(fwd, b3[:, h - 1:h, :], b3[:, h:h + 1, :])
        return jnp.broadcast_to(r, (n, two_h, W)).reshape(C, W)
    b3 = b.reshape(C // 8, 8, W)
    pos8 = lax.broadcasted_iota(jnp.int32, (1, 8, 1), 1)
    out = None
    for blk in range(8 // two_h):
        base = blk * two_h
        r = jnp.where(fwd, b3[:, base + h - 1:base + h, :], b3[:, base + h:base + h + 1, :])
        r = jnp.broadcast_to(r, b3.shape)
        out = r if out is None else jnp.where((pos8 // two_h) == blk, r, out)
    return out.reshape(C, W)


def _hgrn_chunk(q, z, v, lb, tri, st_ref, fwd):
    C, W = q.shape
    q = q.astype(F32)
    v = v.astype(F32)
    e = jnp.exp(-jnp.abs(z))
    s_big = 1.0 / (1.0 + e)
    s_small = e * s_big
    pos = z >= 0.0
    f = lb + (1.0 - lb) * jnp.where(pos, s_big, s_small)
    g = jnp.log2(jnp.maximum(f, FORGET_FLOOR))
    kk = (1.0 - lb) * jnp.where(pos, s_small, s_big)
    g_hi, g_lo = _split2(g)
    b = _dot(tri, g_hi) + _dot(tri, g_lo)
    b_end = jnp.where(fwd, b[C - 1:C, :], b[0:1, :])
    q_in = (q * jnp.exp2(b)).astype(BF16)
    k_out = (kk * jnp.exp2(b_end - b)).astype(BF16)
    s_decay = jnp.exp2(b_end)
    vb = v.astype(BF16)

    row = lax.broadcasted_iota(jnp.int32, (C, 1), 0)
    rr = lax.broadcasted_iota(jnp.int32, (C, C), 0)
    cc = lax.broadcasted_iota(jnp.int32, (C, C), 1)
    a_mats = [jnp.zeros((C, C), F32) for _ in range(HGRN_HEADS)]
    h = C // 2
    while h >= 1:
        upper = ((row % (2 * h)) >= h).astype(jnp.int32)
        q_rows = upper == fwd.astype(jnp.int32)
        ref = _ref_rows(b, h, fwd)
        ql = jnp.where(q_rows, q * jnp.exp2(b - ref), 0.0).astype(BF16)
        kl = jnp.where(q_rows, 0.0, kk * jnp.exp2(ref - b)).astype(BF16)
        level = (jnp.bitwise_xor(rr, cc) // h) == 1
        for hd in range(HGRN_HEADS):
            sl = slice(hd * HGRN_DIM, (hd + 1) * HGRN_DIM)
            a_mats[hd] = jnp.where(level, _dot_nt(ql[:, sl], kl[:, sl]), a_mats[hd])
        h //= 2

    qk = q * kk
    outs = []
    for hd in range(HGRN_HEADS):
        sl = slice(hd * HGRN_DIM, (hd + 1) * HGRN_DIM)
        st = st_ref[hd]
        o = _dot_nt(q_in[:, sl], st.astype(BF16))
        o = o + _dot(a_mats[hd].astype(BF16), vb[:, sl])
        o = o + jnp.sum(qk[:, sl], axis=-1, keepdims=True) * v[:, sl]
        outs.append(o)
        st_ref[hd] = st * s_decay[:, sl] + _dot_tn(vb[:, sl], k_out[:, sl])
    return jnp.concatenate(outs, axis=-1)


def _hgrn_kernel(q_ref, f_ref, v_ref, lb_ref, tri_ref, o_ref, st_ref, *, n_inner):
    d = pl.program_id(0)
    fwd = d == 0

    @pl.when(pl.program_id(2) == 0)
    def _():
        st_ref[...] = jnp.zeros(st_ref.shape, F32)

    lb = lb_ref[0]
    tri = tri_ref[0]

    def body(i, carry):
        ci = jnp.where(fwd, i, n_inner - 1 - i)
        r0 = pl.multiple_of(ci * HGRN_CHUNK, HGRN_CHUNK)
        rows = pl.ds(r0, HGRN_CHUNK)
        o_ref[0, 0, rows, :] = _hgrn_chunk(q_ref[0, rows, :], f_ref[0, rows, :], v_ref[0, rows, :],
                                           lb, tri, st_ref, fwd)
        return carry

    lax.fori_loop(0, n_inner, body, 0)


def _hgrn(hq, hf, hi, lb, B, L):
    W = BRANCH_WIDTH
    rb = _row_tile(L, 512)
    nb = L // rb
    C = HGRN_CHUNK
    tri_f = np.tril(np.ones((C, C), np.float32))
    tri = jnp.asarray(np.stack([tri_f, tri_f.T]), dtype=BF16)
    blk = lambda d, b, c: (b, c + d * (nb - 1 - 2 * c), 0)
    return pl.pallas_call(
        functools.partial(_hgrn_kernel, n_inner=rb // C),
        grid=(2, B, nb),
        in_specs=[pl.BlockSpec((1, rb, W), blk),
                  pl.BlockSpec((1, rb, W), lambda d, b, c: (b, c + d * (nb - 1 - 2 * c), d)),
                  pl.BlockSpec((1, rb, W), blk),
                  pl.BlockSpec((1, 1, W), lambda d, b, c: (d, 0, 0)),
                  pl.BlockSpec((1, C, C), lambda d, b, c: (d, 0, 0))],
        out_specs=pl.BlockSpec((1, 1, rb, W), lambda d, b, c: (d, b, c + d * (nb - 1 - 2 * c), 0)),
        out_shape=jax.ShapeDtypeStruct((2, B, L, W), F32),
        scratch_shapes=[pltpu.VMEM((HGRN_HEADS, HGRN_DIM, HGRN_DIM), F32)],
        compiler_params=_cparams(("parallel", "parallel", "arbitrary")),
        name="hgrn_scan",
    )(hq.reshape(B, L, W), hf.reshape(B, L, 2 * W), hi.reshape(B, L, W), lb[:, None, :], tri)


def _filter_features(L):
    t = jnp.linspace(0.0, 1.0, L, dtype=F32)[:, None]
    w = 2.0 * math.pi * jnp.arange(L, dtype=F32) / L
    bands = jnp.linspace(1e-4, FILTER_BANDS - 1, FILTER_BANDS, dtype=F32)
    ang = w[:, None] * bands[None, :]
    feats = jnp.concatenate([t, jnp.cos(ang), -jnp.sin(ang)], axis=-1)
    feats2 = jnp.concatenate([feats, feats[:1], jnp.flip(feats[1:], axis=0)], axis=0)
    t2 = jnp.concatenate([t, t[:1], jnp.flip(t[1:], axis=0)], axis=0)
    deltas = jnp.abs(jnp.linspace(math.log(FILTER_TARGET) / FILTER_FAST_DECAY,
                                  math.log(FILTER_TARGET) / FILTER_SLOW_DECAY, N_FILTER_CH, dtype=F32))
    return feats2, t2, deltas[None, :]


def _filter_kernel(feat_ref, t_ref, w1_ref, b1_ref, fr_ref, w2_ref, b2_ref, w3f_ref, w3b_ref,
                   df_ref, db_ref, k_ref, h_ref, *, L):
    @pl.when(pl.program_id(0) == 0)
    def _():
        freq = fr_ref[...]
        h1 = jnp.sin(freq * (_dot3(feat_ref[...], w1_ref[...]) + b1_ref[...]))
        h_ref[...] = jnp.sin(freq * (_dot3(h1, w2_ref[...]) + b2_ref[...]))

    h = h_ref[...]
    t = t_ref[...]
    top = _dot3(h[:L], w3f_ref[...]) * jnp.exp(-t[:L] * df_ref[...])
    bot = _dot3(h[L:], w3b_ref[...]) * jnp.exp(-t[L:] * db_ref[...])
    row = lax.broadcasted_iota(jnp.int32, (L, 1), 0)
    bot = jnp.where(row == 0, 0.0, bot)
    norm = jnp.sum(jnp.abs(top), axis=0, keepdims=True) + jnp.sum(jnp.abs(bot), axis=0, keepdims=True)
    k_ref[:L, :] = top / norm
    k_ref[L:, :] = bot / norm


def _hyena_filter(L, consts, w1, b1, freq, w2, b2, w3):
    feats2, t2, deltas = consts
    wb = 256
    nw = HYENA_WIDTH // wb
    full = lambda shape: pl.BlockSpec(shape, lambda c: (0,) * len(shape))
    fcol = lambda c: (0, (c // nw) * 2 * nw + c % nw)
    bcol = lambda c: (0, (c // nw) * 2 * nw + nw + c % nw)
    return pl.pallas_call(
        functools.partial(_filter_kernel, L=L),
        grid=(HYENA_ORDER * nw,),
        in_specs=[full((2 * L, FILTER_EMB)), full((2 * L, 1)),
                  full((FILTER_EMB, FILTER_HIDDEN)), full((1, FILTER_HIDDEN)), full((1, FILTER_HIDDEN)),
                  full((FILTER_HIDDEN, FILTER_HIDDEN)), full((1, FILTER_HIDDEN)),
                  pl.BlockSpec((FILTER_HIDDEN, wb), fcol), pl.BlockSpec((FILTER_HIDDEN, wb), bcol),
                  pl.BlockSpec((1, wb), fcol), pl.BlockSpec((1, wb), bcol)],
        out_specs=pl.BlockSpec((2 * L, wb), lambda c: (0, c)),
        out_shape=jax.ShapeDtypeStruct((2 * L, HYENA_ORDER * HYENA_WIDTH), F32),
        scratch_shapes=[pltpu.VMEM((2 * L, FILTER_HIDDEN), F32)],
        compiler_params=_cparams(("arbitrary",)),
        name="hyena_filter",
    )(feats2, t2, w1, b1[None, :], freq[None, :], w2, b2[None, :], w3, w3, deltas, deltas)


def _dft_consts(L):
    N = 2 * L
    N2 = FFT_N2
    N1 = N // N2
    k1 = np.arange(N1)
    ang_a = 2.0 * np.pi * np.outer(k1, np.arange(N1)) / N1
    fa_full = np.concatenate([np.cos(ang_a), -np.sin(ang_a)], axis=0)
    fa_half = fa_full[:, :N1 // 2]
    n2 = np.arange(N2)
    kk = k1[:, None, None] + N1 * np.arange(N2)[None, :, None]
    ang_b = 2.0 * np.pi * (kk * n2[None, None, :]) / N
    mr, mi = np.cos(ang_b), -np.sin(ang_b)
    m_fwd = np.concatenate([np.concatenate([mr, -mi], axis=2),
                            np.concatenate([mi, mr], axis=2)], axis=1)
    wr, wi = np.swapaxes(mr, 1, 2), -np.swapaxes(mi, 1, 2)
    m_inv = np.concatenate([np.concatenate([wr, -wi], axis=2),
                            np.concatenate([wi, wr], axis=2)], axis=1)
    ang_c = 2.0 * np.pi * np.outer(np.arange(N1 // 2), k1) / N1
    g_inv = np.concatenate([np.cos(ang_c), -np.sin(ang_c)], axis=1) / N

    def hi_lo(a):
        a = jnp.asarray(a, dtype=F32)
        hi = a.astype(BF16)
        return hi, (a - hi.astype(F32)).astype(BF16)

    eye = np.eye(SUBLANES)
    fa_tiles = jnp.asarray(np.kron(fa_half, eye), dtype=BF16)
    g_tiles = jnp.asarray(np.kron(g_inv, eye), dtype=BF16)
    return dict(N1=N1, N2=N2, fa_full=hi_lo(fa_full), fa_tiles=fa_tiles, g_tiles=g_tiles,
                m_fwd=hi_lo(m_fwd), m_inv=hi_lo(m_inv))


def _mm3(m_hi, m_lo, x):
    x_hi, x_lo = _split2(x)
    return _dot(m_hi, x_hi) + _dot(m_hi, x_lo) + _dot(m_lo, x_hi)


def _stage_a_kernel(z_ref, fh_ref, fl_ref, a_ref, *, exact):
    n1 = a_ref.shape[2]
    if exact:
        res = _mm3(fh_ref[...], fl_ref[...], z_ref[0])
    else:
        res = _dot(fh_ref[...], z_ref[0].astype(BF16))
    a_ref[0, 0] = res[:n1].astype(a_ref.dtype)
    a_ref[0, 1] = res[n1:].astype(a_ref.dtype)


def _fft_stage_a(z, fa, N1, exact):
    B, nr, cols = z.shape
    cb = min(cols, 4096)
    return pl.pallas_call(
        functools.partial(_stage_a_kernel, exact=exact),
        grid=(B, cols // cb),
        in_specs=[pl.BlockSpec((1, nr, cb), lambda b, j: (b, 0, j)),
                  pl.BlockSpec((2 * N1, nr), lambda b, j: (0, 0)),
                  pl.BlockSpec((2 * N1, nr), lambda b, j: (0, 0))],
        out_specs=pl.BlockSpec((1, 2, N1, cb), lambda b, j: (b, 0, 0, j)),
        out_shape=jax.ShapeDtypeStruct((B, 2, N1, cols), F32 if exact else BF16),
        compiler_params=_cparams(("parallel", "parallel")),
        name="fft_stage_a",
    )(z, fa[0], fa[1])


def _stage_b_filter_kernel(a_ref, mh_ref, ml_ref, o_ref):
    n2 = a_ref.shape[3]
    a = a_ref[0, :, 0].reshape(2 * n2, a_ref.shape[4])
    res = _mm3(mh_ref[0], ml_ref[0], a)
    o_ref[0, 0] = res[:n2]
    o_ref[1, 0] = res[n2:]


def _filter_spectrum(k2l, dc):
    N1, N2 = dc["N1"], dc["N2"]
    OW = k2l.shape[1]
    a = _fft_stage_a(k2l.reshape(1, N1, N2 * OW), dc["fa_full"], N1, True)
    a = a.reshape(1, 2, N1, N2, OW)
    wb = 512
    return pl.pallas_call(
        _stage_b_filter_kernel,
        grid=(N1, OW // wb),
        in_specs=[pl.BlockSpec((1, 2, 1, N2, wb), lambda k, j: (0, 0, k, 0, j)),
                  pl.BlockSpec((1, 2 * N2, 2 * N2), lambda k, j: (k, 0, 0)),
                  pl.BlockSpec((1, 2 * N2, 2 * N2), lambda k, j: (k, 0, 0))],
        out_specs=pl.BlockSpec((2, 1, N2, wb), lambda k, j: (0, k, 0, j)),
        out_shape=jax.ShapeDtypeStruct((2, N1, N2, OW), F32),
        compiler_params=_cparams(("parallel", "parallel")),
        name="fft_filter_spectrum",
    )(a, dc["m_fwd"][0], dc["m_fwd"][1])


FFT_SPECTRUM_BYTES = 8 * 1024 * 1024


def _long_conv_kernel(z_ref, gate_ref, h_ref, fa_ref, mf_ref, mi_ref, g_ref, bias_ref, o_ref, a_ref):
    n1, n2, wq = a_ref.shape[1], a_ref.shape[2], a_ref.shape[3]
    nr = z_ref.shape[1]
    s8 = SUBLANES
    fa = fa_ref[...]
    for jt in range(n2 // s8):
        tile = slice(jt * s8, (jt + 1) * s8)
        zt = z_ref[0, :, tile, :].reshape(nr * s8, wq).astype(BF16)
        res = _dot(fa, zt)
        a_ref[0, :, tile, :] = res[:n1 * s8].reshape(n1, s8, wq)
        a_ref[1, :, tile, :] = res[n1 * s8:].reshape(n1, s8, wq)

    def per_k1(k, carry):
        a = jnp.concatenate([a_ref[0, k], a_ref[1, k]], axis=0).astype(BF16)
        zz = _dot(mf_ref[k], a)
        zr, zi = zz[:n2], zz[n2:]
        hr, hi = h_ref[0, k], h_ref[1, k]
        y = jnp.concatenate([zr * hr - zi * hi, zr * hi + zi * hr], axis=0).astype(BF16)
        c = _dot(mi_ref[k], y)
        a_ref[0, k] = c[:n2]
        a_ref[1, k] = c[n2:]
        return carry

    lax.fori_loop(0, n1, per_k1, 0, unroll=8)
    g = g_ref[...]
    bias = bias_ref[...]
    for jt in range(n2 // s8):
        tile = slice(jt * s8, (jt + 1) * s8)
        c = jnp.concatenate([a_ref[0, :, tile, :].reshape(n1 * s8, wq),
                             a_ref[1, :, tile, :].reshape(n1 * s8, wq)], axis=0).astype(BF16)
        y = _dot(g, c).reshape(nr, s8, wq)
        o_ref[0, :, tile, :] = gate_ref[0, :, tile, :] * (y + z_ref[0, :, tile, :] * bias)


def _long_conv_gated(z, gate, hf, order, skip, dc):
    B, nr, N2, W = z.shape
    N1 = dc["N1"]
    wq = min(W, max(LANES, FFT_SPECTRUM_BYTES // (2 * N1 * N2 * 4) // LANES * LANES))
    nw = W // wq
    rows = pl.BlockSpec((1, nr, N2, wq), lambda w, b: (b, 0, 0, w))
    once = pl.Buffered(1)
    mats = pl.BlockSpec((N1, 2 * N2, 2 * N2), lambda w, b: (0, 0, 0), pipeline_mode=once)
    return pl.pallas_call(
        _long_conv_kernel,
        grid=(nw, B),
        in_specs=[rows, rows,
                  pl.BlockSpec((2, N1, N2, wq), lambda w, b: (0, 0, 0, order * nw + w), pipeline_mode=once),
                  pl.BlockSpec((2 * N1 * SUBLANES, nr * SUBLANES), lambda w, b: (0, 0)),
                  mats, mats,
                  pl.BlockSpec((nr * SUBLANES, 2 * N1 * SUBLANES), lambda w, b: (0, 0)),
                  pl.BlockSpec((1, wq), lambda w, b: (0, w))],
        out_specs=rows,
        out_shape=jax.ShapeDtypeStruct((B, nr, N2, W), F32),
        scratch_shapes=[pltpu.VMEM((2, N1, N2, wq), F32)],
        compiler_params=_cparams(("parallel", "parallel")),
        name="hyena_long_conv",
    )(z, gate, hf, dc["fa_tiles"], dc["m_fwd"][0], dc["m_inv"][0], dc["g_tiles"], skip[None, :])


HALO_ROWS = 16


def _short_conv_kernel(u_ref, prev_ref, next_ref, w_ref, b_ref, v_ref, x1_ref, x2_ref):
    i = pl.program_id(1)
    u = u_ref[0].astype(F32)
    tl = u.shape[0]
    row = lax.broadcasted_iota(jnp.int32, (tl, 1), 0)
    prev_last = prev_ref[0].astype(F32)[HALO_ROWS - 1:HALO_ROWS, :]
    next_first = next_ref[0].astype(F32)[0:1, :]
    before = jnp.where(i == 0, 0.0, prev_last)
    after = jnp.where(i == pl.num_programs(1) - 1, 0.0, next_first)
    up = jnp.where(row == 0, before, pltpu.roll(u, 1, 0))
    un = jnp.where(row == tl - 1, after, pltpu.roll(u, tl - 1, 0))
    w = w_ref[...]
    uc = up * w[0:1] + u * w[1:2] + un * w[2:3] + b_ref[...]
    W = HYENA_WIDTH
    v_ref[0] = uc[:, :W]
    x1_ref[0] = uc[:, W:2 * W]
    x2_ref[0] = uc[:, 2 * W:]


def _short_conv(hy, conv_w, conv_b, B, L):
    W3 = 3 * HYENA_WIDTH
    tl = _row_tile(L, 512)
    nl = L // tl
    u = hy.reshape(B, L, W3)
    rh = tl // HALO_ROWS
    spec_o = pl.BlockSpec((1, tl, HYENA_WIDTH), lambda b, i: (b, i, 0))
    return pl.pallas_call(
        _short_conv_kernel,
        grid=(B, nl),
        in_specs=[pl.BlockSpec((1, tl, W3), lambda b, i: (b, i, 0)),
                  pl.BlockSpec((1, HALO_ROWS, W3), lambda b, i: (b, jnp.maximum(i * rh - 1, 0), 0)),
                  pl.BlockSpec((1, HALO_ROWS, W3),
                               lambda b, i: (b, jnp.minimum((i + 1) * rh, L // HALO_ROWS - 1), 0)),
                  pl.BlockSpec((3, W3), lambda b, i: (0, 0)),
                  pl.BlockSpec((1, W3), lambda b, i: (0, 0))],
        out_specs=[spec_o, spec_o, spec_o],
        out_shape=[jax.ShapeDtypeStruct((B, L, HYENA_WIDTH), F32)] * 3,
        compiler_params=_cparams(("parallel", "parallel")),
        name="hyena_short_conv",
    )(u, u, u, conv_w, conv_b[None, :])


def _hyena(hy, hf, conv_w, conv_b, skip, dc, B, L):
    v, x1, x2 = _short_conv(hy, conv_w, conv_b, B, L)
    shape = (B, dc["N1"] // 2, dc["N2"], HYENA_WIDTH)
    z = _long_conv_gated(v.reshape(shape), x1.reshape(shape), hf, 0, skip[0], dc)
    y = _long_conv_gated(z, x2.reshape(shape), hf, 1, skip[1], dc)
    return y.reshape(B * L, HYENA_WIDTH)


def _merge_kernel(x_ref, oa_ref, oh_ref, hg_ref, oc_ref, gl_ref, hgn_ref, wb_ref, wo_ref, g_ref, b_ref,
                  out_ref):
    o = oh_ref[0] + oh_ref[1]
    parts = []
    for hd in range(HGRN_HEADS):
        oh = o[:, hd * HGRN_DIM:(hd + 1) * HGRN_DIM]
        ms = jnp.mean(oh * oh, axis=-1, keepdims=True)
        parts.append(oh * lax.rsqrt(ms + RMS_EPS))
    o_h = jnp.concatenate(parts, axis=-1) * hgn_ref[...] * _silu(hg_ref[...].astype(F32))
    gl = gl_ref[...].astype(F32)
    branches = (oa_ref[...], o_h, oc_ref[...])
    merged = None
    for j, ob in enumerate(branches):
        term = _sigmoid(gl[:, j * D_MODEL:(j + 1) * D_MODEL]) * _dot(ob.astype(BF16), wb_ref[j])
        merged = term if merged is None else merged + term
    mix = _dot(merged.astype(BF16), wo_ref[...])
    out_ref[...] = _layer_norm(ALPHA * x_ref[...] + mix, g_ref[...], b_ref[...])


def _merge(x2, o_a, o_dir, hg, o_c, gl, hgrn_gain, wb_bf16, wo_bf16, ln_g, ln_b):
    T = x2.shape[0]
    tm = _row_tile(T, 512)
    W = BRANCH_WIDTH
    row = lambda w: pl.BlockSpec((tm, w), lambda i: (i, 0))
    return pl.pallas_call(
        _merge_kernel,
        grid=(T // tm,),
        in_specs=[row(D_MODEL), row(W), pl.BlockSpec((2, tm, W), lambda i: (0, i, 0)), row(W), row(W),
                  row(N_BRANCH * D_MODEL),
                  pl.BlockSpec((1, W), lambda i: (0, 0)),
                  pl.BlockSpec((N_BRANCH, W, D_MODEL), lambda i: (0, 0, 0)),
                  pl.BlockSpec((D_MODEL, D_MODEL), lambda i: (0, 0)),
                  pl.BlockSpec((1, D_MODEL), lambda i: (0, 0)),
                  pl.BlockSpec((1, D_MODEL), lambda i: (0, 0))],
        out_specs=row(D_MODEL),
        out_shape=jax.ShapeDtypeStruct((T, D_MODEL), F32),
        compiler_params=_cparams(("parallel",)),
        name="merge_out_ln",
    )(x2, o_a, o_dir.reshape(2, T, W), hg, o_c, gl, jnp.tile(hgrn_gain, HGRN_HEADS)[None, :],
      wb_bf16, wo_bf16, ln_g[None, :], ln_b[None, :])


FFN_TILE = 1408
MOE_FF_TILE = 896


def _gate_up_tiles(w_gu, ft):
    *lead, d, f2 = w_gu.shape
    nf = f2 // 2 // ft
    w = w_gu.reshape(*lead, d, 2, nf, ft)
    w = jnp.moveaxis(w, -2, -4)
    return w.reshape(*lead, nf, d, 2 * ft).astype(BF16)


def _ffn_kernel(x_ref, wgu_ref, wd_ref, g_ref, b_ref, out_ref, acc_ref):
    j = pl.program_id(1)
    x = x_ref[...]
    xb = x.astype(BF16)
    ft = wd_ref.shape[0]
    hmid = _silu(_dot(xb, wgu_ref[0, :, :ft])) * _dot(xb, wgu_ref[0, :, ft:])
    contrib = _dot(hmid.astype(BF16), wd_ref[...])

    @pl.when(j == 0)
    def _():
        acc_ref[...] = contrib

    @pl.when(j > 0)
    def _():
        acc_ref[...] += contrib

    @pl.when(j == pl.num_programs(1) - 1)
    def _():
        out_ref[...] = _layer_norm(ALPHA * x + acc_ref[...], g_ref[...], b_ref[...])


def _ffn_dense(x2, w_gu, w_down, ln_g, ln_b):
    T = x2.shape[0]
    tm = _row_tile(T, 512)
    nf, _, ft2 = w_gu.shape
    ft = ft2 // 2
    return pl.pallas_call(
        _ffn_kernel,
        grid=(T // tm, nf),
        in_specs=[pl.BlockSpec((tm, D_MODEL), lambda i, j: (i, 0)),
                  pl.BlockSpec((1, D_MODEL, 2 * ft), lambda i, j: (j, 0, 0)),
                  pl.BlockSpec((ft, D_MODEL), lambda i, j: (j, 0)),
                  pl.BlockSpec((1, D_MODEL), lambda i, j: (0, 0)),
                  pl.BlockSpec((1, D_MODEL), lambda i, j: (0, 0))],
        out_specs=pl.BlockSpec((tm, D_MODEL), lambda i, j: (i, 0)),
        out_shape=jax.ShapeDtypeStruct((T, D_MODEL), F32),
        scratch_shapes=[pltpu.VMEM((tm, D_MODEL), F32)],
        compiler_params=_cparams(("parallel", "arbitrary")),
        name="ffn_dense_ln",
    )(x2, w_gu, w_down, ln_g[None, :], ln_b[None, :])


MOE_BLOCK = 1024
MOE_SUB = 64
MOE_BIG = 8
MOE_TILE = 256


def _route_kernel(x_ref, r_ref, col_ref, row_ref, s_ref):
    tb = x_ref.shape[0]
    logits = _dot3(x_ref[...], r_ref[...])
    lane = lax.broadcasted_iota(jnp.int32, logits.shape, 1)
    logits = jnp.where(lane < N_EXPERTS, logits, -jnp.inf)
    m1 = jnp.max(logits, axis=-1, keepdims=True)
    i1 = jnp.min(jnp.where(logits == m1, lane, LANES), axis=-1, keepdims=True)
    rest = jnp.where(lane == i1, -jnp.inf, logits)
    m2 = jnp.max(rest, axis=-1, keepdims=True)
    i2 = jnp.min(jnp.where(rest == m2, lane, LANES), axis=-1, keepdims=True)
    e2 = jnp.exp(m2 - m1)
    w1 = 1.0 / (1.0 + e2)
    w2 = e2 / (1.0 + e2)
    oh1 = lane == i1
    oh2 = lane == i2
    sel = jnp.where(oh1, 1.0, jnp.where(oh2, 1.0, 0.0))
    rr = lax.broadcasted_iota(jnp.int32, (tb, tb), 0)
    cc = lax.broadcasted_iota(jnp.int32, (tb, tb), 1)
    earlier = jnp.where(rr > cc, 1.0, 0.0).astype(BF16)
    rank = _dot(earlier, sel.astype(BF16))
    cnt = jnp.sum(sel, axis=0, keepdims=True)
    nsub = jnp.floor((cnt + (MOE_SUB - 1)) * (1.0 / MOE_SUB))
    r2 = lax.broadcasted_iota(jnp.int32, (LANES, LANES), 0)
    c2 = lax.broadcasted_iota(jnp.int32, (LANES, LANES), 1)
    before = jnp.where(r2 < c2, 1.0, 0.0).astype(BF16)
    nsub8 = jnp.broadcast_to(nsub, (8, LANES))
    off8 = _dot(nsub8.astype(BF16), before) * MOE_SUB
    base = off8[0:1] + rank
    d1 = jnp.sum(jnp.where(oh1, base, 0.0), axis=-1, keepdims=True)
    d2 = jnp.sum(jnp.where(oh2, base, 0.0), axis=-1, keepdims=True)
    col = jnp.where(lane == 0, d1, jnp.where(lane == 1, d2, jnp.where(lane == 2, w1, jnp.where(lane == 3, w2, 0.0))))
    col_ref[...] = col
    row_ref[0] = col.T[0:8, :]
    lane8 = lax.broadcasted_iota(jnp.int32, (8, LANES), 1)
    info = jnp.where(lane8 < N_EXPERTS, off8, pltpu.roll(nsub8, N_EXPERTS, 1))
    s_ref[0] = info.astype(jnp.int32)


def _moe_route(x2, router):
    T = x2.shape[0]
    tb = _row_tile(T, MOE_BLOCK)
    nb = T // tb
    r_pad = jnp.pad(router, ((0, 0), (0, LANES - N_EXPERTS)))
    return pl.pallas_call(
        _route_kernel,
        grid=(nb,),
        in_specs=[pl.BlockSpec((tb, D_MODEL), lambda i: (i, 0)),
                  pl.BlockSpec((D_MODEL, LANES), lambda i: (0, 0))],
        out_specs=[pl.BlockSpec((tb, LANES), lambda i: (i, 0)),
                   pl.BlockSpec((1, 8, tb), lambda i: (i, 0, 0)),
                   pl.BlockSpec((1, 8, LANES), lambda i: (i, 0, 0))],
        out_shape=[jax.ShapeDtypeStruct((T, LANES), F32),
                   jax.ShapeDtypeStruct((nb, 8, tb), F32),
                   jax.ShapeDtypeStruct((nb, 8, LANES), jnp.int32)],
        compiler_params=_cparams(("parallel",)),
        name="moe_route",
    )(x2, r_pad)


def _moe_kernel(s_ref, x_ref, col_ref, row_ref, wgu_ref, wd_ref, g_ref, b_ref, out_ref,
                xs_ref, ys_ref):
    blk = pl.program_id(0)
    e = pl.program_id(1)
    j = pl.program_id(2)
    tb = x_ref.shape[0]
    n_tiles = xs_ref.shape[0] // MOE_TILE
    used = s_ref[blk, N_EXPERTS - 1] + s_ref[blk, 2 * N_EXPERTS - 1] * MOE_SUB
    d1r = row_ref[0, 0:1, :]
    d2r = row_ref[0, 1:2, :]

    @pl.when(jnp.logical_and(e == 0, j == 0))
    def _():
        xb = x_ref[...].astype(BF16)
        for rt in range(n_tiles):
            @pl.when(rt * MOE_TILE < used)
            def _():
                r = (lax.broadcasted_iota(jnp.int32, (MOE_TILE, tb), 0) + rt * MOE_TILE).astype(F32)
                p = jnp.where(r == d1r, 1.0, jnp.where(r == d2r, 1.0, 0.0)).astype(BF16)
                xs_ref[rt * MOE_TILE:(rt + 1) * MOE_TILE, :] = _dot(p, xb).astype(BF16)

    start = s_ref[blk, e]
    nsub = s_ref[blk, N_EXPERTS + e]

    def expert_rows(first, n_rows):
        rows = pl.ds(pl.multiple_of(first, MOE_SUB), n_rows)
        xt = xs_ref[rows, :]
        ft = wd_ref.shape[1]
        hmid = _silu(_dot(xt, wgu_ref[0, 0, :, :ft])) * _dot(xt, wgu_ref[0, 0, :, ft:])
        y = _dot(hmid.astype(BF16), wd_ref[0])

        @pl.when(j == 0)
        def _():
            ys_ref[rows, :] = y

        @pl.when(j > 0)
        def _():
            ys_ref[rows, :] += y

    n_big = nsub // MOE_BIG
    rem = nsub - MOE_BIG * n_big

    def big(i, carry):
        expert_rows(start + i * (MOE_BIG * MOE_SUB), MOE_BIG * MOE_SUB)
        return carry

    lax.fori_loop(0, n_big, big, 0)
    done = n_big * MOE_BIG
    part = MOE_BIG // 2
    while part >= 1:
        @pl.when((rem // part) % 2 == 1)
        def _(done=done, part=part):
            expert_rows(start + (done + (rem // (2 * part)) * (2 * part)) * MOE_SUB, part * MOE_SUB)
        part //= 2

    @pl.when(jnp.logical_and(e == pl.num_programs(1) - 1, j == pl.num_programs(2) - 1))
    def _():
        sub_per_tile = MOE_TILE // MOE_SUB
        filled = (used // MOE_SUB) % sub_per_tile
        for k in range(1, sub_per_tile):
            @pl.when(jnp.logical_and(filled > 0, filled <= k))
            def _(k=k):
                first = (used // MOE_TILE) * MOE_TILE + k * MOE_SUB
                ys_ref[pl.ds(pl.multiple_of(first, MOE_SUB), MOE_SUB), :] = jnp.zeros((MOE_SUB, D_MODEL), F32)

        w1r = row_ref[0, 2:3, :]
        w2r = row_ref[0, 3:4, :]
        for rt in range(n_tiles):
            @pl.when(rt * MOE_TILE < used)
            def _():
                r = (lax.broadcasted_iota(jnp.int32, (MOE_TILE, tb), 0) + rt * MOE_TILE).astype(F32)
                w_rows = jnp.sum(jnp.where(r == d1r, w1r, 0.0) + jnp.where(r == d2r, w2r, 0.0),
                                 axis=-1, keepdims=True)
                rows = slice(rt * MOE_TILE, (rt + 1) * MOE_TILE)
                xs_ref[rows, :] = (ys_ref[rows, :] * w_rows).astype(BF16)

        d1c = col_ref[:, 0:1]
        d2c = col_ref[:, 1:2]
        out_ref[...] = jnp.zeros(out_ref.shape, F32)
        for rt in range(n_tiles):
            @pl.when(rt * MOE_TILE < used)
            def _():
                c = (lax.broadcasted_iota(jnp.int32, (tb, MOE_TILE), 1) + rt * MOE_TILE).astype(F32)
                q = jnp.where(c == d1c, 1.0, jnp.where(c == d2c, 1.0, 0.0)).astype(BF16)
                out_ref[...] += _dot(q, xs_ref[rt * MOE_TILE:(rt + 1) * MOE_TILE, :])

        out_ref[...] = _layer_norm(ALPHA * x_ref[...] + out_ref[...], g_ref[...], b_ref[...])


def _ffn_moe(x2, router, w_gu, w_down, ln_g, ln_b):
    T = x2.shape[0]
    tb = _row_tile(T, MOE_BLOCK)
    nb = T // tb
    n_rows = 2 * tb + N_EXPERTS * MOE_SUB
    col, row, sinfo = _moe_route(x2, router)
    _, nf, _, ft2 = w_gu.shape
    ft = ft2 // 2
    grid_spec = pltpu.PrefetchScalarGridSpec(
        num_scalar_prefetch=1,
        grid=(nb, N_EXPERTS, nf),
        in_specs=[pl.BlockSpec((tb, D_MODEL), lambda i, e, j, s: (i, 0)),
                  pl.BlockSpec((tb, LANES), lambda i, e, j, s: (i, 0)),
                  pl.BlockSpec((1, 8, tb), lambda i, e, j, s: (i, 0, 0)),
                  pl.BlockSpec((1, 1, D_MODEL, 2 * ft), lambda i, e, j, s: (e, j, 0, 0)),
                  pl.BlockSpec((1, ft, D_MODEL), lambda i, e, j, s: (e, j, 0)),
                  pl.BlockSpec((1, D_MODEL), lambda i, e, j, s: (0, 0)),
                  pl.BlockSpec((1, D_MODEL), lambda i, e, j, s: (0, 0))],
        out_specs=pl.BlockSpec((tb, D_MODEL), lambda i, e, j, s: (i, 0)),
        scratch_shapes=[pltpu.VMEM((n_rows, D_MODEL), BF16), pltpu.VMEM((n_rows, D_MODEL), F32)])
    return pl.pallas_call(
        _moe_kernel,
        grid_spec=grid_spec,
        out_shape=jax.ShapeDtypeStruct((T, D_MODEL), F32),
        compiler_params=_cparams(("parallel", "arbitrary", "arbitrary")),
        name="ffn_moe_ln",
    )(sinfo[:, 0, :], x2, col, row, w_gu, w_down, ln_g[None, :], ln_b[None, :])


def _trunk(x, p):
    B, L, _ = x.shape
    cos_t, sin_t = _rope_tables(L)
    filt_consts = _filter_features(L)
    dc = _dft_consts(L)
    x2 = x.reshape(B * L, D_MODEL)
    for l in range(DEPTH):
        aq, akv, hq, hf, hi, hg, hy, gl = _in_proj(x2, p["w_in"][l])
        qn, kn, vn = _attn_prep(aq, akv, cos_t, sin_t, p["q_norm_g"][l], p["k_norm_g"][l], B, L)
        o_a = _flash_attention(qn, kn, vn, B, L)
        o_dir = _hgrn(hq, hf, hi, p["lower_bounds"][l], B, L)
        k2l = _hyena_filter(L, filt_consts, p["filt_w1"][l], p["filt_b1"][l], p["filt_freq"][l],
                            p["filt_w2"][l], p["filt_b2"][l], p["filt_w3"][l])
        spec = _filter_spectrum(k2l, dc)
        o_c = _hyena(hy, spec, p["conv_w"][l], p["conv_b"][l], p["hyena_skip"][l], dc, B, L)
        x2 = _merge(x2, o_a, o_dir, hg, o_c, gl, p["hgrn_norm_g"][l], p["w_branch"][l], p["w_out"][l],
                    p["ln1_g"][l], p["ln1_b"][l])
        if l % 2 == 0:
            x2 = _ffn_dense(x2, p["ffn_w_gu"][l // 2], p["ffn_w_down"][l // 2], p["ln2_g"][l], p["ln2_b"][l])
        else:
            x2 = _ffn_moe(x2, p["router_w"][l // 2], p["expert_w_gu"][l // 2], p["expert_w_down"][l // 2],
                          p["ln2_g"][l], p["ln2_b"][l])
    return x2.reshape(B, L, D_MODEL)


def _in_proj_weight(w_in):
    return w_in.astype(BF16)


def kernel(x_prompt, x_sample, w_in, q_norm_g, k_norm_g, hgrn_lb, hgrn_norm_g, conv_w, conv_b, filt_w1, filt_b1, filt_freq, filt_w2, filt_b2, filt_w3, hyena_skip, w_branch, w_out, ln1_g, ln1_b, ln2_g, ln2_b, ffn_w_gu, ffn_w_down, router_w, expert_w_gu, expert_w_down):
    s = jax.nn.softmax(hgrn_lb.astype(F32), axis=0)
    p = dict(
        w_in=_in_proj_weight(w_in), q_norm_g=q_norm_g, k_norm_g=k_norm_g,
        lower_bounds=jnp.cumsum(s, axis=0) - s[:1], hgrn_norm_g=hgrn_norm_g,
        conv_w=conv_w, conv_b=conv_b, filt_w1=filt_w1, filt_b1=filt_b1, filt_freq=filt_freq,
        filt_w2=filt_w2, filt_b2=filt_b2, filt_w3=filt_w3, hyena_skip=hyena_skip,
        w_branch=w_branch.astype(BF16), w_out=w_out.astype(BF16),
        ln1_g=ln1_g, ln1_b=ln1_b, ln2_g=ln2_g, ln2_b=ln2_b,
        ffn_w_gu=_gate_up_tiles(ffn_w_gu, FFN_TILE), ffn_w_down=ffn_w_down.astype(BF16), router_w=router_w,
        expert_w_gu=_gate_up_tiles(expert_w_gu, MOE_FF_TILE), expert_w_down=expert_w_down.astype(BF16))
    return (_trunk(x_prompt, p), _trunk(x_sample, p))
```

```python
import functools
import math

import numpy as np
import jax
import jax.numpy as jnp
from jax import lax
from jax.experimental import pallas as pl
from jax.experimental.pallas import tpu as pltpu

F32 = jnp.float32
BF16 = jnp.bfloat16

D_MODEL = 1024
DEPTH = 2
GRID_W = 64
BRANCH_WIDTH = 512
ATTN_HEADS = 8
ATTN_KV_HEADS = 2
ATTN_GROUP = ATTN_HEADS // ATTN_KV_HEADS
HEAD_DIM = 64
KV_WIDTH = ATTN_KV_HEADS * HEAD_DIM
ROPE_AXIS_DIM = HEAD_DIM // 2
ROPE_HALF = ROPE_AXIS_DIM // 2
ROPE_THETA = 10000.0
HGRN_HEADS = 4
HGRN_DIM = 128
HGRN_CHUNK = 64
FORGET_FLOOR = 1e-30
HYENA_WIDTH = BRANCH_WIDTH
HYENA_ORDER = 2
FILTER_EMB = 33
FILTER_BANDS = (FILTER_EMB - 1) // 2
FILTER_HIDDEN = 64
FILTER_FAST_DECAY = 0.3
FILTER_SLOW_DECAY = 1.5
FILTER_TARGET = 1e-2
N_FILTER_CH = HYENA_ORDER * 2 * HYENA_WIDTH
N_BRANCH = 3
D_FF_DENSE = 2816
N_EXPERTS = 8
D_FF_EXPERT = 3584
ALPHA = (2 * DEPTH) ** 0.25
LN_EPS = 1e-5
RMS_EPS = 1e-6

SEG_WIDTHS = (BRANCH_WIDTH,
              2 * KV_WIDTH,
              BRANCH_WIDTH,
              2 * BRANCH_WIDTH,
              BRANCH_WIDTH,
              BRANCH_WIDTH,
              3 * HYENA_WIDTH,
              N_BRANCH * D_MODEL)
IN_WIDTH = sum(SEG_WIDTHS)
SEG_DTYPES = (BF16, BF16, BF16, F32, BF16, BF16, BF16, BF16)

LANES = 128
SUBLANES = 8
FFT_N2 = 128
VMEM_LIMIT = 56 * 1024 * 1024


def _cparams(sem):
    return pltpu.CompilerParams(dimension_semantics=sem, vmem_limit_bytes=VMEM_LIMIT)


def _dot(a, b):
    return jnp.dot(a, b, preferred_element_type=F32)


def _dot_nt(a, b):
    return lax.dot_general(a, b, (((1,), (1,)), ((), ())), preferred_element_type=F32)


def _dot_tn(a, b):
    return lax.dot_general(a, b, (((0,), (0,)), ((), ())), preferred_element_type=F32)


def _split2(a):
    hi = a.astype(BF16)
    lo = (a - hi.astype(F32)).astype(BF16)
    return hi, lo


def _dot3(a, b):
    ah, al = _split2(a)
    bh, bl = _split2(b)
    return _dot(ah, bh) + _dot(al, bh) + _dot(ah, bl)


def _layer_norm(y, g, b):
    mu = jnp.mean(y, axis=-1, keepdims=True)
    yc = y - mu
    var = jnp.mean(yc * yc, axis=-1, keepdims=True)
    return yc * lax.rsqrt(var + LN_EPS) * g + b


def _sigmoid(x):
    return 1.0 / (1.0 + jnp.exp(-x))


def _silu(x):
    return x * _sigmoid(x)


def _row_tile(n, want):
    t = min(n, want)
    assert n % t == 0
    return t


def _in_proj_kernel(x_ref, w_ref, *out_refs):
    x = x_ref[...].astype(BF16)
    off = 0
    for o_ref, width in zip(out_refs, SEG_WIDTHS):
        o_ref[...] = _dot(x, w_ref[:, off:off + width]).astype(o_ref.dtype)
        off += width


def _in_proj(x2, w_bf16):
    T = x2.shape[0]
    tm = _row_tile(T, 512)
    return pl.pallas_call(
        _in_proj_kernel,
        grid=(T // tm,),
        in_specs=[pl.BlockSpec((tm, D_MODEL), lambda i: (i, 0)),
                  pl.BlockSpec((D_MODEL, IN_WIDTH), lambda i: (0, 0), pipeline_mode=pl.Buffered(1))],
        out_specs=[pl.BlockSpec((tm, w), lambda i: (i, 0)) for w in SEG_WIDTHS],
        out_shape=[jax.ShapeDtypeStruct((T, w), dt) for w, dt in zip(SEG_WIDTHS, SEG_DTYPES)],
        compiler_params=_cparams(("parallel",)),
        name="in_proj",
    )(x2, w_bf16)


def _rope_tables(L):
    rows = L // GRID_W
    row = jnp.broadcast_to(jnp.arange(rows, dtype=F32)[:, None], (rows, GRID_W)).reshape(-1)
    col = jnp.broadcast_to(jnp.arange(GRID_W, dtype=F32)[None, :], (rows, GRID_W)).reshape(-1)
    inv = jnp.power(ROPE_THETA, -2.0 * jnp.arange(ROPE_HALF, dtype=F32) / ROPE_AXIS_DIM)
    a0, a1 = row[:, None] * inv, col[:, None] * inv
    c0, s0, c1, s1 = jnp.cos(a0), jnp.sin(a0), jnp.cos(a1), jnp.sin(a1)
    cos_h = jnp.concatenate([c0, c0, c1, c1], axis=-1)
    sin_h = jnp.concatenate([-s0, s0, -s1, s1], axis=-1)
    return jnp.tile(cos_h, (1, ATTN_HEADS)), jnp.tile(sin_h, (1, ATTN_HEADS))


def _head_block_ones(width):
    idx = np.arange(width) // HEAD_DIM
    return jnp.asarray(idx[:, None] == idx[None, :], dtype=BF16)


def _norm_rope(x, gain, cos, sin, ones_bd):
    width = x.shape[-1]
    sq_hi, sq_lo = _split2(x * x)
    ms = (_dot(sq_hi, ones_bd) + _dot(sq_lo, ones_bd)) * (1.0 / HEAD_DIM)
    xn = x * lax.rsqrt(ms + RMS_EPS) * gain
    lane = lax.broadcasted_iota(jnp.int32, xn.shape, 1)
    first_half = (lane % ROPE_AXIS_DIM) < ROPE_HALF
    swapped = jnp.where(first_half, pltpu.roll(xn, width - ROPE_HALF, 1), pltpu.roll(xn, ROPE_HALF, 1))
    return xn * cos + swapped * sin


LOG2E = 1.4426950408889634
V_EXT = 2 * HEAD_DIM
FLASH_SKEW = 4


def _attn_prep_kernel(aq_ref, akv_ref, cos_ref, sin_ref, qg_ref, kg_ref, bdq_ref, bdk_ref,
                      q_out, kt_out, v_out):
    cos, sin = cos_ref[...], sin_ref[...]
    q = _norm_rope(aq_ref[...].astype(F32), qg_ref[...], cos, sin, bdq_ref[...])
    q_out[...] = (q * (LOG2E * HEAD_DIM ** -0.5)).astype(BF16)
    kv = akv_ref[...].astype(F32)
    k = _norm_rope(kv[:, :KV_WIDTH], kg_ref[...], cos[:, :KV_WIDTH], sin[:, :KV_WIDTH], bdk_ref[...])
    kt_out[0] = k.T.astype(BF16)
    ones = jnp.ones((kv.shape[0], HEAD_DIM), F32)
    v_ext = [kv[:, KV_WIDTH + h * HEAD_DIM:KV_WIDTH + (h + 1) * HEAD_DIM] if part == 0 else ones
             for h in range(ATTN_KV_HEADS) for part in range(2)]
    v_out[...] = jnp.concatenate(v_ext, axis=-1).astype(BF16)


def _attn_prep(aq, akv, cos_t, sin_t, q_gain, k_gain, B, L):
    T = aq.shape[0]
    tm = _row_tile(L, 512)
    nl = L // tm
    qg = jnp.tile(q_gain, ATTN_HEADS)[None, :]
    kg = jnp.tile(k_gain, ATTN_KV_HEADS)[None, :]
    const = lambda i: (0, 0)
    return pl.pallas_call(
        _attn_prep_kernel,
        grid=(T // tm,),
        in_specs=[pl.BlockSpec((tm, BRANCH_WIDTH), lambda i: (i, 0)),
                  pl.BlockSpec((tm, 2 * KV_WIDTH), lambda i: (i, 0)),
                  pl.BlockSpec((tm, BRANCH_WIDTH), lambda i: (i % nl, 0)),
                  pl.BlockSpec((tm, BRANCH_WIDTH), lambda i: (i % nl, 0)),
                  pl.BlockSpec((1, BRANCH_WIDTH), const),
                  pl.BlockSpec((1, KV_WIDTH), const),
                  pl.BlockSpec((BRANCH_WIDTH, BRANCH_WIDTH), const),
                  pl.BlockSpec((KV_WIDTH, KV_WIDTH), const)],
        out_specs=[pl.BlockSpec((tm, BRANCH_WIDTH), lambda i: (i, 0)),
                   pl.BlockSpec((1, KV_WIDTH, tm), lambda i: (i // nl, 0, i % nl)),
                   pl.BlockSpec((tm, ATTN_KV_HEADS * V_EXT), lambda i: (i, 0))],
        out_shape=[jax.ShapeDtypeStruct((T, BRANCH_WIDTH), BF16),
                   jax.ShapeDtypeStruct((B, KV_WIDTH, L), BF16),
                   jax.ShapeDtypeStruct((T, ATTN_KV_HEADS * V_EXT), BF16)],
        compiler_params=_cparams(("parallel",)),
        name="attn_prep",
    )(aq, akv, cos_t, sin_t, qg, kg, _head_block_ones(BRANCH_WIDTH), _head_block_ones(KV_WIDTH))


def _flash_kernel(q_ref, kt_ref, v_ref, o_ref, qs_ref, m_ref, acc_ref, *, tq, rs):
    kv = pl.program_id(2)

    @pl.when(kv == 0)
    def _():
        for h in range(ATTN_HEADS):
            g = h % ATTN_GROUP
            qs_ref[h // ATTN_GROUP, g * tq:(g + 1) * tq, :] = q_ref[0, :, h * HEAD_DIM:(h + 1) * HEAD_DIM]
        m_ref[...] = jnp.full(m_ref.shape, -jnp.inf, F32)
        acc_ref[...] = jnp.zeros(acc_ref.shape, F32)

    kts = [kt_ref[0, kh * HEAD_DIM:(kh + 1) * HEAD_DIM, :] for kh in range(ATTN_KV_HEADS)]
    vs = [v_ref[0, :, kh * V_EXT:(kh + 1) * V_EXT] for kh in range(ATTN_KV_HEADS)]
    tasks = [(kh, slice(sb * rs, (sb + 1) * rs))
             for sb in range(ATTN_GROUP * tq // rs) for kh in range(ATTN_KV_HEADS)]

    def scores(t):
        kh, rows = tasks[t]
        return _dot(qs_ref[kh, rows, :], kts[kh])

    pending = {t: scores(t) for t in range(min(FLASH_SKEW, len(tasks)))}
    for t, (kh, rows) in enumerate(tasks):
        s = pending.pop(t)
        if t + FLASH_SKEW < len(tasks):
            pending[t + FLASH_SKEW] = scores(t + FLASH_SKEW)
        m_prev = m_ref[kh, rows, :]
        m_new = jnp.maximum(m_prev, jnp.max(s, axis=-1, keepdims=True))
        p = jnp.exp2(s - m_new[:, 0:1])
        acc_ref[kh, rows, :] = jnp.exp2(m_prev - m_new) * acc_ref[kh, rows, :] + _dot(p.astype(BF16), vs[kh])
        m_ref[kh, rows, :] = m_new

    @pl.when(kv == pl.num_programs(2) - 1)
    def _():
        for h in range(ATTN_HEADS):
            kh, g = h // ATTN_GROUP, h % ATTN_GROUP
            a = acc_ref[kh, g * tq:(g + 1) * tq, :]
            o_ref[0, :, h * HEAD_DIM:(h + 1) * HEAD_DIM] = a[:, :HEAD_DIM] / a[:, HEAD_DIM:]


def _flash_attention(qn, kt, v_ext, B, L):
    tq = _row_tile(L, 256)
    tk = _row_tile(L, 1024)
    rs = min(128, ATTN_GROUP * tq)
    out = pl.pallas_call(
        functools.partial(_flash_kernel, tq=tq, rs=rs),
        grid=(B, L // tq, L // tk),
        in_specs=[pl.BlockSpec((1, tq, BRANCH_WIDTH), lambda b, i, j: (b, i, 0)),
                  pl.BlockSpec((1, KV_WIDTH, tk), lambda b, i, j: (b, 0, j)),
                  pl.BlockSpec((1, tk, ATTN_KV_HEADS * V_EXT), lambda b, i, j: (b, j, 0))],
        out_specs=pl.BlockSpec((1, tq, BRANCH_WIDTH), lambda b, i, j: (b, i, 0)),
        out_shape=jax.ShapeDtypeStruct((B, L, BRANCH_WIDTH), F32),
        scratch_shapes=[pltpu.VMEM((ATTN_KV_HEADS, ATTN_GROUP * tq, HEAD_DIM), BF16),
                        pltpu.VMEM((ATTN_KV_HEADS, ATTN_GROUP * tq, LANES), F32),
                        pltpu.VMEM((ATTN_KV_HEADS, ATTN_GROUP * tq, V_EXT), F32)],
        compiler_params=_cparams(("parallel", "parallel", "arbitrary")),
        name="flash_attention",
    )(qn.reshape(B, L, BRANCH_WIDTH), kt, v_ext.reshape(B, L, ATTN_KV_HEADS * V_EXT))
    return out.reshape(B * L, BRANCH_WIDTH)


def _ref_rows(b, h, fwd):
    C, W = b.shape
    two_h = 2 * h
    if two_h >= 8:
        n = C // two_h
        b3 = b.reshape(n, two_h, W)
        r = jnp.where(fwd, b3[:, h - 1:h, :], b3[:, h:h + 1, :])
        return jnp.broadcast_to(r, (n, two_h, W)).reshape(C, W)
    b3 = b.reshape(C // 8, 8, W)
    pos8 = lax.broadcasted_iota(jnp.int32, (1, 8, 1), 1)
    out = None
    for blk in range(8 // two_h):
        base = blk * two_h
        r = jnp.where(fwd, b3[:, base + h - 1:base + h, :], b3[:, base + h:base + h + 1, :])
        r = jnp.broadcast_to(r, b3.shape)
        out = r if out is None else jnp.where((pos8 // two_h) == blk, r, out)
    return out.reshape(C, W)


def _hgrn_chunk(q, z, v, lb, tri, st_ref, fwd, head0):
    C, W = q.shape
    q = q.astype(F32)
    v = v.astype(F32)
    e = jnp.exp(-jnp.abs(z))
    s_big = 1.0 / (1.0 + e)
    s_small = e * s_big
    pos = z >= 0.0
    f = lb + (1.0 - lb) * jnp.where(pos, s_big, s_small)
    g = jnp.log2(jnp.maximum(f, FORGET_FLOOR))
    kk = (1.0 - lb) * jnp.where(pos, s_small, s_big)
    g_hi, g_lo = _split2(g)
    b = _dot(tri, g_hi) + _dot(tri, g_lo)
    b_end = jnp.where(fwd, b[C - 1:C, :], b[0:1, :])
    q_in = (q * jnp.exp2(b)).astype(BF16)
    k_out = (kk * jnp.exp2(b_end - b)).astype(BF16)
    s_decay = jnp.exp2(b_end)
    vb = v.astype(BF16)

    row = lax.broadcasted_iota(jnp.int32, (C, 1), 0)
    rr = lax.broadcasted_iota(jnp.int32, (C, C), 0)
    cc = lax.broadcasted_iota(jnp.int32, (C, C), 1)
    n_heads = W // HGRN_DIM
    a_mats = [jnp.zeros((C, C), F32) for _ in range(n_heads)]
    h = C // 2
    while h >= 1:
        upper = ((row % (2 * h)) >= h).astype(jnp.int32)
        q_rows = upper == fwd.astype(jnp.int32)
        ref = _ref_rows(b, h, fwd)
        ql = jnp.where(q_rows, q * jnp.exp2(b - ref), 0.0).astype(BF16)
        kl = jnp.where(q_rows, 0.0, kk * jnp.exp2(ref - b)).astype(BF16)
        level = (jnp.bitwise_xor(rr, cc) // h) == 1
        for hd in range(n_heads):
            sl = slice(hd * HGRN_DIM, (hd + 1) * HGRN_DIM)
            a_mats[hd] = jnp.where(level, _dot_nt(ql[:, sl], kl[:, sl]), a_mats[hd])
        h //= 2

    qk = q * kk
    outs = []
    for hd in range(n_heads):
        sl = slice(hd * HGRN_DIM, (hd + 1) * HGRN_DIM)
        st = st_ref[head0 + hd]
        o = _dot_nt(q_in[:, sl], st.astype(BF16))
        o = o + _dot(a_mats[hd].astype(BF16), vb[:, sl])
        o = o + jnp.sum(qk[:, sl], axis=-1, keepdims=True) * v[:, sl]
        outs.append(o)
        st_ref[head0 + hd] = st * s_decay[:, sl] + _dot_tn(vb[:, sl], k_out[:, sl])
    return jnp.concatenate(outs, axis=-1)


HGRN_GROUP = 4


def _hgrn_kernel(q_ref, f_ref, v_ref, lb_ref, tri_ref, o_ref, st_ref, *, n_inner):
    d = pl.program_id(0)
    fwd = d == 0

    @pl.when(pl.program_id(2) == 0)
    def _():
        st_ref[...] = jnp.zeros(st_ref.shape, F32)

    lb = lb_ref[0]
    tri = tri_ref[0]

    def body(i, carry):
        ci = jnp.where(fwd, i, n_inner - 1 - i)
        r0 = pl.multiple_of(ci * HGRN_CHUNK, HGRN_CHUNK)
        rows = pl.ds(r0, HGRN_CHUNK)
        for grp in range(HGRN_HEADS // HGRN_GROUP):
            cols = slice(grp * HGRN_GROUP * HGRN_DIM, (grp + 1) * HGRN_GROUP * HGRN_DIM)
            o_ref[0, 0, rows, cols] = _hgrn_chunk(q_ref[0, rows, cols], f_ref[0, rows, cols],
                                                  v_ref[0, rows, cols], lb[:, cols], tri, st_ref, fwd,
                                                  grp * HGRN_GROUP)
        return carry

    lax.fori_loop(0, n_inner, body, 0, unroll=4)


def _hgrn(hq, hf, hi, lb, B, L):
    W = BRANCH_WIDTH
    rb = _row_tile(L, 512)
    nb = L // rb
    C = HGRN_CHUNK
    tri_f = np.tril(np.ones((C, C), np.float32))
    tri = jnp.asarray(np.stack([tri_f, tri_f.T]), dtype=BF16)
    blk = lambda d, b, c: (b, c + d * (nb - 1 - 2 * c), 0)
    return pl.pallas_call(
        functools.partial(_hgrn_kernel, n_inner=rb // C),
        grid=(2, B, nb),
        in_specs=[pl.BlockSpec((1, rb, W), blk),
                  pl.BlockSpec((1, rb, W), lambda d, b, c: (b, c + d * (nb - 1 - 2 * c), d)),
                  pl.BlockSpec((1, rb, W), blk),
                  pl.BlockSpec((1, 1, W), lambda d, b, c: (d, 0, 0)),
                  pl.BlockSpec((1, C, C), lambda d, b, c: (d, 0, 0))],
        out_specs=pl.BlockSpec((1, 1, rb, W), lambda d, b, c: (d, b, c + d * (nb - 1 - 2 * c), 0)),
        out_shape=jax.ShapeDtypeStruct((2, B, L, W), F32),
        scratch_shapes=[pltpu.VMEM((HGRN_HEADS, HGRN_DIM, HGRN_DIM), F32)],
        compiler_params=_cparams(("parallel", "parallel", "arbitrary")),
        name="hgrn_scan",
    )(hq.reshape(B, L, W), hf.reshape(B, L, 2 * W), hi.reshape(B, L, W), lb[:, None, :], tri)


def _filter_features(L):
    t = jnp.linspace(0.0, 1.0, L, dtype=F32)[:, None]
    w = 2.0 * math.pi * jnp.arange(L, dtype=F32) / L
    bands = jnp.linspace(1e-4, FILTER_BANDS - 1, FILTER_BANDS, dtype=F32)
    ang = w[:, None] * bands[None, :]
    feats = jnp.concatenate([t, jnp.cos(ang), -jnp.sin(ang)], axis=-1)
    feats2 = jnp.concatenate([feats, feats[:1], jnp.flip(feats[1:], axis=0)], axis=0)
    t2 = jnp.concatenate([t, t[:1], jnp.flip(t[1:], axis=0)], axis=0)
    deltas = jnp.abs(jnp.linspace(math.log(FILTER_TARGET) / FILTER_FAST_DECAY,
                                  math.log(FILTER_TARGET) / FILTER_SLOW_DECAY, N_FILTER_CH, dtype=F32))
    return feats2, t2, deltas[None, :]


def _filter_kernel(feat_ref, t_ref, w1_ref, b1_ref, fr_ref, w2_ref, b2_ref, w3f_ref, w3b_ref,
                   df_ref, db_ref, k_ref, h_ref, *, L):
    @pl.when(pl.program_id(0) == 0)
    def _():
        freq = fr_ref[...]
        h1 = jnp.sin(freq * (_dot3(feat_ref[...], w1_ref[...]) + b1_ref[...]))
        h_ref[...] = jnp.sin(freq * (_dot3(h1, w2_ref[...]) + b2_ref[...]))

    h = h_ref[...]
    t = t_ref[...]
    top = _dot3(h[:L], w3f_ref[...]) * jnp.exp(-t[:L] * df_ref[...])
    bot = _dot3(h[L:], w3b_ref[...]) * jnp.exp(-t[L:] * db_ref[...])
    row = lax.broadcasted_iota(jnp.int32, (L, 1), 0)
    bot = jnp.where(row == 0, 0.0, bot)
    norm = jnp.sum(jnp.abs(top), axis=0, keepdims=True) + jnp.sum(jnp.abs(bot), axis=0, keepdims=True)
    k_ref[:L, :] = top / norm
    k_ref[L:, :] = bot / norm


def _hyena_filter(L, consts, w1, b1, freq, w2, b2, w3):
    feats2, t2, deltas = consts
    wb = 256
    nw = HYENA_WIDTH // wb
    full = lambda shape: pl.BlockSpec(shape, lambda c: (0,) * len(shape))
    fcol = lambda c: (0, (c // nw) * 2 * nw + c % nw)
    bcol = lambda c: (0, (c // nw) * 2 * nw + nw + c % nw)
    return pl.pallas_call(
        functools.partial(_filter_kernel, L=L),
        grid=(HYENA_ORDER * nw,),
        in_specs=[full((2 * L, FILTER_EMB)), full((2 * L, 1)),
                  full((FILTER_EMB, FILTER_HIDDEN)), full((1, FILTER_HIDDEN)), full((1, FILTER_HIDDEN)),
                  full((FILTER_HIDDEN, FILTER_HIDDEN)), full((1, FILTER_HIDDEN)),
                  pl.BlockSpec((FILTER_HIDDEN, wb), fcol), pl.BlockSpec((FILTER_HIDDEN, wb), bcol),
                  pl.BlockSpec((1, wb), fcol), pl.BlockSpec((1, wb), bcol)],
        out_specs=pl.BlockSpec((2 * L, wb), lambda c: (0, c)),
        out_shape=jax.ShapeDtypeStruct((2 * L, HYENA_ORDER * HYENA_WIDTH), F32),
        scratch_shapes=[pltpu.VMEM((2 * L, FILTER_HIDDEN), F32)],
        compiler_params=_cparams(("arbitrary",)),
        name="hyena_filter",
    )(feats2, t2, w1, b1[None, :], freq[None, :], w2, b2[None, :], w3, w3, deltas, deltas)


def _dft_consts(L):
    N = 2 * L
    N2 = FFT_N2
    N1 = N // N2
    k1 = np.arange(N1)
    ang_a = 2.0 * np.pi * np.outer(k1, np.arange(N1)) / N1
    fa_full = np.concatenate([np.cos(ang_a), -np.sin(ang_a)], axis=0)
    fa_half = fa_full[:, :N1 // 2]
    n2 = np.arange(N2)
    kk = k1[:, None, None] + N1 * np.arange(N2)[None, :, None]
    ang_b = 2.0 * np.pi * (kk * n2[None, None, :]) / N
    mr, mi = np.cos(ang_b), -np.sin(ang_b)
    m_fwd = np.concatenate([np.concatenate([mr, -mi], axis=2),
                            np.concatenate([mi, mr], axis=2)], axis=1)
    wr, wi = np.swapaxes(mr, 1, 2), -np.swapaxes(mi, 1, 2)
    m_inv = np.concatenate([np.concatenate([wr, -wi], axis=2),
                            np.concatenate([wi, wr], axis=2)], axis=1)
    ang_c = 2.0 * np.pi * np.outer(np.arange(N1 // 2), k1) / N1
    g_inv = np.concatenate([np.cos(ang_c), -np.sin(ang_c)], axis=1) / N

    def hi_lo(a):
        a = jnp.asarray(a, dtype=F32)
        hi = a.astype(BF16)
        return hi, (a - hi.astype(F32)).astype(BF16)

    eye = np.eye(SUBLANES)
    fa_tiles = jnp.asarray(np.kron(fa_half, eye), dtype=BF16)
    g_tiles = jnp.asarray(np.kron(g_inv, eye), dtype=BF16)
    return dict(N1=N1, N2=N2, fa_full=hi_lo(fa_full), fa_tiles=fa_tiles, g_tiles=g_tiles,
                m_fwd=hi_lo(m_fwd), m_inv=hi_lo(m_inv))


def _mm3(m_hi, m_lo, x):
    x_hi, x_lo = _split2(x)
    return _dot(m_hi, x_hi) + _dot(m_hi, x_lo) + _dot(m_lo, x_hi)


def _stage_a_kernel(z_ref, fh_ref, fl_ref, a_ref, *, exact):
    n1 = a_ref.shape[2]
    if exact:
        res = _mm3(fh_ref[...], fl_ref[...], z_ref[0])
    else:
        res = _dot(fh_ref[...], z_ref[0].astype(BF16))
    a_ref[0, 0] = res[:n1].astype(a_ref.dtype)
    a_ref[0, 1] = res[n1:].astype(a_ref.dtype)


def _fft_stage_a(z, fa, N1, exact):
    B, nr, cols = z.shape
    cb = min(cols, 4096)
    return pl.pallas_call(
        functools.partial(_stage_a_kernel, exact=exact),
        grid=(B, cols // cb),
        in_specs=[pl.BlockSpec((1, nr, cb), lambda b, j: (b, 0, j)),
                  pl.BlockSpec((2 * N1, nr), lambda b, j: (0, 0)),
                  pl.BlockSpec((2 * N1, nr), lambda b, j: (0, 0))],
        out_specs=pl.BlockSpec((1, 2, N1, cb), lambda b, j: (b, 0, 0, j)),
        out_shape=jax.ShapeDtypeStruct((B, 2, N1, cols), F32 if exact else BF16),
        compiler_params=_cparams(("parallel", "parallel")),
        name="fft_stage_a",
    )(z, fa[0], fa[1])


def _stage_b_filter_kernel(a_ref, mh_ref, ml_ref, o_ref):
    n2 = a_ref.shape[3]
    a = a_ref[0, :, 0].reshape(2 * n2, a_ref.shape[4])
    res = _mm3(mh_ref[0], ml_ref[0], a)
    o_ref[0, 0] = res[:n2]
    o_ref[1, 0] = res[n2:]


def _filter_spectrum(k2l, dc):
    N1, N2 = dc["N1"], dc["N2"]
    OW = k2l.shape[1]
    a = _fft_stage_a(k2l.reshape(1, N1, N2 * OW), dc["fa_full"], N1, True)
    a = a.reshape(1, 2, N1, N2, OW)
    wb = 512
    return pl.pallas_call(
        _stage_b_filter_kernel,
        grid=(N1, OW // wb),
        in_specs=[pl.BlockSpec((1, 2, 1, N2, wb), lambda k, j: (0, 0, k, 0, j)),
                  pl.BlockSpec((1, 2 * N2, 2 * N2), lambda k, j: (k, 0, 0)),
                  pl.BlockSpec((1, 2 * N2, 2 * N2), lambda k, j: (k, 0, 0))],
        out_specs=pl.BlockSpec((2, 1, N2, wb), lambda k, j: (0, k, 0, j)),
        out_shape=jax.ShapeDtypeStruct((2, N1, N2, OW), F32),
        compiler_params=_cparams(("parallel", "parallel")),
        name="fft_filter_spectrum",
    )(a, dc["m_fwd"][0], dc["m_fwd"][1])


FFT_SPECTRUM_BYTES = 8 * 1024 * 1024


def _long_conv_kernel(z_ref, gate_ref, h_ref, fa_ref, mf_ref, mi_ref, g_ref, bias_ref, o_ref, a_ref):
    n1, n2, wq = a_ref.shape[1], a_ref.shape[2], a_ref.shape[3]
    nr = z_ref.shape[1]
    s8 = SUBLANES
    fa = fa_ref[...]
    for jt in range(n2 // s8):
        tile = slice(jt * s8, (jt + 1) * s8)
        zt = z_ref[0, :, tile, :].reshape(nr * s8, wq).astype(BF16)
        res = _dot(fa, zt)
        a_ref[0, :, tile, :] = res[:n1 * s8].reshape(n1, s8, wq)
        a_ref[1, :, tile, :] = res[n1 * s8:].reshape(n1, s8, wq)

    def per_k1(k, carry):
        a = jnp.concatenate([a_ref[0, k], a_ref[1, k]], axis=0).astype(BF16)
        zz = _dot(mf_ref[k], a)
        zr, zi = zz[:n2], zz[n2:]
        hr, hi = h_ref[0, k], h_ref[1, k]
        y = jnp.concatenate([zr * hr - zi * hi, zr * hi + zi * hr], axis=0).astype(BF16)
        c = _dot(mi_ref[k], y)
        a_ref[0, k] = c[:n2]
        a_ref[1, k] = c[n2:]
        return carry

    lax.fori_loop(0, n1, per_k1, 0, unroll=8)
    g = g_ref[...]
    bias = bias_ref[...]
    for jt in range(n2 // s8):
        tile = slice(jt * s8, (jt + 1) * s8)
        c = jnp.concatenate([a_ref[0, :, tile, :].reshape(n1 * s8, wq),
                             a_ref[1, :, tile, :].reshape(n1 * s8, wq)], axis=0).astype(BF16)
        y = _dot(g, c).reshape(nr, s8, wq)
        o_ref[0, :, tile, :] = gate_ref[0, :, tile, :] * (y + z_ref[0, :, tile, :] * bias)


def _long_conv_gated(z, gate, hf, order, skip, dc):
    B, nr, N2, W = z.shape
    N1 = dc["N1"]
    wq = min(W, max(LANES, FFT_SPECTRUM_BYTES // (2 * N1 * N2 * 4) // LANES * LANES))
    nw = W // wq
    rows = pl.BlockSpec((1, nr, N2, wq), lambda w, b: (b, 0, 0, w))
    once = pl.Buffered(1)
    mats = pl.BlockSpec((N1, 2 * N2, 2 * N2), lambda w, b: (0, 0, 0), pipeline_mode=once)
    return pl.pallas_call(
        _long_conv_kernel,
        grid=(nw, B),
        in_specs=[rows, rows,
                  pl.BlockSpec((2, N1, N2, wq), lambda w, b: (0, 0, 0, order * nw + w), pipeline_mode=once),
                  pl.BlockSpec((2 * N1 * SUBLANES, nr * SUBLANES), lambda w, b: (0, 0)),
                  mats, mats,
                  pl.BlockSpec((nr * SUBLANES, 2 * N1 * SUBLANES), lambda w, b: (0, 0)),
                  pl.BlockSpec((1, wq), lambda w, b: (0, w))],
        out_specs=rows,
        out_shape=jax.ShapeDtypeStruct((B, nr, N2, W), F32),
        scratch_shapes=[pltpu.VMEM((2, N1, N2, wq), F32)],
        compiler_params=_cparams(("parallel", "parallel")),
        name="hyena_long_conv",
    )(z, gate, hf, dc["fa_tiles"], dc["m_fwd"][0], dc["m_inv"][0], dc["g_tiles"], skip[None, :])


HALO_ROWS = 16


def _short_conv_kernel(u_ref, prev_ref, next_ref, w_ref, b_ref, v_ref, x1_ref, x2_ref):
    i = pl.program_id(1)
    u = u_ref[0].astype(F32)
    tl = u.shape[0]
    row = lax.broadcasted_iota(jnp.int32, (tl, 1), 0)
    prev_last = prev_ref[0].astype(F32)[HALO_ROWS - 1:HALO_ROWS, :]
    next_first = next_ref[0].astype(F32)[0:1, :]
    before = jnp.where(i == 0, 0.0, prev_last)
    after = jnp.where(i == pl.num_programs(1) - 1, 0.0, next_first)
    up = jnp.where(row == 0, before, pltpu.roll(u, 1, 0))
    un = jnp.where(row == tl - 1, after, pltpu.roll(u, tl - 1, 0))
    w = w_ref[...]
    uc = up * w[0:1] + u * w[1:2] + un * w[2:3] + b_ref[...]
    W = HYENA_WIDTH
    v_ref[0] = uc[:, :W]
    x1_ref[0] = uc[:, W:2 * W]
    x2_ref[0] = uc[:, 2 * W:]


def _short_conv(hy, conv_w, conv_b, B, L):
    W3 = 3 * HYENA_WIDTH
    tl = _row_tile(L, 512)
    nl = L // tl
    u = hy.reshape(B, L, W3)
    rh = tl // HALO_ROWS
    spec_o = pl.BlockSpec((1, tl, HYENA_WIDTH), lambda b, i: (b, i, 0))
    return pl.pallas_call(
        _short_conv_kernel,
        grid=(B, nl),
        in_specs=[pl.BlockSpec((1, tl, W3), lambda b, i: (b, i, 0)),
                  pl.BlockSpec((1, HALO_ROWS, W3), lambda b, i: (b, jnp.maximum(i * rh - 1, 0), 0)),
                  pl.BlockSpec((1, HALO_ROWS, W3),
                               lambda b, i: (b, jnp.minimum((i + 1) * rh, L // HALO_ROWS - 1), 0)),
                  pl.BlockSpec((3, W3), lambda b, i: (0, 0)),
                  pl.BlockSpec((1, W3), lambda b, i: (0, 0))],
        out_specs=[spec_o, spec_o, spec_o],
        out_shape=[jax.ShapeDtypeStruct((B, L, HYENA_WIDTH), F32)] * 3,
        compiler_params=_cparams(("parallel", "parallel")),
        name="hyena_short_conv",
    )(u, u, u, conv_w, conv_b[None, :])


def _hyena(hy, hf, conv_w, conv_b, skip, dc, B, L):
    v, x1, x2 = _short_conv(hy, conv_w, conv_b, B, L)
    shape = (B, dc["N1"] // 2, dc["N2"], HYENA_WIDTH)
    z = _long_conv_gated(v.reshape(shape), x1.reshape(shape), hf, 0, skip[0], dc)
    y = _long_conv_gated(z, x2.reshape(shape), hf, 1, skip[1], dc)
    return y.reshape(B * L, HYENA_WIDTH)


def _merge_kernel(x_ref, oa_ref, oh_ref, hg_ref, oc_ref, gl_ref, hgn_ref, wb_ref, wo_ref, g_ref, b_ref,
                  out_ref):
    o = oh_ref[0] + oh_ref[1]
    parts = []
    for hd in range(HGRN_HEADS):
        oh = o[:, hd * HGRN_DIM:(hd + 1) * HGRN_DIM]
        ms = jnp.mean(oh * oh, axis=-1, keepdims=True)
        parts.append(oh * lax.rsqrt(ms + RMS_EPS))
    o_h = jnp.concatenate(parts, axis=-1) * hgn_ref[...] * _silu(hg_ref[...].astype(F32))
    gl = gl_ref[...].astype(F32)
    branches = (oa_ref[...], o_h, oc_ref[...])
    merged = None
    for j, ob in enumerate(branches):
        term = _sigmoid(gl[:, j * D_MODEL:(j + 1) * D_MODEL]) * _dot(ob.astype(BF16), wb_ref[j])
        merged = term if merged is None else merged + term
    mix = _dot(merged.astype(BF16), wo_ref[...])
    out_ref[...] = _layer_norm(ALPHA * x_ref[...] + mix, g_ref[...], b_ref[...])


def _merge(x2, o_a, o_dir, hg, o_c, gl, hgrn_gain, wb_bf16, wo_bf16, ln_g, ln_b):
    T = x2.shape[0]
    tm = _row_tile(T, 512)
    W = BRANCH_WIDTH
    row = lambda w: pl.BlockSpec((tm, w), lambda i: (i, 0))
    return pl.pallas_call(
        _merge_kernel,
        grid=(T // tm,),
        in_specs=[row(D_MODEL), row(W), pl.BlockSpec((2, tm, W), lambda i: (0, i, 0)), row(W), row(W),
                  row(N_BRANCH * D_MODEL),
                  pl.BlockSpec((1, W), lambda i: (0, 0)),
                  pl.BlockSpec((N_BRANCH, W, D_MODEL), lambda i: (0, 0, 0)),
                  pl.BlockSpec((D_MODEL, D_MODEL), lambda i: (0, 0)),
                  pl.BlockSpec((1, D_MODEL), lambda i: (0, 0)),
                  pl.BlockSpec((1, D_MODEL), lambda i: (0, 0))],
        out_specs=row(D_MODEL),
        out_shape=jax.ShapeDtypeStruct((T, D_MODEL), F32),
        compiler_params=_cparams(("parallel",)),
        name="merge_out_ln",
    )(x2, o_a, o_dir.reshape(2, T, W), hg, o_c, gl, jnp.tile(hgrn_gain, HGRN_HEADS)[None, :],
      wb_bf16, wo_bf16, ln_g[None, :], ln_b[None, :])


def _ffn_kernel(x_ref, wg_ref, wu_ref, wd_ref, g_ref, b_ref, out_ref, acc_ref):
    j = pl.program_id(1)
    x = x_ref[...]
    xb = x.astype(BF16)
    hmid = _silu(_dot(xb, wg_ref[...])) * _dot(xb, wu_ref[...])
    contrib = _dot(hmid.astype(BF16), wd_ref[...])

    @pl.when(j == 0)
    def _():
        acc_ref[...] = contrib

    @pl.when(j > 0)
    def _():
        acc_ref[...] += contrib

    @pl.when(j == pl.num_programs(1) - 1)
    def _():
        out_ref[...] = _layer_norm(ALPHA * x + acc_ref[...], g_ref[...], b_ref[...])


def _ffn_dense(x2, w_gu, w_down, ln_g, ln_b):
    T = x2.shape[0]
    tm = _row_tile(T, 512)
    ft = 1408
    nf = D_FF_DENSE // ft
    return pl.pallas_call(
        _ffn_kernel,
        grid=(T // tm, nf),
        in_specs=[pl.BlockSpec((tm, D_MODEL), lambda i, j: (i, 0)),
                  pl.BlockSpec((D_MODEL, ft), lambda i, j: (0, j)),
                  pl.BlockSpec((D_MODEL, ft), lambda i, j: (0, j + nf)),
                  pl.BlockSpec((ft, D_MODEL), lambda i, j: (j, 0)),
                  pl.BlockSpec((1, D_MODEL), lambda i, j: (0, 0)),
                  pl.BlockSpec((1, D_MODEL), lambda i, j: (0, 0))],
        out_specs=pl.BlockSpec((tm, D_MODEL), lambda i, j: (i, 0)),
        out_shape=jax.ShapeDtypeStruct((T, D_MODEL), F32),
        scratch_shapes=[pltpu.VMEM((tm, D_MODEL), F32)],
        compiler_params=_cparams(("parallel", "arbitrary")),
        name="ffn_dense_ln",
    )(x2, w_gu, w_gu, w_down, ln_g[None, :], ln_b[None, :])


MOE_BLOCK = 1024
MOE_SUB = 64
MOE_BIG = 8
MOE_TILE = 256


def _route_kernel(x_ref, r_ref, col_ref, row_ref, s_ref):
    tb = x_ref.shape[0]
    logits = _dot3(x_ref[...], r_ref[...])
    lane = lax.broadcasted_iota(jnp.int32, logits.shape, 1)
    logits = jnp.where(lane < N_EXPERTS, logits, -jnp.inf)
    m1 = jnp.max(logits, axis=-1, keepdims=True)
    i1 = jnp.min(jnp.where(logits == m1, lane, LANES), axis=-1, keepdims=True)
    rest = jnp.where(lane == i1, -jnp.inf, logits)
    m2 = jnp.max(rest, axis=-1, keepdims=True)
    i2 = jnp.min(jnp.where(rest == m2, lane, LANES), axis=-1, keepdims=True)
    e2 = jnp.exp(m2 - m1)
    w1 = 1.0 / (1.0 + e2)
    w2 = e2 / (1.0 + e2)
    oh1 = lane == i1
    oh2 = lane == i2
    sel = jnp.where(oh1, 1.0, jnp.where(oh2, 1.0, 0.0))
    rr = lax.broadcasted_iota(jnp.int32, (tb, tb), 0)
    cc = lax.broadcasted_iota(jnp.int32, (tb, tb), 1)
    earlier = jnp.where(rr > cc, 1.0, 0.0).astype(BF16)
    rank = _dot(earlier, sel.astype(BF16))
    cnt = jnp.sum(sel, axis=0, keepdims=True)
    nsub = jnp.floor((cnt + (MOE_SUB - 1)) * (1.0 / MOE_SUB))
    r2 = lax.broadcasted_iota(jnp.int32, (LANES, LANES), 0)
    c2 = lax.broadcasted_iota(jnp.int32, (LANES, LANES), 1)
    before = jnp.where(r2 < c2, 1.0, 0.0).astype(BF16)
    nsub8 = jnp.broadcast_to(nsub, (8, LANES))
    off8 = _dot(nsub8.astype(BF16), before) * MOE_SUB
    base = off8[0:1] + rank
    d1 = jnp.sum(jnp.where(oh1, base, 0.0), axis=-1, keepdims=True)
    d2 = jnp.sum(jnp.where(oh2, base, 0.0), axis=-1, keepdims=True)
    col = jnp.where(lane == 0, d1, jnp.where(lane == 1, d2, jnp.where(lane == 2, w1, jnp.where(lane == 3, w2, 0.0))))
    col_ref[...] = col
    row_ref[0] = col.T[0:8, :]
    lane8 = lax.broadcasted_iota(jnp.int32, (8, LANES), 1)
    info = jnp.where(lane8 < N_EXPERTS, off8, pltpu.roll(nsub8, N_EXPERTS, 1))
    s_ref[0] = info.astype(jnp.int32)


def _moe_route(x2, router):
    T = x2.shape[0]
    tb = _row_tile(T, MOE_BLOCK)
    nb = T // tb
    r_pad = jnp.pad(router, ((0, 0), (0, LANES - N_EXPERTS)))
    return pl.pallas_call(
        _route_kernel,
        grid=(nb,),
        in_specs=[pl.BlockSpec((tb, D_MODEL), lambda i: (i, 0)),
                  pl.BlockSpec((D_MODEL, LANES), lambda i: (0, 0))],
        out_specs=[pl.BlockSpec((tb, LANES), lambda i: (i, 0)),
                   pl.BlockSpec((1, 8, tb), lambda i: (i, 0, 0)),
                   pl.BlockSpec((1, 8, LANES), lambda i: (i, 0, 0))],
        out_shape=[jax.ShapeDtypeStruct((T, LANES), F32),
                   jax.ShapeDtypeStruct((nb, 8, tb), F32),
                   jax.ShapeDtypeStruct((nb, 8, LANES), jnp.int32)],
        compiler_params=_cparams(("parallel",)),
        name="moe_route",
    )(x2, r_pad)


def _moe_kernel(s_ref, x_ref, col_ref, row_ref, wg_ref, wu_ref, wd_ref, g_ref, b_ref, out_ref,
                xs_ref, ys_ref):
    blk = pl.program_id(0)
    e = pl.program_id(1)
    j = pl.program_id(2)
    tb = x_ref.shape[0]
    n_tiles = xs_ref.shape[0] // MOE_TILE
    used = s_ref[blk, N_EXPERTS - 1] + s_ref[blk, 2 * N_EXPERTS - 1] * MOE_SUB
    d1r = row_ref[0, 0:1, :]
    d2r = row_ref[0, 1:2, :]

    @pl.when(jnp.logical_and(e == 0, j == 0))
    def _():
        xb = x_ref[...].astype(BF16)
        for rt in range(n_tiles):
            @pl.when(rt * MOE_TILE < used)
            def _():
                r = (lax.broadcasted_iota(jnp.int32, (MOE_TILE, tb), 0) + rt * MOE_TILE).astype(F32)
                p = jnp.where(r == d1r, 1.0, jnp.where(r == d2r, 1.0, 0.0)).astype(BF16)
                xs_ref[rt * MOE_TILE:(rt + 1) * MOE_TILE, :] = _dot(p, xb).astype(BF16)

    start = s_ref[blk, e]
    nsub = s_ref[blk, N_EXPERTS + e]

    def expert_rows(first, n_rows):
        rows = pl.ds(pl.multiple_of(first, MOE_SUB), n_rows)
        xt = xs_ref[rows, :]
        hmid = _silu(_dot(xt, wg_ref[0])) * _dot(xt, wu_ref[0])
        y = _dot(hmid.astype(BF16), wd_ref[0])

        @pl.when(j == 0)
        def _():
            ys_ref[rows, :] = y

        @pl.when(j > 0)
        def _():
            ys_ref[rows, :] += y

    n_big = nsub // MOE_BIG
    rem = nsub - MOE_BIG * n_big

    def big(i, carry):
        expert_rows(start + i * (MOE_BIG * MOE_SUB), MOE_BIG * MOE_SUB)
        return carry

    lax.fori_loop(0, n_big, big, 0)
    done = n_big * MOE_BIG
    part = MOE_BIG // 2
    while part >= 1:
        @pl.when((rem // part) % 2 == 1)
        def _(done=done, part=part):
            expert_rows(start + (done + (rem // (2 * part)) * (2 * part)) * MOE_SUB, part * MOE_SUB)
        part //= 2

    @pl.when(jnp.logical_and(e == pl.num_programs(1) - 1, j == pl.num_programs(2) - 1))
    def _():
        sub_per_tile = MOE_TILE // MOE_SUB
        filled = (used // MOE_SUB) % sub_per_tile
        for k in range(1, sub_per_tile):
            @pl.when(jnp.logical_and(filled > 0, filled <= k))
            def _(k=k):
                first = (used // MOE_TILE) * MOE_TILE + k * MOE_SUB
                ys_ref[pl.ds(pl.multiple_of(first, MOE_SUB), MOE_SUB), :] = jnp.zeros((MOE_SUB, D_MODEL), F32)

        w1r = row_ref[0, 2:3, :]
        w2r = row_ref[0, 3:4, :]
        for rt in range(n_tiles):
            @pl.when(rt * MOE_TILE < used)
            def _():
                r = (lax.broadcasted_iota(jnp.int32, (MOE_TILE, tb), 0) + rt * MOE_TILE).astype(F32)
                w_rows = jnp.sum(jnp.where(r == d1r, w1r, 0.0) + jnp.where(r == d2r, w2r, 0.0),
                                 axis=-1, keepdims=True)
                rows = slice(rt * MOE_TILE, (rt + 1) * MOE_TILE)
                xs_ref[rows, :] = (ys_ref[rows, :] * w_rows).astype(BF16)

        d1c = col_ref[:, 0:1]
        d2c = col_ref[:, 1:2]
        out_ref[...] = jnp.zeros(out_ref.shape, F32)
        for rt in range(n_tiles):
            @pl.when(rt * MOE_TILE < used)
            def _():
                c = (lax.broadcasted_iota(jnp.int32, (tb, MOE_TILE), 1) + rt * MOE_TILE).astype(F32)
                q = jnp.where(c == d1c, 1.0, jnp.where(c == d2c, 1.0, 0.0)).astype(BF16)
                out_ref[...] += _dot(q, xs_ref[rt * MOE_TILE:(rt + 1) * MOE_TILE, :])

        out_ref[...] = _layer_norm(ALPHA * x_ref[...] + out_ref[...], g_ref[...], b_ref[...])


def _ffn_moe(x2, router, w_gu, w_down, ln_g, ln_b):
    T = x2.shape[0]
    tb = _row_tile(T, MOE_BLOCK)
    nb = T // tb
    n_rows = 2 * tb + N_EXPERTS * MOE_SUB
    col, row, sinfo = _moe_route(x2, router)
    ft = 1792
    nf = D_FF_EXPERT // ft
    grid_spec = pltpu.PrefetchScalarGridSpec(
        num_scalar_prefetch=1,
        grid=(nb, N_EXPERTS, nf),
        in_specs=[pl.BlockSpec((tb, D_MODEL), lambda i, e, j, s: (i, 0), pipeline_mode=pl.Buffered(1)),
                  pl.BlockSpec((tb, LANES), lambda i, e, j, s: (i, 0)),
                  pl.BlockSpec((1, 8, tb), lambda i, e, j, s: (i, 0, 0)),
                  pl.BlockSpec((1, D_MODEL, ft), lambda i, e, j, s: (e, 0, j)),
                  pl.BlockSpec((1, D_MODEL, ft), lambda i, e, j, s: (e, 0, j + nf)),
                  pl.BlockSpec((1, ft, D_MODEL), lambda i, e, j, s: (e, j, 0)),
                  pl.BlockSpec((1, D_MODEL), lambda i, e, j, s: (0, 0)),
                  pl.BlockSpec((1, D_MODEL), lambda i, e, j, s: (0, 0))],
        out_specs=pl.BlockSpec((tb, D_MODEL), lambda i, e, j, s: (i, 0)),
        scratch_shapes=[pltpu.VMEM((n_rows, D_MODEL), BF16), pltpu.VMEM((n_rows, D_MODEL), F32)])
    return pl.pallas_call(
        _moe_kernel,
        grid_spec=grid_spec,
        out_shape=jax.ShapeDtypeStruct((T, D_MODEL), F32),
        compiler_params=_cparams(("parallel", "arbitrary", "arbitrary")),
        name="ffn_moe_ln",
    )(sinfo[:, 0, :], x2, col, row, w_gu, w_gu, w_down, ln_g[None, :], ln_b[None, :])


def _trunk(x, p):
    B, L, _ = x.shape
    cos_t, sin_t = _rope_tables(L)
    filt_consts = _filter_features(L)
    dc = _dft_consts(L)
    x2 = x.reshape(B * L, D_MODEL)
    for l in range(DEPTH):
        aq, akv, hq, hf, hi, hg, hy, gl = _in_proj(x2, p["w_in"][l])
        qn, kn, vn = _attn_prep(aq, akv, cos_t, sin_t, p["q_norm_g"][l], p["k_norm_g"][l], B, L)
        o_a = _flash_attention(qn, kn, vn, B, L)
        o_dir = _hgrn(hq, hf, hi, p["lower_bounds"][l], B, L)
        k2l = _hyena_filter(L, filt_consts, p["filt_w1"][l], p["filt_b1"][l], p["filt_freq"][l],
                            p["filt_w2"][l], p["filt_b2"][l], p["filt_w3"][l])
        spec = _filter_spectrum(k2l, dc)
        o_c = _hyena(hy, spec, p["conv_w"][l], p["conv_b"][l], p["hyena_skip"][l], dc, B, L)
        x2 = _merge(x2, o_a, o_dir, hg, o_c, gl, p["hgrn_norm_g"][l], p["w_branch"][l], p["w_out"][l],
                    p["ln1_g"][l], p["ln1_b"][l])
        if l % 2 == 0:
            x2 = _ffn_dense(x2, p["ffn_w_gu"][l // 2], p["ffn_w_down"][l // 2], p["ln2_g"][l], p["ln2_b"][l])
        else:
            x2 = _ffn_moe(x2, p["router_w"][l // 2], p["expert_w_gu"][l // 2], p["expert_w_down"][l // 2],
                          p["ln2_g"][l], p["ln2_b"][l])
    return x2.reshape(B, L, D_MODEL)


def _in_proj_weight(w_in):
    return w_in.astype(BF16)


def kernel(x_prompt, x_sample, w_in, q_norm_g, k_norm_g, hgrn_lb, hgrn_norm_g, conv_w, conv_b, filt_w1, filt_b1, filt_freq, filt_w2, filt_b2, filt_w3, hyena_skip, w_branch, w_out, ln1_g, ln1_b, ln2_g, ln2_b, ffn_w_gu, ffn_w_down, router_w, expert_w_gu, expert_w_down):
    s = jax.nn.softmax(hgrn_lb.astype(F32), axis=0)
    p = dict(
        w_in=_in_proj_weight(w_in), q_norm_g=q_norm_g, k_norm_g=k_norm_g,
        lower_bounds=jnp.cumsum(s, axis=0) - s[:1], hgrn_norm_g=hgrn_norm_g,
        conv_w=conv_w, conv_b=conv_b, filt_w1=filt_w1, filt_b1=filt_b1, filt_freq=filt_freq,
        filt_w2=filt_w2, filt_b2=filt_b2, filt_w3=filt_w3, hyena_skip=hyena_skip,
        w_branch=w_branch.astype(BF16), w_out=w_out.astype(BF16),
        ln1_g=ln1_g, ln1_b=ln1_b, ln2_g=ln2_g, ln2_b=ln2_b,
        ffn_w_gu=ffn_w_gu.astype(BF16), ffn_w_down=ffn_w_down.astype(BF16), router_w=router_w,
        expert_w_gu=expert_w_gu.astype(BF16), expert_w_down=expert_w_down.astype(BF16))
    return (_trunk(x_prompt, p), _trunk(x_sample, p))
```

```python
import functools
import math

import numpy as np
import jax
import jax.numpy as jnp
from jax import lax
from jax.experimental import pallas as pl
from jax.experimental.pallas import tpu as pltpu

F32 = jnp.float32
BF16 = jnp.bfloat16

D_MODEL = 1024
DEPTH = 2
GRID_W = 64
BRANCH_WIDTH = 512
ATTN_HEADS = 8
ATTN_KV_HEADS = 2
ATTN_GROUP = ATTN_HEADS // ATTN_KV_HEADS
HEAD_DIM = 64
KV_WIDTH = ATTN_KV_HEADS * HEAD_DIM
ROPE_AXIS_DIM = HEAD_DIM // 2
ROPE_HALF = ROPE_AXIS_DIM // 2
ROPE_THETA = 10000.0
HGRN_HEADS = 4
HGRN_DIM = 128
HGRN_CHUNK = 64
FORGET_FLOOR = 1e-30
HYENA_WIDTH = BRANCH_WIDTH
HYENA_ORDER = 2
FILTER_EMB = 33
FILTER_BANDS = (FILTER_EMB - 1) // 2
FILTER_HIDDEN = 64
FILTER_FAST_DECAY = 0.3
FILTER_SLOW_DECAY = 1.5
FILTER_TARGET = 1e-2
N_FILTER_CH = HYENA_ORDER * 2 * HYENA_WIDTH
N_BRANCH = 3
D_FF_DENSE = 2816
N_EXPERTS = 8
D_FF_EXPERT = 3584
ALPHA = (2 * DEPTH) ** 0.25
LN_EPS = 1e-5
RMS_EPS = 1e-6

SEG_WIDTHS = (BRANCH_WIDTH,
              2 * KV_WIDTH,
              BRANCH_WIDTH,
              2 * BRANCH_WIDTH,
              BRANCH_WIDTH,
              BRANCH_WIDTH,
              3 * HYENA_WIDTH,
              N_BRANCH * D_MODEL)
IN_WIDTH = sum(SEG_WIDTHS)
SEG_DTYPES = (BF16, BF16, BF16, F32, BF16, BF16, BF16, BF16)

LANES = 128
SUBLANES = 8
FFT_N2 = 128
VMEM_LIMIT = 56 * 1024 * 1024


def _cparams(sem):
    return pltpu.CompilerParams(dimension_semantics=sem, vmem_limit_bytes=VMEM_LIMIT)


def _dot(a, b):
    return jnp.dot(a, b, preferred_element_type=F32)


def _dot_nt(a, b):
    return lax.dot_general(a, b, (((1,), (1,)), ((), ())), preferred_element_type=F32)


def _dot_tn(a, b):
    return lax.dot_general(a, b, (((0,), (0,)), ((), ())), preferred_element_type=F32)


def _split2(a):
    hi = a.astype(BF16)
    lo = (a - hi.astype(F32)).astype(BF16)
    return hi, lo


def _dot3(a, b):
    ah, al = _split2(a)
    bh, bl = _split2(b)
    return _dot(ah, bh) + _dot(al, bh) + _dot(ah, bl)


def _layer_norm(y, g, b):
    mu = jnp.mean(y, axis=-1, keepdims=True)
    yc = y - mu
    var = jnp.mean(yc * yc, axis=-1, keepdims=True)
    return yc * lax.rsqrt(var + LN_EPS) * g + b


def _sigmoid(x):
    return 1.0 / (1.0 + jnp.exp(-x))


def _silu(x):
    return x * _sigmoid(x)


def _row_tile(n, want):
    t = min(n, want)
    assert n % t == 0
    return t


def _in_proj_kernel(x_ref, w_ref, *out_refs):
    x = x_ref[...].astype(BF16)
    off = 0
    for o_ref, width in zip(out_refs, SEG_WIDTHS):
        o_ref[...] = _dot(x, w_ref[:, off:off + width]).astype(o_ref.dtype)
        off += width


def _in_proj(x2, w_bf16):
    T = x2.shape[0]
    tm = _row_tile(T, 512)
    return pl.pallas_call(
        _in_proj_kernel,
        grid=(T // tm,),
        in_specs=[pl.BlockSpec((tm, D_MODEL), lambda i: (i, 0)),
                  pl.BlockSpec((D_MODEL, IN_WIDTH), lambda i: (0, 0), pipeline_mode=pl.Buffered(1))],
        out_specs=[pl.BlockSpec((tm, w), lambda i: (i, 0)) for w in SEG_WIDTHS],
        out_shape=[jax.ShapeDtypeStruct((T, w), dt) for w, dt in zip(SEG_WIDTHS, SEG_DTYPES)],
        compiler_params=_cparams(("parallel",)),
        name="in_proj",
    )(x2, w_bf16)


def _rope_tables(L):
    rows = L // GRID_W
    row = jnp.broadcast_to(jnp.arange(rows, dtype=F32)[:, None], (rows, GRID_W)).reshape(-1)
    col = jnp.broadcast_to(jnp.arange(GRID_W, dtype=F32)[None, :], (rows, GRID_W)).reshape(-1)
    inv = jnp.power(ROPE_THETA, -2.0 * jnp.arange(ROPE_HALF, dtype=F32) / ROPE_AXIS_DIM)
    a0, a1 = row[:, None] * inv, col[:, None] * inv
    c0, s0, c1, s1 = jnp.cos(a0), jnp.sin(a0), jnp.cos(a1), jnp.sin(a1)
    cos_h = jnp.concatenate([c0, c0, c1, c1], axis=-1)
    sin_h = jnp.concatenate([-s0, s0, -s1, s1], axis=-1)
    return jnp.tile(cos_h, (1, ATTN_HEADS)), jnp.tile(sin_h, (1, ATTN_HEADS))


def _head_block_ones(width):
    idx = np.arange(width) // HEAD_DIM
    return jnp.asarray(idx[:, None] == idx[None, :], dtype=BF16)


def _norm_rope(x, gain, cos, sin, ones_bd):
    width = x.shape[-1]
    sq_hi, sq_lo = _split2(x * x)
    ms = (_dot(sq_hi, ones_bd) + _dot(sq_lo, ones_bd)) * (1.0 / HEAD_DIM)
    xn = x * lax.rsqrt(ms + RMS_EPS) * gain
    lane = lax.broadcasted_iota(jnp.int32, xn.shape, 1)
    first_half = (lane % ROPE_AXIS_DIM) < ROPE_HALF
    swapped = jnp.where(first_half, pltpu.roll(xn, width - ROPE_HALF, 1), pltpu.roll(xn, ROPE_HALF, 1))
    return xn * cos + swapped * sin


LOG2E = 1.4426950408889634
V_EXT = 2 * HEAD_DIM
FLASH_SKEW = 4


def _attn_prep_kernel(aq_ref, akv_ref, cos_ref, sin_ref, qg_ref, kg_ref, bdq_ref, bdk_ref,
                      q_out, kt_out, v_out):
    cos, sin = cos_ref[...], sin_ref[...]
    q = _norm_rope(aq_ref[...].astype(F32), qg_ref[...], cos, sin, bdq_ref[...])
    q_out[...] = (q * (LOG2E * HEAD_DIM ** -0.5)).astype(BF16)
    kv = akv_ref[...].astype(F32)
    k = _norm_rope(kv[:, :KV_WIDTH], kg_ref[...], cos[:, :KV_WIDTH], sin[:, :KV_WIDTH], bdk_ref[...])
    kt_out[0] = k.T.astype(BF16)
    ones = jnp.ones((kv.shape[0], HEAD_DIM), F32)
    v_ext = [kv[:, KV_WIDTH + h * HEAD_DIM:KV_WIDTH + (h + 1) * HEAD_DIM] if part == 0 else ones
             for h in range(ATTN_KV_HEADS) for part in range(2)]
    v_out[...] = jnp.concatenate(v_ext, axis=-1).astype(BF16)


def _attn_prep(aq, akv, cos_t, sin_t, q_gain, k_gain, B, L):
    T = aq.shape[0]
    tm = _row_tile(L, 512)
    nl = L // tm
    qg = jnp.tile(q_gain, ATTN_HEADS)[None, :]
    kg = jnp.tile(k_gain, ATTN_KV_HEADS)[None, :]
    const = lambda i: (0, 0)
    return pl.pallas_call(
        _attn_prep_kernel,
        grid=(T // tm,),
        in_specs=[pl.BlockSpec((tm, BRANCH_WIDTH), lambda i: (i, 0)),
                  pl.BlockSpec((tm, 2 * KV_WIDTH), lambda i: (i, 0)),
                  pl.BlockSpec((tm, BRANCH_WIDTH), lambda i: (i % nl, 0)),
                  pl.BlockSpec((tm, BRANCH_WIDTH), lambda i: (i % nl, 0)),
                  pl.BlockSpec((1, BRANCH_WIDTH), const),
                  pl.BlockSpec((1, KV_WIDTH), const),
                  pl.BlockSpec((BRANCH_WIDTH, BRANCH_WIDTH), const),
                  pl.BlockSpec((KV_WIDTH, KV_WIDTH), const)],
        out_specs=[pl.BlockSpec((tm, BRANCH_WIDTH), lambda i: (i, 0)),
                   pl.BlockSpec((1, KV_WIDTH, tm), lambda i: (i // nl, 0, i % nl)),
                   pl.BlockSpec((tm, ATTN_KV_HEADS * V_EXT), lambda i: (i, 0))],
        out_shape=[jax.ShapeDtypeStruct((T, BRANCH_WIDTH), BF16),
                   jax.ShapeDtypeStruct((B, KV_WIDTH, L), BF16),
                   jax.ShapeDtypeStruct((T, ATTN_KV_HEADS * V_EXT), BF16)],
        compiler_params=_cparams(("parallel",)),
        name="attn_prep",
    )(aq, akv, cos_t, sin_t, qg, kg, _head_block_ones(BRANCH_WIDTH), _head_block_ones(KV_WIDTH))


def _flash_kernel(q_ref, kt_ref, v_ref, o_ref, qs_ref, m_ref, acc_ref, *, tq, rs):
    kv = pl.program_id(2)

    @pl.when(kv == 0)
    def _():
        for h in range(ATTN_HEADS):
            g = h % ATTN_GROUP
            qs_ref[h // ATTN_GROUP, g * tq:(g + 1) * tq, :] = q_ref[0, :, h * HEAD_DIM:(h + 1) * HEAD_DIM]
        m_ref[...] = jnp.full(m_ref.shape, -jnp.inf, F32)
        acc_ref[...] = jnp.zeros(acc_ref.shape, F32)

    kts = [kt_ref[0, kh * HEAD_DIM:(kh + 1) * HEAD_DIM, :] for kh in range(ATTN_KV_HEADS)]
    vs = [v_ref[0, :, kh * V_EXT:(kh + 1) * V_EXT] for kh in range(ATTN_KV_HEADS)]
    tasks = [(kh, slice(sb * rs, (sb + 1) * rs))
             for sb in range(ATTN_GROUP * tq // rs) for kh in range(ATTN_KV_HEADS)]

    def scores(t):
        kh, rows = tasks[t]
        return _dot(qs_ref[kh, rows, :], kts[kh])

    pending = {t: scores(t) for t in range(min(FLASH_SKEW, len(tasks)))}
    for t, (kh, rows) in enumerate(tasks):
        s = pending.pop(t)
        if t + FLASH_SKEW < len(tasks):
            pending[t + FLASH_SKEW] = scores(t + FLASH_SKEW)
        m_prev = m_ref[kh, rows, :]
        m_new = jnp.maximum(m_prev, jnp.max(s, axis=-1, keepdims=True))
        p = jnp.exp2(s - m_new[:, 0:1])
        acc_ref[kh, rows, :] = jnp.exp2(m_prev - m_new) * acc_ref[kh, rows, :] + _dot(p.astype(BF16), vs[kh])
        m_ref[kh, rows, :] = m_new

    @pl.when(kv == pl.num_programs(2) - 1)
    def _():
        for h in range(ATTN_HEADS):
            kh, g = h // ATTN_GROUP, h % ATTN_GROUP
            a = acc_ref[kh, g * tq:(g + 1) * tq, :]
            o_ref[0, :, h * HEAD_DIM:(h + 1) * HEAD_DIM] = a[:, :HEAD_DIM] / a[:, HEAD_DIM:]


def _flash_attention(qn, kt, v_ext, B, L):
    tq = _row_tile(L, 256)
    tk = _row_tile(L, 1024)
    rs = min(128, ATTN_GROUP * tq)
    out = pl.pallas_call(
        functools.partial(_flash_kernel, tq=tq, rs=rs),
        grid=(B, L // tq, L // tk),
        in_specs=[pl.BlockSpec((1, tq, BRANCH_WIDTH), lambda b, i, j: (b, i, 0)),
                  pl.BlockSpec((1, KV_WIDTH, tk), lambda b, i, j: (b, 0, j)),
                  pl.BlockSpec((1, tk, ATTN_KV_HEADS * V_EXT), lambda b, i, j: (b, j, 0))],
        out_specs=pl.BlockSpec((1, tq, BRANCH_WIDTH), lambda b, i, j: (b, i, 0)),
        out_shape=jax.ShapeDtypeStruct((B, L, BRANCH_WIDTH), F32),
        scratch_shapes=[pltpu.VMEM((ATTN_KV_HEADS, ATTN_GROUP * tq, HEAD_DIM), BF16),
                        pltpu.VMEM((ATTN_KV_HEADS, ATTN_GROUP * tq, LANES), F32),
                        pltpu.VMEM((ATTN_KV_HEADS, ATTN_GROUP * tq, V_EXT), F32)],
        compiler_params=_cparams(("parallel", "parallel", "arbitrary")),
        name="flash_attention",
    )(qn.reshape(B, L, BRANCH_WIDTH), kt, v_ext.reshape(B, L, ATTN_KV_HEADS * V_EXT))
    return out.reshape(B * L, BRANCH_WIDTH)


def _ref_rows(b, h, fwd):
    C, W = b.shape
    two_h = 2 * h
    if two_h >= 8:
        n = C // two_h
        b3 = b.reshape(n, two_h, W)
        r = jnp.where(fwd, b3[:, h - 1:h, :], b3[:, h:h + 1, :])
        return jnp.broadcast_to(r, (n, two_h, W)).reshape(C, W)
    b3 = b.reshape(C // 8, 8, W)
    pos8 = lax.broadcasted_iota(jnp.int32, (1, 8, 1), 1)
    out = None
    for blk in range(8 // two_h):
        base = blk * two_h
        r = jnp.where(fwd, b3[:, base + h - 1:base + h, :], b3[:, base + h:base + h + 1, :])
        r = jnp.broadcast_to(r, b3.shape)
        out = r if out is None else jnp.where((pos8 // two_h) == blk, r, out)
    return out.reshape(C, W)


def _hgrn_chunk(q, z, v, lb, tri, st_ref, fwd, head0):
    C, W = q.shape
    q = q.astype(F32)
    v = v.astype(F32)
    e = jnp.exp(-jnp.abs(z))
    s_big = 1.0 / (1.0 + e)
    s_small = e * s_big
    pos = z >= 0.0
    f = lb + (1.0 - lb) * jnp.where(pos, s_big, s_small)
    g = jnp.log2(jnp.maximum(f, FORGET_FLOOR))
    kk = (1.0 - lb) * jnp.where(pos, s_small, s_big)
    g_hi, g_lo = _split2(g)
    b = _dot(tri, g_hi) + _dot(tri, g_lo)
    b_end = jnp.where(fwd, b[C - 1:C, :], b[0:1, :])
    q_in = (q * jnp.exp2(b)).astype(BF16)
    k_out = (kk * jnp.exp2(b_end - b)).astype(BF16)
    s_decay = jnp.exp2(b_end)
    vb = v.astype(BF16)

    row = lax.broadcasted_iota(jnp.int32, (C, 1), 0)
    rr = lax.broadcasted_iota(jnp.int32, (C, C), 0)
    cc = lax.broadcasted_iota(jnp.int32, (C, C), 1)
    n_heads = W // HGRN_DIM
    a_mats = [jnp.zeros((C, C), F32) for _ in range(n_heads)]
    h = C // 2
    while h >= 1:
        upper = ((row % (2 * h)) >= h).astype(jnp.int32)
        q_rows = upper == fwd.astype(jnp.int32)
        ref = _ref_rows(b, h, fwd)
        ql = jnp.where(q_rows, q * jnp.exp2(b - ref), 0.0).astype(BF16)
        kl = jnp.where(q_rows, 0.0, kk * jnp.exp2(ref - b)).astype(BF16)
        level = (jnp.bitwise_xor(rr, cc) // h) == 1
        for hd in range(n_heads):
            sl = slice(hd * HGRN_DIM, (hd + 1) * HGRN_DIM)
            a_mats[hd] = jnp.where(level, _dot_nt(ql[:, sl], kl[:, sl]), a_mats[hd])
        h //= 2

    qk = q * kk
    outs = []
    for hd in range(n_heads):
        sl = slice(hd * HGRN_DIM, (hd + 1) * HGRN_DIM)
        st = st_ref[head0 + hd]
        o = _dot_nt(q_in[:, sl], st.astype(BF16))
        o = o + _dot(a_mats[hd].astype(BF16), vb[:, sl])
        o = o + jnp.sum(qk[:, sl], axis=-1, keepdims=True) * v[:, sl]
        outs.append(o)
        st_ref[head0 + hd] = st * s_decay[:, sl] + _dot_tn(vb[:, sl], k_out[:, sl])
    return jnp.concatenate(outs, axis=-1)


HGRN_GROUP = 4


def _hgrn_kernel(q_ref, f_ref, v_ref, lb_ref, tri_ref, o_ref, st_ref, *, n_inner):
    d = pl.program_id(0)
    fwd = d == 0

    @pl.when(pl.program_id(2) == 0)
    def _():
        st_ref[...] = jnp.zeros(st_ref.shape, F32)

    lb = lb_ref[0]
    tri = tri_ref[0]

    def body(i, carry):
        ci = jnp.where(fwd, i, n_inner - 1 - i)
        r0 = pl.multiple_of(ci * HGRN_CHUNK, HGRN_CHUNK)
        rows = pl.ds(r0, HGRN_CHUNK)
        for grp in range(HGRN_HEADS // HGRN_GROUP):
            cols = slice(grp * HGRN_GROUP * HGRN_DIM, (grp + 1) * HGRN_GROUP * HGRN_DIM)
            o_ref[0, 0, rows, cols] = _hgrn_chunk(q_ref[0, rows, cols], f_ref[0, rows, cols],
                                                  v_ref[0, rows, cols], lb[:, cols], tri, st_ref, fwd,
                                                  grp * HGRN_GROUP)
        return carry

    lax.fori_loop(0, n_inner, body, 0, unroll=4)


def _hgrn(hq, hf, hi, lb, B, L):
    W = BRANCH_WIDTH
    rb = _row_tile(L, 512)
    nb = L // rb
    C = HGRN_CHUNK
    tri_f = np.tril(np.ones((C, C), np.float32))
    tri = jnp.asarray(np.stack([tri_f, tri_f.T]), dtype=BF16)
    blk = lambda d, b, c: (b, c + d * (nb - 1 - 2 * c), 0)
    return pl.pallas_call(
        functools.partial(_hgrn_kernel, n_inner=rb // C),
        grid=(2, B, nb),
        in_specs=[pl.BlockSpec((1, rb, W), blk),
                  pl.BlockSpec((1, rb, W), lambda d, b, c: (b, c + d * (nb - 1 - 2 * c), d)),
                  pl.BlockSpec((1, rb, W), blk),
                  pl.BlockSpec((1, 1, W), lambda d, b, c: (d, 0, 0)),
                  pl.BlockSpec((1, C, C), lambda d, b, c: (d, 0, 0))],
        out_specs=pl.BlockSpec((1, 1, rb, W), lambda d, b, c: (d, b, c + d * (nb - 1 - 2 * c), 0)),
        out_shape=jax.ShapeDtypeStruct((2, B, L, W), F32),
        scratch_shapes=[pltpu.VMEM((HGRN_HEADS, HGRN_DIM, HGRN_DIM), F32)],
        compiler_params=_cparams(("parallel", "parallel", "arbitrary")),
        name="hgrn_scan",
    )(hq.reshape(B, L, W), hf.reshape(B, L, 2 * W), hi.reshape(B, L, W), lb[:, None, :], tri)


def _filter_features(L):
    t = jnp.linspace(0.0, 1.0, L, dtype=F32)[:, None]
    w = 2.0 * math.pi * jnp.arange(L, dtype=F32) / L
    bands = jnp.linspace(1e-4, FILTER_BANDS - 1, FILTER_BANDS, dtype=F32)
    ang = w[:, None] * bands[None, :]
    feats = jnp.concatenate([t, jnp.cos(ang), -jnp.sin(ang)], axis=-1)
    feats2 = jnp.concatenate([feats, feats[:1], jnp.flip(feats[1:], axis=0)], axis=0)
    t2 = jnp.concatenate([t, t[:1], jnp.flip(t[1:], axis=0)], axis=0)
    deltas = jnp.abs(jnp.linspace(math.log(FILTER_TARGET) / FILTER_FAST_DECAY,
                                  math.log(FILTER_TARGET) / FILTER_SLOW_DECAY, N_FILTER_CH, dtype=F32))
    return feats2, t2, deltas[None, :]


def _filter_kernel(feat_ref, t_ref, w1_ref, b1_ref, fr_ref, w2_ref, b2_ref, w3f_ref, w3b_ref,
                   df_ref, db_ref, k_ref, h_ref, *, L):
    @pl.when(pl.program_id(0) == 0)
    def _():
        freq = fr_ref[...]
        h1 = jnp.sin(freq * (_dot3(feat_ref[...], w1_ref[...]) + b1_ref[...]))
        h_ref[...] = jnp.sin(freq * (_dot3(h1, w2_ref[...]) + b2_ref[...]))

    h = h_ref[...]
    t = t_ref[...]
    top = _dot3(h[:L], w3f_ref[...]) * jnp.exp(-t[:L] * df_ref[...])
    bot = _dot3(h[L:], w3b_ref[...]) * jnp.exp(-t[L:] * db_ref[...])
    row = lax.broadcasted_iota(jnp.int32, (L, 1), 0)
    bot = jnp.where(row == 0, 0.0, bot)
    norm = jnp.sum(jnp.abs(top), axis=0, keepdims=True) + jnp.sum(jnp.abs(bot), axis=0, keepdims=True)
    k_ref[:L, :] = top / norm
    k_ref[L:, :] = bot / norm


def _hyena_filter(L, consts, w1, b1, freq, w2, b2, w3):
    feats2, t2, deltas = consts
    wb = 256
    nw = HYENA_WIDTH // wb
    full = lambda shape: pl.BlockSpec(shape, lambda c: (0,) * len(shape))
    fcol = lambda c: (0, (c // nw) * 2 * nw + c % nw)
    bcol = lambda c: (0, (c // nw) * 2 * nw + nw + c % nw)
    return pl.pallas_call(
        functools.partial(_filter_kernel, L=L),
        grid=(HYENA_ORDER * nw,),
        in_specs=[full((2 * L, FILTER_EMB)), full((2 * L, 1)),
                  full((FILTER_EMB, FILTER_HIDDEN)), full((1, FILTER_HIDDEN)), full((1, FILTER_HIDDEN)),
                  full((FILTER_HIDDEN, FILTER_HIDDEN)), full((1, FILTER_HIDDEN)),
                  pl.BlockSpec((FILTER_HIDDEN, wb), fcol), pl.BlockSpec((FILTER_HIDDEN, wb), bcol),
                  pl.BlockSpec((1, wb), fcol), pl.BlockSpec((1, wb), bcol)],
        out_specs=pl.BlockSpec((2 * L, wb), lambda c: (0, c)),
        out_shape=jax.ShapeDtypeStruct((2 * L, HYENA_ORDER * HYENA_WIDTH), F32),
        scratch_shapes=[pltpu.VMEM((2 * L, FILTER_HIDDEN), F32)],
        compiler_params=_cparams(("arbitrary",)),
        name="hyena_filter",
    )(feats2, t2, w1, b1[None, :], freq[None, :], w2, b2[None, :], w3, w3, deltas, deltas)


def _dft_consts(L):
    N = 2 * L
    N2 = FFT_N2
    N1 = N // N2
    k1 = np.arange(N1)
    ang_a = 2.0 * np.pi * np.outer(k1, np.arange(N1)) / N1
    fa_full = np.concatenate([np.cos(ang_a), -np.sin(ang_a)], axis=0)
    fa_half = fa_full[:, :N1 // 2]
    n2 = np.arange(N2)
    kk = k1[:, None, None] + N1 * np.arange(N2)[None, :, None]
    ang_b = 2.0 * np.pi * (kk * n2[None, None, :]) / N
    mr, mi = np.cos(ang_b), -np.sin(ang_b)
    m_fwd = np.concatenate([np.concatenate([mr, -mi], axis=2),
                            np.concatenate([mi, mr], axis=2)], axis=1)
    wr, wi = np.swapaxes(mr, 1, 2), -np.swapaxes(mi, 1, 2)
    m_inv = np.concatenate([np.concatenate([wr, -wi], axis=2),
                            np.concatenate([wi, wr], axis=2)], axis=1)
    ang_c = 2.0 * np.pi * np.outer(np.arange(N1 // 2), k1) / N1
    g_inv = np.concatenate([np.cos(ang_c), -np.sin(ang_c)], axis=1) / N

    def hi_lo(a):
        a = jnp.asarray(a, dtype=F32)
        hi = a.astype(BF16)
        return hi, (a - hi.astype(F32)).astype(BF16)

    eye = np.eye(SUBLANES)
    fa_tiles = jnp.asarray(np.kron(fa_half, eye), dtype=BF16)
    g_tiles = jnp.asarray(np.kron(g_inv, eye), dtype=BF16)
    return dict(N1=N1, N2=N2, fa_full=hi_lo(fa_full), fa_tiles=fa_tiles, g_tiles=g_tiles,
                m_fwd=hi_lo(m_fwd), m_inv=hi_lo(m_inv))


def _mm3(m_hi, m_lo, x):
    x_hi, x_lo = _split2(x)
    return _dot(m_hi, x_hi) + _dot(m_hi, x_lo) + _dot(m_lo, x_hi)


def _stage_a_kernel(z_ref, fh_ref, fl_ref, a_ref):
    n1 = a_ref.shape[2]
    res = _mm3(fh_ref[...], fl_ref[...], z_ref[0])
    a_ref[0, 0] = res[:n1]
    a_ref[0, 1] = res[n1:]


def _fft_stage_a(z, fa, N1):
    B, nr, cols = z.shape
    cb = min(cols, 4096)
    return pl.pallas_call(
        _stage_a_kernel,
        grid=(B, cols // cb),
        in_specs=[pl.BlockSpec((1, nr, cb), lambda b, j: (b, 0, j)),
                  pl.BlockSpec((2 * N1, nr), lambda b, j: (0, 0)),
                  pl.BlockSpec((2 * N1, nr), lambda b, j: (0, 0))],
        out_specs=pl.BlockSpec((1, 2, N1, cb), lambda b, j: (b, 0, 0, j)),
        out_shape=jax.ShapeDtypeStruct((B, 2, N1, cols), F32),
        compiler_params=_cparams(("parallel", "parallel")),
        name="fft_stage_a",
    )(z, fa[0], fa[1])


def _stage_b_filter_kernel(a_ref, mh_ref, ml_ref, o_ref):
    n2 = a_ref.shape[3]
    a = a_ref[0, :, 0].reshape(2 * n2, a_ref.shape[4])
    res = _mm3(mh_ref[0], ml_ref[0], a)
    o_ref[0, 0] = res[:n2]
    o_ref[1, 0] = res[n2:]


def _filter_spectrum(k2l, dc):
    N1, N2 = dc["N1"], dc["N2"]
    OW = k2l.shape[1]
    a = _fft_stage_a(k2l.reshape(1, N1, N2 * OW), dc["fa_full"], N1)
    a = a.reshape(1, 2, N1, N2, OW)
    wb = 512
    return pl.pallas_call(
        _stage_b_filter_kernel,
        grid=(N1, OW // wb),
        in_specs=[pl.BlockSpec((1, 2, 1, N2, wb), lambda k, j: (0, 0, k, 0, j)),
                  pl.BlockSpec((1, 2 * N2, 2 * N2), lambda k, j: (k, 0, 0)),
                  pl.BlockSpec((1, 2 * N2, 2 * N2), lambda k, j: (k, 0, 0))],
        out_specs=pl.BlockSpec((2, 1, N2, wb), lambda k, j: (0, k, 0, j)),
        out_shape=jax.ShapeDtypeStruct((2, N1, N2, OW), F32),
        compiler_params=_cparams(("parallel", "parallel")),
        name="fft_filter_spectrum",
    )(a, dc["m_fwd"][0], dc["m_fwd"][1])


FFT_SPECTRUM_BYTES = 8 * 1024 * 1024


def _long_conv_kernel(z_ref, gate_ref, h_ref, fa_ref, mf_ref, mi_ref, g_ref, bias_ref, o_ref, a_ref):
    n1, n2, wq = a_ref.shape[1], a_ref.shape[2], a_ref.shape[3]
    nr = z_ref.shape[1]
    s8 = SUBLANES
    fa = fa_ref[...]
    for jt in range(n2 // s8):
        tile = slice(jt * s8, (jt + 1) * s8)
        zt = z_ref[0, :, tile, :].reshape(nr * s8, wq).astype(BF16)
        res = _dot(fa, zt)
        a_ref[0, :, tile, :] = res[:n1 * s8].reshape(n1, s8, wq)
        a_ref[1, :, tile, :] = res[n1 * s8:].reshape(n1, s8, wq)

    def per_k1(k, carry):
        a = jnp.concatenate([a_ref[0, k], a_ref[1, k]], axis=0).astype(BF16)
        zz = _dot(mf_ref[k], a)
        zr, zi = zz[:n2], zz[n2:]
        hr, hi = h_ref[0, k], h_ref[1, k]
        y = jnp.concatenate([zr * hr - zi * hi, zr * hi + zi * hr], axis=0).astype(BF16)
        c = _dot(mi_ref[k], y)
        a_ref[0, k] = c[:n2]
        a_ref[1, k] = c[n2:]
        return carry

    lax.fori_loop(0, n1, per_k1, 0, unroll=8)
    g = g_ref[...]
    bias = bias_ref[...]
    for jt in range(n2 // s8):
        tile = slice(jt * s8, (jt + 1) * s8)
        c = jnp.concatenate([a_ref[0, :, tile, :].reshape(n1 * s8, wq),
                             a_ref[1, :, tile, :].reshape(n1 * s8, wq)], axis=0).astype(BF16)
        y = _dot(g, c).reshape(nr, s8, wq)
        o_ref[0, :, tile, :] = gate_ref[0, :, tile, :] * (y + z_ref[0, :, tile, :] * bias)


def _long_conv_gated(z, gate, hf, order, skip, dc):
    B, nr, N2, W = z.shape
    N1 = dc["N1"]
    wq = min(W, max(LANES, FFT_SPECTRUM_BYTES // (2 * N1 * N2 * 4) // LANES * LANES))
    nw = W // wq
    rows = pl.BlockSpec((1, nr, N2, wq), lambda w, b: (b, 0, 0, w))
    once = pl.Buffered(1)
    mats = pl.BlockSpec((N1, 2 * N2, 2 * N2), lambda w, b: (0, 0, 0), pipeline_mode=once)
    return pl.pallas_call(
        _long_conv_kernel,
        grid=(nw, B),
        in_specs=[rows, rows,
                  pl.BlockSpec((2, N1, N2, wq), lambda w, b: (0, 0, 0, order * nw + w), pipeline_mode=once),
                  pl.BlockSpec((2 * N1 * SUBLANES, nr * SUBLANES), lambda w, b: (0, 0)),
                  mats, mats,
                  pl.BlockSpec((nr * SUBLANES, 2 * N1 * SUBLANES), lambda w, b: (0, 0)),
                  pl.BlockSpec((1, wq), lambda w, b: (0, w))],
        out_specs=rows,
        out_shape=jax.ShapeDtypeStruct((B, nr, N2, W), F32),
        scratch_shapes=[pltpu.VMEM((2, N1, N2, wq), F32)],
        compiler_params=_cparams(("parallel", "parallel")),
        name="hyena_long_conv",
    )(z, gate, hf, dc["fa_tiles"], dc["m_fwd"][0], dc["m_inv"][0], dc["g_tiles"], skip[None, :])


HALO_ROWS = 16


def _short_conv_kernel(u_ref, prev_ref, next_ref, w_ref, b_ref, v_ref, x1_ref, x2_ref):
    i = pl.program_id(1)
    u = u_ref[0].astype(F32)
    tl = u.shape[0]
    row = lax.broadcasted_iota(jnp.int32, (tl, 1), 0)
    prev_last = prev_ref[0].astype(F32)[HALO_ROWS - 1:HALO_ROWS, :]
    next_first = next_ref[0].astype(F32)[0:1, :]
    before = jnp.where(i == 0, 0.0, prev_last)
    after = jnp.where(i == pl.num_programs(1) - 1, 0.0, next_first)
    up = jnp.where(row == 0, before, pltpu.roll(u, 1, 0))
    un = jnp.where(row == tl - 1, after, pltpu.roll(u, tl - 1, 0))
    w = w_ref[...]
    uc = up * w[0:1] + u * w[1:2] + un * w[2:3] + b_ref[...]
    W = HYENA_WIDTH
    v_ref[0] = uc[:, :W]
    x1_ref[0] = uc[:, W:2 * W]
    x2_ref[0] = uc[:, 2 * W:]


def _short_conv(hy, conv_w, conv_b, B, L):
    W3 = 3 * HYENA_WIDTH
    tl = _row_tile(L, 512)
    nl = L // tl
    u = hy.reshape(B, L, W3)
    rh = tl // HALO_ROWS
    spec_o = pl.BlockSpec((1, tl, HYENA_WIDTH), lambda b, i: (b, i, 0))
    return pl.pallas_call(
        _short_conv_kernel,
        grid=(B, nl),
        in_specs=[pl.BlockSpec((1, tl, W3), lambda b, i: (b, i, 0)),
                  pl.BlockSpec((1, HALO_ROWS, W3), lambda b, i: (b, jnp.maximum(i * rh - 1, 0), 0)),
                  pl.BlockSpec((1, HALO_ROWS, W3),
                               lambda b, i: (b, jnp.minimum((i + 1) * rh, L // HALO_ROWS - 1), 0)),
                  pl.BlockSpec((3, W3), lambda b, i: (0, 0)),
                  pl.BlockSpec((1, W3), lambda b, i: (0, 0))],
        out_specs=[spec_o, spec_o, spec_o],
        out_shape=[jax.ShapeDtypeStruct((B, L, HYENA_WIDTH), F32)] * 3,
        compiler_params=_cparams(("parallel", "parallel")),
        name="hyena_short_conv",
    )(u, u, u, conv_w, conv_b[None, :])


def _hyena(hy, hf, conv_w, conv_b, skip, dc, B, L):
    v, x1, x2 = _short_conv(hy, conv_w, conv_b, B, L)
    shape = (B, dc["N1"] // 2, dc["N2"], HYENA_WIDTH)
    z = _long_conv_gated(v.reshape(shape), x1.reshape(shape), hf, 0, skip[0], dc)
    y = _long_conv_gated(z, x2.reshape(shape), hf, 1, skip[1], dc)
    return y.reshape(B * L, HYENA_WIDTH)


def _merge_kernel(x_ref, oa_ref, oh_ref, hg_ref, oc_ref, gl_ref, hgn_ref, wb_ref, wo_ref, g_ref, b_ref,
                  out_ref):
    o = oh_ref[0] + oh_ref[1]
    parts = []
    for hd in range(HGRN_HEADS):
        oh = o[:, hd * HGRN_DIM:(hd + 1) * HGRN_DIM]
        ms = jnp.mean(oh * oh, axis=-1, keepdims=True)
        parts.append(oh * lax.rsqrt(ms + RMS_EPS))
    o_h = jnp.concatenate(parts, axis=-1) * hgn_ref[...] * _silu(hg_ref[...].astype(F32))
    gl = gl_ref[...].astype(F32)
    branches = (oa_ref[...], o_h, oc_ref[...])
    merged = None
    for j, ob in enumerate(branches):
        term = _sigmoid(gl[:, j * D_MODEL:(j + 1) * D_MODEL]) * _dot(ob.astype(BF16), wb_ref[j])
        merged = term if merged is None else merged + term
    mix = _dot(merged.astype(BF16), wo_ref[...])
    out_ref[...] = _layer_norm(ALPHA * x_ref[...] + mix, g_ref[...], b_ref[...])


def _merge(x2, o_a, o_dir, hg, o_c, gl, hgrn_gain, wb_bf16, wo_bf16, ln_g, ln_b):
    T = x2.shape[0]
    tm = _row_tile(T, 512)
    W = BRANCH_WIDTH
    row = lambda w: pl.BlockSpec((tm, w), lambda i: (i, 0))
    return pl.pallas_call(
        _merge_kernel,
        grid=(T // tm,),
        in_specs=[row(D_MODEL), row(W), pl.BlockSpec((2, tm, W), lambda i: (0, i, 0)), row(W), row(W),
                  row(N_BRANCH * D_MODEL),
                  pl.BlockSpec((1, W), lambda i: (0, 0)),
                  pl.BlockSpec((N_BRANCH, W, D_MODEL), lambda i: (0, 0, 0)),
                  pl.BlockSpec((D_MODEL, D_MODEL), lambda i: (0, 0)),
                  pl.BlockSpec((1, D_MODEL), lambda i: (0, 0)),
                  pl.BlockSpec((1, D_MODEL), lambda i: (0, 0))],
        out_specs=row(D_MODEL),
        out_shape=jax.ShapeDtypeStruct((T, D_MODEL), F32),
        compiler_params=_cparams(("parallel",)),
        name="merge_out_ln",
    )(x2, o_a, o_dir.reshape(2, T, W), hg, o_c, gl, jnp.tile(hgrn_gain, HGRN_HEADS)[None, :],
      wb_bf16, wo_bf16, ln_g[None, :], ln_b[None, :])


def _ffn_kernel(x_ref, wg_ref, wu_ref, wd_ref, g_ref, b_ref, out_ref, acc_ref):
    j = pl.program_id(1)
    x = x_ref[...]
    xb = x.astype(BF16)
    hmid = _silu(_dot(xb, wg_ref[...])) * _dot(xb, wu_ref[...])
    contrib = _dot(hmid.astype(BF16), wd_ref[...])

    @pl.when(j == 0)
    def _():
        acc_ref[...] = contrib

    @pl.when(j > 0)
    def _():
        acc_ref[...] += contrib

    @pl.when(j == pl.num_programs(1) - 1)
    def _():
        out_ref[...] = _layer_norm(ALPHA * x + acc_ref[...], g_ref[...], b_ref[...])


def _ffn_dense(x2, w_gu, w_down, ln_g, ln_b):
    T = x2.shape[0]
    tm = _row_tile(T, 512)
    ft = D_FF_DENSE
    nf = D_FF_DENSE // ft
    once = pl.Buffered(1)
    return pl.pallas_call(
        _ffn_kernel,
        grid=(T // tm, nf),
        in_specs=[pl.BlockSpec((tm, D_MODEL), lambda i, j: (i, 0)),
                  pl.BlockSpec((D_MODEL, ft), lambda i, j: (0, j), pipeline_mode=once),
                  pl.BlockSpec((D_MODEL, ft), lambda i, j: (0, j + nf), pipeline_mode=once),
                  pl.BlockSpec((ft, D_MODEL), lambda i, j: (j, 0), pipeline_mode=once),
                  pl.BlockSpec((1, D_MODEL), lambda i, j: (0, 0)),
                  pl.BlockSpec((1, D_MODEL), lambda i, j: (0, 0))],
        out_specs=pl.BlockSpec((tm, D_MODEL), lambda i, j: (i, 0)),
        out_shape=jax.ShapeDtypeStruct((T, D_MODEL), F32),
        scratch_shapes=[pltpu.VMEM((tm, D_MODEL), F32)],
        compiler_params=_cparams(("parallel", "arbitrary")),
        name="ffn_dense_ln",
    )(x2, w_gu, w_gu, w_down, ln_g[None, :], ln_b[None, :])


MOE_BLOCK = 1024
MOE_SUB = 64
MOE_BIG = 8
MOE_TILE = 256


def _route_kernel(x_ref, r_ref, col_ref, row_ref, s_ref):
    tb = x_ref.shape[0]
    logits = _dot3(x_ref[...], r_ref[...])
    lane = lax.broadcasted_iota(jnp.int32, logits.shape, 1)
    logits = jnp.where(lane < N_EXPERTS, logits, -jnp.inf)
    m1 = jnp.max(logits, axis=-1, keepdims=True)
    i1 = jnp.min(jnp.where(logits == m1, lane, LANES), axis=-1, keepdims=True)
    rest = jnp.where(lane == i1, -jnp.inf, logits)
    m2 = jnp.max(rest, axis=-1, keepdims=True)
    i2 = jnp.min(jnp.where(rest == m2, lane, LANES), axis=-1, keepdims=True)
    e2 = jnp.exp(m2 - m1)
    w1 = 1.0 / (1.0 + e2)
    w2 = e2 / (1.0 + e2)
    oh1 = lane == i1
    oh2 = lane == i2
    sel = jnp.where(oh1, 1.0, jnp.where(oh2, 1.0, 0.0))
    rr = lax.broadcasted_iota(jnp.int32, (tb, tb), 0)
    cc = lax.broadcasted_iota(jnp.int32, (tb, tb), 1)
    earlier = jnp.where(rr > cc, 1.0, 0.0).astype(BF16)
    rank = _dot(earlier, sel.astype(BF16))
    cnt = jnp.sum(sel, axis=0, keepdims=True)
    nsub = jnp.floor((cnt + (MOE_SUB - 1)) * (1.0 / MOE_SUB))
    r2 = lax.broadcasted_iota(jnp.int32, (LANES, LANES), 0)
    c2 = lax.broadcasted_iota(jnp.int32, (LANES, LANES), 1)
    before = jnp.where(r2 < c2, 1.0, 0.0).astype(BF16)
    nsub8 = jnp.broadcast_to(nsub, (8, LANES))
    off8 = _dot(nsub8.astype(BF16), before) * MOE_SUB
    base = off8[0:1] + rank
    d1 = jnp.sum(jnp.where(oh1, base, 0.0), axis=-1, keepdims=True)
    d2 = jnp.sum(jnp.where(oh2, base, 0.0), axis=-1, keepdims=True)
    col = jnp.where(lane == 0, d1, jnp.where(lane == 1, d2, jnp.where(lane == 2, w1, jnp.where(lane == 3, w2, 0.0))))
    col_ref[...] = col
    row_ref[0] = col.T[0:8, :]
    lane8 = lax.broadcasted_iota(jnp.int32, (8, LANES), 1)
    info = jnp.where(lane8 < N_EXPERTS, off8, pltpu.roll(nsub8, N_EXPERTS, 1))
    s_ref[0] = info.astype(jnp.int32)


def _moe_route(x2, router):
    T = x2.shape[0]
    tb = _row_tile(T, MOE_BLOCK)
    nb = T // tb
    r_pad = jnp.pad(router, ((0, 0), (0, LANES - N_EXPERTS)))
    return pl.pallas_call(
        _route_kernel,
        grid=(nb,),
        in_specs=[pl.BlockSpec((tb, D_MODEL), lambda i: (i, 0)),
                  pl.BlockSpec((D_MODEL, LANES), lambda i: (0, 0))],
        out_specs=[pl.BlockSpec((tb, LANES), lambda i: (i, 0)),
                   pl.BlockSpec((1, 8, tb), lambda i: (i, 0, 0)),
                   pl.BlockSpec((1, 8, LANES), lambda i: (i, 0, 0))],
        out_shape=[jax.ShapeDtypeStruct((T, LANES), F32),
                   jax.ShapeDtypeStruct((nb, 8, tb), F32),
                   jax.ShapeDtypeStruct((nb, 8, LANES), jnp.int32)],
        compiler_params=_cparams(("parallel",)),
        name="moe_route",
    )(x2, r_pad)


def _moe_kernel(s_ref, x_ref, col_ref, row_ref, wg_ref, wu_ref, wd_ref, g_ref, b_ref, out_ref,
                xs_ref, ys_ref):
    blk = pl.program_id(0)
    e = pl.program_id(1)
    j = pl.program_id(2)
    tb = x_ref.shape[0]
    n_tiles = xs_ref.shape[0] // MOE_TILE
    used = s_ref[blk, N_EXPERTS - 1] + s_ref[blk, 2 * N_EXPERTS - 1] * MOE_SUB
    d1r = row_ref[0, 0:1, :]
    d2r = row_ref[0, 1:2, :]

    @pl.when(jnp.logical_and(e == 0, j == 0))
    def _():
        xb = x_ref[...].astype(BF16)
        for rt in range(n_tiles):
            @pl.when(rt * MOE_TILE < used)
            def _():
                r = (lax.broadcasted_iota(jnp.int32, (MOE_TILE, tb), 0) + rt * MOE_TILE).astype(F32)
                p = jnp.where(r == d1r, 1.0, jnp.where(r == d2r, 1.0, 0.0)).astype(BF16)
                xs_ref[rt * MOE_TILE:(rt + 1) * MOE_TILE, :] = _dot(p, xb).astype(BF16)

    start = s_ref[blk, e]
    nsub = s_ref[blk, N_EXPERTS + e]

    def expert_rows(first, n_rows):
        rows = pl.ds(pl.multiple_of(first, MOE_SUB), n_rows)
        xt = xs_ref[rows, :]
        hmid = _silu(_dot(xt, wg_ref[0])) * _dot(xt, wu_ref[0])
        y = _dot(hmid.astype(BF16), wd_ref[0])

        @pl.when(j == 0)
        def _():
            ys_ref[rows, :] = y

        @pl.when(j > 0)
        def _():
            ys_ref[rows, :] += y

    n_big = nsub // MOE_BIG
    rem = nsub - MOE_BIG * n_big

    def big(i, carry):
        expert_rows(start + i * (MOE_BIG * MOE_SUB), MOE_BIG * MOE_SUB)
        return carry

    lax.fori_loop(0, n_big, big, 0)
    done = n_big * MOE_BIG
    part = MOE_BIG // 2
    while part >= 1:
        @pl.when((rem // part) % 2 == 1)
        def _(done=done, part=part):
            expert_rows(start + (done + (rem // (2 * part)) * (2 * part)) * MOE_SUB, part * MOE_SUB)
        part //= 2

    @pl.when(jnp.logical_and(e == pl.num_programs(1) - 1, j == pl.num_programs(2) - 1))
    def _():
        sub_per_tile = MOE_TILE // MOE_SUB
        filled = (used // MOE_SUB) % sub_per_tile
        for k in range(1, sub_per_tile):
            @pl.when(jnp.logical_and(filled > 0, filled <= k))
            def _(k=k):
                first = (used // MOE_TILE) * MOE_TILE + k * MOE_SUB
                ys_ref[pl.ds(pl.multiple_of(first, MOE_SUB), MOE_SUB), :] = jnp.zeros((MOE_SUB, D_MODEL), F32)

        w1r = row_ref[0, 2:3, :]
        w2r = row_ref[0, 3:4, :]
        for rt in range(n_tiles):
            @pl.when(rt * MOE_TILE < used)
            def _():
                r = (lax.broadcasted_iota(jnp.int32, (MOE_TILE, tb), 0) + rt * MOE_TILE).astype(F32)
                w_rows = jnp.sum(jnp.where(r == d1r, w1r, 0.0) + jnp.where(r == d2r, w2r, 0.0),
                                 axis=-1, keepdims=True)
                rows = slice(rt * MOE_TILE, (rt + 1) * MOE_TILE)
                xs_ref[rows, :] = (ys_ref[rows, :] * w_rows).astype(BF16)

        d1c = col_ref[:, 0:1]
        d2c = col_ref[:, 1:2]
        out_ref[...] = jnp.zeros(out_ref.shape, F32)
        for rt in range(n_tiles):
            @pl.when(rt * MOE_TILE < used)
            def _():
                c = (lax.broadcasted_iota(jnp.int32, (tb, MOE_TILE), 1) + rt * MOE_TILE).astype(F32)
                q = jnp.where(c == d1c, 1.0, jnp.where(c == d2c, 1.0, 0.0)).astype(BF16)
                out_ref[...] += _dot(q, xs_ref[rt * MOE_TILE:(rt + 1) * MOE_TILE, :])

        out_ref[...] = _layer_norm(ALPHA * x_ref[...] + out_ref[...], g_ref[...], b_ref[...])


def _ffn_moe(x2, router, w_gu, w_down, ln_g, ln_b):
    T = x2.shape[0]
    tb = _row_tile(T, MOE_BLOCK)
    nb = T // tb
    n_rows = 2 * tb + N_EXPERTS * MOE_SUB
    col, row, sinfo = _moe_route(x2, router)
    ft = 1792
    nf = D_FF_EXPERT // ft
    grid_spec = pltpu.PrefetchScalarGridSpec(
        num_scalar_prefetch=1,
        grid=(nb, N_EXPERTS, nf),
        in_specs=[pl.BlockSpec((tb, D_MODEL), lambda i, e, j, s: (i, 0), pipeline_mode=pl.Buffered(1)),
                  pl.BlockSpec((tb, LANES), lambda i, e, j, s: (i, 0)),
                  pl.BlockSpec((1, 8, tb), lambda i, e, j, s: (i, 0, 0)),
                  pl.BlockSpec((1, D_MODEL, ft), lambda i, e, j, s: (e, 0, j)),
                  pl.BlockSpec((1, D_MODEL, ft), lambda i, e, j, s: (e, 0, j + nf)),
                  pl.BlockSpec((1, ft, D_MODEL), lambda i, e, j, s: (e, j, 0)),
                  pl.BlockSpec((1, D_MODEL), lambda i, e, j, s: (0, 0)),
                  pl.BlockSpec((1, D_MODEL), lambda i, e, j, s: (0, 0))],
        out_specs=pl.BlockSpec((tb, D_MODEL), lambda i, e, j, s: (i, 0)),
        scratch_shapes=[pltpu.VMEM((n_rows, D_MODEL), BF16), pltpu.VMEM((n_rows, D_MODEL), F32)])
    return pl.pallas_call(
        _moe_kernel,
        grid_spec=grid_spec,
        out_shape=jax.ShapeDtypeStruct((T, D_MODEL), F32),
        compiler_params=_cparams(("parallel", "arbitrary", "arbitrary")),
        name="ffn_moe_ln",
    )(sinfo[:, 0, :], x2, col, row, w_gu, w_gu, w_down, ln_g[None, :], ln_b[None, :])


def _trunk(x, p):
    B, L, _ = x.shape
    cos_t, sin_t = _rope_tables(L)
    filt_consts = _filter_features(L)
    dc = _dft_consts(L)
    x2 = x.reshape(B * L, D_MODEL)
    for l in range(DEPTH):
        aq, akv, hq, hf, hi, hg, hy, gl = _in_proj(x2, p["w_in"][l])
        qn, kn, vn = _attn_prep(aq, akv, cos_t, sin_t, p["q_norm_g"][l], p["k_norm_g"][l], B, L)
        o_a = _flash_attention(qn, kn, vn, B, L)
        o_dir = _hgrn(hq, hf, hi, p["lower_bounds"][l], B, L)
        k2l = _hyena_filter(L, filt_consts, p["filt_w1"][l], p["filt_b1"][l], p["filt_freq"][l],
                            p["filt_w2"][l], p["filt_b2"][l], p["filt_w3"][l])
        spec = _filter_spectrum(k2l, dc)
        o_c = _hyena(hy, spec, p["conv_w"][l], p["conv_b"][l], p["hyena_skip"][l], dc, B, L)
        x2 = _merge(x2, o_a, o_dir, hg, o_c, gl, p["hgrn_norm_g"][l], p["w_branch"][l], p["w_out"][l],
                    p["ln1_g"][l], p["ln1_b"][l])
        if l % 2 == 0:
            x2 = _ffn_dense(x2, p["ffn_w_gu"][l // 2], p["ffn_w_down"][l // 2], p["ln2_g"][l], p["ln2_b"][l])
        else:
            x2 = _ffn_moe(x2, p["router_w"][l // 2], p["expert_w_gu"][l // 2], p["expert_w_down"][l // 2],
                          p["ln2_g"][l], p["ln2_b"][l])
    return x2.reshape(B, L, D_MODEL)


def kernel(x_prompt, x_sample, w_in, q_norm_g, k_norm_g, hgrn_lb, hgrn_norm_g, conv_w, conv_b, filt_w1, filt_b1, filt_freq, filt_w2, filt_b2, filt_w3, hyena_skip, w_branch, w_out, ln1_g, ln1_b, ln2_g, ln2_b, ffn_w_gu, ffn_w_down, router_w, expert_w_gu, expert_w_down):
    s = jax.nn.softmax(hgrn_lb.astype(F32), axis=0)
    p = dict(
        w_in=w_in.astype(BF16), q_norm_g=q_norm_g, k_norm_g=k_norm_g,
        lower_bounds=jnp.cumsum(s, axis=0) - s[:1], hgrn_norm_g=hgrn_norm_g,
        conv_w=conv_w, conv_b=conv_b, filt_w1=filt_w1, filt_b1=filt_b1, filt_freq=filt_freq,
        filt_w2=filt_w2, filt_b2=filt_b2, filt_w3=filt_w3, hyena_skip=hyena_skip,
        w_branch=w_branch.astype(BF16), w_out=w_out.astype(BF16),
        ln1_g=ln1_g, ln1_b=ln1_b, ln2_g=ln2_g, ln2_b=ln2_b,
        ffn_w_gu=ffn_w_gu.astype(BF16), ffn_w_down=ffn_w_down.astype(BF16), router_w=router_w,
        expert_w_gu=expert_w_gu.astype(BF16), expert_w_down=expert_w_down.astype(BF16))
    return (_trunk(x_prompt, p), _trunk(x_sample, p))
```

```python
import functools
import math

import numpy as np
import jax
import jax.numpy as jnp
from jax import lax
from jax.experimental import pallas as pl
from jax.experimental.pallas import tpu as pltpu

F32 = jnp.float32
BF16 = jnp.bfloat16

D_MODEL = 1024
DEPTH = 2
GRID_W = 64
BRANCH_WIDTH = 512
ATTN_HEADS = 8
ATTN_KV_HEADS = 2
ATTN_GROUP = ATTN_HEADS // ATTN_KV_HEADS
HEAD_DIM = 64
KV_WIDTH = ATTN_KV_HEADS * HEAD_DIM
ROPE_AXIS_DIM = HEAD_DIM // 2
ROPE_HALF = ROPE_AXIS_DIM // 2
ROPE_THETA = 10000.0
HGRN_HEADS = 4
HGRN_DIM = 128
HGRN_CHUNK = 64
FORGET_FLOOR = 1e-30
HYENA_WIDTH = BRANCH_WIDTH
HYENA_ORDER = 2
FILTER_EMB = 33
FILTER_BANDS = (FILTER_EMB - 1) // 2
FILTER_HIDDEN = 64
FILTER_FAST_DECAY = 0.3
FILTER_SLOW_DECAY = 1.5
FILTER_TARGET = 1e-2
N_FILTER_CH = HYENA_ORDER * 2 * HYENA_WIDTH
N_BRANCH = 3
D_FF_DENSE = 2816
N_EXPERTS = 8
D_FF_EXPERT = 3584
ALPHA = (2 * DEPTH) ** 0.25
LN_EPS = 1e-5
RMS_EPS = 1e-6

SEG_WIDTHS = (BRANCH_WIDTH,
              2 * KV_WIDTH,
              BRANCH_WIDTH,
              2 * BRANCH_WIDTH,
              BRANCH_WIDTH,
              BRANCH_WIDTH,
              3 * HYENA_WIDTH,
              N_BRANCH * D_MODEL)
IN_WIDTH = sum(SEG_WIDTHS)
SEG_DTYPES = (BF16, BF16, BF16, F32, BF16, BF16, BF16, BF16)

LANES = 128
SUBLANES = 8
FFT_N2 = 128
VMEM_LIMIT = 56 * 1024 * 1024


def _cparams(sem):
    return pltpu.CompilerParams(dimension_semantics=sem, vmem_limit_bytes=VMEM_LIMIT)


def _dot(a, b):
    return jnp.dot(a, b, preferred_element_type=F32)


def _dot_nt(a, b):
    return lax.dot_general(a, b, (((1,), (1,)), ((), ())), preferred_element_type=F32)


def _dot_tn(a, b):
    return lax.dot_general(a, b, (((0,), (0,)), ((), ())), preferred_element_type=F32)


def _split2(a):
    hi = a.astype(BF16)
    lo = (a - hi.astype(F32)).astype(BF16)
    return hi, lo


def _dot3(a, b):
    ah, al = _split2(a)
    bh, bl = _split2(b)
    return _dot(ah, bh) + _dot(al, bh) + _dot(ah, bl)


def _layer_norm(y, g, b):
    mu = jnp.mean(y, axis=-1, keepdims=True)
    yc = y - mu
    var = jnp.mean(yc * yc, axis=-1, keepdims=True)
    return yc * lax.rsqrt(var + LN_EPS) * g + b


def _sigmoid(x):
    return 1.0 / (1.0 + jnp.exp(-x))


def _silu(x):
    return x * _sigmoid(x)


def _row_tile(n, want):
    t = min(n, want)
    assert n % t == 0
    return t


def _in_proj_kernel(x_ref, w_ref, *out_refs):
    x = x_ref[...].astype(BF16)
    off = 0
    for o_ref, width in zip(out_refs, SEG_WIDTHS):
        o_ref[...] = _dot(x, w_ref[:, off:off + width]).astype(o_ref.dtype)
        off += width


def _in_proj(x2, w_bf16):
    T = x2.shape[0]
    tm = _row_tile(T, 512)
    return pl.pallas_call(
        _in_proj_kernel,
        grid=(T // tm,),
        in_specs=[pl.BlockSpec((tm, D_MODEL), lambda i: (i, 0)),
                  pl.BlockSpec((D_MODEL, IN_WIDTH), lambda i: (0, 0), pipeline_mode=pl.Buffered(1))],
        out_specs=[pl.BlockSpec((tm, w), lambda i: (i, 0)) for w in SEG_WIDTHS],
        out_shape=[jax.ShapeDtypeStruct((T, w), dt) for w, dt in zip(SEG_WIDTHS, SEG_DTYPES)],
        compiler_params=_cparams(("parallel",)),
        name="in_proj",
    )(x2, w_bf16)


def _rope_tables(L):
    rows = L // GRID_W
    row = jnp.broadcast_to(jnp.arange(rows, dtype=F32)[:, None], (rows, GRID_W)).reshape(-1)
    col = jnp.broadcast_to(jnp.arange(GRID_W, dtype=F32)[None, :], (rows, GRID_W)).reshape(-1)
    inv = jnp.power(ROPE_THETA, -2.0 * jnp.arange(ROPE_HALF, dtype=F32) / ROPE_AXIS_DIM)
    a0, a1 = row[:, None] * inv, col[:, None] * inv
    c0, s0, c1, s1 = jnp.cos(a0), jnp.sin(a0), jnp.cos(a1), jnp.sin(a1)
    cos_h = jnp.concatenate([c0, c0, c1, c1], axis=-1)
    sin_h = jnp.concatenate([-s0, s0, -s1, s1], axis=-1)
    return jnp.tile(cos_h, (1, ATTN_HEADS)), jnp.tile(sin_h, (1, ATTN_HEADS))


def _head_block_ones(width):
    idx = np.arange(width) // HEAD_DIM
    return jnp.asarray(idx[:, None] == idx[None, :], dtype=BF16)


def _norm_rope(x, gain, cos, sin, ones_bd):
    width = x.shape[-1]
    sq_hi, sq_lo = _split2(x * x)
    ms = (_dot(sq_hi, ones_bd) + _dot(sq_lo, ones_bd)) * (1.0 / HEAD_DIM)
    xn = x * lax.rsqrt(ms + RMS_EPS) * gain
    lane = lax.broadcasted_iota(jnp.int32, xn.shape, 1)
    first_half = (lane % ROPE_AXIS_DIM) < ROPE_HALF
    swapped = jnp.where(first_half, pltpu.roll(xn, width - ROPE_HALF, 1), pltpu.roll(xn, ROPE_HALF, 1))
    return xn * cos + swapped * sin


LOG2E = 1.4426950408889634
V_EXT = 2 * HEAD_DIM
FLASH_SKEW = 4


def _attn_prep_kernel(aq_ref, akv_ref, cos_ref, sin_ref, qg_ref, kg_ref, bdq_ref, bdk_ref,
                      q_out, kt_out, v_out):
    cos, sin = cos_ref[...], sin_ref[...]
    q = _norm_rope(aq_ref[...].astype(F32), qg_ref[...], cos, sin, bdq_ref[...])
    q_out[...] = (q * (LOG2E * HEAD_DIM ** -0.5)).astype(BF16)
    kv = akv_ref[...].astype(F32)
    k = _norm_rope(kv[:, :KV_WIDTH], kg_ref[...], cos[:, :KV_WIDTH], sin[:, :KV_WIDTH], bdk_ref[...])
    kt_out[0] = k.T.astype(BF16)
    ones = jnp.ones((kv.shape[0], HEAD_DIM), F32)
    v_ext = [kv[:, KV_WIDTH + h * HEAD_DIM:KV_WIDTH + (h + 1) * HEAD_DIM] if part == 0 else ones
             for h in range(ATTN_KV_HEADS) for part in range(2)]
    v_out[...] = jnp.concatenate(v_ext, axis=-1).astype(BF16)


def _attn_prep(aq, akv, cos_t, sin_t, q_gain, k_gain, B, L):
    T = aq.shape[0]
    tm = _row_tile(L, 512)
    nl = L // tm
    qg = jnp.tile(q_gain, ATTN_HEADS)[None, :]
    kg = jnp.tile(k_gain, ATTN_KV_HEADS)[None, :]
    const = lambda i: (0, 0)
    return pl.pallas_call(
        _attn_prep_kernel,
        grid=(T // tm,),
        in_specs=[pl.BlockSpec((tm, BRANCH_WIDTH), lambda i: (i, 0)),
                  pl.BlockSpec((tm, 2 * KV_WIDTH), lambda i: (i, 0)),
                  pl.BlockSpec((tm, BRANCH_WIDTH), lambda i: (i % nl, 0)),
                  pl.BlockSpec((tm, BRANCH_WIDTH), lambda i: (i % nl, 0)),
                  pl.BlockSpec((1, BRANCH_WIDTH), const),
                  pl.BlockSpec((1, KV_WIDTH), const),
                  pl.BlockSpec((BRANCH_WIDTH, BRANCH_WIDTH), const),
                  pl.BlockSpec((KV_WIDTH, KV_WIDTH), const)],
        out_specs=[pl.BlockSpec((tm, BRANCH_WIDTH), lambda i: (i, 0)),
                   pl.BlockSpec((1, KV_WIDTH, tm), lambda i: (i // nl, 0, i % nl)),
                   pl.BlockSpec((tm, ATTN_KV_HEADS * V_EXT), lambda i: (i, 0))],
        out_shape=[jax.ShapeDtypeStruct((T, BRANCH_WIDTH), BF16),
                   jax.ShapeDtypeStruct((B, KV_WIDTH, L), BF16),
                   jax.ShapeDtypeStruct((T, ATTN_KV_HEADS * V_EXT), BF16)],
        compiler_params=_cparams(("parallel",)),
        name="attn_prep",
    )(aq, akv, cos_t, sin_t, qg, kg, _head_block_ones(BRANCH_WIDTH), _head_block_ones(KV_WIDTH))


def _flash_kernel(q_ref, kt_ref, v_ref, o_ref, qs_ref, m_ref, acc_ref, *, tq, rs):
    kv = pl.program_id(2)

    @pl.when(kv == 0)
    def _():
        for h in range(ATTN_HEADS):
            g = h % ATTN_GROUP
            qs_ref[h // ATTN_GROUP, g * tq:(g + 1) * tq, :] = q_ref[0, :, h * HEAD_DIM:(h + 1) * HEAD_DIM]
        m_ref[...] = jnp.full(m_ref.shape, -jnp.inf, F32)
        acc_ref[...] = jnp.zeros(acc_ref.shape, F32)

    kts = [kt_ref[0, kh * HEAD_DIM:(kh + 1) * HEAD_DIM, :] for kh in range(ATTN_KV_HEADS)]
    vs = [v_ref[0, :, kh * V_EXT:(kh + 1) * V_EXT] for kh in range(ATTN_KV_HEADS)]
    tasks = [(kh, slice(sb * rs, (sb + 1) * rs))
             for sb in range(ATTN_GROUP * tq // rs) for kh in range(ATTN_KV_HEADS)]

    def scores(t):
        kh, rows = tasks[t]
        return _dot(qs_ref[kh, rows, :], kts[kh])

    pending = {t: scores(t) for t in range(min(FLASH_SKEW, len(tasks)))}
    for t, (kh, rows) in enumerate(tasks):
        s = pending.pop(t)
        if t + FLASH_SKEW < len(tasks):
            pending[t + FLASH_SKEW] = scores(t + FLASH_SKEW)
        m_prev = m_ref[kh, rows, :]
        m_new = jnp.maximum(m_prev, jnp.max(s, axis=-1, keepdims=True))
        p = jnp.exp2(s - m_new[:, 0:1])
        acc_ref[kh, rows, :] = jnp.exp2(m_prev - m_new) * acc_ref[kh, rows, :] + _dot(p.astype(BF16), vs[kh])
        m_ref[kh, rows, :] = m_new

    @pl.when(kv == pl.num_programs(2) - 1)
    def _():
        for h in range(ATTN_HEADS):
            kh, g = h // ATTN_GROUP, h % ATTN_GROUP
            a = acc_ref[kh, g * tq:(g + 1) * tq, :]
            o_ref[0, :, h * HEAD_DIM:(h + 1) * HEAD_DIM] = a[:, :HEAD_DIM] / a[:, HEAD_DIM:]


def _flash_attention(qn, kt, v_ext, B, L):
    tq = _row_tile(L, 256)
    tk = _row_tile(L, 1024)
    rs = min(128, ATTN_GROUP * tq)
    out = pl.pallas_call(
        functools.partial(_flash_kernel, tq=tq, rs=rs),
        grid=(B, L // tq, L // tk),
        in_specs=[pl.BlockSpec((1, tq, BRANCH_WIDTH), lambda b, i, j: (b, i, 0)),
                  pl.BlockSpec((1, KV_WIDTH, tk), lambda b, i, j: (b, 0, j)),
                  pl.BlockSpec((1, tk, ATTN_KV_HEADS * V_EXT), lambda b, i, j: (b, j, 0))],
        out_specs=pl.BlockSpec((1, tq, BRANCH_WIDTH), lambda b, i, j: (b, i, 0)),
        out_shape=jax.ShapeDtypeStruct((B, L, BRANCH_WIDTH), F32),
        scratch_shapes=[pltpu.VMEM((ATTN_KV_HEADS, ATTN_GROUP * tq, HEAD_DIM), BF16),
                        pltpu.VMEM((ATTN_KV_HEADS, ATTN_GROUP * tq, LANES), F32),
                        pltpu.VMEM((ATTN_KV_HEADS, ATTN_GROUP * tq, V_EXT), F32)],
        compiler_params=_cparams(("parallel", "parallel", "arbitrary")),
        name="flash_attention",
    )(qn.reshape(B, L, BRANCH_WIDTH), kt, v_ext.reshape(B, L, ATTN_KV_HEADS * V_EXT))
    return out.reshape(B * L, BRANCH_WIDTH)


def _ref_rows(b, h, fwd):
    C, W = b.shape
    two_h = 2 * h
    if two_h >= 8:
        n = C // two_h
        b3 = b.reshape(n, two_h, W)
        r = jnp.where(fwd, b3[:, h - 1:h, :], b3[:, h:h + 1, :])
        return jnp.broadcast_to(r, (n, two_h, W)).reshape(C, W)
    b3 = b.reshape(C // 8, 8, W)
    pos8 = lax.broadcasted_iota(jnp.int32, (1, 8, 1), 1)
    out = None
    for blk in range(8 // two_h):
        base = blk * two_h
        r = jnp.where(fwd, b3[:, base + h - 1:base + h, :], b3[:, base + h:base + h + 1, :])
        r = jnp.broadcast_to(r, b3.shape)
        out = r if out is None else jnp.where((pos8 // two_h) == blk, r, out)
    return out.reshape(C, W)


def _hgrn_chunk(q, z, v, lb, tri, st_ref, fwd, head0):
    C, W = q.shape
    q = q.astype(F32)
    v = v.astype(F32)
    e = jnp.exp(-jnp.abs(z))
    s_big = 1.0 / (1.0 + e)
    s_small = e * s_big
    pos = z >= 0.0
    f = lb + (1.0 - lb) * jnp.where(pos, s_big, s_small)
    g = jnp.log2(jnp.maximum(f, FORGET_FLOOR))
    kk = (1.0 - lb) * jnp.where(pos, s_small, s_big)
    g_hi, g_lo = _split2(g)
    b = _dot(tri, g_hi) + _dot(tri, g_lo)
    b_end = jnp.where(fwd, b[C - 1:C, :], b[0:1, :])
    q_in = (q * jnp.exp2(b)).astype(BF16)
    k_out = (kk * jnp.exp2(b_end - b)).astype(BF16)
    s_decay = jnp.exp2(b_end)
    vb = v.astype(BF16)

    row = lax.broadcasted_iota(jnp.int32, (C, 1), 0)
    rr = lax.broadcasted_iota(jnp.int32, (C, C), 0)
    cc = lax.broadcasted_iota(jnp.int32, (C, C), 1)
    n_heads = W // HGRN_DIM
    a_mats = [jnp.zeros((C, C), F32) for _ in range(n_heads)]
    h = C // 2
    while h >= 1:
        upper = ((row % (2 * h)) >= h).astype(jnp.int32)
        q_rows = upper == fwd.astype(jnp.int32)
        ref = _ref_rows(b, h, fwd)
        ql = jnp.where(q_rows, q * jnp.exp2(b - ref), 0.0).astype(BF16)
        kl = jnp.where(q_rows, 0.0, kk * jnp.exp2(ref - b)).astype(BF16)
        level = (jnp.bitwise_xor(rr, cc) // h) == 1
        for hd in range(n_heads):
            sl = slice(hd * HGRN_DIM, (hd + 1) * HGRN_DIM)
            a_mats[hd] = jnp.where(level, _dot_nt(ql[:, sl], kl[:, sl]), a_mats[hd])
        h //= 2

    qk = q * kk
    outs = []
    for hd in range(n_heads):
        sl = slice(hd * HGRN_DIM, (hd + 1) * HGRN_DIM)
        st = st_ref[head0 + hd]
        o = _dot_nt(q_in[:, sl], st.astype(BF16))
        o = o + _dot(a_mats[hd].astype(BF16), vb[:, sl])
        o = o + jnp.sum(qk[:, sl], axis=-1, keepdims=True) * v[:, sl]
        outs.append(o)
        st_ref[head0 + hd] = st * s_decay[:, sl] + _dot_tn(vb[:, sl], k_out[:, sl])
    return jnp.concatenate(outs, axis=-1)


HGRN_GROUP = 4


def _hgrn_kernel(q_ref, f_ref, v_ref, lb_ref, tri_ref, o_ref, st_ref, *, n_inner):
    d = pl.program_id(0)
    fwd = d == 0

    @pl.when(pl.program_id(2) == 0)
    def _():
        st_ref[...] = jnp.zeros(st_ref.shape, F32)

    lb = lb_ref[0]
    tri = tri_ref[0]

    def body(i, carry):
        ci = jnp.where(fwd, i, n_inner - 1 - i)
        r0 = pl.multiple_of(ci * HGRN_CHUNK, HGRN_CHUNK)
        rows = pl.ds(r0, HGRN_CHUNK)
        for grp in range(HGRN_HEADS // HGRN_GROUP):
            cols = slice(grp * HGRN_GROUP * HGRN_DIM, (grp + 1) * HGRN_GROUP * HGRN_DIM)
            o_ref[0, 0, rows, cols] = _hgrn_chunk(q_ref[0, rows, cols], f_ref[0, rows, cols],
                                                  v_ref[0, rows, cols], lb[:, cols], tri, st_ref, fwd,
                                                  grp * HGRN_GROUP)
        return carry

    lax.fori_loop(0, n_inner, body, 0, unroll=8)


def _hgrn(hq, hf, hi, lb, B, L):
    W = BRANCH_WIDTH
    rb = _row_tile(L, 512)
    nb = L // rb
    C = HGRN_CHUNK
    tri_f = np.tril(np.ones((C, C), np.float32))
    tri = jnp.asarray(np.stack([tri_f, tri_f.T]), dtype=BF16)
    blk = lambda d, b, c: (b, c + d * (nb - 1 - 2 * c), 0)
    return pl.pallas_call(
        functools.partial(_hgrn_kernel, n_inner=rb // C),
        grid=(2, B, nb),
        in_specs=[pl.BlockSpec((1, rb, W), blk),
                  pl.BlockSpec((1, rb, W), lambda d, b, c: (b, c + d * (nb - 1 - 2 * c), d)),
                  pl.BlockSpec((1, rb, W), blk),
                  pl.BlockSpec((1, 1, W), lambda d, b, c: (d, 0, 0)),
                  pl.BlockSpec((1, C, C), lambda d, b, c: (d, 0, 0))],
        out_specs=pl.BlockSpec((1, 1, rb, W), lambda d, b, c: (d, b, c + d * (nb - 1 - 2 * c), 0)),
        out_shape=jax.ShapeDtypeStruct((2, B, L, W), F32),
        scratch_shapes=[pltpu.VMEM((HGRN_HEADS, HGRN_DIM, HGRN_DIM), F32)],
        compiler_params=_cparams(("parallel", "parallel", "arbitrary")),
        name="hgrn_scan",
    )(hq.reshape(B, L, W), hf.reshape(B, L, 2 * W), hi.reshape(B, L, W), lb[:, None, :], tri)


def _filter_features(L):
    t = jnp.linspace(0.0, 1.0, L, dtype=F32)[:, None]
    w = 2.0 * math.pi * jnp.arange(L, dtype=F32) / L
    bands = jnp.linspace(1e-4, FILTER_BANDS - 1, FILTER_BANDS, dtype=F32)
    ang = w[:, None] * bands[None, :]
    feats = jnp.concatenate([t, jnp.cos(ang), -jnp.sin(ang)], axis=-1)
    feats2 = jnp.concatenate([feats, feats[:1], jnp.flip(feats[1:], axis=0)], axis=0)
    t2 = jnp.concatenate([t, t[:1], jnp.flip(t[1:], axis=0)], axis=0)
    deltas = jnp.abs(jnp.linspace(math.log(FILTER_TARGET) / FILTER_FAST_DECAY,
                                  math.log(FILTER_TARGET) / FILTER_SLOW_DECAY, N_FILTER_CH, dtype=F32))
    return feats2, t2, deltas[None, :]


def _filter_kernel(feat_ref, t_ref, w1_ref, b1_ref, fr_ref, w2_ref, b2_ref, w3f_ref, w3b_ref,
                   df_ref, db_ref, k_ref, h_ref, *, L):
    @pl.when(pl.program_id(0) == 0)
    def _():
        freq = fr_ref[...]
        h1 = jnp.sin(freq * (_dot3(feat_ref[...], w1_ref[...]) + b1_ref[...]))
        h_ref[...] = jnp.sin(freq * (_dot3(h1, w2_ref[...]) + b2_ref[...]))

    h = h_ref[...]
    t = t_ref[...]
    top = _dot3(h[:L], w3f_ref[...]) * jnp.exp(-t[:L] * df_ref[...])
    bot = _dot3(h[L:], w3b_ref[...]) * jnp.exp(-t[L:] * db_ref[...])
    row = lax.broadcasted_iota(jnp.int32, (L, 1), 0)
    bot = jnp.where(row == 0, 0.0, bot)
    norm = jnp.sum(jnp.abs(top), axis=0, keepdims=True) + jnp.sum(jnp.abs(bot), axis=0, keepdims=True)
    k_ref[:L, :] = top / norm
    k_ref[L:, :] = bot / norm


def _hyena_filter(L, consts, w1, b1, freq, w2, b2, w3):
    feats2, t2, deltas = consts
    wb = 256
    nw = HYENA_WIDTH // wb
    full = lambda shape: pl.BlockSpec(shape, lambda c: (0,) * len(shape))
    fcol = lambda c: (0, (c // nw) * 2 * nw + c % nw)
    bcol = lambda c: (0, (c // nw) * 2 * nw + nw + c % nw)
    return pl.pallas_call(
        functools.partial(_filter_kernel, L=L),
        grid=(HYENA_ORDER * nw,),
        in_specs=[full((2 * L, FILTER_EMB)), full((2 * L, 1)),
                  full((FILTER_EMB, FILTER_HIDDEN)), full((1, FILTER_HIDDEN)), full((1, FILTER_HIDDEN)),
                  full((FILTER_HIDDEN, FILTER_HIDDEN)), full((1, FILTER_HIDDEN)),
                  pl.BlockSpec((FILTER_HIDDEN, wb), fcol), pl.BlockSpec((FILTER_HIDDEN, wb), bcol),
                  pl.BlockSpec((1, wb), fcol), pl.BlockSpec((1, wb), bcol)],
        out_specs=pl.BlockSpec((2 * L, wb), lambda c: (0, c)),
        out_shape=jax.ShapeDtypeStruct((2 * L, HYENA_ORDER * HYENA_WIDTH), F32),
        scratch_shapes=[pltpu.VMEM((2 * L, FILTER_HIDDEN), F32)],
        compiler_params=_cparams(("arbitrary",)),
        name="hyena_filter",
    )(feats2, t2, w1, b1[None, :], freq[None, :], w2, b2[None, :], w3, w3, deltas, deltas)


def _dft_consts(L):
    N = 2 * L
    N2 = FFT_N2
    N1 = N // N2
    k1 = np.arange(N1)
    ang_a = 2.0 * np.pi * np.outer(k1, np.arange(N1)) / N1
    fa_full = np.concatenate([np.cos(ang_a), -np.sin(ang_a)], axis=0)
    fa_half = fa_full[:, :N1 // 2]
    n2 = np.arange(N2)
    kk = k1[:, None, None] + N1 * np.arange(N2)[None, :, None]
    ang_b = 2.0 * np.pi * (kk * n2[None, None, :]) / N
    mr, mi = np.cos(ang_b), -np.sin(ang_b)
    m_fwd = np.concatenate([np.concatenate([mr, -mi], axis=2),
                            np.concatenate([mi, mr], axis=2)], axis=1)
    wr, wi = np.swapaxes(mr, 1, 2), -np.swapaxes(mi, 1, 2)
    m_inv = np.concatenate([np.concatenate([wr, -wi], axis=2),
                            np.concatenate([wi, wr], axis=2)], axis=1)
    ang_c = 2.0 * np.pi * np.outer(np.arange(N1 // 2), k1) / N1
    g_inv = np.concatenate([np.cos(ang_c), -np.sin(ang_c)], axis=1) / N

    def hi_lo(a):
        a = jnp.asarray(a, dtype=F32)
        hi = a.astype(BF16)
        return hi, (a - hi.astype(F32)).astype(BF16)

    eye = np.eye(SUBLANES)
    fa_tiles = jnp.asarray(np.kron(fa_half, eye), dtype=BF16)
    g_tiles = jnp.asarray(np.kron(g_inv, eye), dtype=BF16)
    return dict(N1=N1, N2=N2, fa_full=hi_lo(fa_full), fa_tiles=fa_tiles, g_tiles=g_tiles,
                m_fwd=hi_lo(m_fwd), m_inv=hi_lo(m_inv))


def _mm3(m_hi, m_lo, x):
    x_hi, x_lo = _split2(x)
    return _dot(m_hi, x_hi) + _dot(m_hi, x_lo) + _dot(m_lo, x_hi)


def _stage_a_kernel(z_ref, fh_ref, fl_ref, a_ref):
    n1 = a_ref.shape[2]
    res = _mm3(fh_ref[...], fl_ref[...], z_ref[0])
    a_ref[0, 0] = res[:n1]
    a_ref[0, 1] = res[n1:]


def _fft_stage_a(z, fa, N1):
    B, nr, cols = z.shape
    cb = min(cols, 4096)
    return pl.pallas_call(
        _stage_a_kernel,
        grid=(B, cols // cb),
        in_specs=[pl.BlockSpec((1, nr, cb), lambda b, j: (b, 0, j)),
                  pl.BlockSpec((2 * N1, nr), lambda b, j: (0, 0)),
                  pl.BlockSpec((2 * N1, nr), lambda b, j: (0, 0))],
        out_specs=pl.BlockSpec((1, 2, N1, cb), lambda b, j: (b, 0, 0, j)),
        out_shape=jax.ShapeDtypeStruct((B, 2, N1, cols), F32),
        compiler_params=_cparams(("parallel", "parallel")),
        name="fft_stage_a",
    )(z, fa[0], fa[1])


def _stage_b_filter_kernel(a_ref, mh_ref, ml_ref, o_ref):
    n2 = a_ref.shape[3]
    a = a_ref[0, :, 0].reshape(2 * n2, a_ref.shape[4])
    res = _mm3(mh_ref[0], ml_ref[0], a)
    o_ref[0, 0] = res[:n2]
    o_ref[1, 0] = res[n2:]


def _filter_spectrum(k2l, dc):
    N1, N2 = dc["N1"], dc["N2"]
    OW = k2l.shape[1]
    a = _fft_stage_a(k2l.reshape(1, N1, N2 * OW), dc["fa_full"], N1)
    a = a.reshape(1, 2, N1, N2, OW)
    wb = 512
    return pl.pallas_call(
        _stage_b_filter_kernel,
        grid=(N1, OW // wb),
        in_specs=[pl.BlockSpec((1, 2, 1, N2, wb), lambda k, j: (0, 0, k, 0, j)),
                  pl.BlockSpec((1, 2 * N2, 2 * N2), lambda k, j: (k, 0, 0)),
                  pl.BlockSpec((1, 2 * N2, 2 * N2), lambda k, j: (k, 0, 0))],
        out_specs=pl.BlockSpec((2, 1, N2, wb), lambda k, j: (0, k, 0, j)),
        out_shape=jax.ShapeDtypeStruct((2, N1, N2, OW), F32),
        compiler_params=_cparams(("parallel", "parallel")),
        name="fft_filter_spectrum",
    )(a, dc["m_fwd"][0], dc["m_fwd"][1])


FFT_SPECTRUM_BYTES = 8 * 1024 * 1024


def _long_conv_kernel(z_ref, gate_ref, h_ref, fa_ref, mf_ref, mi_ref, g_ref, bias_ref, o_ref, a_ref):
    n1, n2, wq = a_ref.shape[1], a_ref.shape[2], a_ref.shape[3]
    nr = z_ref.shape[1]
    s8 = SUBLANES
    fa = fa_ref[...]
    for jt in range(n2 // s8):
        tile = slice(jt * s8, (jt + 1) * s8)
        zt = z_ref[0, :, tile, :].reshape(nr * s8, wq).astype(BF16)
        res = _dot(fa, zt)
        a_ref[0, :, tile, :] = res[:n1 * s8].reshape(n1, s8, wq)
        a_ref[1, :, tile, :] = res[n1 * s8:].reshape(n1, s8, wq)

    def per_k1(k, carry):
        a = jnp.concatenate([a_ref[0, k], a_ref[1, k]], axis=0).astype(BF16)
        zz = _dot(mf_ref[k], a)
        zr, zi = zz[:n2], zz[n2:]
        hr, hi = h_ref[0, k], h_ref[1, k]
        y = jnp.concatenate([zr * hr - zi * hi, zr * hi + zi * hr], axis=0).astype(BF16)
        c = _dot(mi_ref[k], y)
        a_ref[0, k] = c[:n2]
        a_ref[1, k] = c[n2:]
        return carry

    lax.fori_loop(0, n1, per_k1, 0, unroll=8)
    g = g_ref[...]
    bias = bias_ref[...]
    for jt in range(n2 // s8):
        tile = slice(jt * s8, (jt + 1) * s8)
        c = jnp.concatenate([a_ref[0, :, tile, :].reshape(n1 * s8, wq),
                             a_ref[1, :, tile, :].reshape(n1 * s8, wq)], axis=0).astype(BF16)
        y = _dot(g, c).reshape(nr, s8, wq)
        o_ref[0, :, tile, :] = gate_ref[0, :, tile, :] * (y + z_ref[0, :, tile, :] * bias)


def _long_conv_gated(z, gate, hf, order, skip, dc):
    B, nr, N2, W = z.shape
    N1 = dc["N1"]
    wq = min(W, max(LANES, FFT_SPECTRUM_BYTES // (2 * N1 * N2 * 4) // LANES * LANES))
    nw = W // wq
    rows = pl.BlockSpec((1, nr, N2, wq), lambda w, b: (b, 0, 0, w))
    once = pl.Buffered(1)
    mats = pl.BlockSpec((N1, 2 * N2, 2 * N2), lambda w, b: (0, 0, 0), pipeline_mode=once)
    return pl.pallas_call(
        _long_conv_kernel,
        grid=(nw, B),
        in_specs=[rows, rows,
                  pl.BlockSpec((2, N1, N2, wq), lambda w, b: (0, 0, 0, order * nw + w), pipeline_mode=once),
                  pl.BlockSpec((2 * N1 * SUBLANES, nr * SUBLANES), lambda w, b: (0, 0)),
                  mats, mats,
                  pl.BlockSpec((nr * SUBLANES, 2 * N1 * SUBLANES), lambda w, b: (0, 0)),
                  pl.BlockSpec((1, wq), lambda w, b: (0, w))],
        out_specs=rows,
        out_shape=jax.ShapeDtypeStruct((B, nr, N2, W), F32),
        scratch_shapes=[pltpu.VMEM((2, N1, N2, wq), F32)],
        compiler_params=_cparams(("parallel", "parallel")),
        name="hyena_long_conv",
    )(z, gate, hf, dc["fa_tiles"], dc["m_fwd"][0], dc["m_inv"][0], dc["g_tiles"], skip[None, :])


HALO_ROWS = 16


def _short_conv_kernel(u_ref, prev_ref, next_ref, w_ref, b_ref, v_ref, x1_ref, x2_ref):
    i = pl.program_id(1)
    u = u_ref[0].astype(F32)
    tl = u.shape[0]
    row = lax.broadcasted_iota(jnp.int32, (tl, 1), 0)
    prev_last = prev_ref[0].astype(F32)[HALO_ROWS - 1:HALO_ROWS, :]
    next_first = next_ref[0].astype(F32)[0:1, :]
    before = jnp.where(i == 0, 0.0, prev_last)
    after = jnp.where(i == pl.num_programs(1) - 1, 0.0, next_first)
    up = jnp.where(row == 0, before, pltpu.roll(u, 1, 0))
    un = jnp.where(row == tl - 1, after, pltpu.roll(u, tl - 1, 0))
    w = w_ref[...]
    uc = up * w[0:1] + u * w[1:2] + un * w[2:3] + b_ref[...]
    W = HYENA_WIDTH
    v_ref[0] = uc[:, :W]
    x1_ref[0] = uc[:, W:2 * W]
    x2_ref[0] = uc[:, 2 * W:]


def _short_conv(hy, conv_w, conv_b, B, L):
    W3 = 3 * HYENA_WIDTH
    tl = _row_tile(L, 512)
    nl = L // tl
    u = hy.reshape(B, L, W3)
    rh = tl // HALO_ROWS
    spec_o = pl.BlockSpec((1, tl, HYENA_WIDTH), lambda b, i: (b, i, 0))
    return pl.pallas_call(
        _short_conv_kernel,
        grid=(B, nl),
        in_specs=[pl.BlockSpec((1, tl, W3), lambda b, i: (b, i, 0)),
                  pl.BlockSpec((1, HALO_ROWS, W3), lambda b, i: (b, jnp.maximum(i * rh - 1, 0), 0)),
                  pl.BlockSpec((1, HALO_ROWS, W3),
                               lambda b, i: (b, jnp.minimum((i + 1) * rh, L // HALO_ROWS - 1), 0)),
                  pl.BlockSpec((3, W3), lambda b, i: (0, 0)),
                  pl.BlockSpec((1, W3), lambda b, i: (0, 0))],
        out_specs=[spec_o, spec_o, spec_o],
        out_shape=[jax.ShapeDtypeStruct((B, L, HYENA_WIDTH), F32)] * 3,
        compiler_params=_cparams(("parallel", "parallel")),
        name="hyena_short_conv",
    )(u, u, u, conv_w, conv_b[None, :])


def _hyena(hy, hf, conv_w, conv_b, skip, dc, B, L):
    v, x1, x2 = _short_conv(hy, conv_w, conv_b, B, L)
    shape = (B, dc["N1"] // 2, dc["N2"], HYENA_WIDTH)
    z = _long_conv_gated(v.reshape(shape), x1.reshape(shape), hf, 0, skip[0], dc)
    y = _long_conv_gated(z, x2.reshape(shape), hf, 1, skip[1], dc)
    return y.reshape(B * L, HYENA_WIDTH)


def _merge_kernel(x_ref, oa_ref, oh_ref, hg_ref, oc_ref, gl_ref, hgn_ref, wb_ref, wo_ref, g_ref, b_ref,
                  out_ref):
    o = oh_ref[0] + oh_ref[1]
    parts = []
    for hd in range(HGRN_HEADS):
        oh = o[:, hd * HGRN_DIM:(hd + 1) * HGRN_DIM]
        ms = jnp.mean(oh * oh, axis=-1, keepdims=True)
        parts.append(oh * lax.rsqrt(ms + RMS_EPS))
    o_h = jnp.concatenate(parts, axis=-1) * hgn_ref[...] * _silu(hg_ref[...].astype(F32))
    gl = gl_ref[...].astype(F32)
    branches = (oa_ref[...], o_h, oc_ref[...])
    merged = None
    for j, ob in enumerate(branches):
        term = _sigmoid(gl[:, j * D_MODEL:(j + 1) * D_MODEL]) * _dot(ob.astype(BF16), wb_ref[j])
        merged = term if merged is None else merged + term
    mix = _dot(merged.astype(BF16), wo_ref[...])
    out_ref[...] = _layer_norm(ALPHA * x_ref[...] + mix, g_ref[...], b_ref[...])


def _merge(x2, o_a, o_dir, hg, o_c, gl, hgrn_gain, wb_bf16, wo_bf16, ln_g, ln_b):
    T = x2.shape[0]
    tm = _row_tile(T, 512)
    W = BRANCH_WIDTH
    row = lambda w: pl.BlockSpec((tm, w), lambda i: (i, 0))
    return pl.pallas_call(
        _merge_kernel,
        grid=(T // tm,),
        in_specs=[row(D_MODEL), row(W), pl.BlockSpec((2, tm, W), lambda i: (0, i, 0)), row(W), row(W),
                  row(N_BRANCH * D_MODEL),
                  pl.BlockSpec((1, W), lambda i: (0, 0)),
                  pl.BlockSpec((N_BRANCH, W, D_MODEL), lambda i: (0, 0, 0)),
                  pl.BlockSpec((D_MODEL, D_MODEL), lambda i: (0, 0)),
                  pl.BlockSpec((1, D_MODEL), lambda i: (0, 0)),
                  pl.BlockSpec((1, D_MODEL), lambda i: (0, 0))],
        out_specs=row(D_MODEL),
        out_shape=jax.ShapeDtypeStruct((T, D_MODEL), F32),
        compiler_params=_cparams(("parallel",)),
        name="merge_out_ln",
    )(x2, o_a, o_dir.reshape(2, T, W), hg, o_c, gl, jnp.tile(hgrn_gain, HGRN_HEADS)[None, :],
      wb_bf16, wo_bf16, ln_g[None, :], ln_b[None, :])


def _ffn_kernel(x_ref, wg_ref, wu_ref, wd_ref, g_ref, b_ref, out_ref, acc_ref):
    j = pl.program_id(1)
    x = x_ref[...]
    xb = x.astype(BF16)
    hmid = _silu(_dot(xb, wg_ref[...])) * _dot(xb, wu_ref[...])
    contrib = _dot(hmid.astype(BF16), wd_ref[...])

    @pl.when(j == 0)
    def _():
        acc_ref[...] = contrib

    @pl.when(j > 0)
    def _():
        acc_ref[...] += contrib

    @pl.when(j == pl.num_programs(1) - 1)
    def _():
        out_ref[...] = _layer_norm(ALPHA * x + acc_ref[...], g_ref[...], b_ref[...])


def _ffn_dense(x2, w_gu, w_down, ln_g, ln_b):
    T = x2.shape[0]
    tm = _row_tile(T, 512)
    ft = D_FF_DENSE
    nf = D_FF_DENSE // ft
    once = pl.Buffered(1)
    return pl.pallas_call(
        _ffn_kernel,
        grid=(T // tm, nf),
        in_specs=[pl.BlockSpec((tm, D_MODEL), lambda i, j: (i, 0)),
                  pl.BlockSpec((D_MODEL, ft), lambda i, j: (0, j), pipeline_mode=once),
                  pl.BlockSpec((D_MODEL, ft), lambda i, j: (0, j + nf), pipeline_mode=once),
                  pl.BlockSpec((ft, D_MODEL), lambda i, j: (j, 0), pipeline_mode=once),
                  pl.BlockSpec((1, D_MODEL), lambda i, j: (0, 0)),
                  pl.BlockSpec((1, D_MODEL), lambda i, j: (0, 0))],
        out_specs=pl.BlockSpec((tm, D_MODEL), lambda i, j: (i, 0)),
        out_shape=jax.ShapeDtypeStruct((T, D_MODEL), F32),
        scratch_shapes=[pltpu.VMEM((tm, D_MODEL), F32)],
        compiler_params=_cparams(("parallel", "arbitrary")),
        name="ffn_dense_ln",
    )(x2, w_gu, w_gu, w_down, ln_g[None, :], ln_b[None, :])


MOE_BLOCK = 1024
MOE_SUB = 64
MOE_BIG = 8
MOE_TILE = 256


def _route_kernel(x_ref, r_ref, col_ref, row_ref, s_ref):
    tb = x_ref.shape[0]
    logits = _dot3(x_ref[...], r_ref[...])
    lane = lax.broadcasted_iota(jnp.int32, logits.shape, 1)
    logits = jnp.where(lane < N_EXPERTS, logits, -jnp.inf)
    m1 = jnp.max(logits, axis=-1, keepdims=True)
    i1 = jnp.min(jnp.where(logits == m1, lane, LANES), axis=-1, keepdims=True)
    rest = jnp.where(lane == i1, -jnp.inf, logits)
    m2 = jnp.max(rest, axis=-1, keepdims=True)
    i2 = jnp.min(jnp.where(rest == m2, lane, LANES), axis=-1, keepdims=True)
    e2 = jnp.exp(m2 - m1)
    w1 = 1.0 / (1.0 + e2)
    w2 = e2 / (1.0 + e2)
    oh1 = lane == i1
    oh2 = lane == i2
    sel = jnp.where(oh1, 1.0, jnp.where(oh2, 1.0, 0.0))
    rr = lax.broadcasted_iota(jnp.int32, (tb, tb), 0)
    cc = lax.broadcasted_iota(jnp.int32, (tb, tb), 1)
    earlier = jnp.where(rr > cc, 1.0, 0.0).astype(BF16)
    rank = _dot(earlier, sel.astype(BF16))
    cnt = jnp.sum(sel, axis=0, keepdims=True)
    nsub = jnp.floor((cnt + (MOE_SUB - 1)) * (1.0 / MOE_SUB))
    r2 = lax.broadcasted_iota(jnp.int32, (LANES, LANES), 0)
    c2 = lax.broadcasted_iota(jnp.int32, (LANES, LANES), 1)
    before = jnp.where(r2 < c2, 1.0, 0.0).astype(BF16)
    nsub8 = jnp.broadcast_to(nsub, (8, LANES))
    off8 = _dot(nsub8.astype(BF16), before) * MOE_SUB
    base = off8[0:1] + rank
    d1 = jnp.sum(jnp.where(oh1, base, 0.0), axis=-1, keepdims=True)
    d2 = jnp.sum(jnp.where(oh2, base, 0.0), axis=-1, keepdims=True)
    col = jnp.where(lane == 0, d1, jnp.where(lane == 1, d2, jnp.where(lane == 2, w1, jnp.where(lane == 3, w2, 0.0))))
    col_ref[...] = col
    row_ref[0] = col.T[0:8, :]
    lane8 = lax.broadcasted_iota(jnp.int32, (8, LANES), 1)
    info = jnp.where(lane8 < N_EXPERTS, off8, pltpu.roll(nsub8, N_EXPERTS, 1))
    s_ref[0] = info.astype(jnp.int32)


def _moe_route(x2, router):
    T = x2.shape[0]
    tb = _row_tile(T, MOE_BLOCK)
    nb = T // tb
    r_pad = jnp.pad(router, ((0, 0), (0, LANES - N_EXPERTS)))
    return pl.pallas_call(
        _route_kernel,
        grid=(nb,),
        in_specs=[pl.BlockSpec((tb, D_MODEL), lambda i: (i, 0)),
                  pl.BlockSpec((D_MODEL, LANES), lambda i: (0, 0))],
        out_specs=[pl.BlockSpec((tb, LANES), lambda i: (i, 0)),
                   pl.BlockSpec((1, 8, tb), lambda i: (i, 0, 0)),
                   pl.BlockSpec((1, 8, LANES), lambda i: (i, 0, 0))],
        out_shape=[jax.ShapeDtypeStruct((T, LANES), F32),
                   jax.ShapeDtypeStruct((nb, 8, tb), F32),
                   jax.ShapeDtypeStruct((nb, 8, LANES), jnp.int32)],
        compiler_params=_cparams(("parallel",)),
        name="moe_route",
    )(x2, r_pad)


def _moe_kernel(s_ref, x_ref, col_ref, row_ref, wg_ref, wu_ref, wd_ref, g_ref, b_ref, out_ref,
                xs_ref, ys_ref):
    blk = pl.program_id(0)
    e = pl.program_id(1)
    j = pl.program_id(2)
    tb = x_ref.shape[0]
    n_tiles = xs_ref.shape[0] // MOE_TILE
    used = s_ref[blk, N_EXPERTS - 1] + s_ref[blk, 2 * N_EXPERTS - 1] * MOE_SUB
    d1r = row_ref[0, 0:1, :]
    d2r = row_ref[0, 1:2, :]

    @pl.when(jnp.logical_and(e == 0, j == 0))
    def _():
        xb = x_ref[...].astype(BF16)
        for rt in range(n_tiles):
            @pl.when(rt * MOE_TILE < used)
            def _():
                r = (lax.broadcasted_iota(jnp.int32, (MOE_TILE, tb), 0) + rt * MOE_TILE).astype(F32)
                p = jnp.where(r == d1r, 1.0, jnp.where(r == d2r, 1.0, 0.0)).astype(BF16)
                xs_ref[rt * MOE_TILE:(rt + 1) * MOE_TILE, :] = _dot(p, xb).astype(BF16)

    start = s_ref[blk, e]
    nsub = s_ref[blk, N_EXPERTS + e]

    def expert_rows(first, n_rows):
        rows = pl.ds(pl.multiple_of(first, MOE_SUB), n_rows)
        xt = xs_ref[rows, :]
        hmid = _silu(_dot(xt, wg_ref[0])) * _dot(xt, wu_ref[0])
        y = _dot(hmid.astype(BF16), wd_ref[0])

        @pl.when(j == 0)
        def _():
            ys_ref[rows, :] = y

        @pl.when(j > 0)
        def _():
            ys_ref[rows, :] += y

    n_big = nsub // MOE_BIG
    rem = nsub - MOE_BIG * n_big

    def big(i, carry):
        expert_rows(start + i * (MOE_BIG * MOE_SUB), MOE_BIG * MOE_SUB)
        return carry

    lax.fori_loop(0, n_big, big, 0)
    done = n_big * MOE_BIG
    part = MOE_BIG // 2
    while part >= 1:
        @pl.when((rem // part) % 2 == 1)
        def _(done=done, part=part):
            expert_rows(start + (done + (rem // (2 * part)) * (2 * part)) * MOE_SUB, part * MOE_SUB)
        part //= 2

    @pl.when(jnp.logical_and(e == pl.num_programs(1) - 1, j == pl.num_programs(2) - 1))
    def _():
        sub_per_tile = MOE_TILE // MOE_SUB
        filled = (used // MOE_SUB) % sub_per_tile
        for k in range(1, sub_per_tile):
            @pl.when(jnp.logical_and(filled > 0, filled <= k))
            def _(k=k):
                first = (used // MOE_TILE) * MOE_TILE + k * MOE_SUB
                ys_ref[pl.ds(pl.multiple_of(first, MOE_SUB), MOE_SUB), :] = jnp.zeros((MOE_SUB, D_MODEL), F32)

        w1r = row_ref[0, 2:3, :]
        w2r = row_ref[0, 3:4, :]
        for rt in range(n_tiles):
            @pl.when(rt * MOE_TILE < used)
            def _():
                r = (lax.broadcasted_iota(jnp.int32, (MOE_TILE, tb), 0) + rt * MOE_TILE).astype(F32)
                w_rows = jnp.sum(jnp.where(r == d1r, w1r, 0.0) + jnp.where(r == d2r, w2r, 0.0),
                                 axis=-1, keepdims=True)
                rows = slice(rt * MOE_TILE, (rt + 1) * MOE_TILE)
                xs_ref[rows, :] = (ys_ref[rows, :] * w_rows).astype(BF16)

            @pl.when(rt * MOE_TILE >= used)
            def _():
                xs_ref[rt * MOE_TILE:(rt + 1) * MOE_TILE, :] = jnp.zeros((MOE_TILE, D_MODEL), BF16)

        n_all = xs_ref.shape[0]
        c = lax.broadcasted_iota(jnp.int32, (MOE_TILE, n_all), 1).astype(F32)
        for tt in range(tb // MOE_TILE):
            toks = slice(tt * MOE_TILE, (tt + 1) * MOE_TILE)
            d1c = col_ref[toks, 0:1]
            d2c = col_ref[toks, 1:2]
            q = jnp.where(c == d1c, 1.0, jnp.where(c == d2c, 1.0, 0.0)).astype(BF16)
            ff = _dot(q, xs_ref[...])
            out_ref[toks, :] = _layer_norm(ALPHA * x_ref[toks, :] + ff, g_ref[...], b_ref[...])


def _ffn_moe(x2, router, w_gu, w_down, ln_g, ln_b):
    T = x2.shape[0]
    tb = _row_tile(T, MOE_BLOCK)
    nb = T // tb
    n_rows = 2 * tb + N_EXPERTS * MOE_SUB
    col, row, sinfo = _moe_route(x2, router)
    ft = 1792
    nf = D_FF_EXPERT // ft
    grid_spec = pltpu.PrefetchScalarGridSpec(
        num_scalar_prefetch=1,
        grid=(nb, N_EXPERTS, nf),
        in_specs=[pl.BlockSpec((tb, D_MODEL), lambda i, e, j, s: (i, 0), pipeline_mode=pl.Buffered(1)),
                  pl.BlockSpec((tb, LANES), lambda i, e, j, s: (i, 0)),
                  pl.BlockSpec((1, 8, tb), lambda i, e, j, s: (i, 0, 0)),
                  pl.BlockSpec((1, D_MODEL, ft), lambda i, e, j, s: (e, 0, j)),
                  pl.BlockSpec((1, D_MODEL, ft), lambda i, e, j, s: (e, 0, j + nf)),
                  pl.BlockSpec((1, ft, D_MODEL), lambda i, e, j, s: (e, j, 0)),
                  pl.BlockSpec((1, D_MODEL), lambda i, e, j, s: (0, 0)),
                  pl.BlockSpec((1, D_MODEL), lambda i, e, j, s: (0, 0))],
        out_specs=pl.BlockSpec((tb, D_MODEL), lambda i, e, j, s: (i, 0)),
        scratch_shapes=[pltpu.VMEM((n_rows, D_MODEL), BF16), pltpu.VMEM((n_rows, D_MODEL), F32)])
    return pl.pallas_call(
        _moe_kernel,
        grid_spec=grid_spec,
        out_shape=jax.ShapeDtypeStruct((T, D_MODEL), F32),
        compiler_params=_cparams(("parallel", "arbitrary", "arbitrary")),
        name="ffn_moe_ln",
    )(sinfo[:, 0, :], x2, col, row, w_gu, w_gu, w_down, ln_g[None, :], ln_b[None, :])


def _trunk(x, p):
    B, L, _ = x.shape
    cos_t, sin_t = _rope_tables(L)
    filt_consts = _filter_features(L)
    dc = _dft_consts(L)
    x2 = x.reshape(B * L, D_MODEL)
    for l in range(DEPTH):
        aq, akv, hq, hf, hi, hg, hy, gl = _in_proj(x2, p["w_in"][l])
        qn, kn, vn = _attn_prep(aq, akv, cos_t, sin_t, p["q_norm_g"][l], p["k_norm_g"][l], B, L)
        o_a = _flash_attention(qn, kn, vn, B, L)
        o_dir = _hgrn(hq, hf, hi, p["lower_bounds"][l], B, L)
        k2l = _hyena_filter(L, filt_consts, p["filt_w1"][l], p["filt_b1"][l], p["filt_freq"][l],
                            p["filt_w2"][l], p["filt_b2"][l], p["filt_w3"][l])
        spec = _filter_spectrum(k2l, dc)
        o_c = _hyena(hy, spec, p["conv_w"][l], p["conv_b"][l], p["hyena_skip"][l], dc, B, L)
        x2 = _merge(x2, o_a, o_dir, hg, o_c, gl, p["hgrn_norm_g"][l], p["w_branch"][l], p["w_out"][l],
                    p["ln1_g"][l], p["ln1_b"][l])
        if l % 2 == 0:
            x2 = _ffn_dense(x2, p["ffn_w_gu"][l // 2], p["ffn_w_down"][l // 2], p["ln2_g"][l], p["ln2_b"][l])
        else:
            x2 = _ffn_moe(x2, p["router_w"][l // 2], p["expert_w_gu"][l // 2], p["expert_w_down"][l // 2],
                          p["ln2_g"][l], p["ln2_b"][l])
    return x2.reshape(B, L, D_MODEL)


def kernel(x_prompt, x_sample, w_in, q_norm_g, k_norm_g, hgrn_lb, hgrn_norm_g, conv_w, conv_b, filt_w1, filt_b1, filt_freq, filt_w2, filt_b2, filt_w3, hyena_skip, w_branch, w_out, ln1_g, ln1_b, ln2_g, ln2_b, ffn_w_gu, ffn_w_down, router_w, expert_w_gu, expert_w_down):
    s = jax.nn.softmax(hgrn_lb.astype(F32), axis=0)
    p = dict(
        w_in=w_in.astype(BF16), q_norm_g=q_norm_g, k_norm_g=k_norm_g,
        lower_bounds=jnp.cumsum(s, axis=0) - s[:1], hgrn_norm_g=hgrn_norm_g,
        conv_w=conv_w, conv_b=conv_b, filt_w1=filt_w1, filt_b1=filt_b1, filt_freq=filt_freq,
        filt_w2=filt_w2, filt_b2=filt_b2, filt_w3=filt_w3, hyena_skip=hyena_skip,
        w_branch=w_branch.astype(BF16), w_out=w_out.astype(BF16),
        ln1_g=ln1_g, ln1_b=ln1_b, ln2_g=ln2_g, ln2_b=ln2_b,
        ffn_w_gu=ffn_w_gu.astype(BF16), ffn_w_down=ffn_w_down.astype(BF16), router_w=router_w,
        expert_w_gu=expert_w_gu.astype(BF16), expert_w_down=expert_w_down.astype(BF16))
    return (_trunk(x_prompt, p), _trunk(x_sample, p))
```

```python
import functools
import math

import numpy as np
import jax
import jax.numpy as jnp
from jax import lax
from jax.experimental import pallas as pl
from jax.experimental.pallas import tpu as pltpu

F32 = jnp.float32
BF16 = jnp.bfloat16

D_MODEL = 1024
DEPTH = 2
GRID_W = 64
BRANCH_WIDTH = 512
ATTN_HEADS = 8
ATTN_KV_HEADS = 2
ATTN_GROUP = ATTN_HEADS // ATTN_KV_HEADS
HEAD_DIM = 64
KV_WIDTH = ATTN_KV_HEADS * HEAD_DIM
ROPE_AXIS_DIM = HEAD_DIM // 2
ROPE_HALF = ROPE_AXIS_DIM // 2
ROPE_THETA = 10000.0
HGRN_HEADS = 4
HGRN_DIM = 128
HGRN_CHUNK = 64
FORGET_FLOOR = 1e-30
HYENA_WIDTH = BRANCH_WIDTH
HYENA_ORDER = 2
FILTER_EMB = 33
FILTER_BANDS = (FILTER_EMB - 1) // 2
FILTER_HIDDEN = 64
FILTER_FAST_DECAY = 0.3
FILTER_SLOW_DECAY = 1.5
FILTER_TARGET = 1e-2
N_FILTER_CH = HYENA_ORDER * 2 * HYENA_WIDTH
N_BRANCH = 3
D_FF_DENSE = 2816
N_EXPERTS = 8
D_FF_EXPERT = 3584
ALPHA = (2 * DEPTH) ** 0.25
LN_EPS = 1e-5
RMS_EPS = 1e-6

SEG_WIDTHS = (BRANCH_WIDTH,
              2 * KV_WIDTH,
              BRANCH_WIDTH,
              2 * BRANCH_WIDTH,
              BRANCH_WIDTH,
              BRANCH_WIDTH,
              3 * HYENA_WIDTH,
              N_BRANCH * D_MODEL)
IN_WIDTH = sum(SEG_WIDTHS)
SEG_DTYPES = (BF16, BF16, BF16, F32, BF16, BF16, BF16, BF16)

LANES = 128
SUBLANES = 8
MXU_COLS = 256
FFT_N2 = 128
VMEM_LIMIT = 56 * 1024 * 1024


def _cparams(sem):
    return pltpu.CompilerParams(dimension_semantics=sem, vmem_limit_bytes=VMEM_LIMIT)


def _dot(a, b):
    return jnp.dot(a, b, preferred_element_type=F32)


def _dot_nt(a, b):
    return lax.dot_general(a, b, (((1,), (1,)), ((), ())), preferred_element_type=F32)


def _dot_tn(a, b):
    return lax.dot_general(a, b, (((0,), (0,)), ((), ())), preferred_element_type=F32)


def _split2(a):
    hi = a.astype(BF16)
    lo = (a - hi.astype(F32)).astype(BF16)
    return hi, lo


def _dot3(a, b):
    ah, al = _split2(a)
    bh, bl = _split2(b)
    return _dot(ah, bh) + _dot(al, bh) + _dot(ah, bl)


def _layer_norm(y, g, b):
    mu = jnp.mean(y, axis=-1, keepdims=True)
    yc = y - mu
    var = jnp.mean(yc * yc, axis=-1, keepdims=True)
    return yc * lax.rsqrt(var + LN_EPS) * g + b


def _sigmoid(x):
    return 1.0 / (1.0 + jnp.exp(-x))


def _silu(x):
    return x * _sigmoid(x)


def _row_tile(n, want):
    t = min(n, want)
    assert n % t == 0
    return t


def _in_proj_kernel(x_ref, w_ref, *out_refs):
    x = x_ref[...].astype(BF16)
    off = 0
    for o_ref, width in zip(out_refs, SEG_WIDTHS):
        o_ref[...] = _dot(x, w_ref[:, off:off + width]).astype(o_ref.dtype)
        off += width


def _in_proj(x2, w_bf16):
    T = x2.shape[0]
    tm = _row_tile(T, 512)
    return pl.pallas_call(
        _in_proj_kernel,
        grid=(T // tm,),
        in_specs=[pl.BlockSpec((tm, D_MODEL), lambda i: (i, 0)),
                  pl.BlockSpec((D_MODEL, IN_WIDTH), lambda i: (0, 0), pipeline_mode=pl.Buffered(1))],
        out_specs=[pl.BlockSpec((tm, w), lambda i: (i, 0)) for w in SEG_WIDTHS],
        out_shape=[jax.ShapeDtypeStruct((T, w), dt) for w, dt in zip(SEG_WIDTHS, SEG_DTYPES)],
        compiler_params=_cparams(("parallel",)),
        name="in_proj",
    )(x2, w_bf16)


def _rope_tables(L):
    rows = L // GRID_W
    row = jnp.broadcast_to(jnp.arange(rows, dtype=F32)[:, None], (rows, GRID_W)).reshape(-1)
    col = jnp.broadcast_to(jnp.arange(GRID_W, dtype=F32)[None, :], (rows, GRID_W)).reshape(-1)
    inv = jnp.power(ROPE_THETA, -2.0 * jnp.arange(ROPE_HALF, dtype=F32) / ROPE_AXIS_DIM)
    a0, a1 = row[:, None] * inv, col[:, None] * inv
    c0, s0, c1, s1 = jnp.cos(a0), jnp.sin(a0), jnp.cos(a1), jnp.sin(a1)
    cos_h = jnp.concatenate([c0, c0, c1, c1], axis=-1)
    sin_h = jnp.concatenate([-s0, s0, -s1, s1], axis=-1)
    return jnp.tile(cos_h, (1, ATTN_HEADS)), jnp.tile(sin_h, (1, ATTN_HEADS))


def _head_block_ones(width):
    idx = np.arange(width) // HEAD_DIM
    return jnp.asarray(idx[:, None] == idx[None, :], dtype=BF16)


def _norm_rope(x, gain, cos, sin, ones_bd):
    width = x.shape[-1]
    sq_hi, sq_lo = _split2(x * x)
    ms = (_dot(sq_hi, ones_bd) + _dot(sq_lo, ones_bd)) * (1.0 / HEAD_DIM)
    xn = x * lax.rsqrt(ms + RMS_EPS) * gain
    lane = lax.broadcasted_iota(jnp.int32, xn.shape, 1)
    first_half = (lane % ROPE_AXIS_DIM) < ROPE_HALF
    swapped = jnp.where(first_half, pltpu.roll(xn, width - ROPE_HALF, 1), pltpu.roll(xn, ROPE_HALF, 1))
    return xn * cos + swapped * sin


LOG2E = 1.4426950408889634
V_EXT = 2 * HEAD_DIM
FLASH_SKEW = 4


def _attn_prep_kernel(aq_ref, akv_ref, cos_ref, sin_ref, qg_ref, kg_ref, bdq_ref, bdk_ref,
                      q_out, kt_out, v_out):
    cos, sin = cos_ref[...], sin_ref[...]
    q = _norm_rope(aq_ref[...].astype(F32), qg_ref[...], cos, sin, bdq_ref[...])
    q_out[...] = (q * (LOG2E * HEAD_DIM ** -0.5)).astype(BF16)
    kv = akv_ref[...].astype(F32)
    k = _norm_rope(kv[:, :KV_WIDTH], kg_ref[...], cos[:, :KV_WIDTH], sin[:, :KV_WIDTH], bdk_ref[...])
    kt_out[0] = k.T.astype(BF16)
    ones = jnp.ones((kv.shape[0], HEAD_DIM), F32)
    v_ext = [kv[:, KV_WIDTH + h * HEAD_DIM:KV_WIDTH + (h + 1) * HEAD_DIM] if part == 0 else ones
             for h in range(ATTN_KV_HEADS) for part in range(2)]
    v_out[...] = jnp.concatenate(v_ext, axis=-1).astype(BF16)


def _attn_prep(aq, akv, cos_t, sin_t, q_gain, k_gain, B, L):
    T = aq.shape[0]
    tm = _row_tile(L, 512)
    nl = L // tm
    qg = jnp.tile(q_gain, ATTN_HEADS)[None, :]
    kg = jnp.tile(k_gain, ATTN_KV_HEADS)[None, :]
    const = lambda i: (0, 0)
    return pl.pallas_call(
        _attn_prep_kernel,
        grid=(T // tm,),
        in_specs=[pl.BlockSpec((tm, BRANCH_WIDTH), lambda i: (i, 0)),
                  pl.BlockSpec((tm, 2 * KV_WIDTH), lambda i: (i, 0)),
                  pl.BlockSpec((tm, BRANCH_WIDTH), lambda i: (i % nl, 0)),
                  pl.BlockSpec((tm, BRANCH_WIDTH), lambda i: (i % nl, 0)),
                  pl.BlockSpec((1, BRANCH_WIDTH), const),
                  pl.BlockSpec((1, KV_WIDTH), const),
                  pl.BlockSpec((BRANCH_WIDTH, BRANCH_WIDTH), const),
                  pl.BlockSpec((KV_WIDTH, KV_WIDTH), const)],
        out_specs=[pl.BlockSpec((tm, BRANCH_WIDTH), lambda i: (i, 0)),
                   pl.BlockSpec((1, KV_WIDTH, tm), lambda i: (i // nl, 0, i % nl)),
                   pl.BlockSpec((tm, ATTN_KV_HEADS * V_EXT), lambda i: (i, 0))],
        out_shape=[jax.ShapeDtypeStruct((T, BRANCH_WIDTH), BF16),
                   jax.ShapeDtypeStruct((B, KV_WIDTH, L), BF16),
                   jax.ShapeDtypeStruct((T, ATTN_KV_HEADS * V_EXT), BF16)],
        compiler_params=_cparams(("parallel",)),
        name="attn_prep",
    )(aq, akv, cos_t, sin_t, qg, kg, _head_block_ones(BRANCH_WIDTH), _head_block_ones(KV_WIDTH))


def _flash_kernel(q_ref, kt_ref, v_ref, o_ref, qs_ref, m_ref, acc_ref, *, tq, rs):
    kv = pl.program_id(2)

    @pl.when(kv == 0)
    def _():
        for h in range(ATTN_HEADS):
            g = h % ATTN_GROUP
            qs_ref[h // ATTN_GROUP, g * tq:(g + 1) * tq, :] = q_ref[0, :, h * HEAD_DIM:(h + 1) * HEAD_DIM]
        m_ref[...] = jnp.full(m_ref.shape, -jnp.inf, F32)
        acc_ref[...] = jnp.zeros(acc_ref.shape, F32)

    kts = [kt_ref[0, kh * HEAD_DIM:(kh + 1) * HEAD_DIM, :] for kh in range(ATTN_KV_HEADS)]
    vs = [v_ref[0, :, kh * V_EXT:(kh + 1) * V_EXT] for kh in range(ATTN_KV_HEADS)]
    tasks = [(kh, slice(sb * rs, (sb + 1) * rs))
             for sb in range(ATTN_GROUP * tq // rs) for kh in range(ATTN_KV_HEADS)]

    def scores(t):
        kh, rows = tasks[t]
        return _dot(qs_ref[kh, rows, :], kts[kh])

    pending = {t: scores(t) for t in range(min(FLASH_SKEW, len(tasks)))}
    for t, (kh, rows) in enumerate(tasks):
        s = pending.pop(t)
        if t + FLASH_SKEW < len(tasks):
            pending[t + FLASH_SKEW] = scores(t + FLASH_SKEW)
        m_prev = m_ref[kh, rows, :]
        m_new = jnp.maximum(m_prev, jnp.max(s, axis=-1, keepdims=True))
        p = jnp.exp2(s - m_new[:, 0:1])
        acc_ref[kh, rows, :] = jnp.exp2(m_prev - m_new) * acc_ref[kh, rows, :] + _dot(p.astype(BF16), vs[kh])
        m_ref[kh, rows, :] = m_new

    @pl.when(kv == pl.num_programs(2) - 1)
    def _():
        for h in range(ATTN_HEADS):
            kh, g = h // ATTN_GROUP, h % ATTN_GROUP
            a = acc_ref[kh, g * tq:(g + 1) * tq, :]
            o_ref[0, :, h * HEAD_DIM:(h + 1) * HEAD_DIM] = a[:, :HEAD_DIM] / a[:, HEAD_DIM:]


def _flash_attention(qn, kt, v_ext, B, L):
    tq = _row_tile(L, 256)
    tk = _row_tile(L, 1024)
    rs = min(128, ATTN_GROUP * tq)
    out = pl.pallas_call(
        functools.partial(_flash_kernel, tq=tq, rs=rs),
        grid=(B, L // tq, L // tk),
        in_specs=[pl.BlockSpec((1, tq, BRANCH_WIDTH), lambda b, i, j: (b, i, 0)),
                  pl.BlockSpec((1, KV_WIDTH, tk), lambda b, i, j: (b, 0, j)),
                  pl.BlockSpec((1, tk, ATTN_KV_HEADS * V_EXT), lambda b, i, j: (b, j, 0))],
        out_specs=pl.BlockSpec((1, tq, BRANCH_WIDTH), lambda b, i, j: (b, i, 0)),
        out_shape=jax.ShapeDtypeStruct((B, L, BRANCH_WIDTH), F32),
        scratch_shapes=[pltpu.VMEM((ATTN_KV_HEADS, ATTN_GROUP * tq, HEAD_DIM), BF16),
                        pltpu.VMEM((ATTN_KV_HEADS, ATTN_GROUP * tq, LANES), F32),
                        pltpu.VMEM((ATTN_KV_HEADS, ATTN_GROUP * tq, V_EXT), F32)],
        compiler_params=_cparams(("parallel", "parallel", "arbitrary")),
        name="flash_attention",
    )(qn.reshape(B, L, BRANCH_WIDTH), kt, v_ext.reshape(B, L, ATTN_KV_HEADS * V_EXT))
    return out.reshape(B * L, BRANCH_WIDTH)


def _ref_rows(b, h, fwd):
    C, W = b.shape
    two_h = 2 * h
    if two_h >= 8:
        n = C // two_h
        b3 = b.reshape(n, two_h, W)
        r = jnp.where(fwd, b3[:, h - 1:h, :], b3[:, h:h + 1, :])
        return jnp.broadcast_to(r, (n, two_h, W)).reshape(C, W)
    b3 = b.reshape(C // 8, 8, W)
    pos8 = lax.broadcasted_iota(jnp.int32, (1, 8, 1), 1)
    out = None
    for blk in range(8 // two_h):
        base = blk * two_h
        r = jnp.where(fwd, b3[:, base + h - 1:base + h, :], b3[:, base + h:base + h + 1, :])
        r = jnp.broadcast_to(r, b3.shape)
        out = r if out is None else jnp.where((pos8 // two_h) == blk, r, out)
    return out.reshape(C, W)


def _hgrn_chunk(q, z, v, lb, tri, st_ref, fwd, head0):
    C, W = q.shape
    q = q.astype(F32)
    v = v.astype(F32)
    e = jnp.exp(-jnp.abs(z))
    s_big = 1.0 / (1.0 + e)
    s_small = e * s_big
    pos = z >= 0.0
    f = lb + (1.0 - lb) * jnp.where(pos, s_big, s_small)
    g = jnp.log2(jnp.maximum(f, FORGET_FLOOR))
    kk = (1.0 - lb) * jnp.where(pos, s_small, s_big)
    g_hi, g_lo = _split2(g)
    b = _dot(tri, g_hi) + _dot(tri, g_lo)
    b_end = jnp.where(fwd, b[C - 1:C, :], b[0:1, :])
    q_in = (q * jnp.exp2(b)).astype(BF16)
    k_out = (kk * jnp.exp2(b_end - b)).astype(BF16)
    s_decay = jnp.exp2(b_end)
    vb = v.astype(BF16)

    row = lax.broadcasted_iota(jnp.int32, (C, 1), 0)
    rr = lax.broadcasted_iota(jnp.int32, (C, C), 0)
    cc = lax.broadcasted_iota(jnp.int32, (C, C), 1)
    n_heads = W // HGRN_DIM
    a_mats = [jnp.zeros((C, C), F32) for _ in range(n_heads)]
    h = C // 2
    while h >= 1:
        upper = ((row % (2 * h)) >= h).astype(jnp.int32)
        q_rows = upper == fwd.astype(jnp.int32)
        ref = _ref_rows(b, h, fwd)
        ql = jnp.where(q_rows, q * jnp.exp2(b - ref), 0.0).astype(BF16)
        kl = jnp.where(q_rows, 0.0, kk * jnp.exp2(ref - b)).astype(BF16)
        level = (jnp.bitwise_xor(rr, cc) // h) == 1
        for hd in range(n_heads):
            sl = slice(hd * HGRN_DIM, (hd + 1) * HGRN_DIM)
            a_mats[hd] = jnp.where(level, _dot_nt(ql[:, sl], kl[:, sl]), a_mats[hd])
        h //= 2

    qk = q * kk
    outs = []
    for hd in range(n_heads):
        sl = slice(hd * HGRN_DIM, (hd + 1) * HGRN_DIM)
        st = st_ref[head0 + hd]
        o = _dot_nt(q_in[:, sl], st.astype(BF16))
        o = o + _dot(a_mats[hd].astype(BF16), vb[:, sl])
        o = o + jnp.sum(qk[:, sl], axis=-1, keepdims=True) * v[:, sl]
        outs.append(o)
        st_ref[head0 + hd] = st * s_decay[:, sl] + _dot_tn(vb[:, sl], k_out[:, sl])
    return jnp.concatenate(outs, axis=-1)


HGRN_GROUP = 4


def _hgrn_kernel(q_ref, f_ref, v_ref, lb_ref, tri_ref, o_ref, st_ref, *, n_inner):
    d = pl.program_id(0)
    fwd = d == 0

    @pl.when(pl.program_id(2) == 0)
    def _():
        st_ref[...] = jnp.zeros(st_ref.shape, F32)

    lb = lb_ref[0]
    tri = tri_ref[0]

    def body(i, carry):
        ci = jnp.where(fwd, i, n_inner - 1 - i)
        r0 = pl.multiple_of(ci * HGRN_CHUNK, HGRN_CHUNK)
        rows = pl.ds(r0, HGRN_CHUNK)
        for grp in range(HGRN_HEADS // HGRN_GROUP):
            cols = slice(grp * HGRN_GROUP * HGRN_DIM, (grp + 1) * HGRN_GROUP * HGRN_DIM)
            o_ref[0, 0, rows, cols] = _hgrn_chunk(q_ref[0, rows, cols], f_ref[0, rows, cols],
                                                  v_ref[0, rows, cols], lb[:, cols], tri, st_ref, fwd,
                                                  grp * HGRN_GROUP)
        return carry

    lax.fori_loop(0, n_inner, body, 0, unroll=8)


def _hgrn(hq, hf, hi, lb, B, L):
    W = BRANCH_WIDTH
    rb = _row_tile(L, 512)
    nb = L // rb
    C = HGRN_CHUNK
    tri_f = np.tril(np.ones((C, C), np.float32))
    tri = jnp.asarray(np.stack([tri_f, tri_f.T]), dtype=BF16)
    blk = lambda d, b, c: (b, c + d * (nb - 1 - 2 * c), 0)
    return pl.pallas_call(
        functools.partial(_hgrn_kernel, n_inner=rb // C),
        grid=(2, B, nb),
        in_specs=[pl.BlockSpec((1, rb, W), blk),
                  pl.BlockSpec((1, rb, W), lambda d, b, c: (b, c + d * (nb - 1 - 2 * c), d)),
                  pl.BlockSpec((1, rb, W), blk),
                  pl.BlockSpec((1, 1, W), lambda d, b, c: (d, 0, 0)),
                  pl.BlockSpec((1, C, C), lambda d, b, c: (d, 0, 0))],
        out_specs=pl.BlockSpec((1, 1, rb, W), lambda d, b, c: (d, b, c + d * (nb - 1 - 2 * c), 0)),
        out_shape=jax.ShapeDtypeStruct((2, B, L, W), F32),
        scratch_shapes=[pltpu.VMEM((HGRN_HEADS, HGRN_DIM, HGRN_DIM), F32)],
        compiler_params=_cparams(("parallel", "parallel", "arbitrary")),
        name="hgrn_scan",
    )(hq.reshape(B, L, W), hf.reshape(B, L, 2 * W), hi.reshape(B, L, W), lb[:, None, :], tri)


def _filter_features(L):
    t = jnp.linspace(0.0, 1.0, L, dtype=F32)[:, None]
    w = 2.0 * math.pi * jnp.arange(L, dtype=F32) / L
    bands = jnp.linspace(1e-4, FILTER_BANDS - 1, FILTER_BANDS, dtype=F32)
    ang = w[:, None] * bands[None, :]
    feats = jnp.concatenate([t, jnp.cos(ang), -jnp.sin(ang)], axis=-1)
    feats2 = jnp.concatenate([feats, feats[:1], jnp.flip(feats[1:], axis=0)], axis=0)
    t2 = jnp.concatenate([t, t[:1], jnp.flip(t[1:], axis=0)], axis=0)
    deltas = jnp.abs(jnp.linspace(math.log(FILTER_TARGET) / FILTER_FAST_DECAY,
                                  math.log(FILTER_TARGET) / FILTER_SLOW_DECAY, N_FILTER_CH, dtype=F32))
    return feats2, t2, deltas[None, :]


def _filter_kernel(feat_ref, t_ref, w1_ref, b1_ref, fr_ref, w2_ref, b2_ref, w3f_ref, w3b_ref,
                   df_ref, db_ref, k_ref, h_ref, *, L):
    @pl.when(pl.program_id(0) == 0)
    def _():
        freq = fr_ref[...]
        h1 = jnp.sin(freq * (_dot3(feat_ref[...], w1_ref[...]) + b1_ref[...]))
        h_ref[...] = jnp.sin(freq * (_dot3(h1, w2_ref[...]) + b2_ref[...]))

    h = h_ref[...]
    t = t_ref[...]
    top = _dot3(h[:L], w3f_ref[...]) * jnp.exp(-t[:L] * df_ref[...])
    bot = _dot3(h[L:], w3b_ref[...]) * jnp.exp(-t[L:] * db_ref[...])
    row = lax.broadcasted_iota(jnp.int32, (L, 1), 0)
    bot = jnp.where(row == 0, 0.0, bot)
    norm = jnp.sum(jnp.abs(top), axis=0, keepdims=True) + jnp.sum(jnp.abs(bot), axis=0, keepdims=True)
    k_ref[:L, :] = top / norm
    k_ref[L:, :] = bot / norm


def _hyena_filter(L, consts, w1, b1, freq, w2, b2, w3):
    feats2, t2, deltas = consts
    wb = 256
    nw = HYENA_WIDTH // wb
    full = lambda shape: pl.BlockSpec(shape, lambda c: (0,) * len(shape))
    fcol = lambda c: (0, (c // nw) * 2 * nw + c % nw)
    bcol = lambda c: (0, (c // nw) * 2 * nw + nw + c % nw)
    return pl.pallas_call(
        functools.partial(_filter_kernel, L=L),
        grid=(HYENA_ORDER * nw,),
        in_specs=[full((2 * L, FILTER_EMB)), full((2 * L, 1)),
                  full((FILTER_EMB, FILTER_HIDDEN)), full((1, FILTER_HIDDEN)), full((1, FILTER_HIDDEN)),
                  full((FILTER_HIDDEN, FILTER_HIDDEN)), full((1, FILTER_HIDDEN)),
                  pl.BlockSpec((FILTER_HIDDEN, wb), fcol), pl.BlockSpec((FILTER_HIDDEN, wb), bcol),
                  pl.BlockSpec((1, wb), fcol), pl.BlockSpec((1, wb), bcol)],
        out_specs=pl.BlockSpec((2 * L, wb), lambda c: (0, c)),
        out_shape=jax.ShapeDtypeStruct((2 * L, HYENA_ORDER * HYENA_WIDTH), F32),
        scratch_shapes=[pltpu.VMEM((2 * L, FILTER_HIDDEN), F32)],
        compiler_params=_cparams(("arbitrary",)),
        name="hyena_filter",
    )(feats2, t2, w1, b1[None, :], freq[None, :], w2, b2[None, :], w3, w3, deltas, deltas)


def _dft_consts(L):
    N = 2 * L
    N2 = FFT_N2
    N1 = N // N2
    k1 = np.arange(N1)
    ang_a = 2.0 * np.pi * np.outer(k1, np.arange(N1)) / N1
    fa_full = np.concatenate([np.cos(ang_a), -np.sin(ang_a)], axis=0)
    fa_half = fa_full[:, :N1 // 2]
    n2 = np.arange(N2)
    kk = k1[:, None, None] + N1 * np.arange(N2)[None, :, None]
    ang_b = 2.0 * np.pi * (kk * n2[None, None, :]) / N
    mr, mi = np.cos(ang_b), -np.sin(ang_b)
    m_fwd = np.concatenate([np.concatenate([mr, -mi], axis=2),
                            np.concatenate([mi, mr], axis=2)], axis=1)
    wr, wi = np.swapaxes(mr, 1, 2), -np.swapaxes(mi, 1, 2)
    m_inv = np.concatenate([np.concatenate([wr, -wi], axis=2),
                            np.concatenate([wi, wr], axis=2)], axis=1)
    ang_c = 2.0 * np.pi * np.outer(np.arange(N1 // 2), k1) / N1
    g_inv = np.concatenate([np.cos(ang_c), -np.sin(ang_c)], axis=1) / N

    def hi_lo(a):
        a = jnp.asarray(a, dtype=F32)
        hi = a.astype(BF16)
        return hi, (a - hi.astype(F32)).astype(BF16)

    eye = np.eye(SUBLANES)
    fa_tiles = jnp.asarray(np.kron(fa_half, eye), dtype=BF16)
    g_tiles = jnp.asarray(np.kron(g_inv, eye), dtype=BF16)
    return dict(N1=N1, N2=N2, fa_full=hi_lo(fa_full), fa_tiles=fa_tiles, g_tiles=g_tiles,
                m_fwd=hi_lo(m_fwd), m_inv=hi_lo(m_inv))


def _mm3(m_hi, m_lo, x):
    x_hi, x_lo = _split2(x)
    return _dot(m_hi, x_hi) + _dot(m_hi, x_lo) + _dot(m_lo, x_hi)


def _stage_a_kernel(z_ref, fh_ref, fl_ref, a_ref):
    n1 = a_ref.shape[2]
    res = _mm3(fh_ref[...], fl_ref[...], z_ref[0])
    a_ref[0, 0] = res[:n1]
    a_ref[0, 1] = res[n1:]


def _fft_stage_a(z, fa, N1):
    B, nr, cols = z.shape
    cb = min(cols, 4096)
    return pl.pallas_call(
        _stage_a_kernel,
        grid=(B, cols // cb),
        in_specs=[pl.BlockSpec((1, nr, cb), lambda b, j: (b, 0, j)),
                  pl.BlockSpec((2 * N1, nr), lambda b, j: (0, 0)),
                  pl.BlockSpec((2 * N1, nr), lambda b, j: (0, 0))],
        out_specs=pl.BlockSpec((1, 2, N1, cb), lambda b, j: (b, 0, 0, j)),
        out_shape=jax.ShapeDtypeStruct((B, 2, N1, cols), F32),
        compiler_params=_cparams(("parallel", "parallel")),
        name="fft_stage_a",
    )(z, fa[0], fa[1])


def _stage_b_filter_kernel(a_ref, mh_ref, ml_ref, o_ref):
    n2 = a_ref.shape[3]
    a = a_ref[0, :, 0].reshape(2 * n2, a_ref.shape[4])
    res = _mm3(mh_ref[0], ml_ref[0], a)
    o_ref[0, 0] = res[:n2]
    o_ref[1, 0] = res[n2:]


def _filter_spectrum(k2l, dc):
    N1, N2 = dc["N1"], dc["N2"]
    OW = k2l.shape[1]
    a = _fft_stage_a(k2l.reshape(1, N1, N2 * OW), dc["fa_full"], N1)
    a = a.reshape(1, 2, N1, N2, OW)
    wb = 512
    return pl.pallas_call(
        _stage_b_filter_kernel,
        grid=(N1, OW // wb),
        in_specs=[pl.BlockSpec((1, 2, 1, N2, wb), lambda k, j: (0, 0, k, 0, j)),
                  pl.BlockSpec((1, 2 * N2, 2 * N2), lambda k, j: (k, 0, 0)),
                  pl.BlockSpec((1, 2 * N2, 2 * N2), lambda k, j: (k, 0, 0))],
        out_specs=pl.BlockSpec((2, 1, N2, wb), lambda k, j: (0, k, 0, j)),
        out_shape=jax.ShapeDtypeStruct((2, N1, N2, OW), F32),
        compiler_params=_cparams(("parallel", "parallel")),
        name="fft_filter_spectrum",
    )(a, dc["m_fwd"][0], dc["m_fwd"][1])


FFT_SPECTRUM_BYTES = 8 * 1024 * 1024


def _long_conv_kernel(z_ref, gate_ref, h_ref, fa_ref, mf_ref, mi_ref, g_ref, bias_ref, o_ref, a_ref):
    n1, n2, wq = a_ref.shape[1], a_ref.shape[2], a_ref.shape[3]
    nr = z_ref.shape[1]
    s8 = SUBLANES
    fa = fa_ref[...]
    side = max(1, MXU_COLS // wq)
    tiles = [slice(jt * s8, (jt + 1) * s8) for jt in range(n2 // s8)]
    for j0 in range(0, len(tiles), side):
        group = tiles[j0:j0 + side]
        zt = jnp.concatenate([z_ref[0, :, t, :].reshape(nr * s8, wq) for t in group], axis=-1).astype(BF16)
        res = _dot(fa, zt)
        for i, t in enumerate(group):
            part = res[:, i * wq:(i + 1) * wq]
            a_ref[0, :, t, :] = part[:n1 * s8].reshape(n1, s8, wq)
            a_ref[1, :, t, :] = part[n1 * s8:].reshape(n1, s8, wq)

    def per_k1(k, carry):
        a = jnp.concatenate([a_ref[0, k], a_ref[1, k]], axis=0).astype(BF16)
        zz = _dot(mf_ref[k], a)
        zr, zi = zz[:n2], zz[n2:]
        hr, hi = h_ref[0, k], h_ref[1, k]
        y = jnp.concatenate([zr * hr - zi * hi, zr * hi + zi * hr], axis=0).astype(BF16)
        c = _dot(mi_ref[k], y)
        a_ref[0, k] = c[:n2]
        a_ref[1, k] = c[n2:]
        return carry

    lax.fori_loop(0, n1, per_k1, 0, unroll=8)
    g = g_ref[...]
    bias = bias_ref[...]
    for j0 in range(0, len(tiles), side):
        group = tiles[j0:j0 + side]
        c = jnp.concatenate(
            [jnp.concatenate([a_ref[0, :, t, :].reshape(n1 * s8, wq),
                              a_ref[1, :, t, :].reshape(n1 * s8, wq)], axis=0) for t in group],
            axis=-1).astype(BF16)
        y_all = _dot(g, c)
        for i, t in enumerate(group):
            y = y_all[:, i * wq:(i + 1) * wq].reshape(nr, s8, wq)
            o_ref[0, :, t, :] = gate_ref[0, :, t, :] * (y + z_ref[0, :, t, :] * bias)


def _long_conv_gated(z, gate, hf, order, skip, dc):
    B, nr, N2, W = z.shape
    N1 = dc["N1"]
    wq = min(W, max(LANES, FFT_SPECTRUM_BYTES // (2 * N1 * N2 * 4) // LANES * LANES))
    nw = W // wq
    rows = pl.BlockSpec((1, nr, N2, wq), lambda w, b: (b, 0, 0, w))
    once = pl.Buffered(1)
    mats = pl.BlockSpec((N1, 2 * N2, 2 * N2), lambda w, b: (0, 0, 0), pipeline_mode=once)
    return pl.pallas_call(
        _long_conv_kernel,
        grid=(nw, B),
        in_specs=[rows, rows,
                  pl.BlockSpec((2, N1, N2, wq), lambda w, b: (0, 0, 0, order * nw + w), pipeline_mode=once),
                  pl.BlockSpec((2 * N1 * SUBLANES, nr * SUBLANES), lambda w, b: (0, 0)),
                  mats, mats,
                  pl.BlockSpec((nr * SUBLANES, 2 * N1 * SUBLANES), lambda w, b: (0, 0)),
                  pl.BlockSpec((1, wq), lambda w, b: (0, w))],
        out_specs=rows,
        out_shape=jax.ShapeDtypeStruct((B, nr, N2, W), F32),
        scratch_shapes=[pltpu.VMEM((2, N1, N2, wq), F32)],
        compiler_params=_cparams(("parallel", "parallel")),
        name="hyena_long_conv",
    )(z, gate, hf, dc["fa_tiles"], dc["m_fwd"][0], dc["m_inv"][0], dc["g_tiles"], skip[None, :])


HALO_ROWS = 16


def _short_conv_kernel(u_ref, prev_ref, next_ref, w_ref, b_ref, v_ref, x1_ref, x2_ref):
    i = pl.program_id(1)
    u = u_ref[0].astype(F32)
    tl = u.shape[0]
    row = lax.broadcasted_iota(jnp.int32, (tl, 1), 0)
    prev_last = prev_ref[0].astype(F32)[HALO_ROWS - 1:HALO_ROWS, :]
    next_first = next_ref[0].astype(F32)[0:1, :]
    before = jnp.where(i == 0, 0.0, prev_last)
    after = jnp.where(i == pl.num_programs(1) - 1, 0.0, next_first)
    up = jnp.where(row == 0, before, pltpu.roll(u, 1, 0))
    un = jnp.where(row == tl - 1, after, pltpu.roll(u, tl - 1, 0))
    w = w_ref[...]
    uc = up * w[0:1] + u * w[1:2] + un * w[2:3] + b_ref[...]
    W = HYENA_WIDTH
    v_ref[0] = uc[:, :W]
    x1_ref[0] = uc[:, W:2 * W]
    x2_ref[0] = uc[:, 2 * W:]


def _short_conv(hy, conv_w, conv_b, B, L):
    W3 = 3 * HYENA_WIDTH
    tl = _row_tile(L, 512)
    nl = L // tl
    u = hy.reshape(B, L, W3)
    rh = tl // HALO_ROWS
    spec_o = pl.BlockSpec((1, tl, HYENA_WIDTH), lambda b, i: (b, i, 0))
    return pl.pallas_call(
        _short_conv_kernel,
        grid=(B, nl),
        in_specs=[pl.BlockSpec((1, tl, W3), lambda b, i: (b, i, 0)),
                  pl.BlockSpec((1, HALO_ROWS, W3), lambda b, i: (b, jnp.maximum(i * rh - 1, 0), 0)),
                  pl.BlockSpec((1, HALO_ROWS, W3),
                               lambda b, i: (b, jnp.minimum((i + 1) * rh, L // HALO_ROWS - 1), 0)),
                  pl.BlockSpec((3, W3), lambda b, i: (0, 0)),
                  pl.BlockSpec((1, W3), lambda b, i: (0, 0))],
        out_specs=[spec_o, spec_o, spec_o],
        out_shape=[jax.ShapeDtypeStruct((B, L, HYENA_WIDTH), F32)] * 3,
        compiler_params=_cparams(("parallel", "parallel")),
        name="hyena_short_conv",
    )(u, u, u, conv_w, conv_b[None, :])


def _hyena(hy, hf, conv_w, conv_b, skip, dc, B, L):
    v, x1, x2 = _short_conv(hy, conv_w, conv_b, B, L)
    shape = (B, dc["N1"] // 2, dc["N2"], HYENA_WIDTH)
    z = _long_conv_gated(v.reshape(shape), x1.reshape(shape), hf, 0, skip[0], dc)
    y = _long_conv_gated(z, x2.reshape(shape), hf, 1, skip[1], dc)
    return y.reshape(B * L, HYENA_WIDTH)


def _merge_kernel(x_ref, oa_ref, oh_ref, hg_ref, oc_ref, gl_ref, hgn_ref, wb_ref, wo_ref, g_ref, b_ref,
                  out_ref):
    o = oh_ref[0] + oh_ref[1]
    parts = []
    for hd in range(HGRN_HEADS):
        oh = o[:, hd * HGRN_DIM:(hd + 1) * HGRN_DIM]
        ms = jnp.mean(oh * oh, axis=-1, keepdims=True)
        parts.append(oh * lax.rsqrt(ms + RMS_EPS))
    o_h = jnp.concatenate(parts, axis=-1) * hgn_ref[...] * _silu(hg_ref[...].astype(F32))
    gl = gl_ref[...].astype(F32)
    branches = (oa_ref[...], o_h, oc_ref[...])
    merged = None
    for j, ob in enumerate(branches):
        term = _sigmoid(gl[:, j * D_MODEL:(j + 1) * D_MODEL]) * _dot(ob.astype(BF16), wb_ref[j])
        merged = term if merged is None else merged + term
    mix = _dot(merged.astype(BF16), wo_ref[...])
    out_ref[...] = _layer_norm(ALPHA * x_ref[...] + mix, g_ref[...], b_ref[...])


def _merge(x2, o_a, o_dir, hg, o_c, gl, hgrn_gain, wb_bf16, wo_bf16, ln_g, ln_b):
    T = x2.shape[0]
    tm = _row_tile(T, 512)
    W = BRANCH_WIDTH
    row = lambda w: pl.BlockSpec((tm, w), lambda i: (i, 0))
    return pl.pallas_call(
        _merge_kernel,
        grid=(T // tm,),
        in_specs=[row(D_MODEL), row(W), pl.BlockSpec((2, tm, W), lambda i: (0, i, 0)), row(W), row(W),
                  row(N_BRANCH * D_MODEL),
                  pl.BlockSpec((1, W), lambda i: (0, 0)),
                  pl.BlockSpec((N_BRANCH, W, D_MODEL), lambda i: (0, 0, 0)),
                  pl.BlockSpec((D_MODEL, D_MODEL), lambda i: (0, 0)),
                  pl.BlockSpec((1, D_MODEL), lambda i: (0, 0)),
                  pl.BlockSpec((1, D_MODEL), lambda i: (0, 0))],
        out_specs=row(D_MODEL),
        out_shape=jax.ShapeDtypeStruct((T, D_MODEL), F32),
        compiler_params=_cparams(("parallel",)),
        name="merge_out_ln",
    )(x2, o_a, o_dir.reshape(2, T, W), hg, o_c, gl, jnp.tile(hgrn_gain, HGRN_HEADS)[None, :],
      wb_bf16, wo_bf16, ln_g[None, :], ln_b[None, :])


def _ffn_kernel(x_ref, wg_ref, wu_ref, wd_ref, g_ref, b_ref, out_ref, acc_ref):
    j = pl.program_id(1)
    x = x_ref[...]
    xb = x.astype(BF16)
    hmid = _silu(_dot(xb, wg_ref[...])) * _dot(xb, wu_ref[...])
    contrib = _dot(hmid.astype(BF16), wd_ref[...])

    @pl.when(j == 0)
    def _():
        acc_ref[...] = contrib

    @pl.when(j > 0)
    def _():
        acc_ref[...] += contrib

    @pl.when(j == pl.num_programs(1) - 1)
    def _():
        out_ref[...] = _layer_norm(ALPHA * x + acc_ref[...], g_ref[...], b_ref[...])


def _ffn_dense(x2, w_gu, w_down, ln_g, ln_b):
    T = x2.shape[0]
    tm = _row_tile(T, 512)
    ft = D_FF_DENSE
    nf = D_FF_DENSE // ft
    once = pl.Buffered(1)
    return pl.pallas_call(
        _ffn_kernel,
        grid=(T // tm, nf),
        in_specs=[pl.BlockSpec((tm, D_MODEL), lambda i, j: (i, 0)),
                  pl.BlockSpec((D_MODEL, ft), lambda i, j: (0, j), pipeline_mode=once),
                  pl.BlockSpec((D_MODEL, ft), lambda i, j: (0, j + nf), pipeline_mode=once),
                  pl.BlockSpec((ft, D_MODEL), lambda i, j: (j, 0), pipeline_mode=once),
                  pl.BlockSpec((1, D_MODEL), lambda i, j: (0, 0)),
                  pl.BlockSpec((1, D_MODEL), lambda i, j: (0, 0))],
        out_specs=pl.BlockSpec((tm, D_MODEL), lambda i, j: (i, 0)),
        out_shape=jax.ShapeDtypeStruct((T, D_MODEL), F32),
        scratch_shapes=[pltpu.VMEM((tm, D_MODEL), F32)],
        compiler_params=_cparams(("parallel", "arbitrary")),
        name="ffn_dense_ln",
    )(x2, w_gu, w_gu, w_down, ln_g[None, :], ln_b[None, :])


MOE_BLOCK = 1024
MOE_SUB = 64
MOE_BIG = 8
MOE_TILE = 256


def _route_kernel(x_ref, r_ref, col_ref, row_ref, s_ref):
    tb = x_ref.shape[0]
    logits = _dot3(x_ref[...], r_ref[...])
    lane = lax.broadcasted_iota(jnp.int32, logits.shape, 1)
    logits = jnp.where(lane < N_EXPERTS, logits, -jnp.inf)
    m1 = jnp.max(logits, axis=-1, keepdims=True)
    i1 = jnp.min(jnp.where(logits == m1, lane, LANES), axis=-1, keepdims=True)
    rest = jnp.where(lane == i1, -jnp.inf, logits)
    m2 = jnp.max(rest, axis=-1, keepdims=True)
    i2 = jnp.min(jnp.where(rest == m2, lane, LANES), axis=-1, keepdims=True)
    e2 = jnp.exp(m2 - m1)
    w1 = 1.0 / (1.0 + e2)
    w2 = e2 / (1.0 + e2)
    oh1 = lane == i1
    oh2 = lane == i2
    sel = jnp.where(oh1, 1.0, jnp.where(oh2, 1.0, 0.0))
    rr = lax.broadcasted_iota(jnp.int32, (tb, tb), 0)
    cc = lax.broadcasted_iota(jnp.int32, (tb, tb), 1)
    earlier = jnp.where(rr > cc, 1.0, 0.0).astype(BF16)
    rank = _dot(earlier, sel.astype(BF16))
    cnt = jnp.sum(sel, axis=0, keepdims=True)
    nsub = jnp.floor((cnt + (MOE_SUB - 1)) * (1.0 / MOE_SUB))
    r2 = lax.broadcasted_iota(jnp.int32, (LANES, LANES), 0)
    c2 = lax.broadcasted_iota(jnp.int32, (LANES, LANES), 1)
    before = jnp.where(r2 < c2, 1.0, 0.0).astype(BF16)
    nsub8 = jnp.broadcast_to(nsub, (8, LANES))
    off8 = _dot(nsub8.astype(BF16), before) * MOE_SUB
    base = off8[0:1] + rank
    d1 = jnp.sum(jnp.where(oh1, base, 0.0), axis=-1, keepdims=True)
    d2 = jnp.sum(jnp.where(oh2, base, 0.0), axis=-1, keepdims=True)
    col = jnp.where(lane == 0, d1, jnp.where(lane == 1, d2, jnp.where(lane == 2, w1, jnp.where(lane == 3, w2, 0.0))))
    col_ref[...] = col
    row_ref[0] = col.T[0:8, :]
    lane8 = lax.broadcasted_iota(jnp.int32, (8, LANES), 1)
    info = jnp.where(lane8 < N_EXPERTS, off8, pltpu.roll(nsub8, N_EXPERTS, 1))
    s_ref[0] = info.astype(jnp.int32)


def _moe_route(x2, router):
    T = x2.shape[0]
    tb = _row_tile(T, MOE_BLOCK)
    nb = T // tb
    r_pad = jnp.pad(router, ((0, 0), (0, LANES - N_EXPERTS)))
    return pl.pallas_call(
        _route_kernel,
        grid=(nb,),
        in_specs=[pl.BlockSpec((tb, D_MODEL), lambda i: (i, 0)),
                  pl.BlockSpec((D_MODEL, LANES), lambda i: (0, 0))],
        out_specs=[pl.BlockSpec((tb, LANES), lambda i: (i, 0)),
                   pl.BlockSpec((1, 8, tb), lambda i: (i, 0, 0)),
                   pl.BlockSpec((1, 8, LANES), lambda i: (i, 0, 0))],
        out_shape=[jax.ShapeDtypeStruct((T, LANES), F32),
                   jax.ShapeDtypeStruct((nb, 8, tb), F32),
                   jax.ShapeDtypeStruct((nb, 8, LANES), jnp.int32)],
        compiler_params=_cparams(("parallel",)),
        name="moe_route",
    )(x2, r_pad)


def _moe_kernel(s_ref, x_ref, col_ref, row_ref, wg_ref, wu_ref, wd_ref, g_ref, b_ref, out_ref,
                xs_ref, ys_ref):
    blk = pl.program_id(0)
    e = pl.program_id(1)
    j = pl.program_id(2)
    tb = x_ref.shape[0]
    n_tiles = xs_ref.shape[0] // MOE_TILE
    used = s_ref[blk, N_EXPERTS - 1] + s_ref[blk, 2 * N_EXPERTS - 1] * MOE_SUB
    d1r = row_ref[0, 0:1, :]
    d2r = row_ref[0, 1:2, :]

    @pl.when(jnp.logical_and(e == 0, j == 0))
    def _():
        xb = x_ref[...].astype(BF16)
        for rt in range(n_tiles):
            @pl.when(rt * MOE_TILE < used)
            def _():
                r = (lax.broadcasted_iota(jnp.int32, (MOE_TILE, tb), 0) + rt * MOE_TILE).astype(F32)
                p = jnp.where(r == d1r, 1.0, jnp.where(r == d2r, 1.0, 0.0)).astype(BF16)
                xs_ref[rt * MOE_TILE:(rt + 1) * MOE_TILE, :] = _dot(p, xb).astype(BF16)

    start = s_ref[blk, e]
    nsub = s_ref[blk, N_EXPERTS + e]

    def expert_rows(first, n_rows):
        rows = pl.ds(pl.multiple_of(first, MOE_SUB), n_rows)
        xt = xs_ref[rows, :]
        hmid = _silu(_dot(xt, wg_ref[0])) * _dot(xt, wu_ref[0])
        y = _dot(hmid.astype(BF16), wd_ref[0])

        @pl.when(j == 0)
        def _():
            ys_ref[rows, :] = y

        @pl.when(j > 0)
        def _():
            ys_ref[rows, :] += y

    n_big = nsub // MOE_BIG
    rem = nsub - MOE_BIG * n_big

    def big(i, carry):
        expert_rows(start + i * (MOE_BIG * MOE_SUB), MOE_BIG * MOE_SUB)
        return carry

    lax.fori_loop(0, n_big, big, 0)
    done = n_big * MOE_BIG
    part = MOE_BIG // 2
    while part >= 1:
        @pl.when((rem // part) % 2 == 1)
        def _(done=done, part=part):
            expert_rows(start + (done + (rem // (2 * part)) * (2 * part)) * MOE_SUB, part * MOE_SUB)
        part //= 2

    @pl.when(jnp.logical_and(e == pl.num_programs(1) - 1, j == pl.num_programs(2) - 1))
    def _():
        sub_per_tile = MOE_TILE // MOE_SUB
        filled = (used // MOE_SUB) % sub_per_tile
        for k in range(1, sub_per_tile):
            @pl.when(jnp.logical_and(filled > 0, filled <= k))
            def _(k=k):
                first = (used // MOE_TILE) * MOE_TILE + k * MOE_SUB
                ys_ref[pl.ds(pl.multiple_of(first, MOE_SUB), MOE_SUB), :] = jnp.zeros((MOE_SUB, D_MODEL), F32)

        w1r = row_ref[0, 2:3, :]
        w2r = row_ref[0, 3:4, :]
        for rt in range(n_tiles):
            @pl.when(rt * MOE_TILE < used)
            def _():
                r = (lax.broadcasted_iota(jnp.int32, (MOE_TILE, tb), 0) + rt * MOE_TILE).astype(F32)
                w_rows = jnp.sum(jnp.where(r == d1r, w1r, 0.0) + jnp.where(r == d2r, w2r, 0.0),
                                 axis=-1, keepdims=True)
                rows = slice(rt * MOE_TILE, (rt + 1) * MOE_TILE)
                xs_ref[rows, :] = (ys_ref[rows, :] * w_rows).astype(BF16)

            @pl.when(rt * MOE_TILE >= used)
            def _():
                xs_ref[rt * MOE_TILE:(rt + 1) * MOE_TILE, :] = jnp.zeros((MOE_TILE, D_MODEL), BF16)

        n_all = xs_ref.shape[0]
        c = lax.broadcasted_iota(jnp.int32, (MOE_TILE, n_all), 1).astype(F32)
        for tt in range(tb // MOE_TILE):
            toks = slice(tt * MOE_TILE, (tt + 1) * MOE_TILE)
            d1c = col_ref[toks, 0:1]
            d2c = col_ref[toks, 1:2]
            q = jnp.where(c == d1c, 1.0, jnp.where(c == d2c, 1.0, 0.0)).astype(BF16)
            ff = _dot(q, xs_ref[...])
            out_ref[toks, :] = _layer_norm(ALPHA * x_ref[toks, :] + ff, g_ref[...], b_ref[...])


def _ffn_moe(x2, router, w_gu, w_down, ln_g, ln_b):
    T = x2.shape[0]
    tb = _row_tile(T, MOE_BLOCK)
    nb = T // tb
    n_rows = 2 * tb + N_EXPERTS * MOE_SUB
    col, row, sinfo = _moe_route(x2, router)
    ft = 1792
    nf = D_FF_EXPERT // ft
    grid_spec = pltpu.PrefetchScalarGridSpec(
        num_scalar_prefetch=1,
        grid=(nb, N_EXPERTS, nf),
        in_specs=[pl.BlockSpec((tb, D_MODEL), lambda i, e, j, s: (i, 0), pipeline_mode=pl.Buffered(1)),
                  pl.BlockSpec((tb, LANES), lambda i, e, j, s: (i, 0)),
                  pl.BlockSpec((1, 8, tb), lambda i, e, j, s: (i, 0, 0)),
                  pl.BlockSpec((1, D_MODEL, ft), lambda i, e, j, s: (e, 0, j)),
                  pl.BlockSpec((1, D_MODEL, ft), lambda i, e, j, s: (e, 0, j + nf)),
                  pl.BlockSpec((1, ft, D_MODEL), lambda i, e, j, s: (e, j, 0)),
                  pl.BlockSpec((1, D_MODEL), lambda i, e, j, s: (0, 0)),
                  pl.BlockSpec((1, D_MODEL), lambda i, e, j, s: (0, 0))],
        out_specs=pl.BlockSpec((tb, D_MODEL), lambda i, e, j, s: (i, 0)),
        scratch_shapes=[pltpu.VMEM((n_rows, D_MODEL), BF16), pltpu.VMEM((n_rows, D_MODEL), F32)])
    return pl.pallas_call(
        _moe_kernel,
        grid_spec=grid_spec,
        out_shape=jax.ShapeDtypeStruct((T, D_MODEL), F32),
        compiler_params=_cparams(("parallel", "arbitrary", "arbitrary")),
        name="ffn_moe_ln",
    )(sinfo[:, 0, :], x2, col, row, w_gu, w_gu, w_down, ln_g[None, :], ln_b[None, :])


def _trunk(x, p):
    B, L, _ = x.shape
    cos_t, sin_t = _rope_tables(L)
    filt_consts = _filter_features(L)
    dc = _dft_consts(L)
    x2 = x.reshape(B * L, D_MODEL)
    for l in range(DEPTH):
        aq, akv, hq, hf, hi, hg, hy, gl = _in_proj(x2, p["w_in"][l])
        qn, kn, vn = _attn_prep(aq, akv, cos_t, sin_t, p["q_norm_g"][l], p["k_norm_g"][l], B, L)
        o_a = _flash_attention(qn, kn, vn, B, L)
        o_dir = _hgrn(hq, hf, hi, p["lower_bounds"][l], B, L)
        k2l = _hyena_filter(L, filt_consts, p["filt_w1"][l], p["filt_b1"][l], p["filt_freq"][l],
                            p["filt_w2"][l], p["filt_b2"][l], p["filt_w3"][l])
        spec = _filter_spectrum(k2l, dc)
        o_c = _hyena(hy, spec, p["conv_w"][l], p["conv_b"][l], p["hyena_skip"][l], dc, B, L)
        x2 = _merge(x2, o_a, o_dir, hg, o_c, gl, p["hgrn_norm_g"][l], p["w_branch"][l], p["w_out"][l],
                    p["ln1_g"][l], p["ln1_b"][l])
        if l % 2 == 0:
            x2 = _ffn_dense(x2, p["ffn_w_gu"][l // 2], p["ffn_w_down"][l // 2], p["ln2_g"][l], p["ln2_b"][l])
        else:
            x2 = _ffn_moe(x2, p["router_w"][l // 2], p["expert_w_gu"][l // 2], p["expert_w_down"][l // 2],
                          p["ln2_g"][l], p["ln2_b"][l])
    return x2.reshape(B, L, D_MODEL)


def kernel(x_prompt, x_sample, w_in, q_norm_g, k_norm_g, hgrn_lb, hgrn_norm_g, conv_w, conv_b, filt_w1, filt_b1, filt_freq, filt_w2, filt_b2, filt_w3, hyena_skip, w_branch, w_out, ln1_g, ln1_b, ln2_g, ln2_b, ffn_w_gu, ffn_w_down, router_w, expert_w_gu, expert_w_down):
    s = jax.nn.softmax(hgrn_lb.astype(F32), axis=0)
    p = dict(
        w_in=w_in.astype(BF16), q_norm_g=q_norm_g, k_norm_g=k_norm_g,
        lower_bounds=jnp.cumsum(s, axis=0) - s[:1], hgrn_norm_g=hgrn_norm_g,
        conv_w=conv_w, conv_b=conv_b, filt_w1=filt_w1, filt_b1=filt_b1, filt_freq=filt_freq,
        filt_w2=filt_w2, filt_b2=filt_b2, filt_w3=filt_w3, hyena_skip=hyena_skip,
        w_branch=w_branch.astype(BF16), w_out=w_out.astype(BF16),
        ln1_g=ln1_g, ln1_b=ln1_b, ln2_g=ln2_g, ln2_b=ln2_b,
        ffn_w_gu=ffn_w_gu.astype(BF16), ffn_w_down=ffn_w_down.astype(BF16), router_w=router_w,
        expert_w_gu=expert_w_gu.astype(BF16), expert_w_down=expert_w_down.astype(BF16))
    return (_trunk(x_prompt, p), _trunk(x_sample, p))
```

```python
import functools
import math

import numpy as np
import jax
import jax.numpy as jnp
from jax import lax
from jax.experimental import pallas as pl
from jax.experimental.pallas import tpu as pltpu

F32 = jnp.float32
BF16 = jnp.bfloat16

D_MODEL = 1024
DEPTH = 2
GRID_W = 64
BRANCH_WIDTH = 512
ATTN_HEADS = 8
ATTN_KV_HEADS = 2
ATTN_GROUP = ATTN_HEADS // ATTN_KV_HEADS
HEAD_DIM = 64
KV_WIDTH = ATTN_KV_HEADS * HEAD_DIM
ROPE_AXIS_DIM = HEAD_DIM // 2
ROPE_HALF = ROPE_AXIS_DIM // 2
ROPE_THETA = 10000.0
HGRN_HEADS = 4
HGRN_DIM = 128
HGRN_CHUNK = 64
FORGET_FLOOR = 1e-30
HYENA_WIDTH = BRANCH_WIDTH
HYENA_ORDER = 2
FILTER_EMB = 33
FILTER_BANDS = (FILTER_EMB - 1) // 2
FILTER_HIDDEN = 64
FILTER_FAST_DECAY = 0.3
FILTER_SLOW_DECAY = 1.5
FILTER_TARGET = 1e-2
N_FILTER_CH = HYENA_ORDER * 2 * HYENA_WIDTH
N_BRANCH = 3
D_FF_DENSE = 2816
N_EXPERTS = 8
D_FF_EXPERT = 3584
ALPHA = (2 * DEPTH) ** 0.25
LN_EPS = 1e-5
RMS_EPS = 1e-6

SEG_WIDTHS = (BRANCH_WIDTH,
              2 * KV_WIDTH,
              BRANCH_WIDTH,
              2 * BRANCH_WIDTH,
              BRANCH_WIDTH,
              BRANCH_WIDTH,
              3 * HYENA_WIDTH,
              N_BRANCH * D_MODEL)
IN_WIDTH = sum(SEG_WIDTHS)
SEG_DTYPES = (BF16, BF16, BF16, F32, BF16, BF16, BF16, BF16)

LANES = 128
SUBLANES = 8
MXU_COLS = 256
FFT_N2 = 128
VMEM_LIMIT = 56 * 1024 * 1024


def _cparams(sem):
    return pltpu.CompilerParams(dimension_semantics=sem, vmem_limit_bytes=VMEM_LIMIT)


def _dot(a, b):
    return jnp.dot(a, b, preferred_element_type=F32)


def _dot_nt(a, b):
    return lax.dot_general(a, b, (((1,), (1,)), ((), ())), preferred_element_type=F32)


def _dot_tn(a, b):
    return lax.dot_general(a, b, (((0,), (0,)), ((), ())), preferred_element_type=F32)


def _split2(a):
    hi = a.astype(BF16)
    lo = (a - hi.astype(F32)).astype(BF16)
    return hi, lo


def _dot3(a, b):
    ah, al = _split2(a)
    bh, bl = _split2(b)
    return _dot(ah, bh) + _dot(al, bh) + _dot(ah, bl)


def _layer_norm(y, g, b):
    mu = jnp.mean(y, axis=-1, keepdims=True)
    yc = y - mu
    var = jnp.mean(yc * yc, axis=-1, keepdims=True)
    return yc * lax.rsqrt(var + LN_EPS) * g + b


def _sigmoid(x):
    return 1.0 / (1.0 + jnp.exp(-x))


def _silu(x):
    return x * _sigmoid(x)


def _row_tile(n, want):
    t = min(n, want)
    assert n % t == 0
    return t


def _in_proj_kernel(x_ref, w_ref, *out_refs):
    x = x_ref[...].astype(BF16)
    off = 0
    for o_ref, width in zip(out_refs, SEG_WIDTHS):
        o_ref[...] = _dot(x, w_ref[:, off:off + width]).astype(o_ref.dtype)
        off += width


def _in_proj(x2, w_bf16):
    T = x2.shape[0]
    tm = _row_tile(T, 512)
    return pl.pallas_call(
        _in_proj_kernel,
        grid=(T // tm,),
        in_specs=[pl.BlockSpec((tm, D_MODEL), lambda i: (i, 0)),
                  pl.BlockSpec((D_MODEL, IN_WIDTH), lambda i: (0, 0), pipeline_mode=pl.Buffered(1))],
        out_specs=[pl.BlockSpec((tm, w), lambda i: (i, 0)) for w in SEG_WIDTHS],
        out_shape=[jax.ShapeDtypeStruct((T, w), dt) for w, dt in zip(SEG_WIDTHS, SEG_DTYPES)],
        compiler_params=_cparams(("parallel",)),
        name="in_proj",
    )(x2, w_bf16)


def _rope_tables(L):
    rows = L // GRID_W
    row = jnp.broadcast_to(jnp.arange(rows, dtype=F32)[:, None], (rows, GRID_W)).reshape(-1)
    col = jnp.broadcast_to(jnp.arange(GRID_W, dtype=F32)[None, :], (rows, GRID_W)).reshape(-1)
    inv = jnp.power(ROPE_THETA, -2.0 * jnp.arange(ROPE_HALF, dtype=F32) / ROPE_AXIS_DIM)
    a0, a1 = row[:, None] * inv, col[:, None] * inv
    c0, s0, c1, s1 = jnp.cos(a0), jnp.sin(a0), jnp.cos(a1), jnp.sin(a1)
    cos_h = jnp.concatenate([c0, c0, c1, c1], axis=-1)
    sin_h = jnp.concatenate([-s0, s0, -s1, s1], axis=-1)
    return jnp.tile(cos_h, (1, ATTN_HEADS)), jnp.tile(sin_h, (1, ATTN_HEADS))


def _head_block_ones(width):
    idx = np.arange(width) // HEAD_DIM
    return jnp.asarray(idx[:, None] == idx[None, :], dtype=BF16)


def _norm_rope(x, gain, cos, sin, ones_bd):
    width = x.shape[-1]
    sq_hi, sq_lo = _split2(x * x)
    ms = (_dot(sq_hi, ones_bd) + _dot(sq_lo, ones_bd)) * (1.0 / HEAD_DIM)
    xn = x * lax.rsqrt(ms + RMS_EPS) * gain
    lane = lax.broadcasted_iota(jnp.int32, xn.shape, 1)
    first_half = (lane % ROPE_AXIS_DIM) < ROPE_HALF
    swapped = jnp.where(first_half, pltpu.roll(xn, width - ROPE_HALF, 1), pltpu.roll(xn, ROPE_HALF, 1))
    return xn * cos + swapped * sin


LOG2E = 1.4426950408889634
V_EXT = 2 * HEAD_DIM
FLASH_SKEW = 4


def _attn_prep_kernel(aq_ref, akv_ref, cos_ref, sin_ref, qg_ref, kg_ref, bdq_ref, bdk_ref,
                      q_out, kt_out, v_out):
    cos, sin = cos_ref[...], sin_ref[...]
    q = _norm_rope(aq_ref[...].astype(F32), qg_ref[...], cos, sin, bdq_ref[...])
    q_out[...] = (q * (LOG2E * HEAD_DIM ** -0.5)).astype(BF16)
    kv = akv_ref[...].astype(F32)
    k = _norm_rope(kv[:, :KV_WIDTH], kg_ref[...], cos[:, :KV_WIDTH], sin[:, :KV_WIDTH], bdk_ref[...])
    kt_out[0] = k.T.astype(BF16)
    ones = jnp.ones((kv.shape[0], HEAD_DIM), F32)
    v_ext = [kv[:, KV_WIDTH + h * HEAD_DIM:KV_WIDTH + (h + 1) * HEAD_DIM] if part == 0 else ones
             for h in range(ATTN_KV_HEADS) for part in range(2)]
    v_out[...] = jnp.concatenate(v_ext, axis=-1).astype(BF16)


def _attn_prep(aq, akv, cos_t, sin_t, q_gain, k_gain, B, L):
    T = aq.shape[0]
    tm = _row_tile(L, 512)
    nl = L // tm
    qg = jnp.tile(q_gain, ATTN_HEADS)[None, :]
    kg = jnp.tile(k_gain, ATTN_KV_HEADS)[None, :]
    const = lambda i: (0, 0)
    return pl.pallas_call(
        _attn_prep_kernel,
        grid=(T // tm,),
        in_specs=[pl.BlockSpec((tm, BRANCH_WIDTH), lambda i: (i, 0)),
                  pl.BlockSpec((tm, 2 * KV_WIDTH), lambda i: (i, 0)),
                  pl.BlockSpec((tm, BRANCH_WIDTH), lambda i: (i % nl, 0)),
                  pl.BlockSpec((tm, BRANCH_WIDTH), lambda i: (i % nl, 0)),
                  pl.BlockSpec((1, BRANCH_WIDTH), const),
                  pl.BlockSpec((1, KV_WIDTH), const),
                  pl.BlockSpec((BRANCH_WIDTH, BRANCH_WIDTH), const),
                  pl.BlockSpec((KV_WIDTH, KV_WIDTH), const)],
        out_specs=[pl.BlockSpec((tm, BRANCH_WIDTH), lambda i: (i, 0)),
                   pl.BlockSpec((1, KV_WIDTH, tm), lambda i: (i // nl, 0, i % nl)),
                   pl.BlockSpec((tm, ATTN_KV_HEADS * V_EXT), lambda i: (i, 0))],
        out_shape=[jax.ShapeDtypeStruct((T, BRANCH_WIDTH), BF16),
                   jax.ShapeDtypeStruct((B, KV_WIDTH, L), BF16),
                   jax.ShapeDtypeStruct((T, ATTN_KV_HEADS * V_EXT), BF16)],
        compiler_params=_cparams(("parallel",)),
        name="attn_prep",
    )(aq, akv, cos_t, sin_t, qg, kg, _head_block_ones(BRANCH_WIDTH), _head_block_ones(KV_WIDTH))


def _flash_kernel(q_ref, kt_ref, v_ref, o_ref, qs_ref, m_ref, acc_ref, *, tq, rs):
    kv = pl.program_id(2)

    @pl.when(kv == 0)
    def _():
        for h in range(ATTN_HEADS):
            g = h % ATTN_GROUP
            qs_ref[h // ATTN_GROUP, g * tq:(g + 1) * tq, :] = q_ref[0, :, h * HEAD_DIM:(h + 1) * HEAD_DIM]
        m_ref[...] = jnp.full(m_ref.shape, -jnp.inf, F32)
        acc_ref[...] = jnp.zeros(acc_ref.shape, F32)

    kts = [kt_ref[0, kh * HEAD_DIM:(kh + 1) * HEAD_DIM, :] for kh in range(ATTN_KV_HEADS)]
    vs = [v_ref[0, :, kh * V_EXT:(kh + 1) * V_EXT] for kh in range(ATTN_KV_HEADS)]
    tasks = [(kh, slice(sb * rs, (sb + 1) * rs))
             for sb in range(ATTN_GROUP * tq // rs) for kh in range(ATTN_KV_HEADS)]

    def scores(t):
        kh, rows = tasks[t]
        return _dot(qs_ref[kh, rows, :], kts[kh])

    pending = {t: scores(t) for t in range(min(FLASH_SKEW, len(tasks)))}
    for t, (kh, rows) in enumerate(tasks):
        s = pending.pop(t)
        if t + FLASH_SKEW < len(tasks):
            pending[t + FLASH_SKEW] = scores(t + FLASH_SKEW)
        m_prev = m_ref[kh, rows, :]
        m_new = jnp.maximum(m_prev, jnp.max(s, axis=-1, keepdims=True))
        p = jnp.exp2(s - m_new[:, 0:1])
        acc_ref[kh, rows, :] = jnp.exp2(m_prev - m_new) * acc_ref[kh, rows, :] + _dot(p.astype(BF16), vs[kh])
        m_ref[kh, rows, :] = m_new

    @pl.when(kv == pl.num_programs(2) - 1)
    def _():
        for h in range(ATTN_HEADS):
            kh, g = h // ATTN_GROUP, h % ATTN_GROUP
            a = acc_ref[kh, g * tq:(g + 1) * tq, :]
            o_ref[0, :, h * HEAD_DIM:(h + 1) * HEAD_DIM] = a[:, :HEAD_DIM] / a[:, HEAD_DIM:]


def _flash_attention(qn, kt, v_ext, B, L):
    tq = _row_tile(L, 256)
    tk = _row_tile(L, 2048)
    rs = min(128, ATTN_GROUP * tq)
    out = pl.pallas_call(
        functools.partial(_flash_kernel, tq=tq, rs=rs),
        grid=(B, L // tq, L // tk),
        in_specs=[pl.BlockSpec((1, tq, BRANCH_WIDTH), lambda b, i, j: (b, i, 0)),
                  pl.BlockSpec((1, KV_WIDTH, tk), lambda b, i, j: (b, 0, j)),
                  pl.BlockSpec((1, tk, ATTN_KV_HEADS * V_EXT), lambda b, i, j: (b, j, 0))],
        out_specs=pl.BlockSpec((1, tq, BRANCH_WIDTH), lambda b, i, j: (b, i, 0)),
        out_shape=jax.ShapeDtypeStruct((B, L, BRANCH_WIDTH), F32),
        scratch_shapes=[pltpu.VMEM((ATTN_KV_HEADS, ATTN_GROUP * tq, HEAD_DIM), BF16),
                        pltpu.VMEM((ATTN_KV_HEADS, ATTN_GROUP * tq, LANES), F32),
                        pltpu.VMEM((ATTN_KV_HEADS, ATTN_GROUP * tq, V_EXT), F32)],
        compiler_params=_cparams(("parallel", "parallel", "arbitrary")),
        name="flash_attention",
    )(qn.reshape(B, L, BRANCH_WIDTH), kt, v_ext.reshape(B, L, ATTN_KV_HEADS * V_EXT))
    return out.reshape(B * L, BRANCH_WIDTH)


def _ref_rows(b, h, fwd):
    C, W = b.shape
    two_h = 2 * h
    if two_h >= 8:
        n = C // two_h
        b3 = b.reshape(n, two_h, W)
        r = jnp.where(fwd, b3[:, h - 1:h, :], b3[:, h:h + 1, :])
        return jnp.broadcast_to(r, (n, two_h, W)).reshape(C, W)
    b3 = b.reshape(C // 8, 8, W)
    pos8 = lax.broadcasted_iota(jnp.int32, (1, 8, 1), 1)
    out = None
    for blk in range(8 // two_h):
        base = blk * two_h
        r = jnp.where(fwd, b3[:, base + h - 1:base + h, :], b3[:, base + h:base + h + 1, :])
        r = jnp.broadcast_to(r, b3.shape)
        out = r if out is None else jnp.where((pos8 // two_h) == blk, r, out)
    return out.reshape(C, W)


def _hgrn_chunk(q, z, v, lb, tri, st_ref, fwd, head0):
    C, W = q.shape
    q = q.astype(F32)
    v = v.astype(F32)
    e = jnp.exp(-jnp.abs(z))
    s_big = 1.0 / (1.0 + e)
    s_small = e * s_big
    pos = z >= 0.0
    f = lb + (1.0 - lb) * jnp.where(pos, s_big, s_small)
    g = jnp.log2(jnp.maximum(f, FORGET_FLOOR))
    kk = (1.0 - lb) * jnp.where(pos, s_small, s_big)
    g_hi, g_lo = _split2(g)
    b = _dot(tri, g_hi) + _dot(tri, g_lo)
    b_end = jnp.where(fwd, b[C - 1:C, :], b[0:1, :])
    q_in = (q * jnp.exp2(b)).astype(BF16)
    k_out = (kk * jnp.exp2(b_end - b)).astype(BF16)
    s_decay = jnp.exp2(b_end)
    vb = v.astype(BF16)

    row = lax.broadcasted_iota(jnp.int32, (C, 1), 0)
    rr = lax.broadcasted_iota(jnp.int32, (C, C), 0)
    cc = lax.broadcasted_iota(jnp.int32, (C, C), 1)
    n_heads = W // HGRN_DIM
    a_mats = [jnp.zeros((C, C), F32) for _ in range(n_heads)]
    h = C // 2
    while h >= 1:
        upper = ((row % (2 * h)) >= h).astype(jnp.int32)
        q_rows = upper == fwd.astype(jnp.int32)
        ref = _ref_rows(b, h, fwd)
        ql = jnp.where(q_rows, q * jnp.exp2(b - ref), 0.0).astype(BF16)
        kl = jnp.where(q_rows, 0.0, kk * jnp.exp2(ref - b)).astype(BF16)
        level = (jnp.bitwise_xor(rr, cc) // h) == 1
        for hd in range(n_heads):
            sl = slice(hd * HGRN_DIM, (hd + 1) * HGRN_DIM)
            a_mats[hd] = jnp.where(level, _dot_nt(ql[:, sl], kl[:, sl]), a_mats[hd])
        h //= 2

    qk = q * kk
    outs = []
    for hd in range(n_heads):
        sl = slice(hd * HGRN_DIM, (hd + 1) * HGRN_DIM)
        st = st_ref[head0 + hd]
        o = _dot_nt(q_in[:, sl], st.astype(BF16))
        o = o + _dot(a_mats[hd].astype(BF16), vb[:, sl])
        o = o + jnp.sum(qk[:, sl], axis=-1, keepdims=True) * v[:, sl]
        outs.append(o)
        st_ref[head0 + hd] = st * s_decay[:, sl] + _dot_tn(vb[:, sl], k_out[:, sl])
    return jnp.concatenate(outs, axis=-1)


HGRN_GROUP = 4


def _hgrn_kernel(q_ref, f_ref, v_ref, lb_ref, tri_ref, o_ref, st_ref, *, n_inner):
    d = pl.program_id(0)
    fwd = d == 0

    @pl.when(pl.program_id(2) == 0)
    def _():
        st_ref[...] = jnp.zeros(st_ref.shape, F32)

    lb = lb_ref[0]
    tri = tri_ref[0]

    def body(i, carry):
        ci = jnp.where(fwd, i, n_inner - 1 - i)
        r0 = pl.multiple_of(ci * HGRN_CHUNK, HGRN_CHUNK)
        rows = pl.ds(r0, HGRN_CHUNK)
        for grp in range(HGRN_HEADS // HGRN_GROUP):
            cols = slice(grp * HGRN_GROUP * HGRN_DIM, (grp + 1) * HGRN_GROUP * HGRN_DIM)
            o_ref[0, 0, rows, cols] = _hgrn_chunk(q_ref[0, rows, cols], f_ref[0, rows, cols],
                                                  v_ref[0, rows, cols], lb[:, cols], tri, st_ref, fwd,
                                                  grp * HGRN_GROUP)
        return carry

    lax.fori_loop(0, n_inner, body, 0, unroll=8)


def _hgrn(hq, hf, hi, lb, B, L):
    W = BRANCH_WIDTH
    rb = _row_tile(L, 512)
    nb = L // rb
    C = HGRN_CHUNK
    tri_f = np.tril(np.ones((C, C), np.float32))
    tri = jnp.asarray(np.stack([tri_f, tri_f.T]), dtype=BF16)
    blk = lambda d, b, c: (b, c + d * (nb - 1 - 2 * c), 0)
    return pl.pallas_call(
        functools.partial(_hgrn_kernel, n_inner=rb // C),
        grid=(2, B, nb),
        in_specs=[pl.BlockSpec((1, rb, W), blk),
                  pl.BlockSpec((1, rb, W), lambda d, b, c: (b, c + d * (nb - 1 - 2 * c), d)),
                  pl.BlockSpec((1, rb, W), blk),
                  pl.BlockSpec((1, 1, W), lambda d, b, c: (d, 0, 0)),
                  pl.BlockSpec((1, C, C), lambda d, b, c: (d, 0, 0))],
        out_specs=pl.BlockSpec((1, 1, rb, W), lambda d, b, c: (d, b, c + d * (nb - 1 - 2 * c), 0)),
        out_shape=jax.ShapeDtypeStruct((2, B, L, W), F32),
        scratch_shapes=[pltpu.VMEM((HGRN_HEADS, HGRN_DIM, HGRN_DIM), F32)],
        compiler_params=_cparams(("parallel", "parallel", "arbitrary")),
        name="hgrn_scan",
    )(hq.reshape(B, L, W), hf.reshape(B, L, 2 * W), hi.reshape(B, L, W), lb[:, None, :], tri)


def _filter_features(L):
    t = jnp.linspace(0.0, 1.0, L, dtype=F32)[:, None]
    w = 2.0 * math.pi * jnp.arange(L, dtype=F32) / L
    bands = jnp.linspace(1e-4, FILTER_BANDS - 1, FILTER_BANDS, dtype=F32)
    ang = w[:, None] * bands[None, :]
    feats = jnp.concatenate([t, jnp.cos(ang), -jnp.sin(ang)], axis=-1)
    feats2 = jnp.concatenate([feats, feats[:1], jnp.flip(feats[1:], axis=0)], axis=0)
    t2 = jnp.concatenate([t, t[:1], jnp.flip(t[1:], axis=0)], axis=0)
    deltas = jnp.abs(jnp.linspace(math.log(FILTER_TARGET) / FILTER_FAST_DECAY,
                                  math.log(FILTER_TARGET) / FILTER_SLOW_DECAY, N_FILTER_CH, dtype=F32))
    return feats2, t2, deltas[None, :]


def _filter_kernel(feat_ref, t_ref, w1_ref, b1_ref, fr_ref, w2_ref, b2_ref, w3f_ref, w3b_ref,
                   df_ref, db_ref, k_ref, h_ref, *, L):
    @pl.when(pl.program_id(0) == 0)
    def _():
        freq = fr_ref[...]
        h1 = jnp.sin(freq * (_dot3(feat_ref[...], w1_ref[...]) + b1_ref[...]))
        h_ref[...] = jnp.sin(freq * (_dot3(h1, w2_ref[...]) + b2_ref[...]))

    h = h_ref[...]
    t = t_ref[...]
    top = _dot3(h[:L], w3f_ref[...]) * jnp.exp(-t[:L] * df_ref[...])
    bot = _dot3(h[L:], w3b_ref[...]) * jnp.exp(-t[L:] * db_ref[...])
    row = lax.broadcasted_iota(jnp.int32, (L, 1), 0)
    bot = jnp.where(row == 0, 0.0, bot)
    norm = jnp.sum(jnp.abs(top), axis=0, keepdims=True) + jnp.sum(jnp.abs(bot), axis=0, keepdims=True)
    k_ref[:L, :] = top / norm
    k_ref[L:, :] = bot / norm


def _hyena_filter(L, consts, w1, b1, freq, w2, b2, w3):
    feats2, t2, deltas = consts
    wb = 256
    nw = HYENA_WIDTH // wb
    full = lambda shape: pl.BlockSpec(shape, lambda c: (0,) * len(shape))
    fcol = lambda c: (0, (c // nw) * 2 * nw + c % nw)
    bcol = lambda c: (0, (c // nw) * 2 * nw + nw + c % nw)
    return pl.pallas_call(
        functools.partial(_filter_kernel, L=L),
        grid=(HYENA_ORDER * nw,),
        in_specs=[full((2 * L, FILTER_EMB)), full((2 * L, 1)),
                  full((FILTER_EMB, FILTER_HIDDEN)), full((1, FILTER_HIDDEN)), full((1, FILTER_HIDDEN)),
                  full((FILTER_HIDDEN, FILTER_HIDDEN)), full((1, FILTER_HIDDEN)),
                  pl.BlockSpec((FILTER_HIDDEN, wb), fcol), pl.BlockSpec((FILTER_HIDDEN, wb), bcol),
                  pl.BlockSpec((1, wb), fcol), pl.BlockSpec((1, wb), bcol)],
        out_specs=pl.BlockSpec((2 * L, wb), lambda c: (0, c)),
        out_shape=jax.ShapeDtypeStruct((2 * L, HYENA_ORDER * HYENA_WIDTH), F32),
        scratch_shapes=[pltpu.VMEM((2 * L, FILTER_HIDDEN), F32)],
        compiler_params=_cparams(("arbitrary",)),
        name="hyena_filter",
    )(feats2, t2, w1, b1[None, :], freq[None, :], w2, b2[None, :], w3, w3, deltas, deltas)


def _dft_consts(L):
    N = 2 * L
    N2 = FFT_N2
    N1 = N // N2
    k1 = np.arange(N1)
    ang_a = 2.0 * np.pi * np.outer(k1, np.arange(N1)) / N1
    fa_full = np.concatenate([np.cos(ang_a), -np.sin(ang_a)], axis=0)
    fa_half = fa_full[:, :N1 // 2]
    n2 = np.arange(N2)
    kk = k1[:, None, None] + N1 * np.arange(N2)[None, :, None]
    ang_b = 2.0 * np.pi * (kk * n2[None, None, :]) / N
    mr, mi = np.cos(ang_b), -np.sin(ang_b)
    m_fwd = np.concatenate([np.concatenate([mr, -mi], axis=2),
                            np.concatenate([mi, mr], axis=2)], axis=1)
    wr, wi = np.swapaxes(mr, 1, 2), -np.swapaxes(mi, 1, 2)
    m_inv = np.concatenate([np.concatenate([wr, -wi], axis=2),
                            np.concatenate([wi, wr], axis=2)], axis=1)
    ang_c = 2.0 * np.pi * np.outer(np.arange(N1 // 2), k1) / N1
    g_inv = np.concatenate([np.cos(ang_c), -np.sin(ang_c)], axis=1) / N

    def hi_lo(a):
        a = jnp.asarray(a, dtype=F32)
        hi = a.astype(BF16)
        return hi, (a - hi.astype(F32)).astype(BF16)

    eye = np.eye(SUBLANES)
    fa_tiles = jnp.asarray(np.kron(fa_half, eye), dtype=BF16)
    g_tiles = jnp.asarray(np.kron(g_inv, eye), dtype=BF16)
    return dict(N1=N1, N2=N2, fa_full=hi_lo(fa_full), fa_tiles=fa_tiles, g_tiles=g_tiles,
                m_fwd=hi_lo(m_fwd), m_inv=hi_lo(m_inv))


def _mm3(m_hi, m_lo, x):
    x_hi, x_lo = _split2(x)
    return _dot(m_hi, x_hi) + _dot(m_hi, x_lo) + _dot(m_lo, x_hi)


def _stage_a_kernel(z_ref, fh_ref, fl_ref, a_ref):
    n1 = a_ref.shape[2]
    res = _mm3(fh_ref[...], fl_ref[...], z_ref[0])
    a_ref[0, 0] = res[:n1]
    a_ref[0, 1] = res[n1:]


def _fft_stage_a(z, fa, N1):
    B, nr, cols = z.shape
    cb = min(cols, 4096)
    return pl.pallas_call(
        _stage_a_kernel,
        grid=(B, cols // cb),
        in_specs=[pl.BlockSpec((1, nr, cb), lambda b, j: (b, 0, j)),
                  pl.BlockSpec((2 * N1, nr), lambda b, j: (0, 0)),
                  pl.BlockSpec((2 * N1, nr), lambda b, j: (0, 0))],
        out_specs=pl.BlockSpec((1, 2, N1, cb), lambda b, j: (b, 0, 0, j)),
        out_shape=jax.ShapeDtypeStruct((B, 2, N1, cols), F32),
        compiler_params=_cparams(("parallel", "parallel")),
        name="fft_stage_a",
    )(z, fa[0], fa[1])


def _stage_b_filter_kernel(a_ref, mh_ref, ml_ref, o_ref):
    n2 = a_ref.shape[3]
    a = a_ref[0, :, 0].reshape(2 * n2, a_ref.shape[4])
    res = _mm3(mh_ref[0], ml_ref[0], a)
    o_ref[0, 0] = res[:n2]
    o_ref[1, 0] = res[n2:]


def _filter_spectrum(k2l, dc):
    N1, N2 = dc["N1"], dc["N2"]
    OW = k2l.shape[1]
    a = _fft_stage_a(k2l.reshape(1, N1, N2 * OW), dc["fa_full"], N1)
    a = a.reshape(1, 2, N1, N2, OW)
    wb = 512
    return pl.pallas_call(
        _stage_b_filter_kernel,
        grid=(N1, OW // wb),
        in_specs=[pl.BlockSpec((1, 2, 1, N2, wb), lambda k, j: (0, 0, k, 0, j)),
                  pl.BlockSpec((1, 2 * N2, 2 * N2), lambda k, j: (k, 0, 0)),
                  pl.BlockSpec((1, 2 * N2, 2 * N2), lambda k, j: (k, 0, 0))],
        out_specs=pl.BlockSpec((2, 1, N2, wb), lambda k, j: (0, k, 0, j)),
        out_shape=jax.ShapeDtypeStruct((2, N1, N2, OW), F32),
        compiler_params=_cparams(("parallel", "parallel")),
        name="fft_filter_spectrum",
    )(a, dc["m_fwd"][0], dc["m_fwd"][1])


FFT_SPECTRUM_BYTES = 8 * 1024 * 1024


def _long_conv_kernel(z_ref, gate_ref, h_ref, fa_ref, mf_ref, mi_ref, g_ref, bias_ref, o_ref, a_ref):
    n1, n2, wq = a_ref.shape[1], a_ref.shape[2], a_ref.shape[3]
    nr = z_ref.shape[1]
    s8 = SUBLANES
    fa = fa_ref[...]
    side = max(1, MXU_COLS // wq)
    tiles = [slice(jt * s8, (jt + 1) * s8) for jt in range(n2 // s8)]
    for j0 in range(0, len(tiles), side):
        group = tiles[j0:j0 + side]
        zt = jnp.concatenate([z_ref[0, :, t, :].reshape(nr * s8, wq) for t in group], axis=-1).astype(BF16)
        res = _dot(fa, zt)
        for i, t in enumerate(group):
            part = res[:, i * wq:(i + 1) * wq]
            a_ref[0, :, t, :] = part[:n1 * s8].reshape(n1, s8, wq)
            a_ref[1, :, t, :] = part[n1 * s8:].reshape(n1, s8, wq)

    def per_k1(k, carry):
        a = jnp.concatenate([a_ref[0, k], a_ref[1, k]], axis=0).astype(BF16)
        zz = _dot(mf_ref[k], a)
        zr, zi = zz[:n2], zz[n2:]
        hr, hi = h_ref[0, k], h_ref[1, k]
        y = jnp.concatenate([zr * hr - zi * hi, zr * hi + zi * hr], axis=0).astype(BF16)
        c = _dot(mi_ref[k], y)
        a_ref[0, k] = c[:n2]
        a_ref[1, k] = c[n2:]
        return carry

    lax.fori_loop(0, n1, per_k1, 0, unroll=8)
    g = g_ref[...]
    bias = bias_ref[...]
    for j0 in range(0, len(tiles), side):
        group = tiles[j0:j0 + side]
        c = jnp.concatenate(
            [jnp.concatenate([a_ref[0, :, t, :].reshape(n1 * s8, wq),
                              a_ref[1, :, t, :].reshape(n1 * s8, wq)], axis=0) for t in group],
            axis=-1).astype(BF16)
        y_all = _dot(g, c)
        for i, t in enumerate(group):
            y = y_all[:, i * wq:(i + 1) * wq].reshape(nr, s8, wq)
            o_ref[0, :, t, :] = gate_ref[0, :, t, :] * (y + z_ref[0, :, t, :] * bias)


def _long_conv_gated(z, gate, hf, order, skip, dc):
    B, nr, N2, W = z.shape
    N1 = dc["N1"]
    wq = min(W, max(LANES, FFT_SPECTRUM_BYTES // (2 * N1 * N2 * 4) // LANES * LANES))
    nw = W // wq
    rows = pl.BlockSpec((1, nr, N2, wq), lambda w, b: (b, 0, 0, w))
    once = pl.Buffered(1)
    mats = pl.BlockSpec((N1, 2 * N2, 2 * N2), lambda w, b: (0, 0, 0), pipeline_mode=once)
    return pl.pallas_call(
        _long_conv_kernel,
        grid=(nw, B),
        in_specs=[rows, rows,
                  pl.BlockSpec((2, N1, N2, wq), lambda w, b: (0, 0, 0, order * nw + w), pipeline_mode=once),
                  pl.BlockSpec((2 * N1 * SUBLANES, nr * SUBLANES), lambda w, b: (0, 0)),
                  mats, mats,
                  pl.BlockSpec((nr * SUBLANES, 2 * N1 * SUBLANES), lambda w, b: (0, 0)),
                  pl.BlockSpec((1, wq), lambda w, b: (0, w))],
        out_specs=rows,
        out_shape=jax.ShapeDtypeStruct((B, nr, N2, W), F32),
        scratch_shapes=[pltpu.VMEM((2, N1, N2, wq), F32)],
        compiler_params=_cparams(("parallel", "parallel")),
        name="hyena_long_conv",
    )(z, gate, hf, dc["fa_tiles"], dc["m_fwd"][0], dc["m_inv"][0], dc["g_tiles"], skip[None, :])


HALO_ROWS = 16


def _short_conv_kernel(u_ref, prev_ref, next_ref, w_ref, b_ref, v_ref, x1_ref, x2_ref):
    i = pl.program_id(1)
    u = u_ref[0].astype(F32)
    tl = u.shape[0]
    row = lax.broadcasted_iota(jnp.int32, (tl, 1), 0)
    prev_last = prev_ref[0].astype(F32)[HALO_ROWS - 1:HALO_ROWS, :]
    next_first = next_ref[0].astype(F32)[0:1, :]
    before = jnp.where(i == 0, 0.0, prev_last)
    after = jnp.where(i == pl.num_programs(1) - 1, 0.0, next_first)
    up = jnp.where(row == 0, before, pltpu.roll(u, 1, 0))
    un = jnp.where(row == tl - 1, after, pltpu.roll(u, tl - 1, 0))
    w = w_ref[...]
    uc = up * w[0:1] + u * w[1:2] + un * w[2:3] + b_ref[...]
    W = HYENA_WIDTH
    v_ref[0] = uc[:, :W]
    x1_ref[0] = uc[:, W:2 * W]
    x2_ref[0] = uc[:, 2 * W:]


def _short_conv(hy, conv_w, conv_b, B, L):
    W3 = 3 * HYENA_WIDTH
    tl = _row_tile(L, 512)
    nl = L // tl
    u = hy.reshape(B, L, W3)
    rh = tl // HALO_ROWS
    spec_o = pl.BlockSpec((1, tl, HYENA_WIDTH), lambda b, i: (b, i, 0))
    return pl.pallas_call(
        _short_conv_kernel,
        grid=(B, nl),
        in_specs=[pl.BlockSpec((1, tl, W3), lambda b, i: (b, i, 0)),
                  pl.BlockSpec((1, HALO_ROWS, W3), lambda b, i: (b, jnp.maximum(i * rh - 1, 0), 0)),
                  pl.BlockSpec((1, HALO_ROWS, W3),
                               lambda b, i: (b, jnp.minimum((i + 1) * rh, L // HALO_ROWS - 1), 0)),
                  pl.BlockSpec((3, W3), lambda b, i: (0, 0)),
                  pl.BlockSpec((1, W3), lambda b, i: (0, 0))],
        out_specs=[spec_o, spec_o, spec_o],
        out_shape=[jax.ShapeDtypeStruct((B, L, HYENA_WIDTH), F32)] * 3,
        compiler_params=_cparams(("parallel", "parallel")),
        name="hyena_short_conv",
    )(u, u, u, conv_w, conv_b[None, :])


def _hyena(hy, hf, conv_w, conv_b, skip, dc, B, L):
    v, x1, x2 = _short_conv(hy, conv_w, conv_b, B, L)
    shape = (B, dc["N1"] // 2, dc["N2"], HYENA_WIDTH)
    z = _long_conv_gated(v.reshape(shape), x1.reshape(shape), hf, 0, skip[0], dc)
    y = _long_conv_gated(z, x2.reshape(shape), hf, 1, skip[1], dc)
    return y.reshape(B * L, HYENA_WIDTH)


def _merge_kernel(x_ref, oa_ref, oh_ref, hg_ref, oc_ref, gl_ref, hgn_ref, wb_ref, wo_ref, g_ref, b_ref,
                  out_ref):
    o = oh_ref[0] + oh_ref[1]
    parts = []
    for hd in range(HGRN_HEADS):
        oh = o[:, hd * HGRN_DIM:(hd + 1) * HGRN_DIM]
        ms = jnp.mean(oh * oh, axis=-1, keepdims=True)
        parts.append(oh * lax.rsqrt(ms + RMS_EPS))
    o_h = jnp.concatenate(parts, axis=-1) * hgn_ref[...] * _silu(hg_ref[...].astype(F32))
    gl = gl_ref[...].astype(F32)
    branches = (oa_ref[...], o_h, oc_ref[...])
    merged = None
    for j, ob in enumerate(branches):
        term = _sigmoid(gl[:, j * D_MODEL:(j + 1) * D_MODEL]) * _dot(ob.astype(BF16), wb_ref[j])
        merged = term if merged is None else merged + term
    mix = _dot(merged.astype(BF16), wo_ref[...])
    out_ref[...] = _layer_norm(ALPHA * x_ref[...] + mix, g_ref[...], b_ref[...])


def _merge(x2, o_a, o_dir, hg, o_c, gl, hgrn_gain, wb_bf16, wo_bf16, ln_g, ln_b):
    T = x2.shape[0]
    tm = _row_tile(T, 512)
    W = BRANCH_WIDTH
    row = lambda w: pl.BlockSpec((tm, w), lambda i: (i, 0))
    return pl.pallas_call(
        _merge_kernel,
        grid=(T // tm,),
        in_specs=[row(D_MODEL), row(W), pl.BlockSpec((2, tm, W), lambda i: (0, i, 0)), row(W), row(W),
                  row(N_BRANCH * D_MODEL),
                  pl.BlockSpec((1, W), lambda i: (0, 0)),
                  pl.BlockSpec((N_BRANCH, W, D_MODEL), lambda i: (0, 0, 0)),
                  pl.BlockSpec((D_MODEL, D_MODEL), lambda i: (0, 0)),
                  pl.BlockSpec((1, D_MODEL), lambda i: (0, 0)),
                  pl.BlockSpec((1, D_MODEL), lambda i: (0, 0))],
        out_specs=row(D_MODEL),
        out_shape=jax.ShapeDtypeStruct((T, D_MODEL), F32),
        compiler_params=_cparams(("parallel",)),
        name="merge_out_ln",
    )(x2, o_a, o_dir.reshape(2, T, W), hg, o_c, gl, jnp.tile(hgrn_gain, HGRN_HEADS)[None, :],
      wb_bf16, wo_bf16, ln_g[None, :], ln_b[None, :])


def _ffn_kernel(x_ref, wg_ref, wu_ref, wd_ref, g_ref, b_ref, out_ref, acc_ref):
    j = pl.program_id(1)
    x = x_ref[...]
    xb = x.astype(BF16)
    hmid = _silu(_dot(xb, wg_ref[...])) * _dot(xb, wu_ref[...])
    contrib = _dot(hmid.astype(BF16), wd_ref[...])

    @pl.when(j == 0)
    def _():
        acc_ref[...] = contrib

    @pl.when(j > 0)
    def _():
        acc_ref[...] += contrib

    @pl.when(j == pl.num_programs(1) - 1)
    def _():
        out_ref[...] = _layer_norm(ALPHA * x + acc_ref[...], g_ref[...], b_ref[...])


def _ffn_dense(x2, w_gu, w_down, ln_g, ln_b):
    T = x2.shape[0]
    tm = _row_tile(T, 512)
    ft = D_FF_DENSE
    nf = D_FF_DENSE // ft
    once = pl.Buffered(1)
    return pl.pallas_call(
        _ffn_kernel,
        grid=(T // tm, nf),
        in_specs=[pl.BlockSpec((tm, D_MODEL), lambda i, j: (i, 0)),
                  pl.BlockSpec((D_MODEL, ft), lambda i, j: (0, j), pipeline_mode=once),
                  pl.BlockSpec((D_MODEL, ft), lambda i, j: (0, j + nf), pipeline_mode=once),
                  pl.BlockSpec((ft, D_MODEL), lambda i, j: (j, 0), pipeline_mode=once),
                  pl.BlockSpec((1, D_MODEL), lambda i, j: (0, 0)),
                  pl.BlockSpec((1, D_MODEL), lambda i, j: (0, 0))],
        out_specs=pl.BlockSpec((tm, D_MODEL), lambda i, j: (i, 0)),
        out_shape=jax.ShapeDtypeStruct((T, D_MODEL), F32),
        scratch_shapes=[pltpu.VMEM((tm, D_MODEL), F32)],
        compiler_params=_cparams(("parallel", "arbitrary")),
        name="ffn_dense_ln",
    )(x2, w_gu, w_gu, w_down, ln_g[None, :], ln_b[None, :])


MOE_BLOCK = 1024
MOE_SUB = 64
MOE_BIG = 8
MOE_TILE = 256


def _route_kernel(x_ref, r_ref, col_ref, row_ref, s_ref):
    tb = x_ref.shape[0]
    logits = _dot3(x_ref[...], r_ref[...])
    lane = lax.broadcasted_iota(jnp.int32, logits.shape, 1)
    logits = jnp.where(lane < N_EXPERTS, logits, -jnp.inf)
    m1 = jnp.max(logits, axis=-1, keepdims=True)
    i1 = jnp.min(jnp.where(logits == m1, lane, LANES), axis=-1, keepdims=True)
    rest = jnp.where(lane == i1, -jnp.inf, logits)
    m2 = jnp.max(rest, axis=-1, keepdims=True)
    i2 = jnp.min(jnp.where(rest == m2, lane, LANES), axis=-1, keepdims=True)
    e2 = jnp.exp(m2 - m1)
    w1 = 1.0 / (1.0 + e2)
    w2 = e2 / (1.0 + e2)
    oh1 = lane == i1
    oh2 = lane == i2
    sel = jnp.where(oh1, 1.0, jnp.where(oh2, 1.0, 0.0))
    rr = lax.broadcasted_iota(jnp.int32, (tb, tb), 0)
    cc = lax.broadcasted_iota(jnp.int32, (tb, tb), 1)
    earlier = jnp.where(rr > cc, 1.0, 0.0).astype(BF16)
    rank = _dot(earlier, sel.astype(BF16))
    cnt = jnp.sum(sel, axis=0, keepdims=True)
    nsub = jnp.floor((cnt + (MOE_SUB - 1)) * (1.0 / MOE_SUB))
    r2 = lax.broadcasted_iota(jnp.int32, (LANES, LANES), 0)
    c2 = lax.broadcasted_iota(jnp.int32, (LANES, LANES), 1)
    before = jnp.where(r2 < c2, 1.0, 0.0).astype(BF16)
    nsub8 = jnp.broadcast_to(nsub, (8, LANES))
    off8 = _dot(nsub8.astype(BF16), before) * MOE_SUB
    base = off8[0:1] + rank
    d1 = jnp.sum(jnp.where(oh1, base, 0.0), axis=-1, keepdims=True)
    d2 = jnp.sum(jnp.where(oh2, base, 0.0), axis=-1, keepdims=True)
    col = jnp.where(lane == 0, d1, jnp.where(lane == 1, d2, jnp.where(lane == 2, w1, jnp.where(lane == 3, w2, 0.0))))
    col_ref[...] = col
    row_ref[0] = col.T[0:8, :]
    lane8 = lax.broadcasted_iota(jnp.int32, (8, LANES), 1)
    info = jnp.where(lane8 < N_EXPERTS, off8, pltpu.roll(nsub8, N_EXPERTS, 1))
    s_ref[0] = info.astype(jnp.int32)


def _moe_route(x2, router):
    T = x2.shape[0]
    tb = _row_tile(T, MOE_BLOCK)
    nb = T // tb
    r_pad = jnp.pad(router, ((0, 0), (0, LANES - N_EXPERTS)))
    return pl.pallas_call(
        _route_kernel,
        grid=(nb,),
        in_specs=[pl.BlockSpec((tb, D_MODEL), lambda i: (i, 0)),
                  pl.BlockSpec((D_MODEL, LANES), lambda i: (0, 0))],
        out_specs=[pl.BlockSpec((tb, LANES), lambda i: (i, 0)),
                   pl.BlockSpec((1, 8, tb), lambda i: (i, 0, 0)),
                   pl.BlockSpec((1, 8, LANES), lambda i: (i, 0, 0))],
        out_shape=[jax.ShapeDtypeStruct((T, LANES), F32),
                   jax.ShapeDtypeStruct((nb, 8, tb), F32),
                   jax.ShapeDtypeStruct((nb, 8, LANES), jnp.int32)],
        compiler_params=_cparams(("parallel",)),
        name="moe_route",
    )(x2, r_pad)


def _moe_kernel(s_ref, x_ref, col_ref, row_ref, wg_ref, wu_ref, wd_ref, g_ref, b_ref, out_ref,
                xs_ref, ys_ref):
    blk = pl.program_id(0)
    e = pl.program_id(1)
    j = pl.program_id(2)
    tb = x_ref.shape[0]
    n_tiles = xs_ref.shape[0] // MOE_TILE
    used = s_ref[blk, N_EXPERTS - 1] + s_ref[blk, 2 * N_EXPERTS - 1] * MOE_SUB
    d1r = row_ref[0, 0:1, :]
    d2r = row_ref[0, 1:2, :]

    @pl.when(jnp.logical_and(e == 0, j == 0))
    def _():
        xb = x_ref[...].astype(BF16)
        for rt in range(n_tiles):
            @pl.when(rt * MOE_TILE < used)
            def _():
                r = (lax.broadcasted_iota(jnp.int32, (MOE_TILE, tb), 0) + rt * MOE_TILE).astype(F32)
                p = jnp.where(r == d1r, 1.0, jnp.where(r == d2r, 1.0, 0.0)).astype(BF16)
                xs_ref[rt * MOE_TILE:(rt + 1) * MOE_TILE, :] = _dot(p, xb).astype(BF16)

    start = s_ref[blk, e]
    nsub = s_ref[blk, N_EXPERTS + e]

    def expert_rows(first, n_rows):
        rows = pl.ds(pl.multiple_of(first, MOE_SUB), n_rows)
        xt = xs_ref[rows, :]
        hmid = _silu(_dot(xt, wg_ref[0])) * _dot(xt, wu_ref[0])
        y = _dot(hmid.astype(BF16), wd_ref[0])

        @pl.when(j == 0)
        def _():
            ys_ref[rows, :] = y

        @pl.when(j > 0)
        def _():
            ys_ref[rows, :] += y

    n_big = nsub // MOE_BIG
    rem = nsub - MOE_BIG * n_big

    def big(i, carry):
        expert_rows(start + i * (MOE_BIG * MOE_SUB), MOE_BIG * MOE_SUB)
        return carry

    lax.fori_loop(0, n_big, big, 0)
    done = n_big * MOE_BIG
    part = MOE_BIG // 2
    while part >= 1:
        @pl.when((rem // part) % 2 == 1)
        def _(done=done, part=part):
            expert_rows(start + (done + (rem // (2 * part)) * (2 * part)) * MOE_SUB, part * MOE_SUB)
        part //= 2

    @pl.when(jnp.logical_and(e == pl.num_programs(1) - 1, j == pl.num_programs(2) - 1))
    def _():
        sub_per_tile = MOE_TILE // MOE_SUB
        filled = (used // MOE_SUB) % sub_per_tile
        for k in range(1, sub_per_tile):
            @pl.when(jnp.logical_and(filled > 0, filled <= k))
            def _(k=k):
                first = (used // MOE_TILE) * MOE_TILE + k * MOE_SUB
                ys_ref[pl.ds(pl.multiple_of(first, MOE_SUB), MOE_SUB), :] = jnp.zeros((MOE_SUB, D_MODEL), F32)

        w1r = row_ref[0, 2:3, :]
        w2r = row_ref[0, 3:4, :]
        for rt in range(n_tiles):
            @pl.when(rt * MOE_TILE < used)
            def _():
                r = (lax.broadcasted_iota(jnp.int32, (MOE_TILE, tb), 0) + rt * MOE_TILE).astype(F32)
                w_rows = jnp.sum(jnp.where(r == d1r, w1r, 0.0) + jnp.where(r == d2r, w2r, 0.0),
                                 axis=-1, keepdims=True)
                rows = slice(rt * MOE_TILE, (rt + 1) * MOE_TILE)
                xs_ref[rows, :] = (ys_ref[rows, :] * w_rows).astype(BF16)

            @pl.when(rt * MOE_TILE >= used)
            def _():
                xs_ref[rt * MOE_TILE:(rt + 1) * MOE_TILE, :] = jnp.zeros((MOE_TILE, D_MODEL), BF16)

        n_all = xs_ref.shape[0]
        c = lax.broadcasted_iota(jnp.int32, (MOE_TILE, n_all), 1).astype(F32)
        for tt in range(tb // MOE_TILE):
            toks = slice(tt * MOE_TILE, (tt + 1) * MOE_TILE)
            d1c = col_ref[toks, 0:1]
            d2c = col_ref[toks, 1:2]
            q = jnp.where(c == d1c, 1.0, jnp.where(c == d2c, 1.0, 0.0)).astype(BF16)
            ff = _dot(q, xs_ref[...])
            out_ref[toks, :] = _layer_norm(ALPHA * x_ref[toks, :] + ff, g_ref[...], b_ref[...])


def _ffn_moe(x2, router, w_gu, w_down, ln_g, ln_b):
    T = x2.shape[0]
    tb = _row_tile(T, MOE_BLOCK)
    nb = T // tb
    n_rows = 2 * tb + N_EXPERTS * MOE_SUB
    col, row, sinfo = _moe_route(x2, router)
    ft = 1792
    nf = D_FF_EXPERT // ft
    grid_spec = pltpu.PrefetchScalarGridSpec(
        num_scalar_prefetch=1,
        grid=(nb, N_EXPERTS, nf),
        in_specs=[pl.BlockSpec((tb, D_MODEL), lambda i, e, j, s: (i, 0), pipeline_mode=pl.Buffered(1)),
                  pl.BlockSpec((tb, LANES), lambda i, e, j, s: (i, 0)),
                  pl.BlockSpec((1, 8, tb), lambda i, e, j, s: (i, 0, 0)),
                  pl.BlockSpec((1, D_MODEL, ft), lambda i, e, j, s: (e, 0, j)),
                  pl.BlockSpec((1, D_MODEL, ft), lambda i, e, j, s: (e, 0, j + nf)),
                  pl.BlockSpec((1, ft, D_MODEL), lambda i, e, j, s: (e, j, 0)),
                  pl.BlockSpec((1, D_MODEL), lambda i, e, j, s: (0, 0)),
                  pl.BlockSpec((1, D_MODEL), lambda i, e, j, s: (0, 0))],
        out_specs=pl.BlockSpec((tb, D_MODEL), lambda i, e, j, s: (i, 0)),
        scratch_shapes=[pltpu.VMEM((n_rows, D_MODEL), BF16), pltpu.VMEM((n_rows, D_MODEL), F32)])
    return pl.pallas_call(
        _moe_kernel,
        grid_spec=grid_spec,
        out_shape=jax.ShapeDtypeStruct((T, D_MODEL), F32),
        compiler_params=_cparams(("parallel", "arbitrary", "arbitrary")),
        name="ffn_moe_ln",
    )(sinfo[:, 0, :], x2, col, row, w_gu, w_gu, w_down, ln_g[None, :], ln_b[None, :])


def _trunk(x, p):
    B, L, _ = x.shape
    cos_t, sin_t = _rope_tables(L)
    filt_consts = _filter_features(L)
    dc = _dft_consts(L)
    x2 = x.reshape(B * L, D_MODEL)
    for l in range(DEPTH):
        aq, akv, hq, hf, hi, hg, hy, gl = _in_proj(x2, p["w_in"][l])
        qn, kn, vn = _attn_prep(aq, akv, cos_t, sin_t, p["q_norm_g"][l], p["k_norm_g"][l], B, L)
        o_a = _flash_attention(qn, kn, vn, B, L)
        o_dir = _hgrn(hq, hf, hi, p["lower_bounds"][l], B, L)
        k2l = _hyena_filter(L, filt_consts, p["filt_w1"][l], p["filt_b1"][l], p["filt_freq"][l],
                            p["filt_w2"][l], p["filt_b2"][l], p["filt_w3"][l])
        spec = _filter_spectrum(k2l, dc)
        o_c = _hyena(hy, spec, p["conv_w"][l], p["conv_b"][l], p["hyena_skip"][l], dc, B, L)
        x2 = _merge(x2, o_a, o_dir, hg, o_c, gl, p["hgrn_norm_g"][l], p["w_branch"][l], p["w_out"][l],
                    p["ln1_g"][l], p["ln1_b"][l])
        if l % 2 == 0:
            x2 = _ffn_dense(x2, p["ffn_w_gu"][l // 2], p["ffn_w_down"][l // 2], p["ln2_g"][l], p["ln2_b"][l])
        else:
            x2 = _ffn_moe(x2, p["router_w"][l // 2], p["expert_w_gu"][l // 2], p["expert_w_down"][l // 2],
                          p["ln2_g"][l], p["ln2_b"][l])
    return x2.reshape(B, L, D_MODEL)


def kernel(x_prompt, x_sample, w_in, q_norm_g, k_norm_g, hgrn_lb, hgrn_norm_g, conv_w, conv_b, filt_w1, filt_b1, filt_freq, filt_w2, filt_b2, filt_w3, hyena_skip, w_branch, w_out, ln1_g, ln1_b, ln2_g, ln2_b, ffn_w_gu, ffn_w_down, router_w, expert_w_gu, expert_w_down):
    s = jax.nn.softmax(hgrn_lb.astype(F32), axis=0)
    p = dict(
        w_in=w_in.astype(BF16), q_norm_g=q_norm_g, k_norm_g=k_norm_g,
        lower_bounds=jnp.cumsum(s, axis=0) - s[:1], hgrn_norm_g=hgrn_norm_g,
        conv_w=conv_w, conv_b=conv_b, filt_w1=filt_w1, filt_b1=filt_b1, filt_freq=filt_freq,
        filt_w2=filt_w2, filt_b2=filt_b2, filt_w3=filt_w3, hyena_skip=hyena_skip,
        w_branch=w_branch.astype(BF16), w_out=w_out.astype(BF16),
        ln1_g=ln1_g, ln1_b=ln1_b, ln2_g=ln2_g, ln2_b=ln2_b,
        ffn_w_gu=ffn_w_gu.astype(BF16), ffn_w_down=ffn_w_down.astype(BF16), router_w=router_w,
        expert_w_gu=expert_w_gu.astype(BF16), expert_w_down=expert_w_down.astype(BF16))
    return (_trunk(x_prompt, p), _trunk(x_sample, p))
```
